```python
import math
import jax, jax.numpy as jnp
from jax import lax
import numpy as np

D_MODEL = 1024
BATCH = 8
SEQ = 4096
DEPTH = 2

NORM_EPS = 1e-6
GLA_HEADS = 4
GLA_DV = D_MODEL // 8
GLA_DK = GLA_DV // 2
GLA_LOWRANK = 16
GLA_TAU = 16.0
GLA_CHUNK = 64
NSA_HEADS = 8
NSA_KV_GROUPS = 2
NSA_HEAD_DIM = D_MODEL // 16
CMP_BLOCK = 32
CMP_STRIDE = 16
CMP_HIDDEN = 4 * NSA_HEAD_DIM
SEL_BLOCK = 64
N_SELECT = 8
WINDOW = 512
Q_BLOCK = 128
D_INNER = 2 * D_MODEL
SSM_HEAD_DIM = 64
SSM_HEADS = D_INNER // SSM_HEAD_DIM
SSM_GROUPS = 4
SSM_STATE = 128
CONV_K = 4
SSM_CHUNK = 64
CONV_CH = D_INNER + 2 * SSM_GROUPS * SSM_STATE
FFN_DENSE = ((8 * D_MODEL // 3 + 127) // 128) * 128
N_EXPERTS = 8
TOP_K = 2
FFN_EXPERT = 7 * D_MODEL // 2
N_EVEN = (DEPTH + 1) // 2
N_ODD = DEPTH // 2
HYB_SPLITS = (GLA_HEADS * GLA_DK, GLA_HEADS * GLA_DK, GLA_HEADS * GLA_DV, GLA_LOWRANK, GLA_HEADS * GLA_DV,
              NSA_HEADS * NSA_HEAD_DIM, 6 * NSA_KV_GROUPS * NSA_HEAD_DIM, 3 * NSA_HEADS)
HYB_IN = sum(HYB_SPLITS)
HYB_MIX = GLA_HEADS * GLA_DV + NSA_HEADS * NSA_HEAD_DIM
SSM_SPLITS = (D_INNER, CONV_CH, SSM_HEADS)
SSM_IN = sum(SSM_SPLITS)

kernel_name = 'hybrid_gla_nsa_ssd_moe_trunk'


def rmsnorm(x, w):
    xf = x.astype(jnp.float32)
    y = xf * lax.rsqrt(jnp.mean(xf * xf, axis=-1, keepdims=True) + NORM_EPS)
    return (y * w.astype(jnp.float32)).astype(x.dtype)


def split_cols(a, sizes):
    return jnp.split(a, [int(s) for s in np.cumsum(sizes)[:-1]], axis=-1)


def adaln(c, w, b):
    mod = jax.nn.silu(c) @ w + b
    shift, scale, gate = jnp.split(mod[:, None, :], 3, axis=-1)
    return shift, scale, gate


def modulate(x, norm_w, shift, scale):
    return rmsnorm(x, norm_w) * (1 + scale) + shift


def masked_softmax(s, mask):
    s = jnp.where(mask, s.astype(jnp.float32), -jnp.inf)
    m = jnp.max(s, axis=-1, keepdims=True)
    m = jnp.where(jnp.isfinite(m), m, 0.0)
    p = jnp.where(mask, jnp.exp(s - m), 0.0)
    return p / jnp.maximum(jnp.sum(p, axis=-1, keepdims=True), 1e-20)


def alibi_slopes(n):
    return 2.0 ** (-8.0 * jnp.arange(1, n + 1, dtype=jnp.float32) / n)


def swiglu(h, w_gu, w_down):
    g, u = jnp.split(h @ w_gu, 2, axis=-1)
    return (jax.nn.silu(g) * u) @ w_down


def gla_mixer(q, k, v, lowrank, r, gk_up, gk_bias, out_norm):
    B, T, _ = q.shape
    H, DK, DV, C = GLA_HEADS, GLA_DK, GLA_DV, GLA_CHUNK
    NC = T // C

    def chunks(a, d):
        return a.astype(jnp.float32).reshape(B, NC, C, H, d).transpose(0, 3, 1, 2, 4)

    log_alpha = jax.nn.log_sigmoid((lowrank @ gk_up + gk_bias).astype(jnp.float32)) / GLA_TAU
    b = jnp.cumsum(chunks(log_alpha, DK), axis=3)
    qc = chunks(q, DK) * DK ** -0.5
    kc = chunks(k, DK)
    vc = chunks(v, DV)
    q_dec = qc * jnp.exp(b)
    k_dec = kc * jnp.exp(-b)
    causal = jnp.tril(jnp.ones((C, C), dtype=bool))
    att = jnp.where(causal, jnp.einsum('bhncd,bhnsd->bhncs', q_dec, k_dec), 0.0)
    o_intra = jnp.einsum('bhncs,bhnsv->bhncv', att, vc)
    b_end = b[:, :, :, -1]
    k_to_end = kc * jnp.exp(b_end[:, :, :, None] - b)

    def step(S, inp):
        qd, kd, vv, decay = inp
        o = jnp.einsum('bhcd,bhdv->bhcv', qd, S)
        S = jnp.exp(decay)[..., None] * S + jnp.einsum('bhcd,bhcv->bhdv', kd, vv)
        return S, o

    S0 = jnp.zeros((B, H, DK, DV), jnp.float32)
    xs = tuple(jnp.moveaxis(a, 2, 0) for a in (q_dec, k_to_end, vc, b_end))
    _, o_inter = lax.scan(step, S0, xs)
    o = o_intra + jnp.moveaxis(o_inter, 0, 2)
    o = o.transpose(0, 2, 3, 1, 4).reshape(B, T, H, DV)
    o = rmsnorm(o, out_norm) * jax.nn.silu(r.astype(jnp.float32).reshape(B, T, H, DV))
    return o.reshape(B, T, H * DV)


def nsa_mixer(q, kv, gate_logits, cmp_pe, cmp_w1, cmp_w2):
    B, T, _ = q.shape
    G, Dh = NSA_KV_GROUPS, NSA_HEAD_DIM
    R = NSA_HEADS // G
    qh = q.astype(jnp.float32).reshape(B, T, G, R, Dh).transpose(0, 2, 3, 1, 4) * Dh ** -0.5
    kvh = kv.astype(jnp.float32).reshape(B, T, 6, G, Dh).transpose(2, 0, 3, 1, 4)
    k_cmp, v_cmp, k_slc, v_slc, k_win, v_win = (kvh[j] for j in range(6))
    gates = jax.nn.sigmoid(gate_logits.astype(jnp.float32)).reshape(B, T, G, R, 3).transpose(0, 2, 3, 1, 4)

    n_cmp = (T - CMP_BLOCK) // CMP_STRIDE + 1
    blk_idx = np.arange(n_cmp)[:, None] * CMP_STRIDE + np.arange(CMP_BLOCK)[None, :]

    def compress(a, j):
        blocks = a[:, :, blk_idx] + cmp_pe[j]
        h = jax.nn.gelu(blocks.reshape(B, G, n_cmp, CMP_BLOCK * Dh) @ cmp_w1[j])
        return h @ cmp_w2[j]

    kc = compress(k_cmp, 0)
    vc = compress(v_cmp, 1)
    cmp_end = jnp.asarray(blk_idx[:, -1], jnp.int32)
    cmp_center = jnp.asarray(blk_idx.mean(axis=-1), jnp.float32)

    n_sel = T // SEL_BLOCK
    n_top = min(N_SELECT, n_sel)
    sel_start_np = np.arange(n_sel) * SEL_BLOCK
    overlap = jnp.asarray((blk_idx[:, :1] < sel_start_np[None, :] + SEL_BLOCK)
                          & (blk_idx[:, -1:] >= sel_start_np[None, :]), jnp.float32)
    sel_start = jnp.asarray(sel_start_np, jnp.int32)
    kb = k_slc.reshape(B, G, n_sel, SEL_BLOCK, Dh)
    vb = v_slc.reshape(B, G, n_sel, SEL_BLOCK, Dh)
    bi = jnp.arange(B)[:, None, None, None]
    gi = jnp.arange(G)[None, :, None, None]

    k_wp = jnp.pad(k_win, ((0, 0), (0, 0), (WINDOW, 0), (0, 0)))
    v_wp = jnp.pad(v_win, ((0, 0), (0, 0), (WINDOW, 0), (0, 0)))
    slopes = alibi_slopes(NSA_HEADS).reshape(G, R, 1, 1)

    def block_fn(i):
        q0 = i * Q_BLOCK
        qb = lax.dynamic_slice_in_dim(qh, q0, Q_BLOCK, axis=3)
        gb = lax.dynamic_slice_in_dim(gates, q0, Q_BLOCK, axis=3)
        t = q0 + jnp.arange(Q_BLOCK, dtype=jnp.int32)
        tf = t.astype(jnp.float32)
        s = jnp.einsum('bgrqd,bgkd->bgrqk', qb, kc) - slopes * (tf[:, None] - cmp_center[None, :])
        p_cmp = masked_softmax(s, cmp_end[None, :] <= t[:, None])
        o_cmp = jnp.einsum('bgrqk,bgkd->bgrqd', p_cmp, vc)
        imp = jnp.einsum('bgqk,ks->bgqs', jnp.sum(p_cmp, axis=2), overlap)
        blk_t = t // SEL_BLOCK
        ids = jnp.arange(n_sel)[None, :]
        forced = (ids == 0) | (ids == blk_t[:, None]) | (ids == blk_t[:, None] - 1)
        future = sel_start[None, :] > t[:, None]
        imp = jnp.where(forced, jnp.inf, jnp.where(future, -jnp.inf, imp))
        _, idx = lax.top_k(imp, n_top)
        ks_ = kb[bi, gi, idx].reshape(B, G, Q_BLOCK, n_top * SEL_BLOCK, Dh)
        vs_ = vb[bi, gi, idx].reshape(B, G, Q_BLOCK, n_top * SEL_BLOCK, Dh)
        pos = (idx[..., None] * SEL_BLOCK + jnp.arange(SEL_BLOCK)).reshape(B, G, Q_BLOCK, n_top * SEL_BLOCK)
        dist = (t[None, None, :, None] - pos)[:, :, None]
        s = jnp.einsum('bgrqd,bgqkd->bgrqk', qb, ks_) - slopes * dist.astype(jnp.float32)
        p_slc = masked_softmax(s, dist >= 0)
        o_slc = jnp.einsum('bgrqk,bgqkd->bgrqd', p_slc, vs_)
        kw = lax.dynamic_slice_in_dim(k_wp, q0, WINDOW + Q_BLOCK, axis=2)
        vw = lax.dynamic_slice_in_dim(v_wp, q0, WINDOW + Q_BLOCK, axis=2)
        s_pos = q0 - WINDOW + jnp.arange(WINDOW + Q_BLOCK, dtype=jnp.int32)
        d = t[:, None] - s_pos[None, :]
        s = jnp.einsum('bgrqd,bgkd->bgrqk', qb, kw) - slopes * d.astype(jnp.float32)
        p_win = masked_softmax(s, (d >= 0) & (d < WINDOW) & (s_pos[None, :] >= 0))
        o_win = jnp.einsum('bgrqk,bgkd->bgrqd', p_win, vw)
        o = gb[..., 0:1] * o_cmp + gb[..., 1:2] * o_slc + gb[..., 2:3] * o_win
        return o.transpose(0, 3, 1, 2, 4).reshape(B, Q_BLOCK, NSA_HEADS * Dh)

    out = lax.map(block_fn, jnp.arange(T // Q_BLOCK))
    return out.transpose(1, 0, 2, 3).reshape(B, T, NSA_HEADS * Dh)


def hybrid_mixer(h, w_in, gk_up, gk_bias, gla_norm, cmp_pe, cmp_w1, cmp_w2, w_out):
    q_a, k_a, v_a, lr_a, r_a, q_b, kv_b, g_b = split_cols(h @ w_in, HYB_SPLITS)
    o_a = gla_mixer(q_a, k_a, v_a, lr_a, r_a, gk_up, gk_bias, gla_norm)
    o_b = nsa_mixer(q_b, kv_b, g_b, cmp_pe, cmp_w1, cmp_w2)
    return jnp.concatenate([o_a, o_b], axis=-1).astype(h.dtype) @ w_out


def mamba2_mixer(h, w_in, conv_w, conv_b, dt_bias, a_log, d_skip, norm_w, w_out):
    B, T, _ = h.shape
    H, P, G, N, Q = SSM_HEADS, SSM_HEAD_DIM, SSM_GROUPS, SSM_STATE, SSM_CHUNK
    R = H // G
    NC = T // Q
    z, xbc, dt = split_cols(h @ w_in, SSM_SPLITS)
    xbc = lax.conv_general_dilated(xbc, conv_w[:, None, :].astype(xbc.dtype), window_strides=(1,),
                                   padding=((CONV_K - 1, 0),), dimension_numbers=('NWC', 'WIO', 'NWC'),
                                   feature_group_count=CONV_CH) + conv_b
    xbc = jax.nn.silu(xbc.astype(jnp.float32))
    xs_, Bm, Cm = split_cols(xbc, (D_INNER, G * N, G * N))
    dt = jax.nn.softplus(dt.astype(jnp.float32) + dt_bias)
    A = -jnp.exp(a_log.astype(jnp.float32))
    x = xs_.reshape(B, T, H, P)
    xdt = (x * dt[..., None]).reshape(B, NC, Q, G, R, P)
    Bc = Bm.reshape(B, NC, Q, G, N)
    Cc = Cm.reshape(B, NC, Q, G, N)
    a = (dt * A).reshape(B, NC, Q, G, R).transpose(0, 3, 4, 1, 2)
    cum = jnp.cumsum(a, axis=-1)
    seg = cum[..., :, None] - cum[..., None, :]
    causal = jnp.tril(jnp.ones((Q, Q), dtype=bool))
    L = jnp.exp(jnp.where(causal, seg, -jnp.inf))
    cb = jnp.einsum('bclgn,bcsgn->bgcls', Cc, Bc)
    y_diag = jnp.einsum('bgcls,bgrcls,bcsgrp->bclgrp', cb, L, xdt)
    decay_to_end = jnp.exp(cum[..., -1:] - cum)
    decay_from_start = jnp.exp(cum)
    chunk_decay = jnp.exp(cum[..., -1])

    def step(S, inp):
        Bn, Cn, xn, dte, dfs, cd = inp
        y = jnp.einsum('blgn,bgrpn,bgrl->blgrp', Cn, S, dfs)
        S = cd[..., None, None] * S + jnp.einsum('bsgn,bgrs,bsgrp->bgrpn', Bn, dte, xn)
        return S, y

    S0 = jnp.zeros((B, G, R, P, N), jnp.float32)
    xs = (jnp.moveaxis(Bc, 1, 0), jnp.moveaxis(Cc, 1, 0), jnp.moveaxis(xdt, 1, 0),
          jnp.moveaxis(decay_to_end, 3, 0), jnp.moveaxis(decay_from_start, 3, 0), jnp.moveaxis(chunk_decay, 3, 0))
    _, y_off = lax.scan(step, S0, xs)
    y = (y_diag + jnp.moveaxis(y_off, 0, 1)).reshape(B, T, H, P) + d_skip[:, None] * x
    y = y.reshape(B, T, D_INNER) * jax.nn.silu(z.astype(jnp.float32))
    y = rmsnorm(y.reshape(B, T, G, D_INNER // G), norm_w.reshape(G, D_INNER // G)).reshape(B, T, D_INNER)
    return y.astype(h.dtype) @ w_out


def moe_ffn(h, router, w_gu, w_down):
    logits = (h @ router).astype(jnp.float32)
    top_val, top_idx = lax.top_k(logits, TOP_K)
    weights = jax.nn.softmax(top_val, axis=-1)
    gate = jnp.sum(jax.nn.one_hot(top_idx, N_EXPERTS, dtype=jnp.float32) * weights[..., None], axis=-2)
    out = jnp.zeros(h.shape, jnp.float32)
    for e in range(N_EXPERTS):
        out = out + gate[..., e:e + 1] * swiglu(h, w_gu[e], w_down[e])
    return out


def setup_inputs(seed: int = 0) -> dict:
    key = jax.random.key(seed)
    ks = jax.random.split(key, 40)
    f32 = jnp.float32

    def nrm(i, shape, scale):
        return scale * jax.random.normal(ks[i], shape, f32)

    def gain(i, shape):
        return 1.0 + 0.1 * jax.random.normal(ks[i], shape, f32)

    dt = jnp.exp(jax.random.uniform(ks[30], (N_ODD, SSM_HEADS), f32,
                                    math.log(0.001), math.log(0.1)))
    return {
        'x': nrm(0, (BATCH, SEQ, D_MODEL), 1.0),
        'c': nrm(1, (BATCH, D_MODEL), 1.0),
        'hyb_norm': gain(2, (N_EVEN, D_MODEL)),
        'hyb_mod_w': nrm(3, (N_EVEN, D_MODEL, 3 * D_MODEL), 0.5 * D_MODEL ** -0.5),
        'hyb_mod_b': nrm(4, (N_EVEN, 3 * D_MODEL), 0.02),
        'hyb_w_in': nrm(5, (N_EVEN, D_MODEL, HYB_IN), D_MODEL ** -0.5),
        'gla_gk_up': nrm(6, (N_EVEN, GLA_LOWRANK, GLA_HEADS * GLA_DK), GLA_LOWRANK ** -0.5),
        'gla_gk_bias': nrm(7, (N_EVEN, GLA_HEADS * GLA_DK), 0.1),
        'gla_out_norm': gain(8, (N_EVEN, GLA_DV)),
        'nsa_cmp_pe': nrm(9, (N_EVEN, 2, CMP_BLOCK, NSA_HEAD_DIM), 0.1),
        'nsa_cmp_w1': nrm(10, (N_EVEN, 2, CMP_BLOCK * NSA_HEAD_DIM, CMP_HIDDEN), (CMP_BLOCK * NSA_HEAD_DIM) ** -0.5),
        'nsa_cmp_w2': nrm(11, (N_EVEN, 2, CMP_HIDDEN, NSA_HEAD_DIM), CMP_HIDDEN ** -0.5),
        'hyb_w_out': nrm(12, (N_EVEN, HYB_MIX, D_MODEL), HYB_MIX ** -0.5),
        'dense_norm': gain(13, (N_EVEN, D_MODEL)),
        'dense_mod_w': nrm(14, (N_EVEN, D_MODEL, 3 * D_MODEL), 0.5 * D_MODEL ** -0.5),
        'dense_mod_b': nrm(15, (N_EVEN, 3 * D_MODEL), 0.02),
        'dense_w_gu': nrm(16, (N_EVEN, D_MODEL, 2 * FFN_DENSE), D_MODEL ** -0.5),
        'dense_w_down': nrm(17, (N_EVEN, FFN_DENSE, D_MODEL), FFN_DENSE ** -0.5),
        'ssm_norm': gain(18, (N_ODD, D_MODEL)),
        'ssm_mod_w': nrm(19, (N_ODD, D_MODEL, 3 * D_MODEL), 0.5 * D_MODEL ** -0.5),
        'ssm_mod_b': nrm(20, (N_ODD, 3 * D_MODEL), 0.02),
        'ssm_w_in': nrm(21, (N_ODD, D_MODEL, SSM_IN), D_MODEL ** -0.5),
        'ssm_conv_w': nrm(22, (N_ODD, CONV_K, CONV_CH), CONV_K ** -0.5),
        'ssm_conv_b': nrm(23, (N_ODD, CONV_CH), 0.02),
        'ssm_dt_bias': dt + jnp.log(-jnp.expm1(-dt)),
        'ssm_a_log': jnp.log(jax.random.uniform(ks[24], (N_ODD, SSM_HEADS), f32, 1.0, 16.0)),
        'ssm_d': gain(25, (N_ODD, SSM_HEADS)),
        'ssm_gate_norm': gain(26, (N_ODD, D_INNER)),
        'ssm_w_out': nrm(27, (N_ODD, D_INNER, D_MODEL), D_INNER ** -0.5),
        'moe_norm': gain(28, (N_ODD, D_MODEL)),
        'moe_mod_w': nrm(29, (N_ODD, D_MODEL, 3 * D_MODEL), 0.5 * D_MODEL ** -0.5),
        'moe_mod_b': nrm(31, (N_ODD, 3 * D_MODEL), 0.02),
        'moe_router': nrm(32, (N_ODD, D_MODEL, N_EXPERTS), D_MODEL ** -0.5),
        'moe_w_gu': nrm(33, (N_ODD, N_EXPERTS, D_MODEL, 2 * FFN_EXPERT), D_MODEL ** -0.5),
        'moe_w_down': nrm(34, (N_ODD, N_EXPERTS, FFN_EXPERT, D_MODEL), FFN_EXPERT ** -0.5),
        'final_norm': gain(35, (D_MODEL,)),
    }


def reference(x, c, hyb_norm, hyb_mod_w, hyb_mod_b, hyb_w_in, gla_gk_up, gla_gk_bias, gla_out_norm,
              nsa_cmp_pe, nsa_cmp_w1, nsa_cmp_w2, hyb_w_out, dense_norm, dense_mod_w, dense_mod_b,
              dense_w_gu, dense_w_down, ssm_norm, ssm_mod_w, ssm_mod_b, ssm_w_in, ssm_conv_w, ssm_conv_b,
              ssm_dt_bias, ssm_a_log, ssm_d, ssm_gate_norm, ssm_w_out, moe_norm, moe_mod_w, moe_mod_b,
              moe_router, moe_w_gu, moe_w_down, final_norm):
    for layer in range(DEPTH):
        i = layer // 2
        if layer % 2 == 0:
            shift, scale, gate = adaln(c, hyb_mod_w[i], hyb_mod_b[i])
            mix = hybrid_mixer(modulate(x, hyb_norm[i], shift, scale), hyb_w_in[i], gla_gk_up[i], gla_gk_bias[i],
                               gla_out_norm[i], nsa_cmp_pe[i], nsa_cmp_w1[i], nsa_cmp_w2[i], hyb_w_out[i])
            x = x + (gate * mix).astype(x.dtype)
            shift, scale, gate = adaln(c, dense_mod_w[i], dense_mod_b[i])
            ffn = swiglu(modulate(x, dense_norm[i], shift, scale), dense_w_gu[i], dense_w_down[i])
            x = x + (gate * ffn).astype(x.dtype)
        else:
            shift, scale, gate = adaln(c, ssm_mod_w[i], ssm_mod_b[i])
            mix = mamba2_mixer(modulate(x, ssm_norm[i], shift, scale), ssm_w_in[i], ssm_conv_w[i], ssm_conv_b[i],
                               ssm_dt_bias[i], ssm_a_log[i], ssm_d[i], ssm_gate_norm[i], ssm_w_out[i])
            x = x + (gate * mix).astype(x.dtype)
            shift, scale, gate = adaln(c, moe_mod_w[i], moe_mod_b[i])
            ffn = moe_ffn(modulate(x, moe_norm[i], shift, scale), moe_router[i], moe_w_gu[i], moe_w_down[i])
            x = x + (gate * ffn).astype(x.dtype)
    return rmsnorm(x, final_norm)
```

```python
import functools

import jax
import jax.numpy as jnp
import numpy as np
from jax import lax
from jax.experimental import pallas as pl
from jax.experimental.pallas import tpu as pltpu

F32 = jnp.float32
BF16 = jnp.bfloat16

D_MODEL = 1024
NORM_EPS = 1e-6
GLA_HEADS = 4
GLA_DV = D_MODEL // 8
GLA_DK = GLA_DV // 2
GLA_LOWRANK = 16
GLA_TAU = 16.0
GLA_CHUNK = 64
NSA_HEADS = 8
NSA_KV_GROUPS = 2
NSA_HEAD_DIM = D_MODEL // 16
CMP_BLOCK = 32
CMP_STRIDE = 16
CMP_HIDDEN = 4 * NSA_HEAD_DIM
SEL_BLOCK = 64
N_SELECT = 8
WINDOW = 512
D_INNER = 2 * D_MODEL
SSM_HEAD_DIM = 64
SSM_HEADS = D_INNER // SSM_HEAD_DIM
SSM_GROUPS = 4
SSM_STATE = 128
CONV_K = 4
SSM_CHUNK = 64
CONV_CH = D_INNER + 2 * SSM_GROUPS * SSM_STATE
FFN_DENSE = ((8 * D_MODEL // 3 + 127) // 128) * 128
N_EXPERTS = 8
FFN_EXPERT = 7 * D_MODEL // 2
HYB_SPLITS = (GLA_HEADS * GLA_DK, GLA_HEADS * GLA_DK, GLA_HEADS * GLA_DV, GLA_LOWRANK, GLA_HEADS * GLA_DV,
              NSA_HEADS * NSA_HEAD_DIM, 6 * NSA_KV_GROUPS * NSA_HEAD_DIM, 3 * NSA_HEADS)
SSM_SPLITS = (D_INNER, CONV_CH, SSM_HEADS)

LANES = 128
VMEM_LIMIT = 56 * 1024 * 1024

NEG_BIG = -1e30


def _cparams(sem):
    return pltpu.CompilerParams(dimension_semantics=sem, vmem_limit_bytes=VMEM_LIMIT)


def _dot(a, b):
    return jnp.dot(a, b, preferred_element_type=F32)


def _dot_nt(a, b):
    return lax.dot_general(a, b, (((1,), (1,)), ((), ())), preferred_element_type=F32)


def _dot_tn(a, b):
    return lax.dot_general(a, b, (((0,), (0,)), ((), ())), preferred_element_type=F32)


def _split_bf16(a, parts):
    out = []
    r = a
    for _ in range(parts):
        p = r.astype(BF16)
        out.append(p)
        r = r - p.astype(F32)
    return out


def _dot_exact_lhs(m, a, parts=3):
    acc = None
    for p in _split_bf16(a, parts):
        t = _dot(m, p)
        acc = t if acc is None else acc + t
    return acc


def _dot_exact_rhs(a, m, parts=3):
    acc = None
    for p in _split_bf16(a, parts):
        t = _dot(p, m)
        acc = t if acc is None else acc + t
    return acc


def _sigmoid(x):
    return 1.0 / (1.0 + jnp.exp(-x))


def _silu(x):
    return x * _sigmoid(x)


def _softplus(x):
    return jnp.maximum(x, 0.0) + jnp.log(1.0 + jnp.exp(-jnp.abs(x)))


def _gelu_tanh(x):
    return x * (0.5 * (1.0 + jnp.tanh(0.7978845608028654 * (x + 0.044715 * (x * x * x)))))


def _modulated_norm(x, nw, shift, scale):
    ms = jnp.mean(x * x, axis=-1, keepdims=True)
    y = x * lax.rsqrt(ms + NORM_EPS)
    return (y * nw) * (1.0 + scale) + shift


def _const_spec(shape):
    nd = len(shape)
    return pl.BlockSpec(shape, lambda *_: (0,) * nd, pipeline_mode=pl.Buffered(1))


def _adaln_kernel(c_ref, w0, w1, w2, w3, b0, b1, b2, b3, o_ref):
    sc = _silu(c_ref[...]).astype(BF16)
    for i, (w, b) in enumerate(((w0, b0), (w1, b1), (w2, b2), (w3, b3))):
        o_ref[i] = _dot(sc, w[...].astype(BF16)) + b[...]


def _adaln(c, ws, bs):
    bsz, d = c.shape
    n = ws[0].shape[1]
    tn = 512
    w_spec = pl.BlockSpec((d, tn), lambda j: (0, j))
    b_spec = pl.BlockSpec((1, tn), lambda j: (0, j))
    return pl.pallas_call(
        _adaln_kernel,
        grid=(n // tn,),
        in_specs=[pl.BlockSpec((bsz, d), lambda j: (0, 0))] + [w_spec] * 4 + [b_spec] * 4,
        out_specs=pl.BlockSpec((4, bsz, tn), lambda j: (0, 0, j)),
        out_shape=jax.ShapeDtypeStruct((4, bsz, n), F32),
        compiler_params=_cparams(("arbitrary",)),
        name="adaln",
    )(c, *ws, *[b[None, :] for b in bs])


def _mod_specs(tile_axes):
    def spec(part):
        if tile_axes == 2:
            return pl.BlockSpec((1, 1, D_MODEL), lambda b, i: (b, 0, part))
        return pl.BlockSpec((1, 1, D_MODEL), lambda b, i, j, k: (b, 0, part))
    return spec(0), spec(1), spec(2)


HYB_SEG = {
    "gq": (0, 256), "gk": (256, 256), "gv": (512, 512), "gr": (1024, 512), "lr": (1536, 128),
    "nq": (1664, 512), "kv": (2176, 768), "ng": (2944, 128),
}
HYB_COLS = 3072


def _hyb_in_kernel(x_ref, nw_ref, shift_ref, scale_ref, w_ref, gkup_ref, gkb_ref,
                   gq_ref, gk_ref, la_ref, gv_ref, gr_ref, nq_ref,
                   kcmp_ref, vcmp_ref, kslc_ref, vslc_ref, kwin_ref, vwin_ref, ng_ref):
    h = _modulated_norm(x_ref[0], nw_ref[...], shift_ref[0], scale_ref[0]).astype(BF16)

    def proj(name, sub=None):
        off, width = HYB_SEG[name]
        if sub is not None:
            off, width = off + sub * LANES, LANES
        return _dot(h, w_ref[:, off:off + width])

    gq_ref[0] = proj("gq")
    gk_ref[0] = proj("gk")
    gv_ref[0] = proj("gv").astype(BF16)
    gr_ref[0] = proj("gr")
    lr = proj("lr").astype(BF16)
    z = _dot(lr, gkup_ref[...]) + gkb_ref[...]
    la_ref[0] = -_softplus(-z) * (1.0 / GLA_TAU)
    nq_ref[0] = proj("nq").astype(BF16)
    kcmp_ref[0] = proj("kv", 0)
    vcmp_ref[0] = proj("kv", 1)
    kslc_ref[0] = proj("kv", 2).astype(BF16)
    vslc_ref[0] = proj("kv", 3).astype(BF16)
    kwin_ref[0] = proj("kv", 4).astype(BF16)
    vwin_ref[0] = proj("kv", 5).astype(BF16)
    ng_ref[0] = proj("ng")


def _hyb_in(x, norm_w, mod, w_perm, gkup_pad, gk_bias):
    bsz, seq, d = x.shape
    tm = 512
    shift_s, scale_s, _ = _mod_specs(2)

    def out(width, dtype):
        return (pl.BlockSpec((1, tm, width), lambda b, i: (b, i, 0)),
                jax.ShapeDtypeStruct((bsz, seq, width), dtype))

    outs = [out(256, F32), out(256, F32), out(256, F32), out(512, BF16), out(512, F32), out(512, BF16),
            out(128, F32), out(128, F32), out(128, BF16), out(128, BF16), out(128, BF16), out(128, BF16),
            out(128, F32)]
    return pl.pallas_call(
        _hyb_in_kernel,
        grid=(bsz, seq // tm),
        in_specs=[pl.BlockSpec((1, tm, d), lambda b, i: (b, i, 0)),
                  _const_spec((1, d)), shift_s, scale_s,
                  _const_spec(w_perm.shape), _const_spec(gkup_pad.shape), _const_spec((1, 256))],
        out_specs=[o[0] for o in outs],
        out_shape=[o[1] for o in outs],
        compiler_params=_cparams(("parallel", "parallel")),
        name="hyb_in",
    )(x, norm_w[None, :], mod, mod, w_perm, gkup_pad, gk_bias[None, :])


GLA_ROWS = 512


def _gla_kernel(q_ref, k_ref, la_ref, v_ref, r_ref, nw_ref, o_ref, *, seq):
    rb = GLA_ROWS
    ncb = rb // GLA_CHUNK
    row = lax.broadcasted_iota(jnp.int32, (rb, rb), 0)
    col = lax.broadcasted_iota(jnp.int32, (rb, rb), 1)
    chunk_causal = ((row // GLA_CHUNK) == (col // GLA_CHUNK)) & (col <= row)
    tril_bd = jnp.where(chunk_causal, 1.0, 0.0).astype(BF16)
    lane = lax.broadcasted_iota(jnp.int32, (rb, LANES), 1)
    head_mask = (lane < GLA_DK, lane >= GLA_DK)
    nw = nw_ref[...]

    def body(i, st):
        r0 = pl.multiple_of(i * rb, rb)
        rows = pl.ds(r0, rb)
        q = q_ref[0, rows, :]
        k = k_ref[0, rows, :]
        la = la_ref[0, rows, :]
        b = _dot_exact_lhs(tril_bd, la)
        b3 = b.reshape(ncb, GLA_CHUNK, LANES)
        bend3 = b3[:, GLA_CHUNK - 1:GLA_CHUNK, :]
        bend = jnp.broadcast_to(bend3, (ncb, GLA_CHUNK, LANES)).reshape(rb, LANES)
        q_dec = (q * GLA_DK ** -0.5) * jnp.exp(b)
        k_dec = (k * jnp.exp(-b)).astype(BF16)
        k_end = k * jnp.exp(bend - b)
        decay = jnp.exp(bend3)

        qm, vh, o_intra, upd = [], [], [], []
        for h in range(2):
            qm_h = jnp.where(head_mask[h], q_dec, 0.0).astype(BF16)
            att = _dot_nt(qm_h, k_dec)
            att = jnp.where(chunk_causal, att, 0.0).astype(BF16)
            v_h = v_ref[0, rows, h * GLA_DV:(h + 1) * GLA_DV]
            o_intra.append(_dot(att, v_h))
            km_h = jnp.where(head_mask[h], k_end, 0.0).astype(BF16)
            upd.append([_dot_tn(v_h[n * GLA_CHUNK:(n + 1) * GLA_CHUNK], km_h[n * GLA_CHUNK:(n + 1) * GLA_CHUNK])
                        for n in range(ncb)])
            qm.append(qm_h)
            vh.append(v_h)

        prev = []
        for n in range(ncb):
            prev.append(st.astype(BF16))
            st = st * decay[n] + (upd[0][n] + upd[1][n])

        for h in range(2):
            o_inter = jnp.concatenate(
                [_dot_nt(qm[h][n * GLA_CHUNK:(n + 1) * GLA_CHUNK], prev[n]) for n in range(ncb)], axis=0)
            o = o_intra[h] + o_inter
            y = o * lax.rsqrt(jnp.mean(o * o, axis=-1, keepdims=True) + NORM_EPS) * nw
            r = r_ref[0, rows, h * GLA_DV:(h + 1) * GLA_DV]
            o_ref[0, rows, h * GLA_DV:(h + 1) * GLA_DV] = (y * _silu(r)).astype(o_ref.dtype)
        return st

    lax.fori_loop(0, seq // rb, body, jnp.zeros((GLA_DV, 2 * GLA_DK), F32))


def _gla(gq, gk, la, gv, gr, out_norm):
    bsz, seq, _ = gq.shape
    qk_spec = pl.BlockSpec((1, seq, 2 * GLA_DK), lambda b, p: (b, 0, p))
    v_spec = pl.BlockSpec((1, seq, 2 * GLA_DV), lambda b, p: (b, 0, p))
    return pl.pallas_call(
        functools.partial(_gla_kernel, seq=seq),
        grid=(bsz, GLA_HEADS // 2),
        in_specs=[qk_spec, qk_spec, qk_spec, v_spec, v_spec, _const_spec((1, GLA_DV))],
        out_specs=v_spec,
        out_shape=jax.ShapeDtypeStruct((bsz, seq, GLA_HEADS * GLA_DV), BF16),
        compiler_params=_cparams(("parallel", "parallel")),
        name="gla",
    )(gq, gk, la, gv, gr, out_norm[None, :])


def _nsa_compress_kernel(k_ref, v_ref, pe_ref, w1_ref, w2_ref, kc_ref, vc_ref, *, n_cmp):
    rows = k_ref.shape[1]
    rid = lax.broadcasted_iota(jnp.int32, (rows, LANES), 0)
    for j, (src, dst) in enumerate(((k_ref, kc_ref), (v_ref, vc_ref))):
        x = src[0]
        lo = _dot((x + pe_ref[j, 0]).astype(BF16), w1_ref[j, 0])
        hi = _dot((x + pe_ref[j, 1]).astype(BF16), w1_ref[j, 1])
        hpre = lo + pltpu.roll(hi, rows - 1, 0)
        out = _dot(_gelu_tanh(hpre).astype(BF16), w2_ref[j])
        dst[0] = jnp.where(rid < n_cmp, out, 0.0).astype(dst.dtype)


def _nsa_compress(kcmp, vcmp, pe_rows, w1x, w2x):
    bsz, seq, _ = kcmp.shape
    rows = seq // CMP_STRIDE
    width = CMP_STRIDE * LANES
    n_cmp = (seq - CMP_BLOCK) // CMP_STRIDE + 1
    x_spec = pl.BlockSpec((1, rows, width), lambda b: (b, 0, 0))
    o_spec = pl.BlockSpec((1, rows, LANES), lambda b: (b, 0, 0))
    o_shape = jax.ShapeDtypeStruct((bsz, rows, LANES), BF16)
    return pl.pallas_call(
        functools.partial(_nsa_compress_kernel, n_cmp=n_cmp),
        grid=(bsz,),
        in_specs=[x_spec, x_spec, _const_spec(pe_rows.shape), _const_spec(w1x.shape), _const_spec(w2x.shape)],
        out_specs=[o_spec, o_spec],
        out_shape=[o_shape, o_shape],
        compiler_params=_cparams(("parallel",)),
        name="nsa_compress",
    )(kcmp.reshape(bsz, rows, width), vcmp.reshape(bsz, rows, width), pe_rows, w1x, w2x)


NSA_QT = 128
NSA_KT = 512
NSA_ROWS = NSA_HEADS * NSA_QT


def _nsa_attn_kernel(q_ref, g_ref, kc_ref, vc_ref, ks_ref, vs_ref, kw_ref, vw_ref, ov_ref, o_ref,
                     m_ref, l_ref, acc_ref, *, seq):
    qt, kt = NSA_QT, NSA_KT
    heads_per_group = NSA_HEADS // NSA_KV_GROUPS
    q0 = pl.program_id(1) * qt
    t = q0 + lax.broadcasted_iota(jnp.int32, (qt, 1), 0)
    lane = lax.broadcasted_iota(jnp.int32, (qt, LANES), 1)
    group_lanes = (lane < NSA_HEAD_DIM, lane >= NSA_HEAD_DIM)
    slopes = [2.0 ** (-(h + 1)) for h in range(NSA_HEADS)]

    qa = q_ref[0].astype(F32) * NSA_HEAD_DIM ** -0.5
    q_rows = []
    for h in range(NSA_HEADS):
        g = h // heads_per_group
        blk = qa[:, (h // 2) * LANES:(h // 2 + 1) * LANES]
        if h % 2 != g:
            blk = pltpu.roll(blk, NSA_HEAD_DIM, 1)
        q_rows.append(jnp.where(group_lanes[g], blk, 0.0).astype(BF16))
    q_ext = jnp.concatenate(q_rows, axis=0)

    def softmax_parts(s, mask):
        s = jnp.where(mask, s, -jnp.inf)
        m = jnp.max(s, axis=-1, keepdims=True)
        m = jnp.where(m == -jnp.inf, 0.0, m)
        p = jnp.exp(s - m)
        return p, jnp.sum(p, axis=-1, keepdims=True)

    n_rows = kc_ref.shape[1]
    cidx = lax.broadcasted_iota(jnp.int32, (1, n_rows), 1)
    cmp_end = cidx * CMP_STRIDE + (CMP_BLOCK - 1)
    cmp_center = cidx.astype(F32) * CMP_STRIDE + (CMP_BLOCK - 1) / 2.0
    cmp_mask = cmp_end <= t
    cmp_dist = t.astype(F32) - cmp_center
    s_all = _dot_nt(q_ext, kc_ref[0])
    p_list, psum = [], [None] * NSA_KV_GROUPS
    for h in range(NSA_HEADS):
        g = h // heads_per_group
        p, l = softmax_parts(s_all[h * qt:(h + 1) * qt] - slopes[h] * cmp_dist, cmp_mask)
        p = p / jnp.maximum(l, 1e-20)
        p_list.append(p.astype(BF16))
        psum[g] = p if psum[g] is None else psum[g] + p
    o_cmp = _dot(jnp.concatenate(p_list, axis=0), vc_ref[0])

    n_sel = seq // SEL_BLOCK
    blk_t = t // SEL_BLOCK
    forced = (lane == 0) | (lane == blk_t) | (lane == blk_t - 1)
    future = (lane * SEL_BLOCK > t) | (lane >= n_sel)
    sel = []
    for g in range(NSA_KV_GROUPS):
        imp = _dot_exact_rhs(psum[g], ov_ref[...])
        v = jnp.where(future, -1.0, jnp.where(forced, 3e38, imp))
        chosen = jnp.zeros((qt, LANES), F32)
        for _ in range(min(N_SELECT, n_sel)):
            m = jnp.max(v, axis=-1, keepdims=True)
            idx = jnp.min(jnp.where(v == m, lane, LANES), axis=-1, keepdims=True)
            pick = (lane == idx) & (m >= 0.0)
            chosen = jnp.where(pick, 1.0, chosen)
            v = jnp.where(pick, -1.0, v)
        sel.append(chosen.astype(BF16))

    m_ref[...] = jnp.full(m_ref.shape, -jnp.inf, F32)
    l_ref[...] = jnp.zeros(l_ref.shape, F32)
    acc_ref[...] = jnp.zeros(acc_ref.shape, F32)
    kpos = lax.broadcasted_iota(jnp.int32, (1, kt), 1)
    erow = lax.broadcasted_iota(jnp.int32, (LANES, kt), 0)
    ecol = lax.broadcasted_iota(jnp.int32, (LANES, kt), 1) // SEL_BLOCK

    def sel_tile(j, carry):
        k0 = pl.multiple_of(j * kt, kt)
        ks = ks_ref[0, pl.ds(k0, kt), :]
        vs = vs_ref[0, pl.ds(k0, kt), :]
        s_t = _dot_nt(q_ext, ks)
        expand = jnp.where(erow == ecol + j * (kt // SEL_BLOCK), 1.0, 0.0).astype(BF16)
        dist = t - (k0 + kpos)
        dist_f = dist.astype(F32)
        p_rows = []
        for g in range(NSA_KV_GROUPS):
            mask = (_dot(sel[g], expand) > 0.5) & (dist >= 0)
            for r in range(heads_per_group):
                h = g * heads_per_group + r
                rows = slice(h * qt, (h + 1) * qt)
                s = jnp.where(mask, s_t[rows] - slopes[h] * dist_f, -jnp.inf)
                m_old = m_ref[rows]
                m_new = jnp.maximum(m_old, jnp.max(s, axis=-1, keepdims=True))
                m_safe = jnp.where(m_new == -jnp.inf, 0.0, m_new)
                p = jnp.exp(s - m_safe)
                alpha = jnp.exp(m_old - m_safe)
                l_ref[rows] = alpha * l_ref[rows] + jnp.sum(p, axis=-1, keepdims=True)
                acc_ref[rows] = alpha * acc_ref[rows]
                m_ref[rows] = m_new
                p_rows.append(p.astype(BF16))
        acc_ref[...] += _dot(jnp.concatenate(p_rows, axis=0), vs)
        return carry

    lax.fori_loop(0, (q0 + qt + kt - 1) // kt, sel_tile, 0)
    o_slc = acc_ref[...] / jnp.maximum(l_ref[...], 1e-20)

    wk = WINDOW + qt
    w0 = pl.multiple_of(jnp.maximum(q0 - WINDOW, 0), qt)
    kw = kw_ref[0, pl.ds(w0, wk), :]
    vw = vw_ref[0, pl.ds(w0, wk), :]
    wdist = t - (w0 + lax.broadcasted_iota(jnp.int32, (1, wk), 1))
    wmask = (wdist >= 0) & (wdist < WINDOW)
    wdist_f = wdist.astype(F32)
    s_all = _dot_nt(q_ext, kw)
    p_list, l_list = [], []
    for h in range(NSA_HEADS):
        p, l = softmax_parts(s_all[h * qt:(h + 1) * qt] - slopes[h] * wdist_f, wmask)
        p_list.append(p.astype(BF16))
        l_list.append(l)
    o_win = _dot(jnp.concatenate(p_list, axis=0), vw) / jnp.maximum(jnp.concatenate(l_list, axis=0), 1e-20)

    gates = _sigmoid(g_ref[0])
    o_heads = []
    for h in range(NSA_HEADS):
        rows = slice(h * qt, (h + 1) * qt)
        o_heads.append(gates[:, 3 * h:3 * h + 1] * o_cmp[rows] + gates[:, 3 * h + 1:3 * h + 2] * o_slc[rows]
                       + gates[:, 3 * h + 2:3 * h + 3] * o_win[rows])
    for c in range(NSA_HEADS // 2):
        g = (2 * c) // heads_per_group
        a, b = o_heads[2 * c], o_heads[2 * c + 1]
        if g == 0:
            blk = jnp.where(group_lanes[0], a, pltpu.roll(b, NSA_HEAD_DIM, 1))
        else:
            blk = jnp.where(group_lanes[0], pltpu.roll(a, NSA_HEAD_DIM, 1), b)
        o_ref[0, :, c * LANES:(c + 1) * LANES] = blk.astype(o_ref.dtype)


def _nsa_attn(nq, ng, kc, vc, kslc, vslc, kwin, vwin, overlap):
    bsz, seq, _ = nq.shape
    n_rows = kc.shape[1]
    full = lambda rows: pl.BlockSpec((1, rows, LANES), lambda b, i: (b, 0, 0))
    return pl.pallas_call(
        functools.partial(_nsa_attn_kernel, seq=seq),
        grid=(bsz, seq // NSA_QT),
        in_specs=[pl.BlockSpec((1, NSA_QT, NSA_HEADS * NSA_HEAD_DIM), lambda b, i: (b, i, 0)),
                  pl.BlockSpec((1, NSA_QT, LANES), lambda b, i: (b, i, 0)),
                  full(n_rows), full(n_rows), full(seq), full(seq), full(seq), full(seq),
                  _const_spec(overlap.shape)],
        out_specs=pl.BlockSpec((1, NSA_QT, NSA_HEADS * NSA_HEAD_DIM), lambda b, i: (b, i, 0)),
        out_shape=jax.ShapeDtypeStruct((bsz, seq, NSA_HEADS * NSA_HEAD_DIM), BF16),
        scratch_shapes=[pltpu.VMEM((NSA_ROWS, 1), F32), pltpu.VMEM((NSA_ROWS, 1), F32),
                        pltpu.VMEM((NSA_ROWS, LANES), F32)],
        compiler_params=_cparams(("parallel", "arbitrary")),
        name="nsa_attn",
    )(nq, ng, kc, vc, kslc, vslc, kwin, vwin, overlap)


FFN_CHUNK = 1408


def _hyb_out_ffn_kernel(x_ref, oa_ref, ob_ref, g1_ref, wa_ref, wb_ref,
                        nw_ref, shift_ref, scale_ref, g2_ref, wg_ref, wu_ref, wd_ref, o_ref, act_ref):
    mix = _dot(oa_ref[0], wa_ref[...]) + _dot(ob_ref[0], wb_ref[...])
    x1 = x_ref[0] + g1_ref[0] * mix
    h = _modulated_norm(x1, nw_ref[...], shift_ref[0], scale_ref[0]).astype(BF16)
    for f in range(0, FFN_DENSE, FFN_CHUNK):
        gate = _dot(h, wg_ref[:, f:f + FFN_CHUNK])
        up = _dot(h, wu_ref[:, f:f + FFN_CHUNK])
        act_ref[:, f:f + FFN_CHUNK] = (_silu(gate) * up).astype(BF16)
    o_ref[0] = x1 + g2_ref[0] * _dot(act_ref[...], wd_ref[...])


def _hyb_out_ffn(x, o_gla, o_nsa, mod1, wa, wb, norm_w, mod2, wg, wu, wd):
    bsz, seq, d = x.shape
    tm = 512
    _, _, gate1 = _mod_specs(2)
    shift2, scale2, gate2 = _mod_specs(2)
    tile = lambda width: pl.BlockSpec((1, tm, width), lambda b, i: (b, i, 0))
    return pl.pallas_call(
        _hyb_out_ffn_kernel,
        grid=(bsz, seq // tm),
        in_specs=[tile(d), tile(o_gla.shape[-1]), tile(o_nsa.shape[-1]), gate1,
                  _const_spec(wa.shape), _const_spec(wb.shape), _const_spec((1, d)), shift2, scale2, gate2,
                  _const_spec(wg.shape), _const_spec(wu.shape), _const_spec(wd.shape)],
        out_specs=tile(d),
        out_shape=jax.ShapeDtypeStruct((bsz, seq, d), F32),
        scratch_shapes=[pltpu.VMEM((tm, FFN_DENSE), BF16)],
        compiler_params=_cparams(("parallel", "parallel")),
        name="hyb_out_ffn",
    )(x, o_gla, o_nsa, mod1, wa, wb, norm_w[None, :], mod2, mod2, mod2, wg, wu, wd)


SSM_DT_OFF = D_INNER + CONV_CH
SSM_COLS = SSM_DT_OFF + LANES
SSM_IN_CHUNK = 1024


def _ssm_in_kernel(x_ref, nw_ref, shift_ref, scale_ref, w_ref, z_ref, xbc_ref, dt_ref):
    h = _modulated_norm(x_ref[0], nw_ref[...], shift_ref[0], scale_ref[0]).astype(BF16)
    for c in range(0, D_INNER, SSM_IN_CHUNK):
        z_ref[0, :, c:c + SSM_IN_CHUNK] = _dot(h, w_ref[:, c:c + SSM_IN_CHUNK])
    for c in range(0, CONV_CH, SSM_IN_CHUNK):
        xbc_ref[0, :, c:c + SSM_IN_CHUNK] = _dot(h, w_ref[:, D_INNER + c:D_INNER + c + SSM_IN_CHUNK])
    dt_ref[0] = _dot(h, w_ref[:, SSM_DT_OFF:SSM_COLS])


def _ssm_in(x, norm_w, mod, w_perm):
    bsz, seq, d = x.shape
    tm = 512
    shift_s, scale_s, _ = _mod_specs(2)
    tile = lambda width: pl.BlockSpec((1, tm, width), lambda b, i: (b, i, 0))
    return pl.pallas_call(
        _ssm_in_kernel,
        grid=(bsz, seq // tm),
        in_specs=[tile(d), _const_spec((1, d)), shift_s, scale_s, _const_spec(w_perm.shape)],
        out_specs=[tile(D_INNER), tile(CONV_CH), tile(LANES)],
        out_shape=[jax.ShapeDtypeStruct((bsz, seq, D_INNER), F32),
                   jax.ShapeDtypeStruct((bsz, seq, CONV_CH), F32),
                   jax.ShapeDtypeStruct((bsz, seq, LANES), F32)],
        compiler_params=_cparams(("parallel", "parallel")),
        name="ssm_in",
    )(x, norm_w[None, :], mod, mod, w_perm)


SSD_ROWS = 512
SSD_GW = D_INNER // SSM_GROUPS
SSD_HPG = SSM_HEADS // SSM_GROUPS
CONV_TAIL = 8


def _ssd_kernel(xbc_ref, z_ref, dt_ref, cw_ref, cb_ref, dtb_ref, alog_ref, dskip_ref, nw_ref, ex_ref, o_ref,
                tail_ref, state_ref, xg_ref, xdt_ref, cum_ref, b_ref, c_ref, y_ref):
    rb, q = SSD_ROWS, SSM_CHUNK
    nchunk = rb // q

    @pl.when(pl.program_id(1) == 0)
    def _():
        tail_ref[...] = jnp.zeros(tail_ref.shape, F32)
        state_ref[...] = jnp.zeros(state_ref.shape, F32)

    row8 = lax.broadcasted_iota(jnp.int32, (CONV_TAIL, 1), 0)

    def conv_silu(c0, width):
        x = xbc_ref[0, :, c0:c0 + width]
        tail = tail_ref[:, c0:c0 + width]
        acc = x * cw_ref[CONV_K - 1:CONV_K, c0:c0 + width] + cb_ref[:, c0:c0 + width]
        for j in range(1, CONV_K):
            xs = pltpu.roll(x, j, 0)
            head = jnp.where(row8 < j, pltpu.roll(tail, j, 0), xs[:CONV_TAIL])
            xs = jnp.concatenate([head, xs[CONV_TAIL:]], axis=0)
            acc = acc + xs * cw_ref[CONV_K - 1 - j:CONV_K - j, c0:c0 + width]
        return _silu(acc)

    dt = _softplus(dt_ref[0] + dtb_ref[...])
    a = dt * (-jnp.exp(alog_ref[...]))
    row = lax.broadcasted_iota(jnp.int32, (rb, rb), 0)
    col = lax.broadcasted_iota(jnp.int32, (rb, rb), 1)
    tril_bd = jnp.where(((row // q) == (col // q)) & (col <= row), 1.0, 0.0).astype(BF16)
    cum = _dot_exact_lhs(tril_bd, a)

    lrow = lax.broadcasted_iota(jnp.int32, (q, SSD_GW), 0)
    lcol = lax.broadcasted_iota(jnp.int32, (q, SSD_GW), 1) % q
    causal_t = lcol <= lrow
    eye_t = lcol == lrow
    half = SSD_GW // 2
    brow = lax.broadcasted_iota(jnp.int32, (half, half), 0) // q
    bcol = lax.broadcasted_iota(jnp.int32, (half, half), 1) // SSM_HEAD_DIM
    same_head = brow == bcol

    for g in range(SSM_GROUPS):
        xg = conv_silu(g * SSD_GW, SSD_GW)
        xg_ref[...] = xg
        b_ref[...] = conv_silu(D_INNER + g * SSM_STATE, SSM_STATE).astype(BF16)
        c_ref[...] = conv_silu(D_INNER + SSM_GROUPS * SSM_STATE + g * SSM_STATE, SSM_STATE).astype(BF16)
        xdt_ref[...] = xg * _dot_exact_rhs(dt, ex_ref[g])
        cum_ref[...] = _dot_exact_rhs(cum, ex_ref[g])

        def chunk(n, carry):
            rows = pl.ds(pl.multiple_of(n * q, q), q)
            cum_c = cum_ref[rows, :]
            cum_s = jnp.sum(jnp.where(eye_t, cum_c, 0.0), axis=0, keepdims=True)
            decay_l = jnp.where(causal_t, jnp.exp(cum_c - cum_s), 0.0)
            bc = b_ref[rows, :]
            cc = c_ref[rows, :]
            cb_t = _dot_nt(cc, jnp.concatenate([bc] * SSD_HPG, axis=0))
            mat = (cb_t * decay_l).astype(BF16)
            xdt_c = xdt_ref[rows, :]
            xdt_b = xdt_c.astype(BF16)
            y_diag = []
            for s in range(2):
                blk = xdt_b[:, s * half:(s + 1) * half]
                bd = jnp.where(same_head, jnp.concatenate([blk] * (half // q), axis=0), 0.0).astype(BF16)
                y_diag.append(_dot(mat[:, s * half:(s + 1) * half], bd))
            y = jnp.concatenate(y_diag, axis=1)
            cum_end = cum_c[q - 1:q, :]
            st = state_ref[g]
            y = y + _dot(cc, st.astype(BF16)) * jnp.exp(cum_c)
            x_end = (xdt_c * jnp.exp(cum_end - cum_c)).astype(BF16)
            state_ref[g] = st * jnp.exp(cum_end) + _dot_tn(bc, x_end)
            y_ref[rows, :] = y
            return carry

        lax.fori_loop(0, nchunk, chunk, 0)
        cols = slice(g * SSD_GW, (g + 1) * SSD_GW)
        y = y_ref[...] + dskip_ref[:, cols] * xg_ref[...]
        y = y * _silu(z_ref[0, :, cols])
        y = y * lax.rsqrt(jnp.mean(y * y, axis=-1, keepdims=True) + NORM_EPS) * nw_ref[:, cols]
        o_ref[0, :, cols] = y.astype(o_ref.dtype)

    tail_ref[...] = xbc_ref[0, rb - CONV_TAIL:rb, :]


def _ssd(xbc, z, dt, conv_w, conv_b, dt_bias_pad, a_log_pad, d_skip_x, norm_w, expand):
    bsz, seq, _ = xbc.shape
    rb = SSD_ROWS
    tile = lambda width: pl.BlockSpec((1, rb, width), lambda b, i: (b, i, 0))
    return pl.pallas_call(
        _ssd_kernel,
        grid=(bsz, seq // rb),
        in_specs=[tile(CONV_CH), tile(D_INNER), tile(LANES),
                  _const_spec((CONV_K, CONV_CH)), _const_spec((1, CONV_CH)),
                  _const_spec((1, LANES)), _const_spec((1, LANES)),
                  _const_spec((1, D_INNER)), _const_spec((1, D_INNER)), _const_spec(expand.shape)],
        out_specs=tile(D_INNER),
        out_shape=jax.ShapeDtypeStruct((bsz, seq, D_INNER), BF16),
        scratch_shapes=[pltpu.VMEM((CONV_TAIL, CONV_CH), F32),
                        pltpu.VMEM((SSM_GROUPS, SSM_STATE, SSD_GW), F32),
                        pltpu.VMEM((rb, SSD_GW), F32), pltpu.VMEM((rb, SSD_GW), F32),
                        pltpu.VMEM((rb, SSD_GW), F32),
                        pltpu.VMEM((rb, SSM_STATE), BF16), pltpu.VMEM((rb, SSM_STATE), BF16),
                        pltpu.VMEM((rb, SSD_GW), F32)],
        compiler_params=_cparams(("parallel", "arbitrary")),
        name="ssd",
    )(xbc, z, dt, conv_w, conv_b[None, :], dt_bias_pad, a_log_pad, d_skip_x, norm_w[None, :], expand)


def _ssm_out_route_kernel(x_ref, y_ref, g1_ref, w_ref, nw_ref, shift_ref, scale_ref, r_ref,
                          x3_ref, h_ref, gw_ref):
    x3 = x_ref[0] + g1_ref[0] * _dot(y_ref[0], w_ref[...])
    x3_ref[0] = x3
    h = _modulated_norm(x3, nw_ref[...], shift_ref[0], scale_ref[0])
    h_ref[0] = h.astype(BF16)
    h_hi, h_lo = _split_bf16(h, 2)
    r_hi, r_lo = _split_bf16(r_ref[...], 2)
    logits = _dot(h_hi, r_hi) + (_dot(h_hi, r_lo) + _dot(h_lo, r_hi))
    lane = lax.broadcasted_iota(jnp.int32, logits.shape, 1)
    logits = jnp.where(lane < N_EXPERTS, logits, -jnp.inf)
    m1 = jnp.max(logits, axis=-1, keepdims=True)
    i1 = jnp.min(jnp.where(logits == m1, lane, LANES), axis=-1, keepdims=True)
    rest = jnp.where(lane == i1, -jnp.inf, logits)
    m2 = jnp.max(rest, axis=-1, keepdims=True)
    i2 = jnp.min(jnp.where(rest == m2, lane, LANES), axis=-1, keepdims=True)
    e2 = jnp.exp(m2 - m1)
    w1 = 1.0 / (1.0 + e2)
    w2 = e2 / (1.0 + e2)
    gw_ref[0] = jnp.where(lane == i1, w1, 0.0) + jnp.where(lane == i2, w2, 0.0)


def _ssm_out_route(x, y, mod1, w_out, norm_w, mod2, router_pad):
    bsz, seq, d = x.shape
    tm = 512
    _, _, gate1 = _mod_specs(2)
    shift2, scale2, _ = _mod_specs(2)
    tile = lambda width: pl.BlockSpec((1, tm, width), lambda b, i: (b, i, 0))
    return pl.pallas_call(
        _ssm_out_route_kernel,
        grid=(bsz, seq // tm),
        in_specs=[tile(d), tile(D_INNER), gate1, _const_spec(w_out.shape), _const_spec((1, d)), shift2, scale2,
                  _const_spec(router_pad.shape)],
        out_specs=[tile(d), tile(d), tile(LANES)],
        out_shape=[jax.ShapeDtypeStruct((bsz, seq, d), F32), jax.ShapeDtypeStruct((bsz, seq, d), BF16),
                   jax.ShapeDtypeStruct((bsz, seq, LANES), F32)],
        compiler_params=_cparams(("parallel", "parallel")),
        name="ssm_out_route",
    )(x, y, mod1, w_out, norm_w[None, :], mod2, mod2, router_pad)


MOE_TM = 1024
MOE_TF = 896


def _moe_kernel(x_ref, h_ref, gw_ref, g_ref, wg_ref, wu_ref, wd_ref, fn_ref, o_ref, acc_ref):
    e = pl.program_id(2)
    f = pl.program_id(3)

    @pl.when((e == 0) & (f == 0))
    def _():
        acc_ref[...] = jnp.zeros(acc_ref.shape, F32)

    h = h_ref[0]
    act = (_silu(_dot(h, wg_ref[0])) * _dot(h, wu_ref[0])).astype(BF16)
    lane = lax.broadcasted_iota(jnp.int32, gw_ref.shape[1:], 1)
    w_e = jnp.sum(jnp.where(lane == e, gw_ref[0], 0.0), axis=-1, keepdims=True)
    acc_ref[...] += w_e * _dot(act, wd_ref[0])

    @pl.when((e == pl.num_programs(2) - 1) & (f == pl.num_programs(3) - 1))
    def _():
        x4 = x_ref[0] + g_ref[0] * acc_ref[...]
        y = x4 * lax.rsqrt(jnp.mean(x4 * x4, axis=-1, keepdims=True) + NORM_EPS)
        o_ref[0] = y * fn_ref[...]


def _moe(x, h, gw, mod, wg, wu, wd, final_norm):
    bsz, seq, d = x.shape
    tm, tf = MOE_TM, MOE_TF
    _, _, gate = _mod_specs(4)
    tile = lambda width: pl.BlockSpec((1, tm, width), lambda b, i, e, f: (b, i, 0))
    return pl.pallas_call(
        _moe_kernel,
        grid=(bsz, seq // tm, N_EXPERTS, FFN_EXPERT // tf),
        in_specs=[tile(d), tile(d), tile(LANES), gate,
                  pl.BlockSpec((1, d, tf), lambda b, i, e, f: (e, 0, f)),
                  pl.BlockSpec((1, d, tf), lambda b, i, e, f: (e, 0, f)),
                  pl.BlockSpec((1, tf, d), lambda b, i, e, f: (e, f, 0)),
                  _const_spec((1, d))],
        out_specs=tile(d),
        out_shape=jax.ShapeDtypeStruct((bsz, seq, d), F32),
        scratch_shapes=[pltpu.VMEM((tm, d), F32)],
        compiler_params=_cparams(("parallel", "parallel", "arbitrary", "arbitrary")),
        name="moe",
    )(x, h, gw, mod, wg, wu, wd, final_norm[None, :])


def _pad_cols(a, width):
    return jnp.pad(a, ((0, 0), (0, width - a.shape[1])))


def _split(a, sizes):
    return jnp.split(a, [int(s) for s in np.cumsum(sizes)[:-1]], axis=-1)


def _prep_hyb_w_in(w):
    q_a, k_a, v_a, lr_a, r_a, q_b, kv_b, g_b = _split(w, HYB_SPLITS)
    return jnp.concatenate([q_a, k_a, v_a, r_a, _pad_cols(lr_a, LANES), q_b, kv_b, _pad_cols(g_b, LANES)],
                           axis=1).astype(BF16)


def _prep_cmp(pe, w1, w2):
    eye = jnp.eye(NSA_KV_GROUPS, dtype=F32)
    half = CMP_BLOCK // 2
    w1r = w1.reshape(2, 2, half, NSA_HEAD_DIM, CMP_HIDDEN)
    w1x = jnp.einsum("jstdc,gh->jstgdhc", w1r, eye).reshape(2, 2, half * LANES, NSA_KV_GROUPS * CMP_HIDDEN)
    w2x = jnp.einsum("jcd,gh->jgchd", w2, eye).reshape(2, NSA_KV_GROUPS * CMP_HIDDEN, LANES)
    per = pe.reshape(2, 2, half, 1, NSA_HEAD_DIM)
    pe_rows = jnp.broadcast_to(per, (2, 2, half, NSA_KV_GROUPS, NSA_HEAD_DIM)).reshape(2, 2, 1, half * LANES)
    return pe_rows, w1x.astype(BF16), w2x.astype(BF16)


def _overlap_matrix(seq):
    n_rows = seq // CMP_STRIDE
    n = np.arange(n_rows)[:, None] * CMP_STRIDE
    s = np.arange(LANES)[None, :] * SEL_BLOCK
    ov = (n < s + SEL_BLOCK) & (n + CMP_BLOCK - 1 >= s) & (np.arange(LANES)[None, :] < seq // SEL_BLOCK)
    return jnp.asarray(ov, BF16)


def _head_expand():
    ex = np.zeros((SSM_GROUPS, LANES, SSD_GW), np.float32)
    for g in range(SSM_GROUPS):
        for j in range(SSD_HPG):
            ex[g, g * SSD_HPG + j, j * SSM_HEAD_DIM:(j + 1) * SSM_HEAD_DIM] = 1.0
    return jnp.asarray(ex, BF16)


def kernel(x, c, hyb_norm, hyb_mod_w, hyb_mod_b, hyb_w_in, gla_gk_up, gla_gk_bias, gla_out_norm, nsa_cmp_pe, nsa_cmp_w1, nsa_cmp_w2, hyb_w_out, dense_norm, dense_mod_w, dense_mod_b, dense_w_gu, dense_w_down, ssm_norm, ssm_mod_w, ssm_mod_b, ssm_w_in, ssm_conv_w, ssm_conv_b, ssm_dt_bias, ssm_a_log, ssm_d, ssm_gate_norm, ssm_w_out, moe_norm, moe_mod_w, moe_mod_b, moe_router, moe_w_gu, moe_w_down, final_norm):
    bsz, seq, d = x.shape
    mods = _adaln(c, (hyb_mod_w[0], dense_mod_w[0], ssm_mod_w[0], moe_mod_w[0]),
                  (hyb_mod_b[0], dense_mod_b[0], ssm_mod_b[0], moe_mod_b[0]))
    mods = mods.reshape(4, bsz, 1, 3 * d)

    gkup_pad = jnp.pad(gla_gk_up[0], ((0, LANES - GLA_LOWRANK), (0, 0))).astype(BF16)
    (gq, gk, la, gv, gr, nq, kcmp, vcmp, kslc, vslc, kwin, vwin, ng) = _hyb_in(
        x, hyb_norm[0], mods[0], _prep_hyb_w_in(hyb_w_in[0]), gkup_pad, gla_gk_bias[0])
    o_gla = _gla(gq, gk, la, gv, gr, gla_out_norm[0])
    pe_rows, w1x, w2x = _prep_cmp(nsa_cmp_pe[0], nsa_cmp_w1[0], nsa_cmp_w2[0])
    kc, vc = _nsa_compress(kcmp, vcmp, pe_rows, w1x, w2x)
    o_nsa = _nsa_attn(nq, ng, kc, vc, kslc, vslc, kwin, vwin, _overlap_matrix(seq))
    w_out = hyb_w_out[0].astype(BF16)
    n_gla = GLA_HEADS * GLA_DV
    w_gu = dense_w_gu[0].astype(BF16)
    x = _hyb_out_ffn(x, o_gla, o_nsa, mods[0], w_out[:n_gla], w_out[n_gla:], dense_norm[0], mods[1],
                     w_gu[:, :FFN_DENSE], w_gu[:, FFN_DENSE:], dense_w_down[0].astype(BF16))

    wz, wxbc, wdt = _split(ssm_w_in[0], SSM_SPLITS)
    w_ssm = jnp.concatenate([wz, wxbc, _pad_cols(wdt, LANES)], axis=1).astype(BF16)
    z, xbc, dt = _ssm_in(x, ssm_norm[0], mods[2], w_ssm)
    pad_heads = lambda a: jnp.pad(a, (0, LANES - SSM_HEADS))[None, :]
    y = _ssd(xbc, z, dt, ssm_conv_w[0], ssm_conv_b[0], pad_heads(ssm_dt_bias[0]), pad_heads(ssm_a_log[0]),
             jnp.repeat(ssm_d[0], SSM_HEAD_DIM)[None, :], ssm_gate_norm[0], _head_expand())
    x3, h4, gw = _ssm_out_route(x, y, mods[2], ssm_w_out[0].astype(BF16), moe_norm[0], mods[3],
                                _pad_cols(moe_router[0], LANES))
    w_gu = moe_w_gu[0].astype(BF16)
    return _moe(x3, h4, gw, mods[3], w_gu[:, :, :FFN_EXPERT], w_gu[:, :, FFN_EXPERT:],
                moe_w_down[0].astype(BF16), final_norm)
```

```python
import functools

import jax
import jax.numpy as jnp
import numpy as np
from jax import lax
from jax.experimental import pallas as pl
from jax.experimental.pallas import tpu as pltpu
from jax.experimental.pallas import tpu_sc as plsc

F32 = jnp.float32
BF16 = jnp.bfloat16

D_MODEL = 1024
NORM_EPS = 1e-6
GLA_HEADS = 4
GLA_DV = D_MODEL // 8
GLA_DK = GLA_DV // 2
GLA_LOWRANK = 16
GLA_TAU = 16.0
GLA_CHUNK = 64
NSA_HEADS = 8
NSA_KV_GROUPS = 2
NSA_HEAD_DIM = D_MODEL // 16
CMP_BLOCK = 32
CMP_STRIDE = 16
CMP_HIDDEN = 4 * NSA_HEAD_DIM
SEL_BLOCK = 64
N_SELECT = 8
WINDOW = 512
D_INNER = 2 * D_MODEL
SSM_HEAD_DIM = 64
SSM_HEADS = D_INNER // SSM_HEAD_DIM
SSM_GROUPS = 4
SSM_STATE = 128
CONV_K = 4
SSM_CHUNK = 64
CONV_CH = D_INNER + 2 * SSM_GROUPS * SSM_STATE
FFN_DENSE = ((8 * D_MODEL // 3 + 127) // 128) * 128
N_EXPERTS = 8
FFN_EXPERT = 7 * D_MODEL // 2
HYB_SPLITS = (GLA_HEADS * GLA_DK, GLA_HEADS * GLA_DK, GLA_HEADS * GLA_DV, GLA_LOWRANK, GLA_HEADS * GLA_DV,
              NSA_HEADS * NSA_HEAD_DIM, 6 * NSA_KV_GROUPS * NSA_HEAD_DIM, 3 * NSA_HEADS)
SSM_SPLITS = (D_INNER, CONV_CH, SSM_HEADS)

LANES = 128
VMEM_LIMIT = 56 * 1024 * 1024

NEG_BIG = -1e30


def _cparams(sem):
    return pltpu.CompilerParams(dimension_semantics=sem, vmem_limit_bytes=VMEM_LIMIT)


def _dot(a, b):
    return jnp.dot(a, b, preferred_element_type=F32)


def _dot_nt(a, b):
    return lax.dot_general(a, b, (((1,), (1,)), ((), ())), preferred_element_type=F32)


def _dot_tn(a, b):
    return lax.dot_general(a, b, (((0,), (0,)), ((), ())), preferred_element_type=F32)


def _split_bf16(a, parts):
    out = []
    r = a
    for _ in range(parts):
        p = r.astype(BF16)
        out.append(p)
        r = r - p.astype(F32)
    return out


def _dot_exact_lhs(m, a, parts=3):
    acc = None
    for p in _split_bf16(a, parts):
        t = _dot(m, p)
        acc = t if acc is None else acc + t
    return acc


def _dot_exact_rhs(a, m, parts=3):
    acc = None
    for p in _split_bf16(a, parts):
        t = _dot(p, m)
        acc = t if acc is None else acc + t
    return acc


def _sigmoid(x):
    return 1.0 / (1.0 + jnp.exp(-x))


def _silu(x):
    return x * _sigmoid(x)


def _softplus(x):
    return jnp.maximum(x, 0.0) + jnp.log(1.0 + jnp.exp(-jnp.abs(x)))


def _gelu_tanh(x):
    return x * (0.5 * (1.0 + jnp.tanh(0.7978845608028654 * (x + 0.044715 * (x * x * x)))))


def _modulated_norm(x, nw, shift, scale):
    ms = jnp.mean(x * x, axis=-1, keepdims=True)
    y = x * lax.rsqrt(ms + NORM_EPS)
    return (y * nw) * (1.0 + scale) + shift


def _const_spec(shape):
    nd = len(shape)
    return pl.BlockSpec(shape, lambda *_: (0,) * nd, pipeline_mode=pl.Buffered(1))


def _adaln_kernel(c_ref, w0, w1, w2, w3, b0, b1, b2, b3, o_ref):
    sc = _silu(c_ref[...]).astype(BF16)
    for i, (w, b) in enumerate(((w0, b0), (w1, b1), (w2, b2), (w3, b3))):
        o_ref[i] = _dot(sc, w[...].astype(BF16)) + b[...]


def _adaln(c, ws, bs):
    bsz, d = c.shape
    n = ws[0].shape[1]
    tn = 512
    w_spec = pl.BlockSpec((d, tn), lambda j: (0, j))
    b_spec = pl.BlockSpec((1, tn), lambda j: (0, j))
    return pl.pallas_call(
        _adaln_kernel,
        grid=(n // tn,),
        in_specs=[pl.BlockSpec((bsz, d), lambda j: (0, 0))] + [w_spec] * 4 + [b_spec] * 4,
        out_specs=pl.BlockSpec((4, bsz, tn), lambda j: (0, 0, j)),
        out_shape=jax.ShapeDtypeStruct((4, bsz, n), F32),
        compiler_params=_cparams(("arbitrary",)),
        name="adaln",
    )(c, *ws, *[b[None, :] for b in bs])


def _mod_specs(tile_axes):
    def spec(part):
        if tile_axes == 2:
            return pl.BlockSpec((1, 1, D_MODEL), lambda b, i: (b, 0, part))
        return pl.BlockSpec((1, 1, D_MODEL), lambda b, i, j, k: (b, 0, part))
    return spec(0), spec(1), spec(2)


HYB_SEG = {
    "gq": (0, 256), "gk": (256, 256), "gv": (512, 512), "gr": (1024, 512), "lr": (1536, 128),
    "nq": (1664, 512), "kv": (2176, 768), "ng": (2944, 128),
}
HYB_COLS = 3072


def _hyb_in_kernel(x_ref, nw_ref, shift_ref, scale_ref, w_ref, gkup_ref, gkb_ref,
                   gq_ref, gk_ref, la_ref, gv_ref, gr_ref, nq_ref,
                   kcmp_ref, vcmp_ref, kslc_ref, vslc_ref, kwin_ref, vwin_ref, ng_ref):
    h = _modulated_norm(x_ref[0], nw_ref[...], shift_ref[0], scale_ref[0]).astype(BF16)

    def proj(name, sub=None):
        off, width = HYB_SEG[name]
        if sub is not None:
            off, width = off + sub * LANES, LANES
        return _dot(h, w_ref[:, off:off + width])

    gq_ref[0] = proj("gq")
    gk_ref[0] = proj("gk")
    gv_ref[0] = proj("gv").astype(BF16)
    gr_ref[0] = proj("gr")
    lr = proj("lr").astype(BF16)
    z = _dot(lr, gkup_ref[...]) + gkb_ref[...]
    la_ref[0] = -_softplus(-z) * (1.0 / GLA_TAU)
    nq_ref[0] = proj("nq").astype(BF16)
    kcmp_ref[0] = proj("kv", 0)
    vcmp_ref[0] = proj("kv", 1)
    kslc_ref[0] = proj("kv", 2).astype(BF16)
    vslc_ref[0] = proj("kv", 3).astype(BF16)
    kwin_ref[0] = proj("kv", 4).astype(BF16)
    vwin_ref[0] = proj("kv", 5).astype(BF16)
    ng_ref[0] = proj("ng")


def _hyb_in(x, norm_w, mod, w_perm, gkup_pad, gk_bias):
    bsz, seq, d = x.shape
    tm = 512
    shift_s, scale_s, _ = _mod_specs(2)

    def out(width, dtype):
        return (pl.BlockSpec((1, tm, width), lambda b, i: (b, i, 0)),
                jax.ShapeDtypeStruct((bsz, seq, width), dtype))

    outs = [out(256, F32), out(256, F32), out(256, F32), out(512, BF16), out(512, F32), out(512, BF16),
            out(128, F32), out(128, F32), out(128, BF16), out(128, BF16), out(128, BF16), out(128, BF16),
            out(128, F32)]
    return pl.pallas_call(
        _hyb_in_kernel,
        grid=(bsz, seq // tm),
        in_specs=[pl.BlockSpec((1, tm, d), lambda b, i: (b, i, 0)),
                  _const_spec((1, d)), shift_s, scale_s,
                  _const_spec(w_perm.shape), _const_spec(gkup_pad.shape), _const_spec((1, 256))],
        out_specs=[o[0] for o in outs],
        out_shape=[o[1] for o in outs],
        compiler_params=_cparams(("parallel", "parallel")),
        name="hyb_in",
    )(x, norm_w[None, :], mod, mod, w_perm, gkup_pad, gk_bias[None, :])


GLA_ROWS = 512


def _gla_kernel(q_ref, k_ref, la_ref, v_ref, r_ref, nw_ref, o_ref, *, seq):
    rb = GLA_ROWS
    ncb = rb // GLA_CHUNK
    row = lax.broadcasted_iota(jnp.int32, (rb, rb), 0)
    col = lax.broadcasted_iota(jnp.int32, (rb, rb), 1)
    chunk_causal = ((row // GLA_CHUNK) == (col // GLA_CHUNK)) & (col <= row)
    tril_bd = jnp.where(chunk_causal, 1.0, 0.0).astype(BF16)
    lane = lax.broadcasted_iota(jnp.int32, (rb, LANES), 1)
    head_mask = (lane < GLA_DK, lane >= GLA_DK)
    nw = nw_ref[...]

    def body(i, st):
        r0 = pl.multiple_of(i * rb, rb)
        rows = pl.ds(r0, rb)
        q = q_ref[0, rows, :]
        k = k_ref[0, rows, :]
        la = la_ref[0, rows, :]
        b = _dot_exact_lhs(tril_bd, la)
        b3 = b.reshape(ncb, GLA_CHUNK, LANES)
        bend3 = b3[:, GLA_CHUNK - 1:GLA_CHUNK, :]
        bend = jnp.broadcast_to(bend3, (ncb, GLA_CHUNK, LANES)).reshape(rb, LANES)
        q_dec = (q * GLA_DK ** -0.5) * jnp.exp(b)
        k_dec = (k * jnp.exp(-b)).astype(BF16)
        k_end = k * jnp.exp(bend - b)
        decay = jnp.exp(bend3)

        qm, vh, o_intra, upd = [], [], [], []
        for h in range(2):
            qm_h = jnp.where(head_mask[h], q_dec, 0.0).astype(BF16)
            att = _dot_nt(qm_h, k_dec)
            att = jnp.where(chunk_causal, att, 0.0).astype(BF16)
            v_h = v_ref[0, rows, h * GLA_DV:(h + 1) * GLA_DV]
            o_intra.append(_dot(att, v_h))
            km_h = jnp.where(head_mask[h], k_end, 0.0).astype(BF16)
            upd.append([_dot_tn(v_h[n * GLA_CHUNK:(n + 1) * GLA_CHUNK], km_h[n * GLA_CHUNK:(n + 1) * GLA_CHUNK])
                        for n in range(ncb)])
            qm.append(qm_h)
            vh.append(v_h)

        prev = []
        for n in range(ncb):
            prev.append(st.astype(BF16))
            st = st * decay[n] + (upd[0][n] + upd[1][n])

        for h in range(2):
            o_inter = jnp.concatenate(
                [_dot_nt(qm[h][n * GLA_CHUNK:(n + 1) * GLA_CHUNK], prev[n]) for n in range(ncb)], axis=0)
            o = o_intra[h] + o_inter
            y = o * lax.rsqrt(jnp.mean(o * o, axis=-1, keepdims=True) + NORM_EPS) * nw
            r = r_ref[0, rows, h * GLA_DV:(h + 1) * GLA_DV]
            o_ref[0, rows, h * GLA_DV:(h + 1) * GLA_DV] = (y * _silu(r)).astype(o_ref.dtype)
        return st

    lax.fori_loop(0, seq // rb, body, jnp.zeros((GLA_DV, 2 * GLA_DK), F32))


def _gla(gq, gk, la, gv, gr, out_norm):
    bsz, seq, _ = gq.shape
    qk_spec = pl.BlockSpec((1, seq, 2 * GLA_DK), lambda b, p: (b, 0, p))
    v_spec = pl.BlockSpec((1, seq, 2 * GLA_DV), lambda b, p: (b, 0, p))
    return pl.pallas_call(
        functools.partial(_gla_kernel, seq=seq),
        grid=(bsz, GLA_HEADS // 2),
        in_specs=[qk_spec, qk_spec, qk_spec, v_spec, v_spec, _const_spec((1, GLA_DV))],
        out_specs=v_spec,
        out_shape=jax.ShapeDtypeStruct((bsz, seq, GLA_HEADS * GLA_DV), BF16),
        compiler_params=_cparams(("parallel", "parallel")),
        name="gla",
    )(gq, gk, la, gv, gr, out_norm[None, :])


def _nsa_compress_kernel(k_ref, v_ref, pe_ref, w1_ref, w2_ref, kc_ref, vc_ref, *, n_cmp):
    rows = k_ref.shape[1]
    rid = lax.broadcasted_iota(jnp.int32, (rows, LANES), 0)
    for j, (src, dst) in enumerate(((k_ref, kc_ref), (v_ref, vc_ref))):
        x = src[0]
        lo = _dot((x + pe_ref[j, 0]).astype(BF16), w1_ref[j, 0])
        hi = _dot((x + pe_ref[j, 1]).astype(BF16), w1_ref[j, 1])
        hpre = lo + pltpu.roll(hi, rows - 1, 0)
        out = _dot(_gelu_tanh(hpre).astype(BF16), w2_ref[j])
        dst[0] = jnp.where(rid < n_cmp, out, 0.0).astype(dst.dtype)


def _nsa_compress(kcmp, vcmp, pe_rows, w1x, w2x):
    bsz, seq, _ = kcmp.shape
    rows = seq // CMP_STRIDE
    width = CMP_STRIDE * LANES
    n_cmp = (seq - CMP_BLOCK) // CMP_STRIDE + 1
    x_spec = pl.BlockSpec((1, rows, width), lambda b: (b, 0, 0))
    o_spec = pl.BlockSpec((1, rows, LANES), lambda b: (b, 0, 0))
    o_shape = jax.ShapeDtypeStruct((bsz, rows, LANES), BF16)
    return pl.pallas_call(
        functools.partial(_nsa_compress_kernel, n_cmp=n_cmp),
        grid=(bsz,),
        in_specs=[x_spec, x_spec, _const_spec(pe_rows.shape), _const_spec(w1x.shape), _const_spec(w2x.shape)],
        out_specs=[o_spec, o_spec],
        out_shape=[o_shape, o_shape],
        compiler_params=_cparams(("parallel",)),
        name="nsa_compress",
    )(kcmp.reshape(bsz, rows, width), vcmp.reshape(bsz, rows, width), pe_rows, w1x, w2x)


NSA_QT = 128
NSA_KT = 512
NSA_ROWS = NSA_HEADS * NSA_QT


def _nsa_attn_kernel(q_ref, g_ref, kc_ref, vc_ref, ks_ref, vs_ref, kw_ref, vw_ref, ov_ref, o_ref,
                     m_ref, l_ref, acc_ref, *, seq):
    qt, kt = NSA_QT, NSA_KT
    heads_per_group = NSA_HEADS // NSA_KV_GROUPS
    q0 = pl.program_id(1) * qt
    t = q0 + lax.broadcasted_iota(jnp.int32, (qt, 1), 0)
    lane = lax.broadcasted_iota(jnp.int32, (qt, LANES), 1)
    group_lanes = (lane < NSA_HEAD_DIM, lane >= NSA_HEAD_DIM)
    slopes = [2.0 ** (-(h + 1)) for h in range(NSA_HEADS)]

    qa = q_ref[0].astype(F32) * NSA_HEAD_DIM ** -0.5
    q_rows = []
    for h in range(NSA_HEADS):
        g = h // heads_per_group
        blk = qa[:, (h // 2) * LANES:(h // 2 + 1) * LANES]
        if h % 2 != g:
            blk = pltpu.roll(blk, NSA_HEAD_DIM, 1)
        q_rows.append(jnp.where(group_lanes[g], blk, 0.0).astype(BF16))
    q_ext = jnp.concatenate(q_rows, axis=0)

    def softmax_parts(s, mask):
        s = jnp.where(mask, s, -jnp.inf)
        m = jnp.max(s, axis=-1, keepdims=True)
        m = jnp.where(m == -jnp.inf, 0.0, m)
        p = jnp.exp(s - m)
        return p, jnp.sum(p, axis=-1, keepdims=True)

    n_rows = kc_ref.shape[1]
    cidx = lax.broadcasted_iota(jnp.int32, (1, n_rows), 1)
    cmp_end = cidx * CMP_STRIDE + (CMP_BLOCK - 1)
    cmp_center = cidx.astype(F32) * CMP_STRIDE + (CMP_BLOCK - 1) / 2.0
    cmp_mask = cmp_end <= t
    cmp_dist = t.astype(F32) - cmp_center
    s_all = _dot_nt(q_ext, kc_ref[0])
    p_list, psum = [], [None] * NSA_KV_GROUPS
    for h in range(NSA_HEADS):
        g = h // heads_per_group
        p, l = softmax_parts(s_all[h * qt:(h + 1) * qt] - slopes[h] * cmp_dist, cmp_mask)
        p = p / jnp.maximum(l, 1e-20)
        p_list.append(p.astype(BF16))
        psum[g] = p if psum[g] is None else psum[g] + p
    o_cmp = _dot(jnp.concatenate(p_list, axis=0), vc_ref[0])

    n_sel = seq // SEL_BLOCK
    blk_t = t // SEL_BLOCK
    forced = (lane == 0) | (lane == blk_t) | (lane == blk_t - 1)
    future = (lane * SEL_BLOCK > t) | (lane >= n_sel)
    sel = []
    for g in range(NSA_KV_GROUPS):
        imp = _dot_exact_rhs(psum[g], ov_ref[...])
        v = jnp.where(future, -1.0, jnp.where(forced, 3e38, imp))
        chosen = jnp.zeros((qt, LANES), F32)
        for _ in range(min(N_SELECT, n_sel)):
            m = jnp.max(v, axis=-1, keepdims=True)
            idx = jnp.min(jnp.where(v == m, lane, LANES), axis=-1, keepdims=True)
            pick = (lane == idx) & (m >= 0.0)
            chosen = jnp.where(pick, 1.0, chosen)
            v = jnp.where(pick, -1.0, v)
        sel.append(chosen.astype(BF16))

    m_ref[...] = jnp.full(m_ref.shape, -jnp.inf, F32)
    l_ref[...] = jnp.zeros(l_ref.shape, F32)
    acc_ref[...] = jnp.zeros(acc_ref.shape, F32)
    kpos = lax.broadcasted_iota(jnp.int32, (1, kt), 1)
    erow = lax.broadcasted_iota(jnp.int32, (LANES, kt), 0)
    ecol = lax.broadcasted_iota(jnp.int32, (LANES, kt), 1) // SEL_BLOCK

    def sel_tile(j, carry):
        k0 = pl.multiple_of(j * kt, kt)
        ks = ks_ref[0, pl.ds(k0, kt), :]
        vs = vs_ref[0, pl.ds(k0, kt), :]
        s_t = _dot_nt(q_ext, ks)
        expand = jnp.where(erow == ecol + j * (kt // SEL_BLOCK), 1.0, 0.0).astype(BF16)
        dist = t - (k0 + kpos)
        dist_f = dist.astype(F32)
        p_rows = []
        for g in range(NSA_KV_GROUPS):
            mask = (_dot(sel[g], expand) > 0.5) & (dist >= 0)
            for r in range(heads_per_group):
                h = g * heads_per_group + r
                rows = slice(h * qt, (h + 1) * qt)
                s = jnp.where(mask, s_t[rows] - slopes[h] * dist_f, -jnp.inf)
                m_old = m_ref[rows]
                m_new = jnp.maximum(m_old, jnp.max(s, axis=-1, keepdims=True))
                m_safe = jnp.where(m_new == -jnp.inf, 0.0, m_new)
                p = jnp.exp(s - m_safe)
                alpha = jnp.exp(m_old - m_safe)
                l_ref[rows] = alpha * l_ref[rows] + jnp.sum(p, axis=-1, keepdims=True)
                acc_ref[rows] = alpha * acc_ref[rows]
                m_ref[rows] = m_new
                p_rows.append(p.astype(BF16))
        acc_ref[...] += _dot(jnp.concatenate(p_rows, axis=0), vs)
        return carry

    lax.fori_loop(0, (q0 + qt + kt - 1) // kt, sel_tile, 0)
    o_slc = acc_ref[...] / jnp.maximum(l_ref[...], 1e-20)

    wk = WINDOW + qt
    w0 = pl.multiple_of(jnp.maximum(q0 - WINDOW, 0), qt)
    kw = kw_ref[0, pl.ds(w0, wk), :]
    vw = vw_ref[0, pl.ds(w0, wk), :]
    wdist = t - (w0 + lax.broadcasted_iota(jnp.int32, (1, wk), 1))
    wmask = (wdist >= 0) & (wdist < WINDOW)
    wdist_f = wdist.astype(F32)
    s_all = _dot_nt(q_ext, kw)
    p_list, l_list = [], []
    for h in range(NSA_HEADS):
        p, l = softmax_parts(s_all[h * qt:(h + 1) * qt] - slopes[h] * wdist_f, wmask)
        p_list.append(p.astype(BF16))
        l_list.append(l)
    o_win = _dot(jnp.concatenate(p_list, axis=0), vw) / jnp.maximum(jnp.concatenate(l_list, axis=0), 1e-20)

    gates = _sigmoid(g_ref[0])
    o_heads = []
    for h in range(NSA_HEADS):
        rows = slice(h * qt, (h + 1) * qt)
        o_heads.append(gates[:, 3 * h:3 * h + 1] * o_cmp[rows] + gates[:, 3 * h + 1:3 * h + 2] * o_slc[rows]
                       + gates[:, 3 * h + 2:3 * h + 3] * o_win[rows])
    for c in range(NSA_HEADS // 2):
        g = (2 * c) // heads_per_group
        a, b = o_heads[2 * c], o_heads[2 * c + 1]
        if g == 0:
            blk = jnp.where(group_lanes[0], a, pltpu.roll(b, NSA_HEAD_DIM, 1))
        else:
            blk = jnp.where(group_lanes[0], pltpu.roll(a, NSA_HEAD_DIM, 1), b)
        o_ref[0, :, c * LANES:(c + 1) * LANES] = blk.astype(o_ref.dtype)


def _nsa_attn(nq, ng, kc, vc, kslc, vslc, kwin, vwin, overlap):
    bsz, seq, _ = nq.shape
    n_rows = kc.shape[1]
    full = lambda rows: pl.BlockSpec((1, rows, LANES), lambda b, i: (b, 0, 0))
    return pl.pallas_call(
        functools.partial(_nsa_attn_kernel, seq=seq),
        grid=(bsz, seq // NSA_QT),
        in_specs=[pl.BlockSpec((1, NSA_QT, NSA_HEADS * NSA_HEAD_DIM), lambda b, i: (b, i, 0)),
                  pl.BlockSpec((1, NSA_QT, LANES), lambda b, i: (b, i, 0)),
                  full(n_rows), full(n_rows), full(seq), full(seq), full(seq), full(seq),
                  _const_spec(overlap.shape)],
        out_specs=pl.BlockSpec((1, NSA_QT, NSA_HEADS * NSA_HEAD_DIM), lambda b, i: (b, i, 0)),
        out_shape=jax.ShapeDtypeStruct((bsz, seq, NSA_HEADS * NSA_HEAD_DIM), BF16),
        scratch_shapes=[pltpu.VMEM((NSA_ROWS, 1), F32), pltpu.VMEM((NSA_ROWS, 1), F32),
                        pltpu.VMEM((NSA_ROWS, LANES), F32)],
        compiler_params=_cparams(("parallel", "arbitrary")),
        name="nsa_attn",
    )(nq, ng, kc, vc, kslc, vslc, kwin, vwin, overlap)


FFN_CHUNK = 1408


def _hyb_out_ffn_kernel(x_ref, oa_ref, ob_ref, g1_ref, wa_ref, wb_ref,
                        nw_ref, shift_ref, scale_ref, g2_ref, wg_ref, wu_ref, wd_ref, o_ref, act_ref):
    mix = _dot(oa_ref[0], wa_ref[...]) + _dot(ob_ref[0], wb_ref[...])
    x1 = x_ref[0] + g1_ref[0] * mix
    h = _modulated_norm(x1, nw_ref[...], shift_ref[0], scale_ref[0]).astype(BF16)
    for f in range(0, FFN_DENSE, FFN_CHUNK):
        gate = _dot(h, wg_ref[:, f:f + FFN_CHUNK])
        up = _dot(h, wu_ref[:, f:f + FFN_CHUNK])
        act_ref[:, f:f + FFN_CHUNK] = (_silu(gate) * up).astype(BF16)
    o_ref[0] = x1 + g2_ref[0] * _dot(act_ref[...], wd_ref[...])


def _hyb_out_ffn(x, o_gla, o_nsa, mod1, wa, wb, norm_w, mod2, wg, wu, wd):
    bsz, seq, d = x.shape
    tm = 512
    _, _, gate1 = _mod_specs(2)
    shift2, scale2, gate2 = _mod_specs(2)
    tile = lambda width: pl.BlockSpec((1, tm, width), lambda b, i: (b, i, 0))
    return pl.pallas_call(
        _hyb_out_ffn_kernel,
        grid=(bsz, seq // tm),
        in_specs=[tile(d), tile(o_gla.shape[-1]), tile(o_nsa.shape[-1]), gate1,
                  _const_spec(wa.shape), _const_spec(wb.shape), _const_spec((1, d)), shift2, scale2, gate2,
                  _const_spec(wg.shape), _const_spec(wu.shape), _const_spec(wd.shape)],
        out_specs=tile(d),
        out_shape=jax.ShapeDtypeStruct((bsz, seq, d), F32),
        scratch_shapes=[pltpu.VMEM((tm, FFN_DENSE), BF16)],
        compiler_params=_cparams(("parallel", "parallel")),
        name="hyb_out_ffn",
    )(x, o_gla, o_nsa, mod1, wa, wb, norm_w[None, :], mod2, mod2, mod2, wg, wu, wd)


SSM_DT_OFF = D_INNER + CONV_CH
SSM_COLS = SSM_DT_OFF + LANES
SSM_IN_CHUNK = 1024


def _ssm_in_kernel(x_ref, nw_ref, shift_ref, scale_ref, w_ref, z_ref, xbc_ref, dt_ref):
    h = _modulated_norm(x_ref[0], nw_ref[...], shift_ref[0], scale_ref[0]).astype(BF16)
    for c in range(0, D_INNER, SSM_IN_CHUNK):
        z_ref[0, :, c:c + SSM_IN_CHUNK] = _dot(h, w_ref[:, c:c + SSM_IN_CHUNK])
    for c in range(0, CONV_CH, SSM_IN_CHUNK):
        xbc_ref[0, :, c:c + SSM_IN_CHUNK] = _dot(h, w_ref[:, D_INNER + c:D_INNER + c + SSM_IN_CHUNK])
    dt_ref[0] = _dot(h, w_ref[:, SSM_DT_OFF:SSM_COLS])


def _ssm_in(x, norm_w, mod, w_perm):
    bsz, seq, d = x.shape
    tm = 512
    shift_s, scale_s, _ = _mod_specs(2)
    tile = lambda width: pl.BlockSpec((1, tm, width), lambda b, i: (b, i, 0))
    return pl.pallas_call(
        _ssm_in_kernel,
        grid=(bsz, seq // tm),
        in_specs=[tile(d), _const_spec((1, d)), shift_s, scale_s, _const_spec(w_perm.shape)],
        out_specs=[tile(D_INNER), tile(CONV_CH), tile(LANES)],
        out_shape=[jax.ShapeDtypeStruct((bsz, seq, D_INNER), F32),
                   jax.ShapeDtypeStruct((bsz, seq, CONV_CH), F32),
                   jax.ShapeDtypeStruct((bsz, seq, LANES), F32)],
        compiler_params=_cparams(("parallel", "parallel")),
        name="ssm_in",
    )(x, norm_w[None, :], mod, mod, w_perm)


SSD_ROWS = 512
SSD_GW = D_INNER // SSM_GROUPS
SSD_HPG = SSM_HEADS // SSM_GROUPS
CONV_TAIL = 8


def _ssd_kernel(xbc_ref, z_ref, dt_ref, cw_ref, cb_ref, dtb_ref, alog_ref, dskip_ref, nw_ref, ex_ref, o_ref,
                tail_ref, state_ref, xg_ref, xdt_ref, cum_ref, b_ref, c_ref, y_ref):
    rb, q = SSD_ROWS, SSM_CHUNK
    nchunk = rb // q

    @pl.when(pl.program_id(1) == 0)
    def _():
        tail_ref[...] = jnp.zeros(tail_ref.shape, F32)
        state_ref[...] = jnp.zeros(state_ref.shape, F32)

    row8 = lax.broadcasted_iota(jnp.int32, (CONV_TAIL, 1), 0)

    def conv_silu(c0, width):
        x = xbc_ref[0, :, c0:c0 + width]
        tail = tail_ref[:, c0:c0 + width]
        acc = x * cw_ref[CONV_K - 1:CONV_K, c0:c0 + width] + cb_ref[:, c0:c0 + width]
        for j in range(1, CONV_K):
            xs = pltpu.roll(x, j, 0)
            head = jnp.where(row8 < j, pltpu.roll(tail, j, 0), xs[:CONV_TAIL])
            xs = jnp.concatenate([head, xs[CONV_TAIL:]], axis=0)
            acc = acc + xs * cw_ref[CONV_K - 1 - j:CONV_K - j, c0:c0 + width]
        return _silu(acc)

    dt = _softplus(dt_ref[0] + dtb_ref[...])
    a = dt * (-jnp.exp(alog_ref[...]))
    row = lax.broadcasted_iota(jnp.int32, (rb, rb), 0)
    col = lax.broadcasted_iota(jnp.int32, (rb, rb), 1)
    tril_bd = jnp.where(((row // q) == (col // q)) & (col <= row), 1.0, 0.0).astype(BF16)
    cum = _dot_exact_lhs(tril_bd, a)

    lrow = lax.broadcasted_iota(jnp.int32, (q, SSD_GW), 0)
    lcol = lax.broadcasted_iota(jnp.int32, (q, SSD_GW), 1) % q
    causal_t = lcol <= lrow
    eye_t = lcol == lrow
    half = SSD_GW // 2
    brow = lax.broadcasted_iota(jnp.int32, (half, half), 0) // q
    bcol = lax.broadcasted_iota(jnp.int32, (half, half), 1) // SSM_HEAD_DIM
    same_head = brow == bcol

    for g in range(SSM_GROUPS):
        xg = conv_silu(g * SSD_GW, SSD_GW)
        xg_ref[...] = xg
        b_ref[...] = conv_silu(D_INNER + g * SSM_STATE, SSM_STATE).astype(BF16)
        c_ref[...] = conv_silu(D_INNER + SSM_GROUPS * SSM_STATE + g * SSM_STATE, SSM_STATE).astype(BF16)
        xdt_ref[...] = xg * _dot_exact_rhs(dt, ex_ref[g])
        cum_ref[...] = _dot_exact_rhs(cum, ex_ref[g])

        def chunk(n, carry):
            rows = pl.ds(pl.multiple_of(n * q, q), q)
            cum_c = cum_ref[rows, :]
            cum_s = jnp.sum(jnp.where(eye_t, cum_c, 0.0), axis=0, keepdims=True)
            decay_l = jnp.where(causal_t, jnp.exp(cum_c - cum_s), 0.0)
            bc = b_ref[rows, :]
            cc = c_ref[rows, :]
            cb_t = _dot_nt(cc, jnp.concatenate([bc] * SSD_HPG, axis=0))
            mat = (cb_t * decay_l).astype(BF16)
            xdt_c = xdt_ref[rows, :]
            xdt_b = xdt_c.astype(BF16)
            y_diag = []
            for s in range(2):
                blk = xdt_b[:, s * half:(s + 1) * half]
                bd = jnp.where(same_head, jnp.concatenate([blk] * (half // q), axis=0), 0.0).astype(BF16)
                y_diag.append(_dot(mat[:, s * half:(s + 1) * half], bd))
            y = jnp.concatenate(y_diag, axis=1)
            cum_end = cum_c[q - 1:q, :]
            st = state_ref[g]
            y = y + _dot(cc, st.astype(BF16)) * jnp.exp(cum_c)
            x_end = (xdt_c * jnp.exp(cum_end - cum_c)).astype(BF16)
            state_ref[g] = st * jnp.exp(cum_end) + _dot_tn(bc, x_end)
            y_ref[rows, :] = y
            return carry

        lax.fori_loop(0, nchunk, chunk, 0)
        cols = slice(g * SSD_GW, (g + 1) * SSD_GW)
        y = y_ref[...] + dskip_ref[:, cols] * xg_ref[...]
        y = y * _silu(z_ref[0, :, cols])
        y = y * lax.rsqrt(jnp.mean(y * y, axis=-1, keepdims=True) + NORM_EPS) * nw_ref[:, cols]
        o_ref[0, :, cols] = y.astype(o_ref.dtype)

    tail_ref[...] = xbc_ref[0, rb - CONV_TAIL:rb, :]


def _ssd(xbc, z, dt, conv_w, conv_b, dt_bias_pad, a_log_pad, d_skip_x, norm_w, expand):
    bsz, seq, _ = xbc.shape
    rb = SSD_ROWS
    tile = lambda width: pl.BlockSpec((1, rb, width), lambda b, i: (b, i, 0))
    return pl.pallas_call(
        _ssd_kernel,
        grid=(bsz, seq // rb),
        in_specs=[tile(CONV_CH), tile(D_INNER), tile(LANES),
                  _const_spec((CONV_K, CONV_CH)), _const_spec((1, CONV_CH)),
                  _const_spec((1, LANES)), _const_spec((1, LANES)),
                  _const_spec((1, D_INNER)), _const_spec((1, D_INNER)), _const_spec(expand.shape)],
        out_specs=tile(D_INNER),
        out_shape=jax.ShapeDtypeStruct((bsz, seq, D_INNER), BF16),
        scratch_shapes=[pltpu.VMEM((CONV_TAIL, CONV_CH), F32),
                        pltpu.VMEM((SSM_GROUPS, SSM_STATE, SSD_GW), F32),
                        pltpu.VMEM((rb, SSD_GW), F32), pltpu.VMEM((rb, SSD_GW), F32),
                        pltpu.VMEM((rb, SSD_GW), F32),
                        pltpu.VMEM((rb, SSM_STATE), BF16), pltpu.VMEM((rb, SSM_STATE), BF16),
                        pltpu.VMEM((rb, SSD_GW), F32)],
        compiler_params=_cparams(("parallel", "arbitrary")),
        name="ssd",
    )(xbc, z, dt, conv_w, conv_b[None, :], dt_bias_pad, a_log_pad, d_skip_x, norm_w[None, :], expand)


def _pack_bf16_pairs(a):
    w = a.shape[1] // 2
    bits = lax.bitcast_convert_type(a.astype(jnp.bfloat16).astype(F32), jnp.uint32)
    packed = bits[:, w:] | (bits[:, :w] >> 16)
    return lax.bitcast_convert_type(packed, jnp.int32)


def _unpack_bf16_pairs(p):
    bits = lax.bitcast_convert_type(p, jnp.uint32)
    lo = lax.bitcast_convert_type(bits << 16, F32)
    hi = lax.bitcast_convert_type(bits & jnp.uint32(0xFFFF0000), F32)
    return jnp.concatenate([lo, hi], axis=1).astype(BF16)


def _ssm_out_route_kernel(x_ref, y_ref, g1_ref, w_ref, nw_ref, shift_ref, scale_ref, r_ref,
                          x3_ref, h_ref, gw_ref, sel_ref, cnt_ref):
    x3 = x_ref[0] + g1_ref[0] * _dot(y_ref[0], w_ref[...])
    x3_ref[0] = x3
    h = _modulated_norm(x3, nw_ref[...], shift_ref[0], scale_ref[0])
    h_ref[0] = _pack_bf16_pairs(h)
    h_hi, h_lo = _split_bf16(h, 2)
    r_hi, r_lo = _split_bf16(r_ref[...], 2)
    logits = _dot(h_hi, r_hi) + (_dot(h_hi, r_lo) + _dot(h_lo, r_hi))
    lane = lax.broadcasted_iota(jnp.int32, logits.shape, 1)
    logits = jnp.where(lane < N_EXPERTS, logits, -jnp.inf)
    m1 = jnp.max(logits, axis=-1, keepdims=True)
    i1 = jnp.min(jnp.where(logits == m1, lane, LANES), axis=-1, keepdims=True)
    rest = jnp.where(lane == i1, -jnp.inf, logits)
    m2 = jnp.max(rest, axis=-1, keepdims=True)
    i2 = jnp.min(jnp.where(rest == m2, lane, LANES), axis=-1, keepdims=True)
    e2 = jnp.exp(m2 - m1)
    w1 = 1.0 / (1.0 + e2)
    w2 = e2 / (1.0 + e2)
    chosen = (lane == i1) | (lane == i2)
    gw_ref[0] = jnp.where(lane == i1, w1, 0.0) + jnp.where(lane == i2, w2, 0.0)
    sel = jnp.where(chosen, 1.0, 0.0)
    sel_ref[0] = sel.astype(BF16)
    cnt_ref[0] = jnp.sum(sel, axis=0, keepdims=True)


def _ssm_out_route(x, y, mod1, w_out, norm_w, mod2, router_pad):
    bsz, seq, d = x.shape
    tm = ROUTE_TM
    tiles = seq // tm
    _, _, gate1 = _mod_specs(2)
    shift2, scale2, _ = _mod_specs(2)
    tile = lambda width: pl.BlockSpec((1, tm, width), lambda b, i: (b, i, 0))
    return pl.pallas_call(
        _ssm_out_route_kernel,
        grid=(bsz, tiles),
        in_specs=[tile(d), tile(D_INNER), gate1, _const_spec(w_out.shape), _const_spec((1, d)), shift2, scale2,
                  _const_spec(router_pad.shape)],
        out_specs=[tile(d), tile(d // 2), tile(LANES), tile(LANES),
                   pl.BlockSpec((1, 1, LANES), lambda b, i: (b * tiles + i, 0, 0))],
        out_shape=[jax.ShapeDtypeStruct((bsz, seq, d), F32), jax.ShapeDtypeStruct((bsz, seq, d // 2), jnp.int32),
                   jax.ShapeDtypeStruct((bsz, seq, LANES), F32), jax.ShapeDtypeStruct((bsz, seq, LANES), BF16),
                   jax.ShapeDtypeStruct((bsz * tiles, 1, LANES), F32)],
        compiler_params=_cparams(("parallel", "parallel")),
        name="ssm_out_route",
    )(x, y, mod1, w_out, norm_w[None, :], mod2, mod2, router_pad)


ROUTE_TM = 512


def _moe_plan_kernel(gw_ref, sel_ref, base_ref, pos_ref, wts_ref):
    tm = gw_ref.shape[0]
    gw = gw_ref[...]
    sel = sel_ref[...]
    lane = lax.broadcasted_iota(jnp.int32, (tm, LANES), 1)
    chosen = sel.astype(F32) > 0.5
    row = lax.broadcasted_iota(jnp.int32, (tm, tm), 0)
    col = lax.broadcasted_iota(jnp.int32, (tm, tm), 1)
    before = jnp.where(col < row, 1.0, 0.0).astype(BF16)
    rank = _dot(before, sel)
    dest = rank + base_ref[0]
    first = jnp.min(jnp.where(chosen, lane, LANES), axis=-1, keepdims=True)
    last = jnp.max(jnp.where(chosen, lane, -1), axis=-1, keepdims=True)
    ones = jnp.ones((8, LANES), BF16)
    for slot, pick in enumerate((lane == first, lane == last)):
        parts = _split_bf16(jnp.where(pick, dest, 0.0), 3)
        pos = _dot_nt(ones, parts[0]) + (_dot_nt(ones, parts[1]) + _dot_nt(ones, parts[2]))
        pos_ref[slot] = pos.astype(jnp.int32)
    w_lo = jnp.sum(jnp.where(lane == first, gw, 0.0), axis=-1, keepdims=True)
    w_hi = jnp.sum(jnp.where(lane == last, gw, 0.0), axis=-1, keepdims=True)
    wts_ref[...] = jnp.where(lane == 0, w_lo, jnp.where(lane == 1, w_hi, 0.0))


def _moe_plan(gw, sel, base):
    n = gw.shape[0]
    tm = ROUTE_TM
    return pl.pallas_call(
        _moe_plan_kernel,
        grid=(n // tm,),
        in_specs=[pl.BlockSpec((tm, LANES), lambda i: (i, 0)), pl.BlockSpec((tm, LANES), lambda i: (i, 0)),
                  pl.BlockSpec((1, 1, LANES), lambda i: (i, 0, 0))],
        out_specs=[pl.BlockSpec((2, 8, tm), lambda i: (0, 0, i)), pl.BlockSpec((tm, LANES), lambda i: (i, 0))],
        out_shape=[jax.ShapeDtypeStruct((2, 8, n), jnp.int32), jax.ShapeDtypeStruct((n, LANES), F32)],
        compiler_params=_cparams(("parallel",)),
        name="moe_plan",
    )(gw, sel, base)


SC_WINDOW = 128


SC_CORES = 2
SC_SUBCORES = 16


def _sc_workers():
    return SC_CORES, SC_CORES * SC_SUBCORES


def _sc_dispatch(rows, pos):
    n, width = rows.shape
    ncores, workers = _sc_workers()
    per_worker = n // workers
    steps = per_worker // SC_WINDOW
    mesh = plsc.VectorSubcoreMesh(core_axis_name="c", subcore_axis_name="s")

    @functools.partial(
        pl.kernel, mesh=mesh, out_type=jax.ShapeDtypeStruct((2 * n, width), rows.dtype),
        scratch_types=[pltpu.VMEM((2, steps, SC_WINDOW), jnp.int32), pltpu.VMEM((SC_WINDOW, width), rows.dtype)],
        name="moe_dispatch")
    def run(rows_hbm, pos_hbm, out_hbm, idx_v, rows_v):
        wid = lax.axis_index("s") * ncores + lax.axis_index("c")
        pltpu.sync_copy(pos_hbm.at[wid], idx_v)
        for j in range(steps):
            pltpu.sync_copy(rows_hbm.at[pl.ds(wid * per_worker + j * SC_WINDOW, SC_WINDOW)], rows_v)
            pltpu.sync_copy(rows_v, out_hbm.at[idx_v.at[0, j]])
            pltpu.sync_copy(rows_v, out_hbm.at[idx_v.at[1, j]])

    pos_w = pos.reshape(2, workers, steps, SC_WINDOW).transpose(1, 0, 2, 3)
    return run(rows, pos_w)


SC_GATHER_WINDOW = 64


def _sc_gather(table, pos):
    _, width = table.shape
    n = pos.shape[1]
    ncores, workers = _sc_workers()
    per_worker = n // workers
    win = SC_GATHER_WINDOW
    steps = per_worker // win
    mesh = plsc.VectorSubcoreMesh(core_axis_name="c", subcore_axis_name="s")

    @functools.partial(
        pl.kernel, mesh=mesh, out_type=jax.ShapeDtypeStruct((2, n, width), table.dtype),
        scratch_types=[pltpu.VMEM((2, steps, win), jnp.int32), pltpu.VMEM((win, width), table.dtype)],
        name="moe_gather")
    def run(table_hbm, pos_hbm, out_hbm, idx_v, rows_v):
        wid = lax.axis_index("s") * ncores + lax.axis_index("c")
        pltpu.sync_copy(pos_hbm.at[wid], idx_v)
        for s in range(2):
            for j in range(steps):
                pltpu.sync_copy(table_hbm.at[idx_v.at[s, j]], rows_v)
                pltpu.sync_copy(rows_v, out_hbm.at[s, pl.ds(wid * per_worker + j * win, win)])

    pos_w = pos.reshape(2, workers, steps, win).transpose(1, 0, 2, 3)
    return run(table, pos_w)


MOE_TM = 512
MOE_TF = 896


def _moe_pairs(goff, n_rows):
    tiles = n_rows // MOE_TM
    steps = tiles + N_EXPERTS - 1
    first_row = jnp.arange(tiles, dtype=jnp.int32) * MOE_TM
    ends = goff[1:]
    e_lo = jnp.sum(first_row[:, None] >= ends[None, :], axis=1).astype(jnp.int32)
    e_hi = jnp.sum((first_row + (MOE_TM - 1))[:, None] >= ends[None, :], axis=1).astype(jnp.int32)
    count = e_hi - e_lo + 1
    start = jnp.cumsum(count) - count
    p = jnp.arange(steps, dtype=jnp.int32)
    tile = jnp.sum(start[None, :] <= p[:, None], axis=1).astype(jnp.int32) - 1
    expert = e_lo[tile] + (p - start[tile])
    valid = p < jnp.sum(count)
    return (jnp.where(valid, tile, tiles - 1).astype(jnp.int32),
            jnp.where(valid, expert, N_EXPERTS - 1).astype(jnp.int32), valid.astype(jnp.int32))


def _moe_group_kernel(tile_ref, exp_ref, valid_ref, goff_ref, x_ref, wg_ref, wu_ref, wd_ref, y_ref, acc_ref):
    p = pl.program_id(0)
    f = pl.program_id(1)
    tm = x_ref.shape[0]
    tile = tile_ref[p]
    expert = exp_ref[p]
    new_tile = (p == 0) | (tile != tile_ref[jnp.maximum(p - 1, 0)])

    @pl.when(new_tile & (f == 0))
    def _():
        acc_ref[...] = jnp.zeros(acc_ref.shape, F32)

    @pl.when(valid_ref[p] == 1)
    def _():
        x = _unpack_bf16_pairs(x_ref[...])
        act = (_silu(_dot(x, wg_ref[0])) * _dot(x, wu_ref[0])).astype(BF16)
        rows = tile * tm + lax.broadcasted_iota(jnp.int32, (tm, 1), 0)
        mine = (rows >= goff_ref[expert]) & (rows < goff_ref[expert + 1])
        acc_ref[...] += jnp.where(mine, _dot(act, wd_ref[0]), 0.0)

    @pl.when(f == pl.num_programs(1) - 1)
    def _():
        y_ref[...] = acc_ref[...]


def _moe_group(xs, goff, wg, wu, wd):
    n_rows, half = xs.shape
    d = 2 * half
    tm, tf = MOE_TM, MOE_TF
    tile, expert, valid = _moe_pairs(goff, n_rows)
    grid_spec = pltpu.PrefetchScalarGridSpec(
        num_scalar_prefetch=4,
        grid=(tile.shape[0], FFN_EXPERT // tf),
        in_specs=[pl.BlockSpec((tm, half), lambda p, f, t, e, v, g: (t[p], 0)),
                  pl.BlockSpec((1, d, tf), lambda p, f, t, e, v, g: (e[p], 0, f)),
                  pl.BlockSpec((1, d, tf), lambda p, f, t, e, v, g: (e[p], 0, f)),
                  pl.BlockSpec((1, tf, d), lambda p, f, t, e, v, g: (e[p], f, 0))],
        out_specs=pl.BlockSpec((tm, d), lambda p, f, t, e, v, g: (t[p], 0)),
        scratch_shapes=[pltpu.VMEM((tm, d), F32)],
    )
    return pl.pallas_call(
        _moe_group_kernel,
        grid_spec=grid_spec,
        out_shape=jax.ShapeDtypeStruct((n_rows, d), F32),
        compiler_params=_cparams(("arbitrary", "arbitrary")),
        name="moe_group",
    )(tile, expert, valid, goff, xs, wg, wu, wd)


def _moe_combine_kernel(x_ref, yg_ref, wts_ref, g_ref, fn_ref, o_ref):
    w = wts_ref[0]
    mix = w[:, 0:1] * yg_ref[0, 0] + w[:, 1:2] * yg_ref[1, 0]
    x4 = x_ref[0] + g_ref[0] * mix
    y = x4 * lax.rsqrt(jnp.mean(x4 * x4, axis=-1, keepdims=True) + NORM_EPS)
    o_ref[0] = y * fn_ref[...]


def _moe_combine(x, yg, wts, mod, final_norm):
    bsz, seq, d = x.shape
    tm = 512
    _, _, gate = _mod_specs(2)
    tile = lambda width: pl.BlockSpec((1, tm, width), lambda b, i: (b, i, 0))
    return pl.pallas_call(
        _moe_combine_kernel,
        grid=(bsz, seq // tm),
        in_specs=[tile(d), pl.BlockSpec((2, 1, tm, d), lambda b, i: (0, b, i, 0)), tile(LANES), gate,
                  _const_spec((1, d))],
        out_specs=tile(d),
        out_shape=jax.ShapeDtypeStruct((bsz, seq, d), F32),
        compiler_params=_cparams(("parallel", "parallel")),
        name="moe_combine",
    )(x, yg, wts, mod, final_norm[None, :])


def _pad_cols(a, width):
    return jnp.pad(a, ((0, 0), (0, width - a.shape[1])))


def _split(a, sizes):
    return jnp.split(a, [int(s) for s in np.cumsum(sizes)[:-1]], axis=-1)


def _prep_hyb_w_in(w):
    q_a, k_a, v_a, lr_a, r_a, q_b, kv_b, g_b = _split(w, HYB_SPLITS)
    return jnp.concatenate([q_a, k_a, v_a, r_a, _pad_cols(lr_a, LANES), q_b, kv_b, _pad_cols(g_b, LANES)],
                           axis=1).astype(BF16)


def _prep_cmp(pe, w1, w2):
    eye = jnp.eye(NSA_KV_GROUPS, dtype=F32)
    half = CMP_BLOCK // 2
    w1r = w1.reshape(2, 2, half, NSA_HEAD_DIM, CMP_HIDDEN)
    w1x = jnp.einsum("jstdc,gh->jstgdhc", w1r, eye).reshape(2, 2, half * LANES, NSA_KV_GROUPS * CMP_HIDDEN)
    w2x = jnp.einsum("jcd,gh->jgchd", w2, eye).reshape(2, NSA_KV_GROUPS * CMP_HIDDEN, LANES)
    per = pe.reshape(2, 2, half, 1, NSA_HEAD_DIM)
    pe_rows = jnp.broadcast_to(per, (2, 2, half, NSA_KV_GROUPS, NSA_HEAD_DIM)).reshape(2, 2, 1, half * LANES)
    return pe_rows, w1x.astype(BF16), w2x.astype(BF16)


def _overlap_matrix(seq):
    n_rows = seq // CMP_STRIDE
    n = np.arange(n_rows)[:, None] * CMP_STRIDE
    s = np.arange(LANES)[None, :] * SEL_BLOCK
    ov = (n < s + SEL_BLOCK) & (n + CMP_BLOCK - 1 >= s) & (np.arange(LANES)[None, :] < seq // SEL_BLOCK)
    return jnp.asarray(ov, BF16)


def _head_expand():
    ex = np.zeros((SSM_GROUPS, LANES, SSD_GW), np.float32)
    for g in range(SSM_GROUPS):
        for j in range(SSD_HPG):
            ex[g, g * SSD_HPG + j, j * SSM_HEAD_DIM:(j + 1) * SSM_HEAD_DIM] = 1.0
    return jnp.asarray(ex, BF16)


def kernel(x, c, hyb_norm, hyb_mod_w, hyb_mod_b, hyb_w_in, gla_gk_up, gla_gk_bias, gla_out_norm, nsa_cmp_pe, nsa_cmp_w1, nsa_cmp_w2, hyb_w_out, dense_norm, dense_mod_w, dense_mod_b, dense_w_gu, dense_w_down, ssm_norm, ssm_mod_w, ssm_mod_b, ssm_w_in, ssm_conv_w, ssm_conv_b, ssm_dt_bias, ssm_a_log, ssm_d, ssm_gate_norm, ssm_w_out, moe_norm, moe_mod_w, moe_mod_b, moe_router, moe_w_gu, moe_w_down, final_norm):
    bsz, seq, d = x.shape
    mods = _adaln(c, (hyb_mod_w[0], dense_mod_w[0], ssm_mod_w[0], moe_mod_w[0]),
                  (hyb_mod_b[0], dense_mod_b[0], ssm_mod_b[0], moe_mod_b[0]))
    mods = mods.reshape(4, bsz, 1, 3 * d)

    gkup_pad = jnp.pad(gla_gk_up[0], ((0, LANES - GLA_LOWRANK), (0, 0))).astype(BF16)
    (gq, gk, la, gv, gr, nq, kcmp, vcmp, kslc, vslc, kwin, vwin, ng) = _hyb_in(
        x, hyb_norm[0], mods[0], _prep_hyb_w_in(hyb_w_in[0]), gkup_pad, gla_gk_bias[0])
    o_gla = _gla(gq, gk, la, gv, gr, gla_out_norm[0])
    pe_rows, w1x, w2x = _prep_cmp(nsa_cmp_pe[0], nsa_cmp_w1[0], nsa_cmp_w2[0])
    kc, vc = _nsa_compress(kcmp, vcmp, pe_rows, w1x, w2x)
    o_nsa = _nsa_attn(nq, ng, kc, vc, kslc, vslc, kwin, vwin, _overlap_matrix(seq))
    w_out = hyb_w_out[0].astype(BF16)
    n_gla = GLA_HEADS * GLA_DV
    w_gu = dense_w_gu[0].astype(BF16)
    x = _hyb_out_ffn(x, o_gla, o_nsa, mods[0], w_out[:n_gla], w_out[n_gla:], dense_norm[0], mods[1],
                     w_gu[:, :FFN_DENSE], w_gu[:, FFN_DENSE:], dense_w_down[0].astype(BF16))

    wz, wxbc, wdt = _split(ssm_w_in[0], SSM_SPLITS)
    w_ssm = jnp.concatenate([wz, wxbc, _pad_cols(wdt, LANES)], axis=1).astype(BF16)
    z, xbc, dt = _ssm_in(x, ssm_norm[0], mods[2], w_ssm)
    pad_heads = lambda a: jnp.pad(a, (0, LANES - SSM_HEADS))[None, :]
    y = _ssd(xbc, z, dt, ssm_conv_w[0], ssm_conv_b[0], pad_heads(ssm_dt_bias[0]), pad_heads(ssm_a_log[0]),
             jnp.repeat(ssm_d[0], SSM_HEAD_DIM)[None, :], ssm_gate_norm[0], _head_expand())
    x3, h4, gw, sel, cnt = _ssm_out_route(x, y, mods[2], ssm_w_out[0].astype(BF16), moe_norm[0], mods[3],
                                          _pad_cols(moe_router[0], LANES))
    n = bsz * seq
    cnt = cnt[:, 0, :]
    totals = jnp.sum(cnt, axis=0)
    goff_f = jnp.cumsum(totals) - totals
    base = (jnp.cumsum(cnt, axis=0) - cnt + goff_f[None, :])[:, None, :]
    goff = jnp.concatenate([goff_f[:N_EXPERTS], jnp.full((1,), 2.0 * n, F32)]).astype(jnp.int32)
    pos, wts = _moe_plan(gw.reshape(n, LANES), sel.reshape(n, LANES), base)
    pos = pos[:, 0, :]
    xs = _sc_dispatch(h4.reshape(n, d // 2), pos)
    w_gu = moe_w_gu[0].astype(BF16)
    ys = _moe_group(xs, goff, w_gu[:, :, :FFN_EXPERT], w_gu[:, :, FFN_EXPERT:], moe_w_down[0].astype(BF16))
    yg = _sc_gather(ys, pos)
    return _moe_combine(x3, yg.reshape(2, bsz, seq, d), wts.reshape(bsz, seq, LANES), mods[3], final_norm)
```

```python
import functools

import jax
import jax.numpy as jnp
import numpy as np
from jax import lax
from jax.experimental import pallas as pl
from jax.experimental.pallas import tpu as pltpu
from jax.experimental.pallas import tpu_sc as plsc

F32 = jnp.float32
BF16 = jnp.bfloat16

D_MODEL = 1024
NORM_EPS = 1e-6
GLA_HEADS = 4
GLA_DV = D_MODEL // 8
GLA_DK = GLA_DV // 2
GLA_LOWRANK = 16
GLA_TAU = 16.0
GLA_CHUNK = 64
NSA_HEADS = 8
NSA_KV_GROUPS = 2
NSA_HEAD_DIM = D_MODEL // 16
CMP_BLOCK = 32
CMP_STRIDE = 16
CMP_HIDDEN = 4 * NSA_HEAD_DIM
SEL_BLOCK = 64
N_SELECT = 8
WINDOW = 512
D_INNER = 2 * D_MODEL
SSM_HEAD_DIM = 64
SSM_HEADS = D_INNER // SSM_HEAD_DIM
SSM_GROUPS = 4
SSM_STATE = 128
CONV_K = 4
SSM_CHUNK = 64
CONV_CH = D_INNER + 2 * SSM_GROUPS * SSM_STATE
FFN_DENSE = ((8 * D_MODEL // 3 + 127) // 128) * 128
N_EXPERTS = 8
FFN_EXPERT = 7 * D_MODEL // 2
HYB_SPLITS = (GLA_HEADS * GLA_DK, GLA_HEADS * GLA_DK, GLA_HEADS * GLA_DV, GLA_LOWRANK, GLA_HEADS * GLA_DV,
              NSA_HEADS * NSA_HEAD_DIM, 6 * NSA_KV_GROUPS * NSA_HEAD_DIM, 3 * NSA_HEADS)
SSM_SPLITS = (D_INNER, CONV_CH, SSM_HEADS)

LANES = 128
VMEM_LIMIT = 56 * 1024 * 1024

NEG_BIG = -1e30


def _cparams(sem):
    return pltpu.CompilerParams(dimension_semantics=sem, vmem_limit_bytes=VMEM_LIMIT)


def _dot(a, b):
    return jnp.dot(a, b, preferred_element_type=F32)


def _dot_nt(a, b):
    return lax.dot_general(a, b, (((1,), (1,)), ((), ())), preferred_element_type=F32)


def _dot_tn(a, b):
    return lax.dot_general(a, b, (((0,), (0,)), ((), ())), preferred_element_type=F32)


def _split_bf16(a, parts):
    out = []
    r = a
    for _ in range(parts):
        p = r.astype(BF16)
        out.append(p)
        r = r - p.astype(F32)
    return out


def _dot_exact_lhs(m, a, parts=3):
    acc = None
    for p in _split_bf16(a, parts):
        t = _dot(m, p)
        acc = t if acc is None else acc + t
    return acc


def _dot_exact_rhs(a, m, parts=3):
    acc = None
    for p in _split_bf16(a, parts):
        t = _dot(p, m)
        acc = t if acc is None else acc + t
    return acc


def _sigmoid(x):
    return 1.0 / (1.0 + jnp.exp(-x))


def _silu(x):
    return x * _sigmoid(x)


def _softplus(x):
    return jnp.maximum(x, 0.0) + jnp.log(1.0 + jnp.exp(-jnp.abs(x)))


def _gelu_tanh(x):
    return x * (0.5 * (1.0 + jnp.tanh(0.7978845608028654 * (x + 0.044715 * (x * x * x)))))


def _modulated_norm(x, nw, shift, scale):
    ms = jnp.mean(x * x, axis=-1, keepdims=True)
    y = x * lax.rsqrt(ms + NORM_EPS)
    return (y * nw) * (1.0 + scale) + shift


def _const_spec(shape):
    nd = len(shape)
    return pl.BlockSpec(shape, lambda *_: (0,) * nd, pipeline_mode=pl.Buffered(1))


def _adaln_kernel(c_ref, w0, w1, w2, w3, b0, b1, b2, b3, o_ref):
    sc = _silu(c_ref[...]).astype(BF16)
    for i, (w, b) in enumerate(((w0, b0), (w1, b1), (w2, b2), (w3, b3))):
        o_ref[i] = _dot(sc, w[...].astype(BF16)) + b[...]


def _adaln(c, ws, bs):
    bsz, d = c.shape
    n = ws[0].shape[1]
    tn = 512
    w_spec = pl.BlockSpec((d, tn), lambda j: (0, j))
    b_spec = pl.BlockSpec((1, tn), lambda j: (0, j))
    return pl.pallas_call(
        _adaln_kernel,
        grid=(n // tn,),
        in_specs=[pl.BlockSpec((bsz, d), lambda j: (0, 0))] + [w_spec] * 4 + [b_spec] * 4,
        out_specs=pl.BlockSpec((4, bsz, tn), lambda j: (0, 0, j)),
        out_shape=jax.ShapeDtypeStruct((4, bsz, n), F32),
        compiler_params=_cparams(("arbitrary",)),
        name="adaln",
    )(c, *ws, *[b[None, :] for b in bs])


def _mod_specs(tile_axes):
    def spec(part):
        if tile_axes == 2:
            return pl.BlockSpec((1, 1, D_MODEL), lambda b, i: (b, 0, part))
        return pl.BlockSpec((1, 1, D_MODEL), lambda b, i, j, k: (b, 0, part))
    return spec(0), spec(1), spec(2)


HYB_SEG = {
    "gq": (0, 256), "gk": (256, 256), "gv": (512, 512), "gr": (1024, 512), "lr": (1536, 128),
    "nq": (1664, 512), "kv": (2176, 768), "ng": (2944, 128),
}
HYB_COLS = 3072


def _hyb_in_kernel(x_ref, nw_ref, shift_ref, scale_ref, w_ref, gkup_ref, gkb_ref,
                   gq_ref, gk_ref, la_ref, gv_ref, gr_ref, nq_ref,
                   kcmp_ref, vcmp_ref, kslc_ref, vslc_ref, kwin_ref, vwin_ref, ng_ref):
    h = _modulated_norm(x_ref[0], nw_ref[...], shift_ref[0], scale_ref[0]).astype(BF16)

    def proj(name, sub=None):
        off, width = HYB_SEG[name]
        if sub is not None:
            off, width = off + sub * LANES, LANES
        return _dot(h, w_ref[:, off:off + width])

    gq_ref[0] = proj("gq")
    gk_ref[0] = proj("gk")
    gv_ref[0] = proj("gv").astype(BF16)
    gr_ref[0] = proj("gr")
    lr = proj("lr").astype(BF16)
    z = _dot(lr, gkup_ref[...]) + gkb_ref[...]
    la_ref[0] = -_softplus(-z) * (1.0 / GLA_TAU)
    nq_ref[0] = proj("nq").astype(BF16)
    kcmp_ref[0] = proj("kv", 0)
    vcmp_ref[0] = proj("kv", 1)
    kslc_ref[0] = proj("kv", 2).T.astype(BF16)
    vslc_ref[0] = proj("kv", 3).astype(BF16)
    kwin_ref[0] = proj("kv", 4).T.astype(BF16)
    vwin_ref[0] = proj("kv", 5).astype(BF16)
    ng_ref[0] = proj("ng")


def _hyb_in(x, norm_w, mod, w_perm, gkup_pad, gk_bias):
    bsz, seq, d = x.shape
    tm = 512
    shift_s, scale_s, _ = _mod_specs(2)

    def out(width, dtype):
        return (pl.BlockSpec((1, tm, width), lambda b, i: (b, i, 0)),
                jax.ShapeDtypeStruct((bsz, seq, width), dtype))

    def out_t(dtype):
        return (pl.BlockSpec((1, LANES, tm), lambda b, i: (b, 0, i)),
                jax.ShapeDtypeStruct((bsz, LANES, seq), dtype))

    outs = [out(256, F32), out(256, F32), out(256, F32), out(512, BF16), out(512, F32), out(512, BF16),
            out(128, F32), out(128, F32), out_t(BF16), out(128, BF16), out_t(BF16), out(128, BF16),
            out(128, F32)]
    return pl.pallas_call(
        _hyb_in_kernel,
        grid=(bsz, seq // tm),
        in_specs=[pl.BlockSpec((1, tm, d), lambda b, i: (b, i, 0)),
                  _const_spec((1, d)), shift_s, scale_s,
                  _const_spec(w_perm.shape), _const_spec(gkup_pad.shape), _const_spec((1, 256))],
        out_specs=[o[0] for o in outs],
        out_shape=[o[1] for o in outs],
        compiler_params=_cparams(("parallel", "parallel")),
        name="hyb_in",
    )(x, norm_w[None, :], mod, mod, w_perm, gkup_pad, gk_bias[None, :])


GLA_ROWS = 512


def _gla_kernel(q_ref, k_ref, la_ref, v_ref, r_ref, nw_ref, o_ref, *, seq):
    rb = GLA_ROWS
    ncb = rb // GLA_CHUNK
    row = lax.broadcasted_iota(jnp.int32, (rb, rb), 0)
    col = lax.broadcasted_iota(jnp.int32, (rb, rb), 1)
    chunk_causal = ((row // GLA_CHUNK) == (col // GLA_CHUNK)) & (col <= row)
    tril_bd = jnp.where(chunk_causal, 1.0, 0.0).astype(BF16)
    lane = lax.broadcasted_iota(jnp.int32, (rb, LANES), 1)
    head_mask = (lane < GLA_DK, lane >= GLA_DK)
    nw = nw_ref[...]

    def body(i, st):
        r0 = pl.multiple_of(i * rb, rb)
        rows = pl.ds(r0, rb)
        q = q_ref[0, rows, :]
        k = k_ref[0, rows, :]
        la = la_ref[0, rows, :]
        b = _dot_exact_lhs(tril_bd, la)
        b3 = b.reshape(ncb, GLA_CHUNK, LANES)
        bend3 = b3[:, GLA_CHUNK - 1:GLA_CHUNK, :]
        bend = jnp.broadcast_to(bend3, (ncb, GLA_CHUNK, LANES)).reshape(rb, LANES)
        q_dec = (q * GLA_DK ** -0.5) * jnp.exp(b)
        k_dec = (k * jnp.exp(-b)).astype(BF16)
        k_end = k * jnp.exp(bend - b)
        decay = jnp.exp(bend3)

        qm, vh, o_intra, upd = [], [], [], []
        for h in range(2):
            qm_h = jnp.where(head_mask[h], q_dec, 0.0).astype(BF16)
            att = _dot_nt(qm_h, k_dec)
            att = jnp.where(chunk_causal, att, 0.0).astype(BF16)
            v_h = v_ref[0, rows, h * GLA_DV:(h + 1) * GLA_DV]
            o_intra.append(_dot(att, v_h))
            km_h = jnp.where(head_mask[h], k_end, 0.0).astype(BF16)
            upd.append([_dot_tn(v_h[n * GLA_CHUNK:(n + 1) * GLA_CHUNK], km_h[n * GLA_CHUNK:(n + 1) * GLA_CHUNK])
                        for n in range(ncb)])
            qm.append(qm_h)
            vh.append(v_h)

        prev = []
        for n in range(ncb):
            prev.append(st.astype(BF16))
            st = st * decay[n] + (upd[0][n] + upd[1][n])

        for h in range(2):
            o_inter = jnp.concatenate(
                [_dot_nt(qm[h][n * GLA_CHUNK:(n + 1) * GLA_CHUNK], prev[n]) for n in range(ncb)], axis=0)
            o = o_intra[h] + o_inter
            y = o * lax.rsqrt(jnp.mean(o * o, axis=-1, keepdims=True) + NORM_EPS) * nw
            r = r_ref[0, rows, h * GLA_DV:(h + 1) * GLA_DV]
            o_ref[0, rows, h * GLA_DV:(h + 1) * GLA_DV] = (y * _silu(r)).astype(o_ref.dtype)
        return st

    lax.fori_loop(0, seq // rb, body, jnp.zeros((GLA_DV, 2 * GLA_DK), F32))


def _gla(gq, gk, la, gv, gr, out_norm):
    bsz, seq, _ = gq.shape
    qk_spec = pl.BlockSpec((1, seq, 2 * GLA_DK), lambda b, p: (b, 0, p))
    v_spec = pl.BlockSpec((1, seq, 2 * GLA_DV), lambda b, p: (b, 0, p))
    return pl.pallas_call(
        functools.partial(_gla_kernel, seq=seq),
        grid=(bsz, GLA_HEADS // 2),
        in_specs=[qk_spec, qk_spec, qk_spec, v_spec, v_spec, _const_spec((1, GLA_DV))],
        out_specs=v_spec,
        out_shape=jax.ShapeDtypeStruct((bsz, seq, GLA_HEADS * GLA_DV), BF16),
        compiler_params=_cparams(("parallel", "parallel")),
        name="gla",
    )(gq, gk, la, gv, gr, out_norm[None, :])


def _nsa_compress_kernel(k_ref, v_ref, pe_ref, w1_ref, w2_ref, kc_ref, vc_ref, *, n_cmp):
    rows = k_ref.shape[1]
    rid = lax.broadcasted_iota(jnp.int32, (rows, LANES), 0)
    for j, (src, dst) in enumerate(((k_ref, kc_ref), (v_ref, vc_ref))):
        x = src[0]
        lo = _dot((x + pe_ref[j, 0]).astype(BF16), w1_ref[j, 0])
        hi = _dot((x + pe_ref[j, 1]).astype(BF16), w1_ref[j, 1])
        hpre = lo + pltpu.roll(hi, rows - 1, 0)
        out = jnp.where(rid < n_cmp, _dot(_gelu_tanh(hpre).astype(BF16), w2_ref[j]), 0.0)
        dst[0] = (out.T if j == 0 else out).astype(dst.dtype)


def _nsa_compress(kcmp, vcmp, pe_rows, w1x, w2x):
    bsz, seq, _ = kcmp.shape
    rows = seq // CMP_STRIDE
    width = CMP_STRIDE * LANES
    n_cmp = (seq - CMP_BLOCK) // CMP_STRIDE + 1
    x_spec = pl.BlockSpec((1, rows, width), lambda b: (b, 0, 0))
    return pl.pallas_call(
        functools.partial(_nsa_compress_kernel, n_cmp=n_cmp),
        grid=(bsz,),
        in_specs=[x_spec, x_spec, _const_spec(pe_rows.shape), _const_spec(w1x.shape), _const_spec(w2x.shape)],
        out_specs=[pl.BlockSpec((1, LANES, rows), lambda b: (b, 0, 0)),
                   pl.BlockSpec((1, rows, LANES), lambda b: (b, 0, 0))],
        out_shape=[jax.ShapeDtypeStruct((bsz, LANES, rows), BF16), jax.ShapeDtypeStruct((bsz, rows, LANES), BF16)],
        compiler_params=_cparams(("parallel",)),
        name="nsa_compress",
    )(kcmp.reshape(bsz, rows, width), vcmp.reshape(bsz, rows, width), pe_rows, w1x, w2x)


NSA_QT = 128
NSA_KT = 512
NSA_ROWS = NSA_HEADS * NSA_QT


NSA_MASK_PENALTY = 1e30
NSA_LANE_POS_HI, NSA_LANE_POS_LO, NSA_LANE_CMP_IDX, NSA_LANE_CMP_ONE = 64, 65, 66, 67


def _nsa_attn_kernel(q_ref, g_ref, kc_ref, vc_ref, ks_ref, vs_ref, kw_ref, vw_ref, ov_ref, kf_ref, cf_ref, o_ref,
                     m_ref, acc_ref, *, seq):
    qt, kt = NSA_QT, NSA_KT
    heads_per_group = NSA_HEADS // NSA_KV_GROUPS
    q0 = pl.program_id(1) * qt
    t = q0 + lax.broadcasted_iota(jnp.int32, (qt, 1), 0)
    lane = lax.broadcasted_iota(jnp.int32, (qt, LANES), 1)
    group_lanes = (lane < NSA_HEAD_DIM, lane >= NSA_HEAD_DIM)
    slopes = [2.0 ** (-(h + 1)) for h in range(NSA_HEADS)]

    qa = q_ref[0].astype(F32) * NSA_HEAD_DIM ** -0.5
    q_rows = []
    for h in range(NSA_HEADS):
        g = h // heads_per_group
        blk = qa[:, (h // 2) * LANES:(h // 2 + 1) * LANES]
        if h % 2 != g:
            blk = pltpu.roll(blk, NSA_HEAD_DIM, 1)
        q_rows.append(jnp.where(group_lanes[g], blk, 0.0).astype(BF16))
    q_ext = jnp.concatenate(q_rows, axis=0)

    def alibi_lanes(h):
        s = slopes[h]
        return jnp.where(lane == NSA_LANE_POS_HI, SEL_BLOCK * s,
                         jnp.where(lane == NSA_LANE_POS_LO, s,
                                   jnp.where(lane == NSA_LANE_CMP_IDX, CMP_STRIDE * s,
                                             jnp.where(lane == NSA_LANE_CMP_ONE, (CMP_BLOCK - 1) / 2.0 * s, 0.0))))

    q_pos = [alibi_lanes(h) for h in range(NSA_HEADS)]
    q_full = jnp.concatenate([q_ext, jnp.concatenate(q_pos, axis=0).astype(BF16)], axis=1)

    def softmax_parts(s, mask):
        s = jnp.where(mask, s, -jnp.inf)
        m = jnp.max(s, axis=-1, keepdims=True)
        m = jnp.where(m == -jnp.inf, 0.0, m)
        p = jnp.exp(s - m)
        return p, jnp.sum(p, axis=-1, keepdims=True)

    n_rows = kc_ref.shape[2]
    cidx = lax.broadcasted_iota(jnp.int32, (1, n_rows), 1)
    cmp_mask = cidx * CMP_STRIDE + (CMP_BLOCK - 1) <= t
    s_all = _dot(q_full, jnp.concatenate([kc_ref[0], cf_ref[...]], axis=0))
    p_list, psum = [], [None] * NSA_KV_GROUPS
    for h in range(NSA_HEADS):
        g = h // heads_per_group
        p, l = softmax_parts(s_all[h * qt:(h + 1) * qt], cmp_mask)
        p = p / jnp.maximum(l, 1e-20)
        p_list.append(p.astype(BF16))
        psum[g] = p if psum[g] is None else psum[g] + p
    o_cmp = _dot(jnp.concatenate(p_list, axis=0), vc_ref[0])

    n_blk = LANES // 2
    tq = q0 + lax.broadcasted_iota(jnp.int32, (n_blk, qt), 1)
    bidx = lax.broadcasted_iota(jnp.int32, (n_blk, qt), 0)
    bidx_f = bidx.astype(F32)
    blk_t = tq // SEL_BLOCK
    forced = (bidx == 0) | (bidx == blk_t) | (bidx == blk_t - 1)
    future = bidx * SEL_BLOCK > tq
    q_sel_rows = []
    for g in range(NSA_KV_GROUPS):
        imp = _dot_exact_rhs(psum[g], ov_ref[...]).T[:n_blk]
        v = jnp.where(future, -1.0, jnp.where(forced, 3e38, imp))
        chosen = jnp.zeros((n_blk, qt), F32)
        for _ in range(min(N_SELECT, seq // SEL_BLOCK)):
            m = jnp.max(v, axis=0, keepdims=True)
            idx = jnp.min(jnp.where(v == m, bidx_f, float(LANES)), axis=0, keepdims=True)
            pick = (bidx_f == idx) & (m >= 0.0)
            chosen = jnp.where(pick, 1.0, chosen)
            v = jnp.where(pick, -1.0, v)
        penalty = jnp.concatenate([(chosen - 1.0) * NSA_MASK_PENALTY, jnp.zeros((n_blk, qt), F32)], axis=0).T
        for r in range(heads_per_group):
            q_sel_rows.append((penalty + q_pos[g * heads_per_group + r]).astype(BF16))
    q_full_sel = jnp.concatenate([q_ext, jnp.concatenate(q_sel_rows, axis=0)], axis=1)

    vlane = lax.broadcasted_iota(jnp.int32, (1, LANES), 1)
    own_lanes = (vlane < NSA_HEAD_DIM, vlane >= NSA_HEAD_DIM)

    def weighted_values(p_rows, v):
        v = v.astype(F32)
        outs = []
        for g in range(NSA_KV_GROUPS):
            pg = jnp.concatenate(p_rows[g * heads_per_group:(g + 1) * heads_per_group], axis=0)
            outs.append(_dot(pg, jnp.where(own_lanes[g], v, 1.0).astype(BF16)))
        return jnp.concatenate(outs, axis=0)

    def normalised(acc):
        return acc / jnp.maximum(pltpu.roll(acc, NSA_HEAD_DIM, 1), 1e-20)

    m_ref[...] = jnp.full(m_ref.shape, -jnp.inf, F32)
    acc_ref[...] = jnp.zeros(acc_ref.shape, F32)
    kpos = lax.broadcasted_iota(jnp.int32, (1, kt), 1)

    def sel_tile(j, causal):
        k0 = pl.multiple_of(j * kt, kt)
        ks = jnp.concatenate([ks_ref[0, :, pl.ds(k0, kt)], kf_ref[:, pl.ds(k0, kt)]], axis=0)
        s_t = _dot(q_full_sel, ks)
        visible = (k0 + kpos) <= t
        p_rows = []
        for h in range(NSA_HEADS):
            rows = slice(h * qt, (h + 1) * qt)
            s = s_t[rows]
            if causal:
                s = jnp.where(visible, s, -jnp.inf)
            m_old = m_ref[rows]
            m_new = jnp.maximum(m_old, jnp.max(s, axis=-1, keepdims=True))
            m_safe = jnp.where(m_new == -jnp.inf, 0.0, m_new)
            p_rows.append(jnp.exp(s - jnp.concatenate([m_safe] * (kt // LANES), axis=1)).astype(BF16))
            acc_ref[rows] = jnp.exp(m_old - m_safe) * acc_ref[rows]
            m_ref[rows] = m_new
        acc_ref[...] += weighted_values(p_rows, vs_ref[0, pl.ds(k0, kt), :])

    def sel_past_tile(j, carry):
        sel_tile(j, False)
        return carry

    last = q0 // kt
    lax.fori_loop(0, last, sel_past_tile, 0)
    sel_tile(last, True)
    o_slc = normalised(acc_ref[...])

    wk = WINDOW + qt
    w0 = pl.multiple_of(jnp.maximum(q0 - WINDOW, 0), qt)
    kw = jnp.concatenate([kw_ref[0, :, pl.ds(w0, wk)], kf_ref[:, pl.ds(w0, wk)]], axis=0)
    wdist = t - (w0 + lax.broadcasted_iota(jnp.int32, (1, wk), 1))
    wmask = (wdist >= 0) & (wdist < WINDOW)
    s_all = _dot(q_full, kw)
    p_rows = []
    for h in range(NSA_HEADS):
        s = jnp.where(wmask, s_all[h * qt:(h + 1) * qt], -jnp.inf)
        m = jnp.max(s, axis=-1, keepdims=True)
        p_rows.append(jnp.exp(s - jnp.where(m == -jnp.inf, 0.0, m)).astype(BF16))
    o_win = normalised(weighted_values(p_rows, vw_ref[0, pl.ds(w0, wk), :]))

    gates = _sigmoid(g_ref[0])
    o_heads = []
    for h in range(NSA_HEADS):
        rows = slice(h * qt, (h + 1) * qt)
        o_heads.append(gates[:, 3 * h:3 * h + 1] * o_cmp[rows] + gates[:, 3 * h + 1:3 * h + 2] * o_slc[rows]
                       + gates[:, 3 * h + 2:3 * h + 3] * o_win[rows])
    for c in range(NSA_HEADS // 2):
        g = (2 * c) // heads_per_group
        a, b = o_heads[2 * c], o_heads[2 * c + 1]
        if g == 0:
            blk = jnp.where(group_lanes[0], a, pltpu.roll(b, NSA_HEAD_DIM, 1))
        else:
            blk = jnp.where(group_lanes[0], pltpu.roll(a, NSA_HEAD_DIM, 1), b)
        o_ref[0, :, c * LANES:(c + 1) * LANES] = blk.astype(o_ref.dtype)


def _nsa_position_features(seq):
    assert seq // SEL_BLOCK <= LANES // 2
    pos = np.arange(seq)
    kf = np.zeros((seq, LANES), np.float32)
    kf[pos, pos // SEL_BLOCK] = 1.0
    kf[:, NSA_LANE_POS_HI] = pos // SEL_BLOCK
    kf[:, NSA_LANE_POS_LO] = pos % SEL_BLOCK
    n_rows = seq // CMP_STRIDE
    cf = np.zeros((n_rows, LANES), np.float32)
    cf[:, NSA_LANE_CMP_IDX] = np.arange(n_rows)
    cf[:, NSA_LANE_CMP_ONE] = 1.0
    return jnp.asarray(kf.T, BF16), jnp.asarray(cf.T, BF16)


def _nsa_attn(nq, ng, kc, vc, kslc, vslc, kwin, vwin, overlap):
    bsz, seq, _ = nq.shape
    n_rows = vc.shape[1]
    kfeat, cfeat = _nsa_position_features(seq)
    full = lambda rows: pl.BlockSpec((1, rows, LANES), lambda b, i: (b, 0, 0))
    full_t = lambda cols: pl.BlockSpec((1, LANES, cols), lambda b, i: (b, 0, 0))
    return pl.pallas_call(
        functools.partial(_nsa_attn_kernel, seq=seq),
        grid=(bsz, seq // NSA_QT),
        in_specs=[pl.BlockSpec((1, NSA_QT, NSA_HEADS * NSA_HEAD_DIM), lambda b, i: (b, i, 0)),
                  pl.BlockSpec((1, NSA_QT, LANES), lambda b, i: (b, i, 0)),
                  full_t(n_rows), full(n_rows), full_t(seq), full(seq), full_t(seq), full(seq),
                  _const_spec(overlap.shape), _const_spec(kfeat.shape), _const_spec(cfeat.shape)],
        out_specs=pl.BlockSpec((1, NSA_QT, NSA_HEADS * NSA_HEAD_DIM), lambda b, i: (b, i, 0)),
        out_shape=jax.ShapeDtypeStruct((bsz, seq, NSA_HEADS * NSA_HEAD_DIM), BF16),
        scratch_shapes=[pltpu.VMEM((NSA_ROWS, LANES), F32), pltpu.VMEM((NSA_ROWS, LANES), F32)],
        compiler_params=_cparams(("parallel", "arbitrary")),
        name="nsa_attn",
    )(nq, ng, kc, vc, kslc, vslc, kwin, vwin, overlap, kfeat, cfeat)


FFN_CHUNK = 1408


def _hyb_out_ffn_kernel(x_ref, oa_ref, ob_ref, g1_ref, wa_ref, wb_ref,
                        nw_ref, shift_ref, scale_ref, g2_ref, wg_ref, wu_ref, wd_ref, o_ref, act_ref):
    mix = _dot(oa_ref[0], wa_ref[...]) + _dot(ob_ref[0], wb_ref[...])
    x1 = x_ref[0] + g1_ref[0] * mix
    h = _modulated_norm(x1, nw_ref[...], shift_ref[0], scale_ref[0]).astype(BF16)
    for f in range(0, FFN_DENSE, FFN_CHUNK):
        gate = _dot(h, wg_ref[:, f:f + FFN_CHUNK])
        up = _dot(h, wu_ref[:, f:f + FFN_CHUNK])
        act_ref[:, f:f + FFN_CHUNK] = (_silu(gate) * up).astype(BF16)
    o_ref[0] = x1 + g2_ref[0] * _dot(act_ref[...], wd_ref[...])


def _hyb_out_ffn(x, o_gla, o_nsa, mod1, wa, wb, norm_w, mod2, wg, wu, wd):
    bsz, seq, d = x.shape
    tm = 512
    _, _, gate1 = _mod_specs(2)
    shift2, scale2, gate2 = _mod_specs(2)
    tile = lambda width: pl.BlockSpec((1, tm, width), lambda b, i: (b, i, 0))
    return pl.pallas_call(
        _hyb_out_ffn_kernel,
        grid=(bsz, seq // tm),
        in_specs=[tile(d), tile(o_gla.shape[-1]), tile(o_nsa.shape[-1]), gate1,
                  _const_spec(wa.shape), _const_spec(wb.shape), _const_spec((1, d)), shift2, scale2, gate2,
                  _const_spec(wg.shape), _const_spec(wu.shape), _const_spec(wd.shape)],
        out_specs=tile(d),
        out_shape=jax.ShapeDtypeStruct((bsz, seq, d), F32),
        scratch_shapes=[pltpu.VMEM((tm, FFN_DENSE), BF16)],
        compiler_params=_cparams(("parallel", "parallel")),
        name="hyb_out_ffn",
    )(x, o_gla, o_nsa, mod1, wa, wb, norm_w[None, :], mod2, mod2, mod2, wg, wu, wd)


SSM_DT_OFF = D_INNER + CONV_CH
SSM_COLS = SSM_DT_OFF + LANES
SSM_IN_CHUNK = 1024


def _ssm_in_kernel(x_ref, nw_ref, shift_ref, scale_ref, w_ref, z_ref, xbc_ref, dt_ref):
    h = _modulated_norm(x_ref[0], nw_ref[...], shift_ref[0], scale_ref[0]).astype(BF16)
    for c in range(0, D_INNER, SSM_IN_CHUNK):
        z_ref[0, :, c:c + SSM_IN_CHUNK] = _dot(h, w_ref[:, c:c + SSM_IN_CHUNK])
    for c in range(0, CONV_CH, SSM_IN_CHUNK):
        xbc_ref[0, :, c:c + SSM_IN_CHUNK] = _dot(h, w_ref[:, D_INNER + c:D_INNER + c + SSM_IN_CHUNK])
    dt_ref[0] = _dot(h, w_ref[:, SSM_DT_OFF:SSM_COLS])


def _ssm_in(x, norm_w, mod, w_perm):
    bsz, seq, d = x.shape
    tm = 512
    shift_s, scale_s, _ = _mod_specs(2)
    tile = lambda width: pl.BlockSpec((1, tm, width), lambda b, i: (b, i, 0))
    return pl.pallas_call(
        _ssm_in_kernel,
        grid=(bsz, seq // tm),
        in_specs=[tile(d), _const_spec((1, d)), shift_s, scale_s, _const_spec(w_perm.shape)],
        out_specs=[tile(D_INNER), tile(CONV_CH), tile(LANES)],
        out_shape=[jax.ShapeDtypeStruct((bsz, seq, D_INNER), F32),
                   jax.ShapeDtypeStruct((bsz, seq, CONV_CH), F32),
                   jax.ShapeDtypeStruct((bsz, seq, LANES), F32)],
        compiler_params=_cparams(("parallel", "parallel")),
        name="ssm_in",
    )(x, norm_w[None, :], mod, mod, w_perm)


SSD_ROWS = 512
SSD_GW = D_INNER // SSM_GROUPS
SSD_HPG = SSM_HEADS // SSM_GROUPS
CONV_TAIL = 8


def _ssd_kernel(xbc_ref, z_ref, dt_ref, cw_ref, cb_ref, dtb_ref, alog_ref, dskip_ref, nw_ref, ex_ref, o_ref,
                tail_ref, state_ref, xdt_ref, cum_ref, b_ref, c_ref, y_ref):
    rb, q = SSD_ROWS, SSM_CHUNK
    nchunk = rb // q

    @pl.when(pl.program_id(1) == 0)
    def _():
        tail_ref[...] = jnp.zeros(tail_ref.shape, F32)
        state_ref[...] = jnp.zeros(state_ref.shape, F32)

    row8 = lax.broadcasted_iota(jnp.int32, (CONV_TAIL, 1), 0)

    def conv_silu(c0, width):
        x = xbc_ref[0, :, c0:c0 + width]
        tail = tail_ref[:, c0:c0 + width]
        acc = x * cw_ref[CONV_K - 1:CONV_K, c0:c0 + width] + cb_ref[:, c0:c0 + width]
        for j in range(1, CONV_K):
            head = jnp.where(row8 < j, pltpu.roll(tail, j, 0), pltpu.roll(x[:CONV_TAIL], j, 0))
            xs = jnp.concatenate([head, xbc_ref[0, CONV_TAIL - j:rb - j, c0:c0 + width]], axis=0)
            acc = acc + xs * cw_ref[CONV_K - 1 - j:CONV_K - j, c0:c0 + width]
        return _silu(acc)

    dt = _softplus(dt_ref[0] + dtb_ref[...])
    a = dt * (-jnp.exp(alog_ref[...]))
    row = lax.broadcasted_iota(jnp.int32, (rb, rb), 0)
    col = lax.broadcasted_iota(jnp.int32, (rb, rb), 1)
    tril_bd = jnp.where(((row // q) == (col // q)) & (col <= row), 1.0, 0.0).astype(BF16)
    cum = _dot_exact_lhs(tril_bd, a)

    lrow = lax.broadcasted_iota(jnp.int32, (q, SSD_GW), 0)
    lcol = lax.broadcasted_iota(jnp.int32, (q, SSD_GW), 1) % q
    causal_t = lcol <= lrow
    eye_t = lcol == lrow
    half = SSD_GW // 2
    brow = lax.broadcasted_iota(jnp.int32, (half, half), 0) // q
    bcol = lax.broadcasted_iota(jnp.int32, (half, half), 1) // SSM_HEAD_DIM
    same_head = brow == bcol

    for g in range(SSM_GROUPS):
        xg = conv_silu(g * SSD_GW, SSD_GW)
        y_ref[g] = dskip_ref[:, g * SSD_GW:(g + 1) * SSD_GW] * xg
        b_ref[g] = conv_silu(D_INNER + g * SSM_STATE, SSM_STATE).astype(BF16)
        c_ref[g] = conv_silu(D_INNER + SSM_GROUPS * SSM_STATE + g * SSM_STATE, SSM_STATE).astype(BF16)
        xdt_ref[g] = xg * _dot_exact_rhs(dt, ex_ref[g])
        cum_ref[g] = _dot_exact_rhs(cum, ex_ref[g])

    def chunk(n, carry):
        rows = pl.ds(pl.multiple_of(n * q, q), q)
        for g in range(SSM_GROUPS):
            cum_c = cum_ref[g, rows, :]
            cum_s = jnp.sum(jnp.where(eye_t, cum_c, 0.0), axis=0, keepdims=True)
            decay_l = jnp.where(causal_t, jnp.exp(cum_c - cum_s), 0.0)
            bc = b_ref[g, rows, :]
            cc = c_ref[g, rows, :]
            cb_t = _dot_nt(cc, jnp.concatenate([bc] * SSD_HPG, axis=0))
            mat = (cb_t * decay_l).astype(BF16)
            xdt_c = xdt_ref[g, rows, :]
            xdt_b = xdt_c.astype(BF16)
            y_diag = []
            for s in range(2):
                blk = xdt_b[:, s * half:(s + 1) * half]
                bd = jnp.where(same_head, jnp.concatenate([blk] * (half // q), axis=0), 0.0).astype(BF16)
                y_diag.append(_dot(mat[:, s * half:(s + 1) * half], bd))
            y = jnp.concatenate(y_diag, axis=1)
            cum_end = cum_c[q - 1:q, :]
            st = state_ref[g]
            y = y + _dot(cc, st.astype(BF16)) * jnp.exp(cum_c)
            x_end = (xdt_c * jnp.exp(cum_end - cum_c)).astype(BF16)
            state_ref[g] = st * jnp.exp(cum_end) + _dot_tn(bc, x_end)
            y_ref[g, rows, :] += y
        return carry

    lax.fori_loop(0, nchunk, chunk, 0)

    for g in range(SSM_GROUPS):
        cols = slice(g * SSD_GW, (g + 1) * SSD_GW)
        y = y_ref[g] * _silu(z_ref[0, :, cols])
        y = y * lax.rsqrt(jnp.mean(y * y, axis=-1, keepdims=True) + NORM_EPS) * nw_ref[:, cols]
        o_ref[0, :, cols] = y.astype(o_ref.dtype)

    tail_ref[...] = xbc_ref[0, rb - CONV_TAIL:rb, :]


def _ssd(xbc, z, dt, conv_w, conv_b, dt_bias_pad, a_log_pad, d_skip_x, norm_w, expand):
    bsz, seq, _ = xbc.shape
    rb = SSD_ROWS
    tile = lambda width: pl.BlockSpec((1, rb, width), lambda b, i: (b, i, 0))
    return pl.pallas_call(
        _ssd_kernel,
        grid=(bsz, seq // rb),
        in_specs=[tile(CONV_CH), tile(D_INNER), tile(LANES),
                  _const_spec((CONV_K, CONV_CH)), _const_spec((1, CONV_CH)),
                  _const_spec((1, LANES)), _const_spec((1, LANES)),
                  _const_spec((1, D_INNER)), _const_spec((1, D_INNER)), _const_spec(expand.shape)],
        out_specs=tile(D_INNER),
        out_shape=jax.ShapeDtypeStruct((bsz, seq, D_INNER), BF16),
        scratch_shapes=[pltpu.VMEM((CONV_TAIL, CONV_CH), F32),
                        pltpu.VMEM((SSM_GROUPS, SSM_STATE, SSD_GW), F32),
                        pltpu.VMEM((SSM_GROUPS, rb, SSD_GW), F32), pltpu.VMEM((SSM_GROUPS, rb, SSD_GW), F32),
                        pltpu.VMEM((SSM_GROUPS, rb, SSM_STATE), BF16), pltpu.VMEM((SSM_GROUPS, rb, SSM_STATE), BF16),
                        pltpu.VMEM((SSM_GROUPS, rb, SSD_GW), F32)],
        compiler_params=_cparams(("parallel", "arbitrary")),
        name="ssd",
    )(xbc, z, dt, conv_w, conv_b[None, :], dt_bias_pad, a_log_pad, d_skip_x, norm_w[None, :], expand)


def _pack_bf16_pairs(a):
    w = a.shape[1] // 2
    bits = lax.bitcast_convert_type(a.astype(jnp.bfloat16).astype(F32), jnp.uint32)
    packed = bits[:, w:] | (bits[:, :w] >> 16)
    return lax.bitcast_convert_type(packed, jnp.int32)


def _unpack_bf16_pairs(p):
    bits = lax.bitcast_convert_type(p, jnp.uint32)
    lo = lax.bitcast_convert_type(bits << 16, F32)
    hi = lax.bitcast_convert_type(bits & jnp.uint32(0xFFFF0000), F32)
    return jnp.concatenate([lo, hi], axis=1).astype(BF16)


def _ssm_out_route_kernel(x_ref, y_ref, g1_ref, w_ref, nw_ref, shift_ref, scale_ref, r_ref,
                          x3_ref, h_ref, gw_ref, sel_ref, cnt_ref):
    x3 = x_ref[0] + g1_ref[0] * _dot(y_ref[0], w_ref[...])
    x3_ref[0] = x3
    h = _modulated_norm(x3, nw_ref[...], shift_ref[0], scale_ref[0])
    h_ref[0] = _pack_bf16_pairs(h)
    h_hi, h_lo = _split_bf16(h, 2)
    r_hi, r_lo = _split_bf16(r_ref[...], 2)
    logits = _dot(h_hi, r_hi) + (_dot(h_hi, r_lo) + _dot(h_lo, r_hi))
    lane = lax.broadcasted_iota(jnp.int32, logits.shape, 1)
    logits = jnp.where(lane < N_EXPERTS, logits, -jnp.inf)
    m1 = jnp.max(logits, axis=-1, keepdims=True)
    i1 = jnp.min(jnp.where(logits == m1, lane, LANES), axis=-1, keepdims=True)
    rest = jnp.where(lane == i1, -jnp.inf, logits)
    m2 = jnp.max(rest, axis=-1, keepdims=True)
    i2 = jnp.min(jnp.where(rest == m2, lane, LANES), axis=-1, keepdims=True)
    e2 = jnp.exp(m2 - m1)
    w1 = 1.0 / (1.0 + e2)
    w2 = e2 / (1.0 + e2)
    chosen = (lane == i1) | (lane == i2)
    gw_ref[0] = jnp.where(lane == i1, w1, 0.0) + jnp.where(lane == i2, w2, 0.0)
    sel = jnp.where(chosen, 1.0, 0.0)
    sel_ref[0] = sel.astype(BF16)
    cnt_ref[0] = jnp.sum(sel, axis=0, keepdims=True)


def _ssm_out_route(x, y, mod1, w_out, norm_w, mod2, router_pad):
    bsz, seq, d = x.shape
    tm = ROUTE_TM
    tiles = seq // tm
    _, _, gate1 = _mod_specs(2)
    shift2, scale2, _ = _mod_specs(2)
    tile = lambda width: pl.BlockSpec((1, tm, width), lambda b, i: (b, i, 0))
    return pl.pallas_call(
        _ssm_out_route_kernel,
        grid=(bsz, tiles),
        in_specs=[tile(d), tile(D_INNER), gate1, _const_spec(w_out.shape), _const_spec((1, d)), shift2, scale2,
                  _const_spec(router_pad.shape)],
        out_specs=[tile(d), tile(d // 2), tile(LANES), tile(LANES),
                   pl.BlockSpec((1, 1, LANES), lambda b, i: (b * tiles + i, 0, 0))],
        out_shape=[jax.ShapeDtypeStruct((bsz, seq, d), F32), jax.ShapeDtypeStruct((bsz, seq, d // 2), jnp.int32),
                   jax.ShapeDtypeStruct((bsz, seq, LANES), F32), jax.ShapeDtypeStruct((bsz, seq, LANES), BF16),
                   jax.ShapeDtypeStruct((bsz * tiles, 1, LANES), F32)],
        compiler_params=_cparams(("parallel", "parallel")),
        name="ssm_out_route",
    )(x, y, mod1, w_out, norm_w[None, :], mod2, mod2, router_pad)


ROUTE_TM = 512


def _moe_plan_kernel(gw_ref, sel_ref, base_ref, pos_ref, wts_ref):
    tm = gw_ref.shape[0]
    gw = gw_ref[...]
    sel = sel_ref[...]
    lane = lax.broadcasted_iota(jnp.int32, (tm, LANES), 1)
    chosen = sel.astype(F32) > 0.5
    row = lax.broadcasted_iota(jnp.int32, (tm, tm), 0)
    col = lax.broadcasted_iota(jnp.int32, (tm, tm), 1)
    before = jnp.where(col < row, 1.0, 0.0).astype(BF16)
    rank = _dot(before, sel)
    dest = rank + base_ref[0]
    first = jnp.min(jnp.where(chosen, lane, LANES), axis=-1, keepdims=True)
    last = jnp.max(jnp.where(chosen, lane, -1), axis=-1, keepdims=True)
    ones = jnp.ones((8, LANES), BF16)
    for slot, pick in enumerate((lane == first, lane == last)):
        parts = _split_bf16(jnp.where(pick, dest, 0.0), 3)
        pos = _dot_nt(ones, parts[0]) + (_dot_nt(ones, parts[1]) + _dot_nt(ones, parts[2]))
        pos_ref[slot] = pos.astype(jnp.int32)
    w_lo = jnp.sum(jnp.where(lane == first, gw, 0.0), axis=-1, keepdims=True)
    w_hi = jnp.sum(jnp.where(lane == last, gw, 0.0), axis=-1, keepdims=True)
    wts_ref[...] = jnp.where(lane == 0, w_lo, jnp.where(lane == 1, w_hi, 0.0))


def _moe_plan(gw, sel, base):
    n = gw.shape[0]
    tm = ROUTE_TM
    return pl.pallas_call(
        _moe_plan_kernel,
        grid=(n // tm,),
        in_specs=[pl.BlockSpec((tm, LANES), lambda i: (i, 0)), pl.BlockSpec((tm, LANES), lambda i: (i, 0)),
                  pl.BlockSpec((1, 1, LANES), lambda i: (i, 0, 0))],
        out_specs=[pl.BlockSpec((2, 8, tm), lambda i: (0, 0, i)), pl.BlockSpec((tm, LANES), lambda i: (i, 0))],
        out_shape=[jax.ShapeDtypeStruct((2, 8, n), jnp.int32), jax.ShapeDtypeStruct((n, LANES), F32)],
        compiler_params=_cparams(("parallel",)),
        name="moe_plan",
    )(gw, sel, base)


SC_WINDOW = 128


SC_CORES = 2
SC_SUBCORES = 16


def _sc_workers():
    return SC_CORES, SC_CORES * SC_SUBCORES


def _sc_dispatch(rows, pos):
    n, width = rows.shape
    ncores, workers = _sc_workers()
    per_worker = n // workers
    steps = per_worker // SC_WINDOW
    mesh = plsc.VectorSubcoreMesh(core_axis_name="c", subcore_axis_name="s")

    @functools.partial(
        pl.kernel, mesh=mesh, out_type=jax.ShapeDtypeStruct((2 * n, width), rows.dtype),
        scratch_types=[pltpu.VMEM((2, steps, SC_WINDOW), jnp.int32), pltpu.VMEM((SC_WINDOW, width), rows.dtype)],
        name="moe_dispatch")
    def run(rows_hbm, pos_hbm, out_hbm, idx_v, rows_v):
        wid = lax.axis_index("s") * ncores + lax.axis_index("c")
        pltpu.sync_copy(pos_hbm.at[wid], idx_v)
        for j in range(steps):
            pltpu.sync_copy(rows_hbm.at[pl.ds(wid * per_worker + j * SC_WINDOW, SC_WINDOW)], rows_v)
            pltpu.sync_copy(rows_v, out_hbm.at[idx_v.at[0, j]])
            pltpu.sync_copy(rows_v, out_hbm.at[idx_v.at[1, j]])

    pos_w = pos.reshape(2, workers, steps, SC_WINDOW).transpose(1, 0, 2, 3)
    return run(rows, pos_w)


SC_GATHER_WINDOW = 64


def _sc_gather(table, pos):
    _, width = table.shape
    n = pos.shape[1]
    ncores, workers = _sc_workers()
    per_worker = n // workers
    win = SC_GATHER_WINDOW
    steps = per_worker // win
    mesh = plsc.VectorSubcoreMesh(core_axis_name="c", subcore_axis_name="s")

    @functools.partial(
        pl.kernel, mesh=mesh, out_type=jax.ShapeDtypeStruct((2, n, width), table.dtype),
        scratch_types=[pltpu.VMEM((2, steps, win), jnp.int32), pltpu.VMEM((win, width), table.dtype)],
        name="moe_gather")
    def run(table_hbm, pos_hbm, out_hbm, idx_v, rows_v):
        wid = lax.axis_index("s") * ncores + lax.axis_index("c")
        pltpu.sync_copy(pos_hbm.at[wid], idx_v)
        for s in range(2):
            for j in range(steps):
                pltpu.sync_copy(table_hbm.at[idx_v.at[s, j]], rows_v)
                pltpu.sync_copy(rows_v, out_hbm.at[s, pl.ds(wid * per_worker + j * win, win)])

    pos_w = pos.reshape(2, workers, steps, win).transpose(1, 0, 2, 3)
    return run(table, pos_w)


MOE_TM = 512
MOE_TF = 896


def _moe_pairs(goff, n_rows):
    tiles = n_rows // MOE_TM
    steps = tiles + N_EXPERTS - 1
    first_row = jnp.arange(tiles, dtype=jnp.int32) * MOE_TM
    ends = goff[1:]
    e_lo = jnp.sum(first_row[:, None] >= ends[None, :], axis=1).astype(jnp.int32)
    e_hi = jnp.sum((first_row + (MOE_TM - 1))[:, None] >= ends[None, :], axis=1).astype(jnp.int32)
    count = e_hi - e_lo + 1
    start = jnp.cumsum(count) - count
    p = jnp.arange(steps, dtype=jnp.int32)
    tile = jnp.sum(start[None, :] <= p[:, None], axis=1).astype(jnp.int32) - 1
    expert = e_lo[tile] + (p - start[tile])
    valid = p < jnp.sum(count)
    return (jnp.where(valid, tile, tiles - 1).astype(jnp.int32),
            jnp.where(valid, expert, N_EXPERTS - 1).astype(jnp.int32), valid.astype(jnp.int32))


def _moe_group_kernel(tile_ref, exp_ref, valid_ref, goff_ref, x_ref, wg_ref, wu_ref, wd_ref, y_ref, acc_ref):
    p = pl.program_id(0)
    f = pl.program_id(1)
    tm = x_ref.shape[0]
    tile = tile_ref[p]
    expert = exp_ref[p]
    new_tile = (p == 0) | (tile != tile_ref[jnp.maximum(p - 1, 0)])

    @pl.when(new_tile & (f == 0))
    def _():
        acc_ref[...] = jnp.zeros(acc_ref.shape, F32)

    @pl.when(valid_ref[p] == 1)
    def _():
        x = _unpack_bf16_pairs(x_ref[...])
        act = (_silu(_dot(x, wg_ref[0])) * _dot(x, wu_ref[0])).astype(BF16)
        rows = tile * tm + lax.broadcasted_iota(jnp.int32, (tm, 1), 0)
        mine = (rows >= goff_ref[expert]) & (rows < goff_ref[expert + 1])
        acc_ref[...] += jnp.where(mine, _dot(act, wd_ref[0]), 0.0)

    @pl.when(f == pl.num_programs(1) - 1)
    def _():
        y_ref[...] = acc_ref[...]


def _moe_group(xs, goff, wg, wu, wd):
    n_rows, half = xs.shape
    d = 2 * half
    tm, tf = MOE_TM, MOE_TF
    tile, expert, valid = _moe_pairs(goff, n_rows)
    grid_spec = pltpu.PrefetchScalarGridSpec(
        num_scalar_prefetch=4,
        grid=(tile.shape[0], FFN_EXPERT // tf),
        in_specs=[pl.BlockSpec((tm, half), lambda p, f, t, e, v, g: (t[p], 0)),
                  pl.BlockSpec((1, d, tf), lambda p, f, t, e, v, g: (e[p], 0, f)),
                  pl.BlockSpec((1, d, tf), lambda p, f, t, e, v, g: (e[p], 0, f)),
                  pl.BlockSpec((1, tf, d), lambda p, f, t, e, v, g: (e[p], f, 0))],
        out_specs=pl.BlockSpec((tm, d), lambda p, f, t, e, v, g: (t[p], 0)),
        scratch_shapes=[pltpu.VMEM((tm, d), F32)],
    )
    return pl.pallas_call(
        _moe_group_kernel,
        grid_spec=grid_spec,
        out_shape=jax.ShapeDtypeStruct((n_rows, d), F32),
        compiler_params=_cparams(("arbitrary", "arbitrary")),
        name="moe_group",
    )(tile, expert, valid, goff, xs, wg, wu, wd)


def _moe_combine_kernel(x_ref, yg_ref, wts_ref, g_ref, fn_ref, o_ref):
    w = wts_ref[0]
    mix = w[:, 0:1] * yg_ref[0, 0] + w[:, 1:2] * yg_ref[1, 0]
    x4 = x_ref[0] + g_ref[0] * mix
    y = x4 * lax.rsqrt(jnp.mean(x4 * x4, axis=-1, keepdims=True) + NORM_EPS)
    o_ref[0] = y * fn_ref[...]


def _moe_combine(x, yg, wts, mod, final_norm):
    bsz, seq, d = x.shape
    tm = 512
    _, _, gate = _mod_specs(2)
    tile = lambda width: pl.BlockSpec((1, tm, width), lambda b, i: (b, i, 0))
    return pl.pallas_call(
        _moe_combine_kernel,
        grid=(bsz, seq // tm),
        in_specs=[tile(d), pl.BlockSpec((2, 1, tm, d), lambda b, i: (0, b, i, 0)), tile(LANES), gate,
                  _const_spec((1, d))],
        out_specs=tile(d),
        out_shape=jax.ShapeDtypeStruct((bsz, seq, d), F32),
        compiler_params=_cparams(("parallel", "parallel")),
        name="moe_combine",
    )(x, yg, wts, mod, final_norm[None, :])


def _pad_cols(a, width):
    return jnp.pad(a, ((0, 0), (0, width - a.shape[1])))


def _split(a, sizes):
    return jnp.split(a, [int(s) for s in np.cumsum(sizes)[:-1]], axis=-1)


def _prep_hyb_w_in(w):
    q_a, k_a, v_a, lr_a, r_a, q_b, kv_b, g_b = _split(w, HYB_SPLITS)
    return jnp.concatenate([q_a, k_a, v_a, r_a, _pad_cols(lr_a, LANES), q_b, kv_b, _pad_cols(g_b, LANES)],
                           axis=1).astype(BF16)


def _prep_cmp(pe, w1, w2):
    eye = jnp.eye(NSA_KV_GROUPS, dtype=F32)
    half = CMP_BLOCK // 2
    w1r = w1.reshape(2, 2, half, NSA_HEAD_DIM, CMP_HIDDEN)
    w1x = jnp.einsum("jstdc,gh->jstgdhc", w1r, eye).reshape(2, 2, half * LANES, NSA_KV_GROUPS * CMP_HIDDEN)
    w2x = jnp.einsum("jcd,gh->jgchd", w2, eye).reshape(2, NSA_KV_GROUPS * CMP_HIDDEN, LANES)
    per = pe.reshape(2, 2, half, 1, NSA_HEAD_DIM)
    pe_rows = jnp.broadcast_to(per, (2, 2, half, NSA_KV_GROUPS, NSA_HEAD_DIM)).reshape(2, 2, 1, half * LANES)
    return pe_rows, w1x.astype(BF16), w2x.astype(BF16)


def _overlap_matrix(seq):
    n_rows = seq // CMP_STRIDE
    n = np.arange(n_rows)[:, None] * CMP_STRIDE
    s = np.arange(LANES)[None, :] * SEL_BLOCK
    ov = (n < s + SEL_BLOCK) & (n + CMP_BLOCK - 1 >= s) & (np.arange(LANES)[None, :] < seq // SEL_BLOCK)
    return jnp.asarray(ov, BF16)


def _head_expand():
    ex = np.zeros((SSM_GROUPS, LANES, SSD_GW), np.float32)
    for g in range(SSM_GROUPS):
        for j in range(SSD_HPG):
            ex[g, g * SSD_HPG + j, j * SSM_HEAD_DIM:(j + 1) * SSM_HEAD_DIM] = 1.0
    return jnp.asarray(ex, BF16)


def kernel(x, c, hyb_norm, hyb_mod_w, hyb_mod_b, hyb_w_in, gla_gk_up, gla_gk_bias, gla_out_norm, nsa_cmp_pe, nsa_cmp_w1, nsa_cmp_w2, hyb_w_out, dense_norm, dense_mod_w, dense_mod_b, dense_w_gu, dense_w_down, ssm_norm, ssm_mod_w, ssm_mod_b, ssm_w_in, ssm_conv_w, ssm_conv_b, ssm_dt_bias, ssm_a_log, ssm_d, ssm_gate_norm, ssm_w_out, moe_norm, moe_mod_w, moe_mod_b, moe_router, moe_w_gu, moe_w_down, final_norm):
    bsz, seq, d = x.shape
    mods = _adaln(c, (hyb_mod_w[0], dense_mod_w[0], ssm_mod_w[0], moe_mod_w[0]),
                  (hyb_mod_b[0], dense_mod_b[0], ssm_mod_b[0], moe_mod_b[0]))
    mods = mods.reshape(4, bsz, 1, 3 * d)

    gkup_pad = jnp.pad(gla_gk_up[0], ((0, LANES - GLA_LOWRANK), (0, 0))).astype(BF16)
    (gq, gk, la, gv, gr, nq, kcmp, vcmp, kslc, vslc, kwin, vwin, ng) = _hyb_in(
        x, hyb_norm[0], mods[0], _prep_hyb_w_in(hyb_w_in[0]), gkup_pad, gla_gk_bias[0])
    o_gla = _gla(gq, gk, la, gv, gr, gla_out_norm[0])
    pe_rows, w1x, w2x = _prep_cmp(nsa_cmp_pe[0], nsa_cmp_w1[0], nsa_cmp_w2[0])
    kc, vc = _nsa_compress(kcmp, vcmp, pe_rows, w1x, w2x)
    o_nsa = _nsa_attn(nq, ng, kc, vc, kslc, vslc, kwin, vwin, _overlap_matrix(seq))
    w_out = hyb_w_out[0].astype(BF16)
    n_gla = GLA_HEADS * GLA_DV
    w_gu = dense_w_gu[0].astype(BF16)
    x = _hyb_out_ffn(x, o_gla, o_nsa, mods[0], w_out[:n_gla], w_out[n_gla:], dense_norm[0], mods[1],
                     w_gu[:, :FFN_DENSE], w_gu[:, FFN_DENSE:], dense_w_down[0].astype(BF16))

    wz, wxbc, wdt = _split(ssm_w_in[0], SSM_SPLITS)
    w_ssm = jnp.concatenate([wz, wxbc, _pad_cols(wdt, LANES)], axis=1).astype(BF16)
    z, xbc, dt = _ssm_in(x, ssm_norm[0], mods[2], w_ssm)
    pad_heads = lambda a: jnp.pad(a, (0, LANES - SSM_HEADS))[None, :]
    y = _ssd(xbc, z, dt, ssm_conv_w[0], ssm_conv_b[0], pad_heads(ssm_dt_bias[0]), pad_heads(ssm_a_log[0]),
             jnp.repeat(ssm_d[0], SSM_HEAD_DIM)[None, :], ssm_gate_norm[0], _head_expand())
    x3, h4, gw, sel, cnt = _ssm_out_route(x, y, mods[2], ssm_w_out[0].astype(BF16), moe_norm[0], mods[3],
                                          _pad_cols(moe_router[0], LANES))
    n = bsz * seq
    cnt = cnt[:, 0, :]
    totals = jnp.sum(cnt, axis=0)
    goff_f = jnp.cumsum(totals) - totals
    base = (jnp.cumsum(cnt, axis=0) - cnt + goff_f[None, :])[:, None, :]
    goff = jnp.concatenate([goff_f[:N_EXPERTS], jnp.full((1,), 2.0 * n, F32)]).astype(jnp.int32)
    pos, wts = _moe_plan(gw.reshape(n, LANES), sel.reshape(n, LANES), base)
    pos = pos[:, 0, :]
    xs = _sc_dispatch(h4.reshape(n, d // 2), pos)
    w_gu = moe_w_gu[0].astype(BF16)
    ys = _moe_group(xs, goff, w_gu[:, :, :FFN_EXPERT], w_gu[:, :, FFN_EXPERT:], moe_w_down[0].astype(BF16))
    yg = _sc_gather(ys, pos)
    return _moe_combine(x3, yg.reshape(2, bsz, seq, d), wts.reshape(bsz, seq, LANES), mods[3], final_norm)
```

```python
import functools

import jax
import jax.numpy as jnp
import numpy as np
from jax import lax
from jax.experimental import pallas as pl
from jax.experimental.pallas import tpu as pltpu
from jax.experimental.pallas import tpu_sc as plsc

F32 = jnp.float32
BF16 = jnp.bfloat16

D_MODEL = 1024
NORM_EPS = 1e-6
GLA_HEADS = 4
GLA_DV = D_MODEL // 8
GLA_DK = GLA_DV // 2
GLA_LOWRANK = 16
GLA_TAU = 16.0
GLA_CHUNK = 64
NSA_HEADS = 8
NSA_KV_GROUPS = 2
NSA_HEAD_DIM = D_MODEL // 16
CMP_BLOCK = 32
CMP_STRIDE = 16
CMP_HIDDEN = 4 * NSA_HEAD_DIM
SEL_BLOCK = 64
N_SELECT = 8
WINDOW = 512
D_INNER = 2 * D_MODEL
SSM_HEAD_DIM = 64
SSM_HEADS = D_INNER // SSM_HEAD_DIM
SSM_GROUPS = 4
SSM_STATE = 128
CONV_K = 4
SSM_CHUNK = 64
CONV_CH = D_INNER + 2 * SSM_GROUPS * SSM_STATE
FFN_DENSE = ((8 * D_MODEL // 3 + 127) // 128) * 128
N_EXPERTS = 8
FFN_EXPERT = 7 * D_MODEL // 2
HYB_SPLITS = (GLA_HEADS * GLA_DK, GLA_HEADS * GLA_DK, GLA_HEADS * GLA_DV, GLA_LOWRANK, GLA_HEADS * GLA_DV,
              NSA_HEADS * NSA_HEAD_DIM, 6 * NSA_KV_GROUPS * NSA_HEAD_DIM, 3 * NSA_HEADS)
SSM_SPLITS = (D_INNER, CONV_CH, SSM_HEADS)

LANES = 128
VMEM_LIMIT = 56 * 1024 * 1024

NEG_BIG = -1e30


def _cparams(sem):
    return pltpu.CompilerParams(dimension_semantics=sem, vmem_limit_bytes=VMEM_LIMIT)


def _dot(a, b):
    return jnp.dot(a, b, preferred_element_type=F32)


def _dot_nt(a, b):
    return lax.dot_general(a, b, (((1,), (1,)), ((), ())), preferred_element_type=F32)


def _dot_tn(a, b):
    return lax.dot_general(a, b, (((0,), (0,)), ((), ())), preferred_element_type=F32)


def _split_bf16(a, parts):
    out = []
    r = a
    for _ in range(parts):
        p = r.astype(BF16)
        out.append(p)
        r = r - p.astype(F32)
    return out


def _dot_exact_lhs(m, a, parts=3):
    acc = None
    for p in _split_bf16(a, parts):
        t = _dot(m, p)
        acc = t if acc is None else acc + t
    return acc


def _dot_exact_rhs(a, m, parts=3):
    acc = None
    for p in _split_bf16(a, parts):
        t = _dot(p, m)
        acc = t if acc is None else acc + t
    return acc


def _sigmoid(x):
    return 1.0 / (1.0 + jnp.exp(-x))


def _silu(x):
    return x * _sigmoid(x)


def _softplus(x):
    return jnp.maximum(x, 0.0) + jnp.log(1.0 + jnp.exp(-jnp.abs(x)))


def _gelu_tanh(x):
    return x * (0.5 * (1.0 + jnp.tanh(0.7978845608028654 * (x + 0.044715 * (x * x * x)))))


def _modulated_norm(x, nw, shift, scale):
    ms = jnp.mean(x * x, axis=-1, keepdims=True)
    y = x * lax.rsqrt(ms + NORM_EPS)
    return (y * nw) * (1.0 + scale) + shift


def _const_spec(shape):
    nd = len(shape)
    return pl.BlockSpec(shape, lambda *_: (0,) * nd, pipeline_mode=pl.Buffered(1))


def _adaln_kernel(c_ref, w0, w1, w2, w3, b0, b1, b2, b3, o_ref):
    sc = _silu(c_ref[...]).astype(BF16)
    for i, (w, b) in enumerate(((w0, b0), (w1, b1), (w2, b2), (w3, b3))):
        o_ref[i] = _dot(sc, w[...].astype(BF16)) + b[...]


def _adaln(c, ws, bs):
    bsz, d = c.shape
    n = ws[0].shape[1]
    tn = 512
    w_spec = pl.BlockSpec((d, tn), lambda j: (0, j))
    b_spec = pl.BlockSpec((1, tn), lambda j: (0, j))
    return pl.pallas_call(
        _adaln_kernel,
        grid=(n // tn,),
        in_specs=[pl.BlockSpec((bsz, d), lambda j: (0, 0))] + [w_spec] * 4 + [b_spec] * 4,
        out_specs=pl.BlockSpec((4, bsz, tn), lambda j: (0, 0, j)),
        out_shape=jax.ShapeDtypeStruct((4, bsz, n), F32),
        compiler_params=_cparams(("arbitrary",)),
        name="adaln",
    )(c, *ws, *[b[None, :] for b in bs])


def _mod_specs(tile_axes):
    def spec(part):
        if tile_axes == 2:
            return pl.BlockSpec((1, 1, D_MODEL), lambda b, i: (b, 0, part))
        return pl.BlockSpec((1, 1, D_MODEL), lambda b, i, j, k: (b, 0, part))
    return spec(0), spec(1), spec(2)


HYB_SEG = {
    "gq": (0, 256), "gk": (256, 256), "gv": (512, 512), "gr": (1024, 512), "lr_ng": (1536, 256),
    "nq": (1792, 512), "kv": (2304, 768),
}
HYB_COLS = 3072


def _hyb_in_kernel(x_ref, nw_ref, shift_ref, scale_ref, w_ref, gkup_ref, gkb_ref,
                   gq_ref, gk_ref, la_ref, gv_ref, gr_ref, nq_ref,
                   kcmp_ref, vcmp_ref, kslc_ref, vslc_ref, kwin_ref, vwin_ref, ng_ref):
    h = _modulated_norm(x_ref[0], nw_ref[...], shift_ref[0], scale_ref[0]).astype(BF16)

    def proj(name):
        off, width = HYB_SEG[name]
        return _dot(h, w_ref[:, off:off + width])

    def proj_pair(name, pair=0):
        off = HYB_SEG[name][0] + pair * 2 * LANES
        y = _dot(h, w_ref[:, off:off + 2 * LANES])
        return y[:, :LANES], y[:, LANES:]

    gq_ref[0] = proj("gq")
    gk_ref[0] = proj("gk")
    gv_ref[0] = proj("gv").astype(BF16)
    gr_ref[0] = proj("gr")
    lr, ng = proj_pair("lr_ng")
    z = _dot(lr.astype(BF16), gkup_ref[...]) + gkb_ref[...]
    la_ref[0] = -_softplus(-z) * (1.0 / GLA_TAU)
    ng_ref[0] = ng
    nq_ref[0] = proj("nq").astype(BF16)
    kcmp_ref[0], vcmp_ref[0] = proj_pair("kv", 0)
    kslc, vslc = proj_pair("kv", 1)
    kslc_ref[0] = kslc.T.astype(BF16)
    vslc_ref[0] = vslc.astype(BF16)
    kwin, vwin = proj_pair("kv", 2)
    kwin_ref[0] = kwin.T.astype(BF16)
    vwin_ref[0] = vwin.astype(BF16)


def _hyb_in(x, norm_w, mod, w_perm, gkup_pad, gk_bias):
    bsz, seq, d = x.shape
    tm = 512
    shift_s, scale_s, _ = _mod_specs(2)

    def out(width, dtype):
        return (pl.BlockSpec((1, tm, width), lambda b, i: (b, i, 0)),
                jax.ShapeDtypeStruct((bsz, seq, width), dtype))

    def out_t(dtype):
        return (pl.BlockSpec((1, LANES, tm), lambda b, i: (b, 0, i)),
                jax.ShapeDtypeStruct((bsz, LANES, seq), dtype))

    outs = [out(256, F32), out(256, F32), out(256, F32), out(512, BF16), out(512, F32), out(512, BF16),
            out(128, F32), out(128, F32), out_t(BF16), out(128, BF16), out_t(BF16), out(128, BF16),
            out(128, F32)]
    return pl.pallas_call(
        _hyb_in_kernel,
        grid=(bsz, seq // tm),
        in_specs=[pl.BlockSpec((1, tm, d), lambda b, i: (b, i, 0)),
                  _const_spec((1, d)), shift_s, scale_s,
                  _const_spec(w_perm.shape), _const_spec(gkup_pad.shape), _const_spec((1, 256))],
        out_specs=[o[0] for o in outs],
        out_shape=[o[1] for o in outs],
        compiler_params=_cparams(("parallel", "parallel")),
        name="hyb_in",
    )(x, norm_w[None, :], mod, mod, w_perm, gkup_pad, gk_bias[None, :])


GLA_ROWS = 512


def _gla_kernel(q_ref, k_ref, la_ref, v_ref, r_ref, nw_ref, o_ref, *, seq):
    rb = GLA_ROWS
    ncb = rb // GLA_CHUNK
    row = lax.broadcasted_iota(jnp.int32, (rb, rb), 0)
    col = lax.broadcasted_iota(jnp.int32, (rb, rb), 1)
    chunk_causal = ((row // GLA_CHUNK) == (col // GLA_CHUNK)) & (col <= row)
    tril_bd = jnp.where(chunk_causal, 1.0, 0.0).astype(BF16)
    lane = lax.broadcasted_iota(jnp.int32, (rb, LANES), 1)
    head_mask = (lane < GLA_DK, lane >= GLA_DK)
    nw = nw_ref[...]

    def body(i, st):
        r0 = pl.multiple_of(i * rb, rb)
        rows = pl.ds(r0, rb)
        q = q_ref[0, rows, :]
        k = k_ref[0, rows, :]
        la = la_ref[0, rows, :]
        b = _dot_exact_lhs(tril_bd, la)
        b3 = b.reshape(ncb, GLA_CHUNK, LANES)
        bend3 = b3[:, GLA_CHUNK - 1:GLA_CHUNK, :]
        bend = jnp.broadcast_to(bend3, (ncb, GLA_CHUNK, LANES)).reshape(rb, LANES)
        q_dec = (q * GLA_DK ** -0.5) * jnp.exp(b)
        k_dec = (k * jnp.exp(-b)).astype(BF16)
        k_end = k * jnp.exp(bend - b)
        decay = jnp.exp(bend3)

        qm, vh, o_intra, upd = [], [], [], []
        for h in range(2):
            qm_h = jnp.where(head_mask[h], q_dec, 0.0).astype(BF16)
            att = _dot_nt(qm_h, k_dec)
            att = jnp.where(chunk_causal, att, 0.0).astype(BF16)
            v_h = v_ref[0, rows, h * GLA_DV:(h + 1) * GLA_DV]
            o_intra.append(_dot(att, v_h))
            km_h = jnp.where(head_mask[h], k_end, 0.0).astype(BF16)
            upd.append([_dot_tn(v_h[n * GLA_CHUNK:(n + 1) * GLA_CHUNK], km_h[n * GLA_CHUNK:(n + 1) * GLA_CHUNK])
                        for n in range(ncb)])
            qm.append(qm_h)
            vh.append(v_h)

        prev = []
        for n in range(ncb):
            prev.append(st.astype(BF16))
            st = st * decay[n] + (upd[0][n] + upd[1][n])

        for h in range(2):
            o_inter = jnp.concatenate(
                [_dot_nt(qm[h][n * GLA_CHUNK:(n + 1) * GLA_CHUNK], prev[n]) for n in range(ncb)], axis=0)
            o = o_intra[h] + o_inter
            y = o * lax.rsqrt(jnp.mean(o * o, axis=-1, keepdims=True) + NORM_EPS) * nw
            r = r_ref[0, rows, h * GLA_DV:(h + 1) * GLA_DV]
            o_ref[0, rows, h * GLA_DV:(h + 1) * GLA_DV] = (y * _silu(r)).astype(o_ref.dtype)
        return st

    lax.fori_loop(0, seq // rb, body, jnp.zeros((GLA_DV, 2 * GLA_DK), F32))


def _gla(gq, gk, la, gv, gr, out_norm):
    bsz, seq, _ = gq.shape
    qk_spec = pl.BlockSpec((1, seq, 2 * GLA_DK), lambda b, p: (b, 0, p))
    v_spec = pl.BlockSpec((1, seq, 2 * GLA_DV), lambda b, p: (b, 0, p))
    return pl.pallas_call(
        functools.partial(_gla_kernel, seq=seq),
        grid=(bsz, GLA_HEADS // 2),
        in_specs=[qk_spec, qk_spec, qk_spec, v_spec, v_spec, _const_spec((1, GLA_DV))],
        out_specs=v_spec,
        out_shape=jax.ShapeDtypeStruct((bsz, seq, GLA_HEADS * GLA_DV), BF16),
        compiler_params=_cparams(("parallel", "parallel")),
        name="gla",
    )(gq, gk, la, gv, gr, out_norm[None, :])


def _nsa_compress_kernel(k_ref, v_ref, pe_ref, w1_ref, w2_ref, kc_ref, vc_ref, *, n_cmp):
    rows = k_ref.shape[1]
    rid = lax.broadcasted_iota(jnp.int32, (rows, LANES), 0)
    for j, (src, dst) in enumerate(((k_ref, kc_ref), (v_ref, vc_ref))):
        x = src[0]
        lo = _dot((x + pe_ref[j, 0]).astype(BF16), w1_ref[j, 0])
        hi = _dot((x + pe_ref[j, 1]).astype(BF16), w1_ref[j, 1])
        hpre = lo + pltpu.roll(hi, rows - 1, 0)
        out = jnp.where(rid < n_cmp, _dot(_gelu_tanh(hpre).astype(BF16), w2_ref[j]), 0.0)
        dst[0] = (out.T if j == 0 else out).astype(dst.dtype)


def _nsa_compress(kcmp, vcmp, pe_rows, w1x, w2x):
    bsz, seq, _ = kcmp.shape
    rows = seq // CMP_STRIDE
    width = CMP_STRIDE * LANES
    n_cmp = (seq - CMP_BLOCK) // CMP_STRIDE + 1
    x_spec = pl.BlockSpec((1, rows, width), lambda b: (b, 0, 0))
    return pl.pallas_call(
        functools.partial(_nsa_compress_kernel, n_cmp=n_cmp),
        grid=(bsz,),
        in_specs=[x_spec, x_spec, _const_spec(pe_rows.shape), _const_spec(w1x.shape), _const_spec(w2x.shape)],
        out_specs=[pl.BlockSpec((1, LANES, rows), lambda b: (b, 0, 0)),
                   pl.BlockSpec((1, rows, LANES), lambda b: (b, 0, 0))],
        out_shape=[jax.ShapeDtypeStruct((bsz, LANES, rows), BF16), jax.ShapeDtypeStruct((bsz, rows, LANES), BF16)],
        compiler_params=_cparams(("parallel",)),
        name="nsa_compress",
    )(kcmp.reshape(bsz, rows, width), vcmp.reshape(bsz, rows, width), pe_rows, w1x, w2x)


NSA_QT = 256
NSA_KT = 512
NSA_ROWS = NSA_HEADS * NSA_QT


NSA_MASK_PENALTY = 1e30
NSA_LANE_POS_HI, NSA_LANE_POS_LO, NSA_LANE_CMP_IDX, NSA_LANE_CMP_ONE = 64, 65, 66, 67


def _nsa_attn_kernel(q_ref, g_ref, kc_ref, vc_ref, ks_ref, vs_ref, kw_ref, vw_ref, ov_ref, kf_ref, cf_ref, o_ref,
                     m_ref, acc_ref, *, seq):
    qt, kt = NSA_QT, NSA_KT
    heads_per_group = NSA_HEADS // NSA_KV_GROUPS
    q0 = pl.program_id(1) * qt
    t = q0 + lax.broadcasted_iota(jnp.int32, (qt, 1), 0)
    lane = lax.broadcasted_iota(jnp.int32, (qt, LANES), 1)
    group_lanes = (lane < NSA_HEAD_DIM, lane >= NSA_HEAD_DIM)
    slopes = [2.0 ** (-(h + 1)) for h in range(NSA_HEADS)]

    qa = q_ref[0].astype(F32) * NSA_HEAD_DIM ** -0.5
    q_rows = []
    for h in range(NSA_HEADS):
        g = h // heads_per_group
        blk = qa[:, (h // 2) * LANES:(h // 2 + 1) * LANES]
        if h % 2 != g:
            blk = pltpu.roll(blk, NSA_HEAD_DIM, 1)
        q_rows.append(jnp.where(group_lanes[g], blk, 0.0).astype(BF16))
    q_ext = jnp.concatenate(q_rows, axis=0)

    def alibi_lanes(h):
        s = slopes[h]
        return jnp.where(lane == NSA_LANE_POS_HI, SEL_BLOCK * s,
                         jnp.where(lane == NSA_LANE_POS_LO, s,
                                   jnp.where(lane == NSA_LANE_CMP_IDX, CMP_STRIDE * s,
                                             jnp.where(lane == NSA_LANE_CMP_ONE, (CMP_BLOCK - 1) / 2.0 * s, 0.0))))

    q_pos = [alibi_lanes(h) for h in range(NSA_HEADS)]
    q_full = jnp.concatenate([q_ext, jnp.concatenate(q_pos, axis=0).astype(BF16)], axis=1)

    def softmax_parts(s, mask):
        s = jnp.where(mask, s, -jnp.inf)
        m = jnp.max(s, axis=-1, keepdims=True)
        m = jnp.where(m == -jnp.inf, 0.0, m)
        p = jnp.exp(s - m)
        return p, jnp.sum(p, axis=-1, keepdims=True)

    n_rows = kc_ref.shape[2]
    cidx = lax.broadcasted_iota(jnp.int32, (1, n_rows), 1)
    cmp_mask = cidx * CMP_STRIDE + (CMP_BLOCK - 1) <= t
    s_all = _dot(q_full, jnp.concatenate([kc_ref[0], cf_ref[...]], axis=0))
    p_list, psum = [], [None] * NSA_KV_GROUPS
    for h in range(NSA_HEADS):
        g = h // heads_per_group
        p, l = softmax_parts(s_all[h * qt:(h + 1) * qt], cmp_mask)
        p = p / jnp.maximum(l, 1e-20)
        p_list.append(p.astype(BF16))
        psum[g] = p if psum[g] is None else psum[g] + p
    o_cmp = _dot(jnp.concatenate(p_list, axis=0), vc_ref[0])

    n_blk = LANES // 2
    tq = q0 + lax.broadcasted_iota(jnp.int32, (n_blk, qt), 1)
    bidx = lax.broadcasted_iota(jnp.int32, (n_blk, qt), 0)
    bidx_f = bidx.astype(F32)
    blk_t = tq // SEL_BLOCK
    forced = (bidx == 0) | (bidx == blk_t) | (bidx == blk_t - 1)
    future = bidx * SEL_BLOCK > tq
    q_sel_rows = []
    for g in range(NSA_KV_GROUPS):
        imp = _dot_exact_rhs(psum[g], ov_ref[...]).T[:n_blk]
        v = jnp.where(future, -1.0, jnp.where(forced, 3e38, imp))
        chosen = jnp.zeros((n_blk, qt), F32)
        for _ in range(min(N_SELECT, seq // SEL_BLOCK)):
            m = jnp.max(v, axis=0, keepdims=True)
            idx = jnp.min(jnp.where(v == m, bidx_f, float(LANES)), axis=0, keepdims=True)
            pick = (bidx_f == idx) & (m >= 0.0)
            chosen = jnp.where(pick, 1.0, chosen)
            v = jnp.where(pick, -1.0, v)
        penalty = jnp.concatenate([(chosen - 1.0) * NSA_MASK_PENALTY, jnp.zeros((n_blk, qt), F32)], axis=0).T
        for r in range(heads_per_group):
            q_sel_rows.append((penalty + q_pos[g * heads_per_group + r]).astype(BF16))
    q_full_sel = jnp.concatenate([q_ext, jnp.concatenate(q_sel_rows, axis=0)], axis=1)

    vlane = lax.broadcasted_iota(jnp.int32, (1, LANES), 1)
    own_lanes = (vlane < NSA_HEAD_DIM, vlane >= NSA_HEAD_DIM)

    def weighted_values(p_rows, v):
        v = v.astype(F32)
        outs = []
        for g in range(NSA_KV_GROUPS):
            pg = jnp.concatenate(p_rows[g * heads_per_group:(g + 1) * heads_per_group], axis=0)
            outs.append(_dot(pg, jnp.where(own_lanes[g], v, 1.0).astype(BF16)))
        return jnp.concatenate(outs, axis=0)

    def normalised(acc):
        return acc / jnp.maximum(pltpu.roll(acc, NSA_HEAD_DIM, 1), 1e-20)

    m_ref[...] = jnp.full(m_ref.shape, -jnp.inf, F32)
    acc_ref[...] = jnp.zeros(acc_ref.shape, F32)
    kpos = lax.broadcasted_iota(jnp.int32, (1, kt), 1)

    def sel_tile(j, causal):
        k0 = pl.multiple_of(j * kt, kt)
        ks = jnp.concatenate([ks_ref[0, :, pl.ds(k0, kt)], kf_ref[:, pl.ds(k0, kt)]], axis=0)
        s_t = _dot(q_full_sel, ks)
        visible = (k0 + kpos) <= t
        p_rows = []
        for h in range(NSA_HEADS):
            rows = slice(h * qt, (h + 1) * qt)
            s = s_t[rows]
            if causal:
                s = jnp.where(visible, s, -jnp.inf)
            m_old = m_ref[rows]
            m_new = jnp.maximum(m_old, jnp.max(s, axis=-1, keepdims=True))
            m_safe = jnp.where(m_new == -jnp.inf, 0.0, m_new)
            p_rows.append(jnp.exp(s - jnp.concatenate([m_safe] * (kt // LANES), axis=1)).astype(BF16))
            acc_ref[rows] = jnp.exp(m_old - m_safe) * acc_ref[rows]
            m_ref[rows] = m_new
        acc_ref[...] += weighted_values(p_rows, vs_ref[0, pl.ds(k0, kt), :])

    def sel_past_tile(j, carry):
        sel_tile(j, False)
        return carry

    last = q0 // kt
    lax.fori_loop(0, last, sel_past_tile, 0)
    sel_tile(last, True)
    o_slc = normalised(acc_ref[...])

    wk = WINDOW + qt
    w0 = pl.multiple_of(jnp.maximum(q0 - WINDOW, 0), qt)
    kw = jnp.concatenate([kw_ref[0, :, pl.ds(w0, wk)], kf_ref[:, pl.ds(w0, wk)]], axis=0)
    wdist = t - (w0 + lax.broadcasted_iota(jnp.int32, (1, wk), 1))
    wmask = (wdist >= 0) & (wdist < WINDOW)
    s_all = _dot(q_full, kw)
    p_rows = []
    for h in range(NSA_HEADS):
        s = jnp.where(wmask, s_all[h * qt:(h + 1) * qt], -jnp.inf)
        m = jnp.max(s, axis=-1, keepdims=True)
        p_rows.append(jnp.exp(s - jnp.where(m == -jnp.inf, 0.0, m)).astype(BF16))
    o_win = normalised(weighted_values(p_rows, vw_ref[0, pl.ds(w0, wk), :]))

    gates = _sigmoid(g_ref[0])
    o_heads = []
    for h in range(NSA_HEADS):
        rows = slice(h * qt, (h + 1) * qt)
        o_heads.append(gates[:, 3 * h:3 * h + 1] * o_cmp[rows] + gates[:, 3 * h + 1:3 * h + 2] * o_slc[rows]
                       + gates[:, 3 * h + 2:3 * h + 3] * o_win[rows])
    for c in range(NSA_HEADS // 2):
        g = (2 * c) // heads_per_group
        a, b = o_heads[2 * c], o_heads[2 * c + 1]
        if g == 0:
            blk = jnp.where(group_lanes[0], a, pltpu.roll(b, NSA_HEAD_DIM, 1))
        else:
            blk = jnp.where(group_lanes[0], pltpu.roll(a, NSA_HEAD_DIM, 1), b)
        o_ref[0, :, c * LANES:(c + 1) * LANES] = blk.astype(o_ref.dtype)


def _nsa_position_features(seq):
    assert seq // SEL_BLOCK <= LANES // 2
    pos = np.arange(seq)
    kf = np.zeros((seq, LANES), np.float32)
    kf[pos, pos // SEL_BLOCK] = 1.0
    kf[:, NSA_LANE_POS_HI] = pos // SEL_BLOCK
    kf[:, NSA_LANE_POS_LO] = pos % SEL_BLOCK
    n_rows = seq // CMP_STRIDE
    cf = np.zeros((n_rows, LANES), np.float32)
    cf[:, NSA_LANE_CMP_IDX] = np.arange(n_rows)
    cf[:, NSA_LANE_CMP_ONE] = 1.0
    return jnp.asarray(kf.T, BF16), jnp.asarray(cf.T, BF16)


def _nsa_attn(nq, ng, kc, vc, kslc, vslc, kwin, vwin, overlap):
    bsz, seq, _ = nq.shape
    n_rows = vc.shape[1]
    kfeat, cfeat = _nsa_position_features(seq)
    full = lambda rows: pl.BlockSpec((1, rows, LANES), lambda b, i: (b, 0, 0))
    full_t = lambda cols: pl.BlockSpec((1, LANES, cols), lambda b, i: (b, 0, 0))
    return pl.pallas_call(
        functools.partial(_nsa_attn_kernel, seq=seq),
        grid=(bsz, seq // NSA_QT),
        in_specs=[pl.BlockSpec((1, NSA_QT, NSA_HEADS * NSA_HEAD_DIM), lambda b, i: (b, i, 0)),
                  pl.BlockSpec((1, NSA_QT, LANES), lambda b, i: (b, i, 0)),
                  full_t(n_rows), full(n_rows), full_t(seq), full(seq), full_t(seq), full(seq),
                  _const_spec(overlap.shape), _const_spec(kfeat.shape), _const_spec(cfeat.shape)],
        out_specs=pl.BlockSpec((1, NSA_QT, NSA_HEADS * NSA_HEAD_DIM), lambda b, i: (b, i, 0)),
        out_shape=jax.ShapeDtypeStruct((bsz, seq, NSA_HEADS * NSA_HEAD_DIM), BF16),
        scratch_shapes=[pltpu.VMEM((NSA_ROWS, LANES), F32), pltpu.VMEM((NSA_ROWS, LANES), F32)],
        compiler_params=_cparams(("parallel", "arbitrary")),
        name="nsa_attn",
    )(nq, ng, kc, vc, kslc, vslc, kwin, vwin, overlap, kfeat, cfeat)


FFN_CHUNK = 1408


def _hyb_out_ffn_kernel(x_ref, oa_ref, ob_ref, g1_ref, wa_ref, wb_ref,
                        nw_ref, shift_ref, scale_ref, g2_ref, wg_ref, wu_ref, wd_ref, o_ref, act_ref):
    mix = _dot(oa_ref[0], wa_ref[...]) + _dot(ob_ref[0], wb_ref[...])
    x1 = x_ref[0] + g1_ref[0] * mix
    h = _modulated_norm(x1, nw_ref[...], shift_ref[0], scale_ref[0]).astype(BF16)
    for f in range(0, FFN_DENSE, FFN_CHUNK):
        gate = _dot(h, wg_ref[:, f:f + FFN_CHUNK])
        up = _dot(h, wu_ref[:, f:f + FFN_CHUNK])
        act_ref[:, f:f + FFN_CHUNK] = (_silu(gate) * up).astype(BF16)
    o_ref[0] = x1 + g2_ref[0] * _dot(act_ref[...], wd_ref[...])


def _hyb_out_ffn(x, o_gla, o_nsa, mod1, wa, wb, norm_w, mod2, wg, wu, wd):
    bsz, seq, d = x.shape
    tm = 512
    _, _, gate1 = _mod_specs(2)
    shift2, scale2, gate2 = _mod_specs(2)
    tile = lambda width: pl.BlockSpec((1, tm, width), lambda b, i: (b, i, 0))
    return pl.pallas_call(
        _hyb_out_ffn_kernel,
        grid=(bsz, seq // tm),
        in_specs=[tile(d), tile(o_gla.shape[-1]), tile(o_nsa.shape[-1]), gate1,
                  _const_spec(wa.shape), _const_spec(wb.shape), _const_spec((1, d)), shift2, scale2, gate2,
                  _const_spec(wg.shape), _const_spec(wu.shape), _const_spec(wd.shape)],
        out_specs=tile(d),
        out_shape=jax.ShapeDtypeStruct((bsz, seq, d), F32),
        scratch_shapes=[pltpu.VMEM((tm, FFN_DENSE), BF16)],
        compiler_params=_cparams(("parallel", "parallel")),
        name="hyb_out_ffn",
    )(x, o_gla, o_nsa, mod1, wa, wb, norm_w[None, :], mod2, mod2, mod2, wg, wu, wd)


SSM_DT_OFF = D_INNER + CONV_CH
SSM_COLS = SSM_DT_OFF + LANES
SSM_IN_CHUNK = 1024


def _ssm_in_kernel(x_ref, nw_ref, shift_ref, scale_ref, w_ref, z_ref, xbc_ref, dt_ref):
    h = _modulated_norm(x_ref[0], nw_ref[...], shift_ref[0], scale_ref[0]).astype(BF16)
    for c in range(0, D_INNER, SSM_IN_CHUNK):
        z_ref[0, :, c:c + SSM_IN_CHUNK] = _dot(h, w_ref[:, c:c + SSM_IN_CHUNK])
    for c in range(0, CONV_CH, SSM_IN_CHUNK):
        xbc_ref[0, :, c:c + SSM_IN_CHUNK] = _dot(h, w_ref[:, D_INNER + c:D_INNER + c + SSM_IN_CHUNK])
    dt_ref[0] = _dot(h, w_ref[:, SSM_DT_OFF:SSM_COLS])


def _ssm_in(x, norm_w, mod, w_perm):
    bsz, seq, d = x.shape
    tm = 512
    shift_s, scale_s, _ = _mod_specs(2)
    tile = lambda width: pl.BlockSpec((1, tm, width), lambda b, i: (b, i, 0))
    return pl.pallas_call(
        _ssm_in_kernel,
        grid=(bsz, seq // tm),
        in_specs=[tile(d), _const_spec((1, d)), shift_s, scale_s, _const_spec(w_perm.shape)],
        out_specs=[tile(D_INNER), tile(CONV_CH), tile(LANES)],
        out_shape=[jax.ShapeDtypeStruct((bsz, seq, D_INNER), F32),
                   jax.ShapeDtypeStruct((bsz, seq, CONV_CH), F32),
                   jax.ShapeDtypeStruct((bsz, seq, LANES), F32)],
        compiler_params=_cparams(("parallel", "parallel")),
        name="ssm_in",
    )(x, norm_w[None, :], mod, mod, w_perm)


SSD_ROWS = 512
SSD_GW = D_INNER // SSM_GROUPS
SSD_HPG = SSM_HEADS // SSM_GROUPS
CONV_TAIL = 8


def _ssd_kernel(xbc_ref, z_ref, dt_ref, cw_ref, cb_ref, dtb_ref, alog_ref, dskip_ref, nw_ref, ex_ref, o_ref,
                tail_ref, state_ref, xdt_ref, cum_ref, b_ref, c_ref, y_ref):
    rb, q = SSD_ROWS, SSM_CHUNK
    nchunk = rb // q

    @pl.when(pl.program_id(1) == 0)
    def _():
        tail_ref[...] = jnp.zeros(tail_ref.shape, F32)
        state_ref[...] = jnp.zeros(state_ref.shape, F32)

    row8 = lax.broadcasted_iota(jnp.int32, (CONV_TAIL, 1), 0)

    def conv_silu(c0, width):
        x = xbc_ref[0, :, c0:c0 + width]
        tail = tail_ref[:, c0:c0 + width]
        acc = x * cw_ref[CONV_K - 1:CONV_K, c0:c0 + width] + cb_ref[:, c0:c0 + width]
        for j in range(1, CONV_K):
            head = jnp.where(row8 < j, pltpu.roll(tail, j, 0), pltpu.roll(x[:CONV_TAIL], j, 0))
            xs = jnp.concatenate([head, xbc_ref[0, CONV_TAIL - j:rb - j, c0:c0 + width]], axis=0)
            acc = acc + xs * cw_ref[CONV_K - 1 - j:CONV_K - j, c0:c0 + width]
        return _silu(acc)

    dt = _softplus(dt_ref[0] + dtb_ref[...])
    a = dt * (-jnp.exp(alog_ref[...]))
    row = lax.broadcasted_iota(jnp.int32, (rb, rb), 0)
    col = lax.broadcasted_iota(jnp.int32, (rb, rb), 1)
    tril_bd = jnp.where(((row // q) == (col // q)) & (col <= row), 1.0, 0.0).astype(BF16)
    cum = _dot_exact_lhs(tril_bd, a)

    lrow = lax.broadcasted_iota(jnp.int32, (q, SSD_GW), 0)
    lcol = lax.broadcasted_iota(jnp.int32, (q, SSD_GW), 1) % q
    causal_t = lcol <= lrow
    eye_t = lcol == lrow
    half = SSD_GW // 2
    brow = lax.broadcasted_iota(jnp.int32, (half, half), 0) // q
    bcol = lax.broadcasted_iota(jnp.int32, (half, half), 1) // SSM_HEAD_DIM
    same_head = brow == bcol

    for g in range(SSM_GROUPS):
        xg = conv_silu(g * SSD_GW, SSD_GW)
        y_ref[g] = dskip_ref[:, g * SSD_GW:(g + 1) * SSD_GW] * xg
        b_ref[g] = conv_silu(D_INNER + g * SSM_STATE, SSM_STATE).astype(BF16)
        c_ref[g] = conv_silu(D_INNER + SSM_GROUPS * SSM_STATE + g * SSM_STATE, SSM_STATE).astype(BF16)
        xdt_ref[g] = xg * _dot_exact_rhs(dt, ex_ref[g])
        cum_ref[g] = _dot_exact_rhs(cum, ex_ref[g])

    def chunk(n, carry):
        rows = pl.ds(pl.multiple_of(n * q, q), q)
        for g in range(SSM_GROUPS):
            cum_c = cum_ref[g, rows, :]
            cum_s = jnp.sum(jnp.where(eye_t, cum_c, 0.0), axis=0, keepdims=True)
            decay_l = jnp.where(causal_t, jnp.exp(cum_c - cum_s), 0.0)
            bc = b_ref[g, rows, :]
            cc = c_ref[g, rows, :]
            cb_t = _dot_nt(cc, jnp.concatenate([bc] * SSD_HPG, axis=0))
            mat = (cb_t * decay_l).astype(BF16)
            xdt_c = xdt_ref[g, rows, :]
            xdt_b = xdt_c.astype(BF16)
            y_diag = []
            for s in range(2):
                blk = xdt_b[:, s * half:(s + 1) * half]
                bd = jnp.where(same_head, jnp.concatenate([blk] * (half // q), axis=0), 0.0).astype(BF16)
                y_diag.append(_dot(mat[:, s * half:(s + 1) * half], bd))
            y = jnp.concatenate(y_diag, axis=1)
            cum_end = cum_c[q - 1:q, :]
            st = state_ref[g]
            y = y + _dot(cc, st.astype(BF16)) * jnp.exp(cum_c)
            x_end = (xdt_c * jnp.exp(cum_end - cum_c)).astype(BF16)
            state_ref[g] = st * jnp.exp(cum_end) + _dot_tn(bc, x_end)
            y_ref[g, rows, :] += y
        return carry

    lax.fori_loop(0, nchunk, chunk, 0)

    for g in range(SSM_GROUPS):
        cols = slice(g * SSD_GW, (g + 1) * SSD_GW)
        y = y_ref[g] * _silu(z_ref[0, :, cols])
        y = y * lax.rsqrt(jnp.mean(y * y, axis=-1, keepdims=True) + NORM_EPS) * nw_ref[:, cols]
        o_ref[0, :, cols] = y.astype(o_ref.dtype)

    tail_ref[...] = xbc_ref[0, rb - CONV_TAIL:rb, :]


def _ssd(xbc, z, dt, conv_w, conv_b, dt_bias_pad, a_log_pad, d_skip_x, norm_w, expand):
    bsz, seq, _ = xbc.shape
    rb = SSD_ROWS
    tile = lambda width: pl.BlockSpec((1, rb, width), lambda b, i: (b, i, 0))
    return pl.pallas_call(
        _ssd_kernel,
        grid=(bsz, seq // rb),
        in_specs=[tile(CONV_CH), tile(D_INNER), tile(LANES),
                  _const_spec((CONV_K, CONV_CH)), _const_spec((1, CONV_CH)),
                  _const_spec((1, LANES)), _const_spec((1, LANES)),
                  _const_spec((1, D_INNER)), _const_spec((1, D_INNER)), _const_spec(expand.shape)],
        out_specs=tile(D_INNER),
        out_shape=jax.ShapeDtypeStruct((bsz, seq, D_INNER), BF16),
        scratch_shapes=[pltpu.VMEM((CONV_TAIL, CONV_CH), F32),
                        pltpu.VMEM((SSM_GROUPS, SSM_STATE, SSD_GW), F32),
                        pltpu.VMEM((SSM_GROUPS, rb, SSD_GW), F32), pltpu.VMEM((SSM_GROUPS, rb, SSD_GW), F32),
                        pltpu.VMEM((SSM_GROUPS, rb, SSM_STATE), BF16), pltpu.VMEM((SSM_GROUPS, rb, SSM_STATE), BF16),
                        pltpu.VMEM((SSM_GROUPS, rb, SSD_GW), F32)],
        compiler_params=_cparams(("parallel", "arbitrary")),
        name="ssd",
    )(xbc, z, dt, conv_w, conv_b[None, :], dt_bias_pad, a_log_pad, d_skip_x, norm_w[None, :], expand)


def _pack_bf16_pairs(a):
    w = a.shape[1] // 2
    bits = lax.bitcast_convert_type(a.astype(jnp.bfloat16).astype(F32), jnp.uint32)
    packed = bits[:, w:] | (bits[:, :w] >> 16)
    return lax.bitcast_convert_type(packed, jnp.int32)


def _unpack_bf16_pairs(p):
    bits = lax.bitcast_convert_type(p, jnp.uint32)
    lo = lax.bitcast_convert_type(bits << 16, F32)
    hi = lax.bitcast_convert_type(bits & jnp.uint32(0xFFFF0000), F32)
    return jnp.concatenate([lo, hi], axis=1).astype(BF16)


def _ssm_out_route_kernel(x_ref, y_ref, g1_ref, w_ref, nw_ref, shift_ref, scale_ref, r_ref,
                          x3_ref, h_ref, gw_ref, sel_ref, cnt_ref):
    x3 = x_ref[0] + g1_ref[0] * _dot(y_ref[0], w_ref[...])
    x3_ref[0] = x3
    h = _modulated_norm(x3, nw_ref[...], shift_ref[0], scale_ref[0])
    h_ref[0] = _pack_bf16_pairs(h)
    h_hi, h_lo = _split_bf16(h, 2)
    r_hi, r_lo = _split_bf16(r_ref[...], 2)
    both = _dot(h_hi, jnp.concatenate([r_hi, r_lo], axis=1))
    logits = both[:, :LANES] + (both[:, LANES:] + _dot(h_lo, r_hi))
    lane = lax.broadcasted_iota(jnp.int32, logits.shape, 1)
    logits = jnp.where(lane < N_EXPERTS, logits, -jnp.inf)
    m1 = jnp.max(logits, axis=-1, keepdims=True)
    i1 = jnp.min(jnp.where(logits == m1, lane, LANES), axis=-1, keepdims=True)
    rest = jnp.where(lane == i1, -jnp.inf, logits)
    m2 = jnp.max(rest, axis=-1, keepdims=True)
    i2 = jnp.min(jnp.where(rest == m2, lane, LANES), axis=-1, keepdims=True)
    e2 = jnp.exp(m2 - m1)
    w1 = 1.0 / (1.0 + e2)
    w2 = e2 / (1.0 + e2)
    chosen = (lane == i1) | (lane == i2)
    gw_ref[0] = jnp.where(lane == i1, w1, 0.0) + jnp.where(lane == i2, w2, 0.0)
    sel = jnp.where(chosen, 1.0, 0.0)
    sel_ref[0] = sel.astype(BF16)
    cnt_ref[0] = jnp.sum(sel, axis=0, keepdims=True)


def _ssm_out_route(x, y, mod1, w_out, norm_w, mod2, router_pad):
    bsz, seq, d = x.shape
    tm = ROUTE_TM
    tiles = seq // tm
    _, _, gate1 = _mod_specs(2)
    shift2, scale2, _ = _mod_specs(2)
    tile = lambda width: pl.BlockSpec((1, tm, width), lambda b, i: (b, i, 0))
    return pl.pallas_call(
        _ssm_out_route_kernel,
        grid=(bsz, tiles),
        in_specs=[tile(d), tile(D_INNER), gate1, _const_spec(w_out.shape), _const_spec((1, d)), shift2, scale2,
                  _const_spec(router_pad.shape)],
        out_specs=[tile(d), tile(d // 2), tile(LANES), tile(LANES),
                   pl.BlockSpec((1, 1, LANES), lambda b, i: (b * tiles + i, 0, 0))],
        out_shape=[jax.ShapeDtypeStruct((bsz, seq, d), F32), jax.ShapeDtypeStruct((bsz, seq, d // 2), jnp.int32),
                   jax.ShapeDtypeStruct((bsz, seq, LANES), F32), jax.ShapeDtypeStruct((bsz, seq, LANES), BF16),
                   jax.ShapeDtypeStruct((bsz * tiles, 1, LANES), F32)],
        compiler_params=_cparams(("parallel", "parallel")),
        name="ssm_out_route",
    )(x, y, mod1, w_out, norm_w[None, :], mod2, mod2, router_pad)


ROUTE_TM = 512


def _moe_plan_kernel(gw_ref, sel_ref, base_ref, pos_ref, wts_ref):
    tm = gw_ref.shape[0]
    gw = gw_ref[...]
    sel = sel_ref[...]
    lane = lax.broadcasted_iota(jnp.int32, (tm, LANES), 1)
    chosen = sel.astype(F32) > 0.5
    row = lax.broadcasted_iota(jnp.int32, (tm, tm), 0)
    col = lax.broadcasted_iota(jnp.int32, (tm, tm), 1)
    before = jnp.where(col < row, 1.0, 0.0).astype(BF16)
    rank = _dot(before, sel)
    dest = rank + base_ref[0]
    first = jnp.min(jnp.where(chosen, lane, LANES), axis=-1, keepdims=True)
    last = jnp.max(jnp.where(chosen, lane, -1), axis=-1, keepdims=True)
    ones = jnp.ones((8, LANES), BF16)
    for slot, pick in enumerate((lane == first, lane == last)):
        parts = _split_bf16(jnp.where(pick, dest, 0.0), 3)
        pos = _dot_nt(ones, parts[0]) + (_dot_nt(ones, parts[1]) + _dot_nt(ones, parts[2]))
        pos_ref[slot] = pos.astype(jnp.int32)
    w_lo = jnp.sum(jnp.where(lane == first, gw, 0.0), axis=-1, keepdims=True)
    w_hi = jnp.sum(jnp.where(lane == last, gw, 0.0), axis=-1, keepdims=True)
    wts_ref[...] = jnp.where(lane == 0, w_lo, jnp.where(lane == 1, w_hi, 0.0))


def _moe_plan(gw, sel, base):
    n = gw.shape[0]
    tm = ROUTE_TM
    return pl.pallas_call(
        _moe_plan_kernel,
        grid=(n // tm,),
        in_specs=[pl.BlockSpec((tm, LANES), lambda i: (i, 0)), pl.BlockSpec((tm, LANES), lambda i: (i, 0)),
                  pl.BlockSpec((1, 1, LANES), lambda i: (i, 0, 0))],
        out_specs=[pl.BlockSpec((2, 8, tm), lambda i: (0, 0, i)), pl.BlockSpec((tm, LANES), lambda i: (i, 0))],
        out_shape=[jax.ShapeDtypeStruct((2, 8, n), jnp.int32), jax.ShapeDtypeStruct((n, LANES), F32)],
        compiler_params=_cparams(("parallel",)),
        name="moe_plan",
    )(gw, sel, base)


SC_WINDOW = 128


SC_CORES = 2
SC_SUBCORES = 16


def _sc_workers():
    return SC_CORES, SC_CORES * SC_SUBCORES


def _sc_dispatch(rows, pos):
    n, width = rows.shape
    ncores, workers = _sc_workers()
    per_worker = n // workers
    steps = per_worker // SC_WINDOW
    mesh = plsc.VectorSubcoreMesh(core_axis_name="c", subcore_axis_name="s")

    @functools.partial(
        pl.kernel, mesh=mesh, out_type=jax.ShapeDtypeStruct((2 * n, width), rows.dtype),
        scratch_types=[pltpu.VMEM((2, steps, SC_WINDOW), jnp.int32), pltpu.VMEM((SC_WINDOW, width), rows.dtype)],
        name="moe_dispatch")
    def run(rows_hbm, pos_hbm, out_hbm, idx_v, rows_v):
        wid = lax.axis_index("s") * ncores + lax.axis_index("c")
        pltpu.sync_copy(pos_hbm.at[wid], idx_v)
        for j in range(steps):
            pltpu.sync_copy(rows_hbm.at[pl.ds(wid * per_worker + j * SC_WINDOW, SC_WINDOW)], rows_v)
            pltpu.sync_copy(rows_v, out_hbm.at[idx_v.at[0, j]])
            pltpu.sync_copy(rows_v, out_hbm.at[idx_v.at[1, j]])

    pos_w = pos.reshape(2, workers, steps, SC_WINDOW).transpose(1, 0, 2, 3)
    return run(rows, pos_w)


SC_GATHER_WINDOW = 128


def _sc_gather(table, pos):
    _, width = table.shape
    n = pos.shape[1]
    ncores, workers = _sc_workers()
    per_worker = n // workers
    win = SC_GATHER_WINDOW
    steps = per_worker // win
    mesh = plsc.VectorSubcoreMesh(core_axis_name="c", subcore_axis_name="s")

    @functools.partial(
        pl.kernel, mesh=mesh, out_type=jax.ShapeDtypeStruct((2, n, width), table.dtype),
        scratch_types=[pltpu.VMEM((2, steps, win), jnp.int32), pltpu.VMEM((win, width), table.dtype)],
        name="moe_gather")
    def run(table_hbm, pos_hbm, out_hbm, idx_v, rows_v):
        wid = lax.axis_index("s") * ncores + lax.axis_index("c")
        pltpu.sync_copy(pos_hbm.at[wid], idx_v)
        for s in range(2):
            for j in range(steps):
                pltpu.sync_copy(table_hbm.at[idx_v.at[s, j]], rows_v)
                pltpu.sync_copy(rows_v, out_hbm.at[s, pl.ds(wid * per_worker + j * win, win)])

    pos_w = pos.reshape(2, workers, steps, win).transpose(1, 0, 2, 3)
    return run(table, pos_w)


MOE_TM = 512
MOE_TF = 896


def _moe_pairs(goff, n_rows):
    tiles = n_rows // MOE_TM
    steps = tiles + N_EXPERTS - 1
    first_row = jnp.arange(tiles, dtype=jnp.int32) * MOE_TM
    ends = goff[1:]
    e_lo = jnp.sum(first_row[:, None] >= ends[None, :], axis=1).astype(jnp.int32)
    e_hi = jnp.sum((first_row + (MOE_TM - 1))[:, None] >= ends[None, :], axis=1).astype(jnp.int32)
    count = e_hi - e_lo + 1
    start = jnp.cumsum(count) - count
    p = jnp.arange(steps, dtype=jnp.int32)
    tile = jnp.sum(start[None, :] <= p[:, None], axis=1).astype(jnp.int32) - 1
    expert = e_lo[tile] + (p - start[tile])
    valid = p < jnp.sum(count)
    return (jnp.where(valid, tile, tiles - 1).astype(jnp.int32),
            jnp.where(valid, expert, N_EXPERTS - 1).astype(jnp.int32), valid.astype(jnp.int32))


def _moe_group_kernel(tile_ref, exp_ref, valid_ref, goff_ref, x_ref, wg_ref, wu_ref, wd_ref, y_ref, act_ref):
    p = pl.program_id(0)
    tm = x_ref.shape[0]
    tile = tile_ref[p]
    expert = exp_ref[p]

    @pl.when((p == 0) | (tile != tile_ref[jnp.maximum(p - 1, 0)]))
    def _():
        y_ref[...] = jnp.zeros(y_ref.shape, y_ref.dtype)

    @pl.when(valid_ref[p] == 1)
    def _():
        x = _unpack_bf16_pairs(x_ref[...])
        for f in range(0, FFN_EXPERT, MOE_TF):
            act_ref[:, f:f + MOE_TF] = (_silu(_dot(x, wg_ref[0, :, f:f + MOE_TF]))
                                        * _dot(x, wu_ref[0, :, f:f + MOE_TF])).astype(BF16)
        out = _pack_bf16_pairs(_dot(act_ref[...], wd_ref[0]))
        rows = tile * tm + lax.broadcasted_iota(jnp.int32, (tm, 1), 0)
        mine = (rows >= goff_ref[expert]) & (rows < goff_ref[expert + 1])
        y_ref[...] = jnp.where(mine, out, y_ref[...])


def _moe_group(xs, goff, wg, wu, wd):
    n_rows, half = xs.shape
    d = 2 * half
    tm = MOE_TM
    tile, expert, valid = _moe_pairs(goff, n_rows)
    resident = lambda shape: pl.BlockSpec(shape, lambda p, t, e, v, g: (e[p], 0, 0), pipeline_mode=pl.Buffered(1))
    grid_spec = pltpu.PrefetchScalarGridSpec(
        num_scalar_prefetch=4,
        grid=(tile.shape[0],),
        in_specs=[pl.BlockSpec((tm, half), lambda p, t, e, v, g: (t[p], 0)),
                  resident((1, d, FFN_EXPERT)), resident((1, d, FFN_EXPERT)), resident((1, FFN_EXPERT, d))],
        out_specs=pl.BlockSpec((tm, half), lambda p, t, e, v, g: (t[p], 0)),
        scratch_shapes=[pltpu.VMEM((tm, FFN_EXPERT), BF16)],
    )
    return pl.pallas_call(
        _moe_group_kernel,
        grid_spec=grid_spec,
        out_shape=jax.ShapeDtypeStruct((n_rows, half), jnp.int32),
        compiler_params=_cparams(("arbitrary",)),
        name="moe_group",
    )(tile, expert, valid, goff, xs, wg, wu, wd)


def _moe_combine_kernel(x_ref, yg_ref, wts_ref, g_ref, fn_ref, o_ref):
    w = wts_ref[0]
    mix = (w[:, 0:1] * _unpack_bf16_pairs(yg_ref[0, 0]).astype(F32)
           + w[:, 1:2] * _unpack_bf16_pairs(yg_ref[1, 0]).astype(F32))
    x4 = x_ref[0] + g_ref[0] * mix
    y = x4 * lax.rsqrt(jnp.mean(x4 * x4, axis=-1, keepdims=True) + NORM_EPS)
    o_ref[0] = y * fn_ref[...]


def _moe_combine(x, yg, wts, mod, final_norm):
    bsz, seq, d = x.shape
    tm = 512
    _, _, gate = _mod_specs(2)
    tile = lambda width: pl.BlockSpec((1, tm, width), lambda b, i: (b, i, 0))
    return pl.pallas_call(
        _moe_combine_kernel,
        grid=(bsz, seq // tm),
        in_specs=[tile(d), pl.BlockSpec((2, 1, tm, d // 2), lambda b, i: (0, b, i, 0)), tile(LANES), gate,
                  _const_spec((1, d))],
        out_specs=tile(d),
        out_shape=jax.ShapeDtypeStruct((bsz, seq, d), F32),
        compiler_params=_cparams(("parallel", "parallel")),
        name="moe_combine",
    )(x, yg, wts, mod, final_norm[None, :])


def _pad_cols(a, width):
    return jnp.pad(a, ((0, 0), (0, width - a.shape[1])))


def _split(a, sizes):
    return jnp.split(a, [int(s) for s in np.cumsum(sizes)[:-1]], axis=-1)


def _prep_hyb_w_in(w):
    q_a, k_a, v_a, lr_a, r_a, q_b, kv_b, g_b = _split(w, HYB_SPLITS)
    return jnp.concatenate([q_a, k_a, v_a, r_a, _pad_cols(lr_a, LANES), _pad_cols(g_b, LANES), q_b, kv_b],
                           axis=1).astype(BF16)


def _prep_cmp(pe, w1, w2):
    eye = jnp.eye(NSA_KV_GROUPS, dtype=F32)
    half = CMP_BLOCK // 2
    w1r = w1.reshape(2, 2, half, NSA_HEAD_DIM, CMP_HIDDEN)
    w1x = jnp.einsum("jstdc,gh->jstgdhc", w1r, eye).reshape(2, 2, half * LANES, NSA_KV_GROUPS * CMP_HIDDEN)
    w2x = jnp.einsum("jcd,gh->jgchd", w2, eye).reshape(2, NSA_KV_GROUPS * CMP_HIDDEN, LANES)
    per = pe.reshape(2, 2, half, 1, NSA_HEAD_DIM)
    pe_rows = jnp.broadcast_to(per, (2, 2, half, NSA_KV_GROUPS, NSA_HEAD_DIM)).reshape(2, 2, 1, half * LANES)
    return pe_rows, w1x.astype(BF16), w2x.astype(BF16)


def _overlap_matrix(seq):
    n_rows = seq // CMP_STRIDE
    n = np.arange(n_rows)[:, None] * CMP_STRIDE
    s = np.arange(LANES)[None, :] * SEL_BLOCK
    ov = (n < s + SEL_BLOCK) & (n + CMP_BLOCK - 1 >= s) & (np.arange(LANES)[None, :] < seq // SEL_BLOCK)
    return jnp.asarray(ov, BF16)


def _head_expand():
    ex = np.zeros((SSM_GROUPS, LANES, SSD_GW), np.float32)
    for g in range(SSM_GROUPS):
        for j in range(SSD_HPG):
            ex[g, g * SSD_HPG + j, j * SSM_HEAD_DIM:(j + 1) * SSM_HEAD_DIM] = 1.0
    return jnp.asarray(ex, BF16)


def kernel(x, c, hyb_norm, hyb_mod_w, hyb_mod_b, hyb_w_in, gla_gk_up, gla_gk_bias, gla_out_norm, nsa_cmp_pe, nsa_cmp_w1, nsa_cmp_w2, hyb_w_out, dense_norm, dense_mod_w, dense_mod_b, dense_w_gu, dense_w_down, ssm_norm, ssm_mod_w, ssm_mod_b, ssm_w_in, ssm_conv_w, ssm_conv_b, ssm_dt_bias, ssm_a_log, ssm_d, ssm_gate_norm, ssm_w_out, moe_norm, moe_mod_w, moe_mod_b, moe_router, moe_w_gu, moe_w_down, final_norm):
    bsz, seq, d = x.shape
    mods = _adaln(c, (hyb_mod_w[0], dense_mod_w[0], ssm_mod_w[0], moe_mod_w[0]),
                  (hyb_mod_b[0], dense_mod_b[0], ssm_mod_b[0], moe_mod_b[0]))
    mods = mods.reshape(4, bsz, 1, 3 * d)

    gkup_pad = jnp.pad(gla_gk_up[0], ((0, LANES - GLA_LOWRANK), (0, 0))).astype(BF16)
    (gq, gk, la, gv, gr, nq, kcmp, vcmp, kslc, vslc, kwin, vwin, ng) = _hyb_in(
        x, hyb_norm[0], mods[0], _prep_hyb_w_in(hyb_w_in[0]), gkup_pad, gla_gk_bias[0])
    o_gla = _gla(gq, gk, la, gv, gr, gla_out_norm[0])
    pe_rows, w1x, w2x = _prep_cmp(nsa_cmp_pe[0], nsa_cmp_w1[0], nsa_cmp_w2[0])
    kc, vc = _nsa_compress(kcmp, vcmp, pe_rows, w1x, w2x)
    o_nsa = _nsa_attn(nq, ng, kc, vc, kslc, vslc, kwin, vwin, _overlap_matrix(seq))
    w_out = hyb_w_out[0].astype(BF16)
    n_gla = GLA_HEADS * GLA_DV
    w_gu = dense_w_gu[0].astype(BF16)
    x = _hyb_out_ffn(x, o_gla, o_nsa, mods[0], w_out[:n_gla], w_out[n_gla:], dense_norm[0], mods[1],
                     w_gu[:, :FFN_DENSE], w_gu[:, FFN_DENSE:], dense_w_down[0].astype(BF16))

    wz, wxbc, wdt = _split(ssm_w_in[0], SSM_SPLITS)
    w_ssm = jnp.concatenate([wz, wxbc, _pad_cols(wdt, LANES)], axis=1).astype(BF16)
    z, xbc, dt = _ssm_in(x, ssm_norm[0], mods[2], w_ssm)
    pad_heads = lambda a: jnp.pad(a, (0, LANES - SSM_HEADS))[None, :]
    y = _ssd(xbc, z, dt, ssm_conv_w[0], ssm_conv_b[0], pad_heads(ssm_dt_bias[0]), pad_heads(ssm_a_log[0]),
             jnp.repeat(ssm_d[0], SSM_HEAD_DIM)[None, :], ssm_gate_norm[0], _head_expand())
    x3, h4, gw, sel, cnt = _ssm_out_route(x, y, mods[2], ssm_w_out[0].astype(BF16), moe_norm[0], mods[3],
                                          _pad_cols(moe_router[0], LANES))
    n = bsz * seq
    cnt = cnt[:, 0, :]
    totals = jnp.sum(cnt, axis=0)
    goff_f = jnp.cumsum(totals) - totals
    base = (jnp.cumsum(cnt, axis=0) - cnt + goff_f[None, :])[:, None, :]
    goff = jnp.concatenate([goff_f[:N_EXPERTS], jnp.full((1,), 2.0 * n, F32)]).astype(jnp.int32)
    pos, wts = _moe_plan(gw.reshape(n, LANES), sel.reshape(n, LANES), base)
    pos = pos[:, 0, :]
    xs = _sc_dispatch(h4.reshape(n, d // 2), pos)
    w_gu = moe_w_gu[0].astype(BF16)
    ys = _moe_group(xs, goff, w_gu[:, :, :FFN_EXPERT], w_gu[:, :, FFN_EXPERT:], moe_w_down[0].astype(BF16))
    yg = _sc_gather(ys, pos)
    return _moe_combine(x3, yg.reshape(2, bsz, seq, d // 2), wts.reshape(bsz, seq, LANES), mods[3], final_norm)
```

```python
import functools

import jax
import jax.numpy as jnp
import numpy as np
from jax import lax
from jax.experimental import pallas as pl
from jax.experimental.pallas import tpu as pltpu
from jax.experimental.pallas import tpu_sc as plsc

F32 = jnp.float32
BF16 = jnp.bfloat16

D_MODEL = 1024
NORM_EPS = 1e-6
GLA_HEADS = 4
GLA_DV = D_MODEL // 8
GLA_DK = GLA_DV // 2
GLA_LOWRANK = 16
GLA_TAU = 16.0
GLA_CHUNK = 64
NSA_HEADS = 8
NSA_KV_GROUPS = 2
NSA_HEAD_DIM = D_MODEL // 16
CMP_BLOCK = 32
CMP_STRIDE = 16
CMP_HIDDEN = 4 * NSA_HEAD_DIM
SEL_BLOCK = 64
N_SELECT = 8
WINDOW = 512
D_INNER = 2 * D_MODEL
SSM_HEAD_DIM = 64
SSM_HEADS = D_INNER // SSM_HEAD_DIM
SSM_GROUPS = 4
SSM_STATE = 128
CONV_K = 4
SSM_CHUNK = 64
CONV_CH = D_INNER + 2 * SSM_GROUPS * SSM_STATE
FFN_DENSE = ((8 * D_MODEL // 3 + 127) // 128) * 128
N_EXPERTS = 8
FFN_EXPERT = 7 * D_MODEL // 2
HYB_SPLITS = (GLA_HEADS * GLA_DK, GLA_HEADS * GLA_DK, GLA_HEADS * GLA_DV, GLA_LOWRANK, GLA_HEADS * GLA_DV,
              NSA_HEADS * NSA_HEAD_DIM, 6 * NSA_KV_GROUPS * NSA_HEAD_DIM, 3 * NSA_HEADS)
SSM_SPLITS = (D_INNER, CONV_CH, SSM_HEADS)

LANES = 128
VMEM_LIMIT = 56 * 1024 * 1024

NEG_BIG = -1e30


def _cparams(sem):
    return pltpu.CompilerParams(dimension_semantics=sem, vmem_limit_bytes=VMEM_LIMIT)


def _dot(a, b):
    return jnp.dot(a, b, preferred_element_type=F32)


def _dot_nt(a, b):
    return lax.dot_general(a, b, (((1,), (1,)), ((), ())), preferred_element_type=F32)


def _dot_tn(a, b):
    return lax.dot_general(a, b, (((0,), (0,)), ((), ())), preferred_element_type=F32)


def _split_bf16(a, parts):
    out = []
    r = a
    for _ in range(parts):
        p = r.astype(BF16)
        out.append(p)
        r = r - p.astype(F32)
    return out


def _dot_exact_lhs(m, a, parts=3):
    acc = None
    for p in _split_bf16(a, parts):
        t = _dot(m, p)
        acc = t if acc is None else acc + t
    return acc


def _dot_exact_rhs(a, m, parts=3):
    acc = None
    for p in _split_bf16(a, parts):
        t = _dot(p, m)
        acc = t if acc is None else acc + t
    return acc


def _sigmoid(x):
    return 1.0 / (1.0 + jnp.exp(-x))


def _silu(x):
    return x * _sigmoid(x)


def _softplus(x):
    return jnp.maximum(x, 0.0) + jnp.log(1.0 + jnp.exp(-jnp.abs(x)))


def _gelu_tanh(x):
    return x * (0.5 * (1.0 + jnp.tanh(0.7978845608028654 * (x + 0.044715 * (x * x * x)))))


def _modulated_norm(x, nw, shift, scale):
    ms = jnp.mean(x * x, axis=-1, keepdims=True)
    y = x * lax.rsqrt(ms + NORM_EPS)
    return (y * nw) * (1.0 + scale) + shift


def _const_spec(shape):
    nd = len(shape)
    return pl.BlockSpec(shape, lambda *_: (0,) * nd, pipeline_mode=pl.Buffered(1))


def _adaln_kernel(c_ref, w0, w1, w2, w3, b0, b1, b2, b3, o_ref):
    sc = _silu(c_ref[...]).astype(BF16)
    for i, (w, b) in enumerate(((w0, b0), (w1, b1), (w2, b2), (w3, b3))):
        o_ref[i] = _dot(sc, w[...].astype(BF16)) + b[...]


def _adaln(c, ws, bs):
    bsz, d = c.shape
    n = ws[0].shape[1]
    tn = 512
    w_spec = pl.BlockSpec((d, tn), lambda j: (0, j))
    b_spec = pl.BlockSpec((1, tn), lambda j: (0, j))
    return pl.pallas_call(
        _adaln_kernel,
        grid=(n // tn,),
        in_specs=[pl.BlockSpec((bsz, d), lambda j: (0, 0))] + [w_spec] * 4 + [b_spec] * 4,
        out_specs=pl.BlockSpec((4, bsz, tn), lambda j: (0, 0, j)),
        out_shape=jax.ShapeDtypeStruct((4, bsz, n), F32),
        compiler_params=_cparams(("arbitrary",)),
        name="adaln",
    )(c, *ws, *[b[None, :] for b in bs])


def _mod_specs(tile_axes):
    def spec(part):
        if tile_axes == 2:
            return pl.BlockSpec((1, 1, D_MODEL), lambda b, i: (b, 0, part))
        return pl.BlockSpec((1, 1, D_MODEL), lambda b, i, j, k: (b, 0, part))
    return spec(0), spec(1), spec(2)


HYB_SEG = {
    "gq": (0, 256), "gk": (256, 256), "gv": (512, 512), "gr": (1024, 512), "lr_ng": (1536, 256),
    "nq": (1792, 512), "kv": (2304, 768),
}
HYB_COLS = 3072


def _hyb_in_kernel(x_ref, nw_ref, shift_ref, scale_ref, w_ref, gkup_ref, gkb_ref,
                   gq_ref, gk_ref, la_ref, gv_ref, gr_ref, nq_ref,
                   kcmp_ref, vcmp_ref, kslc_ref, vslc_ref, kwin_ref, vwin_ref, ng_ref):
    h = _modulated_norm(x_ref[0], nw_ref[...], shift_ref[0], scale_ref[0]).astype(BF16)

    def proj(name):
        off, width = HYB_SEG[name]
        return _dot(h, w_ref[:, off:off + width])

    def proj_pair(name, pair=0):
        off = HYB_SEG[name][0] + pair * 2 * LANES
        y = _dot(h, w_ref[:, off:off + 2 * LANES])
        return y[:, :LANES], y[:, LANES:]

    gq_ref[0] = proj("gq")
    gk_ref[0] = proj("gk")
    gv_ref[0] = proj("gv").astype(BF16)
    gr_ref[0] = proj("gr")
    lr, ng = proj_pair("lr_ng")
    z = _dot(lr.astype(BF16), gkup_ref[...]) + gkb_ref[...]
    la_ref[0] = -_softplus(-z) * (1.0 / GLA_TAU)
    ng_ref[0] = ng
    nq_ref[0] = proj("nq").astype(BF16)
    kcmp_ref[0], vcmp_ref[0] = proj_pair("kv", 0)
    kslc, vslc = proj_pair("kv", 1)
    kslc_ref[0] = kslc.T.astype(BF16)
    vslc_ref[0] = vslc.astype(BF16)
    kwin, vwin = proj_pair("kv", 2)
    kwin_ref[0] = kwin.T.astype(BF16)
    vwin_ref[0] = vwin.astype(BF16)


def _hyb_in(x, norm_w, mod, w_perm, gkup_pad, gk_bias):
    bsz, seq, d = x.shape
    tm = 512
    shift_s, scale_s, _ = _mod_specs(2)

    def out(width, dtype):
        return (pl.BlockSpec((1, tm, width), lambda b, i: (b, i, 0)),
                jax.ShapeDtypeStruct((bsz, seq, width), dtype))

    def out_t(dtype):
        return (pl.BlockSpec((1, LANES, tm), lambda b, i: (b, 0, i)),
                jax.ShapeDtypeStruct((bsz, LANES, seq), dtype))

    outs = [out(256, F32), out(256, F32), out(256, F32), out(512, BF16), out(512, F32), out(512, BF16),
            out(128, F32), out(128, F32), out_t(BF16), out(128, BF16), out_t(BF16), out(128, BF16),
            out(128, F32)]
    return pl.pallas_call(
        _hyb_in_kernel,
        grid=(bsz, seq // tm),
        in_specs=[pl.BlockSpec((1, tm, d), lambda b, i: (b, i, 0)),
                  _const_spec((1, d)), shift_s, scale_s,
                  _const_spec(w_perm.shape), _const_spec(gkup_pad.shape), _const_spec((1, 256))],
        out_specs=[o[0] for o in outs],
        out_shape=[o[1] for o in outs],
        compiler_params=_cparams(("parallel", "parallel")),
        name="hyb_in",
    )(x, norm_w[None, :], mod, mod, w_perm, gkup_pad, gk_bias[None, :])


GLA_ROWS = 512


def _gla_kernel(q_ref, k_ref, la_ref, v_ref, r_ref, nw_ref, o_ref, *, seq):
    rb = GLA_ROWS
    ncb = rb // GLA_CHUNK
    row = lax.broadcasted_iota(jnp.int32, (rb, rb), 0)
    col = lax.broadcasted_iota(jnp.int32, (rb, rb), 1)
    chunk_causal = ((row // GLA_CHUNK) == (col // GLA_CHUNK)) & (col <= row)
    tril_bd = jnp.where(chunk_causal, 1.0, 0.0).astype(BF16)
    lane = lax.broadcasted_iota(jnp.int32, (rb, LANES), 1)
    head_mask = (lane < GLA_DK, lane >= GLA_DK)
    nw = nw_ref[...]

    def body(i, st):
        r0 = pl.multiple_of(i * rb, rb)
        rows = pl.ds(r0, rb)
        q = q_ref[0, rows, :]
        k = k_ref[0, rows, :]
        la = la_ref[0, rows, :]
        b = _dot_exact_lhs(tril_bd, la)
        b3 = b.reshape(ncb, GLA_CHUNK, LANES)
        bend3 = b3[:, GLA_CHUNK - 1:GLA_CHUNK, :]
        bend = jnp.broadcast_to(bend3, (ncb, GLA_CHUNK, LANES)).reshape(rb, LANES)
        q_dec = (q * GLA_DK ** -0.5) * jnp.exp(b)
        k_dec = (k * jnp.exp(-b)).astype(BF16)
        k_end = k * jnp.exp(bend - b)
        decay = jnp.exp(bend3)

        qm, vh, o_intra, upd = [], [], [], []
        for h in range(2):
            qm_h = jnp.where(head_mask[h], q_dec, 0.0).astype(BF16)
            att = _dot_nt(qm_h, k_dec)
            att = jnp.where(chunk_causal, att, 0.0).astype(BF16)
            v_h = v_ref[0, rows, h * GLA_DV:(h + 1) * GLA_DV]
            o_intra.append(_dot(att, v_h))
            km_h = jnp.where(head_mask[h], k_end, 0.0).astype(BF16)
            upd.append([_dot_tn(v_h[n * GLA_CHUNK:(n + 1) * GLA_CHUNK], km_h[n * GLA_CHUNK:(n + 1) * GLA_CHUNK])
                        for n in range(ncb)])
            qm.append(qm_h)
            vh.append(v_h)

        prev = []
        for n in range(ncb):
            prev.append(st.astype(BF16))
            st = st * decay[n] + (upd[0][n] + upd[1][n])

        for h in range(2):
            o_inter = jnp.concatenate(
                [_dot_nt(qm[h][n * GLA_CHUNK:(n + 1) * GLA_CHUNK], prev[n]) for n in range(ncb)], axis=0)
            o = o_intra[h] + o_inter
            y = o * lax.rsqrt(jnp.mean(o * o, axis=-1, keepdims=True) + NORM_EPS) * nw
            r = r_ref[0, rows, h * GLA_DV:(h + 1) * GLA_DV]
            o_ref[0, rows, h * GLA_DV:(h + 1) * GLA_DV] = (y * _silu(r)).astype(o_ref.dtype)
        return st

    lax.fori_loop(0, seq // rb, body, jnp.zeros((GLA_DV, 2 * GLA_DK), F32))


def _gla(gq, gk, la, gv, gr, out_norm):
    bsz, seq, _ = gq.shape
    qk_spec = pl.BlockSpec((1, seq, 2 * GLA_DK), lambda b, p: (b, 0, p))
    v_spec = pl.BlockSpec((1, seq, 2 * GLA_DV), lambda b, p: (b, 0, p))
    return pl.pallas_call(
        functools.partial(_gla_kernel, seq=seq),
        grid=(bsz, GLA_HEADS // 2),
        in_specs=[qk_spec, qk_spec, qk_spec, v_spec, v_spec, _const_spec((1, GLA_DV))],
        out_specs=v_spec,
        out_shape=jax.ShapeDtypeStruct((bsz, seq, GLA_HEADS * GLA_DV), BF16),
        compiler_params=_cparams(("parallel", "parallel")),
        name="gla",
    )(gq, gk, la, gv, gr, out_norm[None, :])


def _nsa_compress_kernel(k_ref, v_ref, pe_ref, w1_ref, w2_ref, kc_ref, vc_ref, *, n_cmp):
    rows = k_ref.shape[1]
    rid = lax.broadcasted_iota(jnp.int32, (rows, LANES), 0)
    for j, (src, dst) in enumerate(((k_ref, kc_ref), (v_ref, vc_ref))):
        x = src[0]
        lo = _dot((x + pe_ref[j, 0]).astype(BF16), w1_ref[j, 0])
        hi = _dot((x + pe_ref[j, 1]).astype(BF16), w1_ref[j, 1])
        hpre = lo + pltpu.roll(hi, rows - 1, 0)
        out = jnp.where(rid < n_cmp, _dot(_gelu_tanh(hpre).astype(BF16), w2_ref[j]), 0.0)
        dst[0] = (out.T if j == 0 else out).astype(dst.dtype)


def _nsa_compress(kcmp, vcmp, pe_rows, w1x, w2x):
    bsz, seq, _ = kcmp.shape
    rows = seq // CMP_STRIDE
    width = CMP_STRIDE * LANES
    n_cmp = (seq - CMP_BLOCK) // CMP_STRIDE + 1
    x_spec = pl.BlockSpec((1, rows, width), lambda b: (b, 0, 0))
    return pl.pallas_call(
        functools.partial(_nsa_compress_kernel, n_cmp=n_cmp),
        grid=(bsz,),
        in_specs=[x_spec, x_spec, _const_spec(pe_rows.shape), _const_spec(w1x.shape), _const_spec(w2x.shape)],
        out_specs=[pl.BlockSpec((1, LANES, rows), lambda b: (b, 0, 0)),
                   pl.BlockSpec((1, rows, LANES), lambda b: (b, 0, 0))],
        out_shape=[jax.ShapeDtypeStruct((bsz, LANES, rows), BF16), jax.ShapeDtypeStruct((bsz, rows, LANES), BF16)],
        compiler_params=_cparams(("parallel",)),
        name="nsa_compress",
    )(kcmp.reshape(bsz, rows, width), vcmp.reshape(bsz, rows, width), pe_rows, w1x, w2x)


NSA_QT = 256
NSA_KT = 512
NSA_ROWS = NSA_HEADS * NSA_QT


NSA_MASK_PENALTY = 1e30
NSA_LANE_POS_HI, NSA_LANE_POS_LO, NSA_LANE_CMP_IDX, NSA_LANE_CMP_ONE = 64, 65, 66, 67


def _nsa_attn_kernel(q_ref, g_ref, kc_ref, vc_ref, ks_ref, vs_ref, kw_ref, vw_ref, ov_ref, kf_ref, cf_ref, o_ref,
                     m_ref, acc_ref, *, seq):
    qt, kt = NSA_QT, NSA_KT
    heads_per_group = NSA_HEADS // NSA_KV_GROUPS
    q0 = pl.program_id(1) * qt
    t = q0 + lax.broadcasted_iota(jnp.int32, (qt, 1), 0)
    lane = lax.broadcasted_iota(jnp.int32, (qt, LANES), 1)
    group_lanes = (lane < NSA_HEAD_DIM, lane >= NSA_HEAD_DIM)
    slopes = [2.0 ** (-(h + 1)) for h in range(NSA_HEADS)]

    qa = q_ref[0].astype(F32) * NSA_HEAD_DIM ** -0.5
    q_rows = []
    for h in range(NSA_HEADS):
        g = h // heads_per_group
        blk = qa[:, (h // 2) * LANES:(h // 2 + 1) * LANES]
        if h % 2 != g:
            blk = pltpu.roll(blk, NSA_HEAD_DIM, 1)
        q_rows.append(jnp.where(group_lanes[g], blk, 0.0).astype(BF16))
    q_ext = jnp.concatenate(q_rows, axis=0)

    def alibi_lanes(h):
        s = slopes[h]
        return jnp.where(lane == NSA_LANE_POS_HI, SEL_BLOCK * s,
                         jnp.where(lane == NSA_LANE_POS_LO, s,
                                   jnp.where(lane == NSA_LANE_CMP_IDX, CMP_STRIDE * s,
                                             jnp.where(lane == NSA_LANE_CMP_ONE, (CMP_BLOCK - 1) / 2.0 * s, 0.0))))

    q_pos = [alibi_lanes(h) for h in range(NSA_HEADS)]
    q_full = jnp.concatenate([q_ext, jnp.concatenate(q_pos, axis=0).astype(BF16)], axis=1)

    def softmax_parts(s, mask):
        s = jnp.where(mask, s, -jnp.inf)
        m = jnp.max(s, axis=-1, keepdims=True)
        m = jnp.where(m == -jnp.inf, 0.0, m)
        p = jnp.exp(s - m)
        return p, jnp.sum(p, axis=-1, keepdims=True)

    n_rows = kc_ref.shape[2]
    cidx = lax.broadcasted_iota(jnp.int32, (1, n_rows), 1)
    cmp_mask = cidx * CMP_STRIDE + (CMP_BLOCK - 1) <= t
    s_all = _dot(q_full, jnp.concatenate([kc_ref[0], cf_ref[...]], axis=0))
    p_list, psum = [], [None] * NSA_KV_GROUPS
    for h in range(NSA_HEADS):
        g = h // heads_per_group
        p, l = softmax_parts(s_all[h * qt:(h + 1) * qt], cmp_mask)
        p = p / jnp.maximum(l, 1e-20)
        p_list.append(p.astype(BF16))
        psum[g] = p if psum[g] is None else psum[g] + p
    o_cmp = _dot(jnp.concatenate(p_list, axis=0), vc_ref[0])

    n_blk = LANES // 2
    tq = q0 + lax.broadcasted_iota(jnp.int32, (n_blk, qt), 1)
    bidx = lax.broadcasted_iota(jnp.int32, (n_blk, qt), 0)
    bidx_f = bidx.astype(F32)
    blk_t = tq // SEL_BLOCK
    forced = (bidx == 0) | (bidx == blk_t) | (bidx == blk_t - 1)
    future = bidx * SEL_BLOCK > tq
    q_sel_rows = []
    for g in range(NSA_KV_GROUPS):
        imp = _dot_exact_rhs(psum[g], ov_ref[...]).T[:n_blk]
        v = jnp.where(future, -1.0, jnp.where(forced, 3e38, imp))
        chosen = jnp.zeros((n_blk, qt), F32)
        for _ in range(min(N_SELECT, seq // SEL_BLOCK)):
            m = jnp.max(v, axis=0, keepdims=True)
            idx = jnp.min(jnp.where(v == m, bidx_f, float(LANES)), axis=0, keepdims=True)
            pick = (bidx_f == idx) & (m >= 0.0)
            chosen = jnp.where(pick, 1.0, chosen)
            v = jnp.where(pick, -1.0, v)
        penalty = jnp.concatenate([(chosen - 1.0) * NSA_MASK_PENALTY, jnp.zeros((n_blk, qt), F32)], axis=0).T
        for r in range(heads_per_group):
            q_sel_rows.append((penalty + q_pos[g * heads_per_group + r]).astype(BF16))
    q_full_sel = jnp.concatenate([q_ext, jnp.concatenate(q_sel_rows, axis=0)], axis=1)

    vlane = lax.broadcasted_iota(jnp.int32, (1, LANES), 1)
    own_lanes = (vlane < NSA_HEAD_DIM, vlane >= NSA_HEAD_DIM)

    def weighted_values(p_rows, v):
        v = v.astype(F32)
        outs = []
        for g in range(NSA_KV_GROUPS):
            pg = jnp.concatenate(p_rows[g * heads_per_group:(g + 1) * heads_per_group], axis=0)
            outs.append(_dot(pg, jnp.where(own_lanes[g], v, 1.0).astype(BF16)))
        return jnp.concatenate(outs, axis=0)

    def normalised(acc):
        return acc / jnp.maximum(pltpu.roll(acc, NSA_HEAD_DIM, 1), 1e-20)

    m_ref[...] = jnp.full(m_ref.shape, -jnp.inf, F32)
    acc_ref[...] = jnp.zeros(acc_ref.shape, F32)
    kpos = lax.broadcasted_iota(jnp.int32, (1, kt), 1)

    def sel_tile(j, causal):
        k0 = pl.multiple_of(j * kt, kt)
        ks = jnp.concatenate([ks_ref[0, :, pl.ds(k0, kt)], kf_ref[:, pl.ds(k0, kt)]], axis=0)
        s_t = _dot(q_full_sel, ks)
        visible = (k0 + kpos) <= t
        p_rows = []
        for h in range(NSA_HEADS):
            rows = slice(h * qt, (h + 1) * qt)
            s = s_t[rows]
            if causal:
                s = jnp.where(visible, s, -jnp.inf)
            m_old = m_ref[rows]
            m_new = jnp.maximum(m_old, jnp.max(s, axis=-1, keepdims=True))
            m_safe = jnp.where(m_new == -jnp.inf, 0.0, m_new)
            p_rows.append(jnp.exp(s - jnp.concatenate([m_safe] * (kt // LANES), axis=1)).astype(BF16))
            acc_ref[rows] = jnp.exp(m_old - m_safe) * acc_ref[rows]
            m_ref[rows] = m_new
        acc_ref[...] += weighted_values(p_rows, vs_ref[0, pl.ds(k0, kt), :])

    def sel_past_tile(j, carry):
        sel_tile(j, False)
        return carry

    last = q0 // kt
    lax.fori_loop(0, last, sel_past_tile, 0)
    sel_tile(last, True)
    o_slc = normalised(acc_ref[...])

    wk = WINDOW + qt
    w0 = pl.multiple_of(jnp.maximum(q0 - WINDOW, 0), qt)
    kw = jnp.concatenate([kw_ref[0, :, pl.ds(w0, wk)], kf_ref[:, pl.ds(w0, wk)]], axis=0)
    wdist = t - (w0 + lax.broadcasted_iota(jnp.int32, (1, wk), 1))
    wmask = (wdist >= 0) & (wdist < WINDOW)
    s_all = _dot(q_full, kw)
    p_rows = []
    for h in range(NSA_HEADS):
        s = jnp.where(wmask, s_all[h * qt:(h + 1) * qt], -jnp.inf)
        m = jnp.max(s, axis=-1, keepdims=True)
        p_rows.append(jnp.exp(s - jnp.where(m == -jnp.inf, 0.0, m)).astype(BF16))
    o_win = normalised(weighted_values(p_rows, vw_ref[0, pl.ds(w0, wk), :]))

    gates = _sigmoid(g_ref[0])
    o_heads = []
    for h in range(NSA_HEADS):
        rows = slice(h * qt, (h + 1) * qt)
        o_heads.append(gates[:, 3 * h:3 * h + 1] * o_cmp[rows] + gates[:, 3 * h + 1:3 * h + 2] * o_slc[rows]
                       + gates[:, 3 * h + 2:3 * h + 3] * o_win[rows])
    for c in range(NSA_HEADS // 2):
        g = (2 * c) // heads_per_group
        a, b = o_heads[2 * c], o_heads[2 * c + 1]
        if g == 0:
            blk = jnp.where(group_lanes[0], a, pltpu.roll(b, NSA_HEAD_DIM, 1))
        else:
            blk = jnp.where(group_lanes[0], pltpu.roll(a, NSA_HEAD_DIM, 1), b)
        o_ref[0, :, c * LANES:(c + 1) * LANES] = blk.astype(o_ref.dtype)


def _nsa_position_features(seq):
    assert seq // SEL_BLOCK <= LANES // 2
    pos = np.arange(seq)
    kf = np.zeros((seq, LANES), np.float32)
    kf[pos, pos // SEL_BLOCK] = 1.0
    kf[:, NSA_LANE_POS_HI] = pos // SEL_BLOCK
    kf[:, NSA_LANE_POS_LO] = pos % SEL_BLOCK
    n_rows = seq // CMP_STRIDE
    cf = np.zeros((n_rows, LANES), np.float32)
    cf[:, NSA_LANE_CMP_IDX] = np.arange(n_rows)
    cf[:, NSA_LANE_CMP_ONE] = 1.0
    return jnp.asarray(kf.T, BF16), jnp.asarray(cf.T, BF16)


def _nsa_attn(nq, ng, kc, vc, kslc, vslc, kwin, vwin, overlap):
    bsz, seq, _ = nq.shape
    n_rows = vc.shape[1]
    kfeat, cfeat = _nsa_position_features(seq)
    full = lambda rows: pl.BlockSpec((1, rows, LANES), lambda b, i: (b, 0, 0))
    full_t = lambda cols: pl.BlockSpec((1, LANES, cols), lambda b, i: (b, 0, 0))
    return pl.pallas_call(
        functools.partial(_nsa_attn_kernel, seq=seq),
        grid=(bsz, seq // NSA_QT),
        in_specs=[pl.BlockSpec((1, NSA_QT, NSA_HEADS * NSA_HEAD_DIM), lambda b, i: (b, i, 0)),
                  pl.BlockSpec((1, NSA_QT, LANES), lambda b, i: (b, i, 0)),
                  full_t(n_rows), full(n_rows), full_t(seq), full(seq), full_t(seq), full(seq),
                  _const_spec(overlap.shape), _const_spec(kfeat.shape), _const_spec(cfeat.shape)],
        out_specs=pl.BlockSpec((1, NSA_QT, NSA_HEADS * NSA_HEAD_DIM), lambda b, i: (b, i, 0)),
        out_shape=jax.ShapeDtypeStruct((bsz, seq, NSA_HEADS * NSA_HEAD_DIM), BF16),
        scratch_shapes=[pltpu.VMEM((NSA_ROWS, LANES), F32), pltpu.VMEM((NSA_ROWS, LANES), F32)],
        compiler_params=_cparams(("parallel", "arbitrary")),
        name="nsa_attn",
    )(nq, ng, kc, vc, kslc, vslc, kwin, vwin, overlap, kfeat, cfeat)


FFN_CHUNK = 1408


def _hyb_out_ffn_kernel(x_ref, oa_ref, ob_ref, g1_ref, wa_ref, wb_ref,
                        nw_ref, shift_ref, scale_ref, g2_ref, wg_ref, wu_ref, wd_ref, o_ref, act_ref):
    mix = _dot(oa_ref[0], wa_ref[...]) + _dot(ob_ref[0], wb_ref[...])
    x1 = x_ref[0] + g1_ref[0] * mix
    h = _modulated_norm(x1, nw_ref[...], shift_ref[0], scale_ref[0]).astype(BF16)
    for f in range(0, FFN_DENSE, FFN_CHUNK):
        gate = _dot(h, wg_ref[:, f:f + FFN_CHUNK])
        up = _dot(h, wu_ref[:, f:f + FFN_CHUNK])
        act_ref[:, f:f + FFN_CHUNK] = (_silu(gate) * up).astype(BF16)
    o_ref[0] = x1 + g2_ref[0] * _dot(act_ref[...], wd_ref[...])


def _hyb_out_ffn(x, o_gla, o_nsa, mod1, wa, wb, norm_w, mod2, wg, wu, wd):
    bsz, seq, d = x.shape
    tm = 512
    _, _, gate1 = _mod_specs(2)
    shift2, scale2, gate2 = _mod_specs(2)
    tile = lambda width: pl.BlockSpec((1, tm, width), lambda b, i: (b, i, 0))
    return pl.pallas_call(
        _hyb_out_ffn_kernel,
        grid=(bsz, seq // tm),
        in_specs=[tile(d), tile(o_gla.shape[-1]), tile(o_nsa.shape[-1]), gate1,
                  _const_spec(wa.shape), _const_spec(wb.shape), _const_spec((1, d)), shift2, scale2, gate2,
                  _const_spec(wg.shape), _const_spec(wu.shape), _const_spec(wd.shape)],
        out_specs=tile(d),
        out_shape=jax.ShapeDtypeStruct((bsz, seq, d), F32),
        scratch_shapes=[pltpu.VMEM((tm, FFN_DENSE), BF16)],
        compiler_params=_cparams(("parallel", "parallel")),
        name="hyb_out_ffn",
    )(x, o_gla, o_nsa, mod1, wa, wb, norm_w[None, :], mod2, mod2, mod2, wg, wu, wd)


SSM_DT_OFF = D_INNER + CONV_CH
SSM_COLS = SSM_DT_OFF + LANES
SSM_IN_CHUNK = 512


CONV_TAIL = 8


def _ssm_in_kernel(x_ref, nw_ref, shift_ref, scale_ref, w_ref, cw_ref, cb_ref, dtb_ref,
                   z_ref, xbc_ref, dt_ref, xb_ref):
    tm = x_ref.shape[1]

    @pl.when(pl.program_id(1) == 0)
    def _():
        xb_ref[0:CONV_TAIL, :] = jnp.zeros((CONV_TAIL, CONV_CH), F32)

    h = _modulated_norm(x_ref[0], nw_ref[...], shift_ref[0], scale_ref[0]).astype(BF16)
    for c in range(0, CONV_CH, SSM_IN_CHUNK):
        xb_ref[CONV_TAIL:, c:c + SSM_IN_CHUNK] = _dot(h, w_ref[:, D_INNER + c:D_INNER + c + SSM_IN_CHUNK])
    dt_ref[0] = _softplus(_dot(h, w_ref[:, SSM_DT_OFF:SSM_COLS]) + dtb_ref[...])

    def conv_chunk(c):
        cols = slice(c, c + SSM_IN_CHUNK)
        acc = cb_ref[:, cols]
        for j in range(CONV_K):
            acc = acc + xb_ref[CONV_TAIL - j:CONV_TAIL - j + tm, cols] * cw_ref[CONV_K - 1 - j:CONV_K - j, cols]
        xbc_ref[0, :, cols] = _silu(acc)

    def gate_chunk(c):
        z_ref[0, :, c:c + SSM_IN_CHUNK] = _silu(_dot(h, w_ref[:, c:c + SSM_IN_CHUNK]))

    conv_chunks = list(range(0, CONV_CH, SSM_IN_CHUNK))
    gate_chunks = list(range(0, D_INNER, SSM_IN_CHUNK))
    for i in range(max(len(conv_chunks), len(gate_chunks))):
        if i < len(gate_chunks):
            gate_chunk(gate_chunks[i])
        if i < len(conv_chunks):
            conv_chunk(conv_chunks[i])
    xb_ref[0:CONV_TAIL, :] = xb_ref[tm:tm + CONV_TAIL, :]


def _ssm_in(x, norm_w, mod, w_perm, conv_w, conv_b, dt_bias_pad):
    bsz, seq, d = x.shape
    tm = 512
    shift_s, scale_s, _ = _mod_specs(2)
    tile = lambda width: pl.BlockSpec((1, tm, width), lambda b, i: (b, i, 0))
    return pl.pallas_call(
        _ssm_in_kernel,
        grid=(bsz, seq // tm),
        in_specs=[tile(d), _const_spec((1, d)), shift_s, scale_s, _const_spec(w_perm.shape),
                  _const_spec((CONV_K, CONV_CH)), _const_spec((1, CONV_CH)), _const_spec((1, LANES))],
        out_specs=[tile(D_INNER), tile(CONV_CH), tile(LANES)],
        out_shape=[jax.ShapeDtypeStruct((bsz, seq, D_INNER), F32),
                   jax.ShapeDtypeStruct((bsz, seq, CONV_CH), F32),
                   jax.ShapeDtypeStruct((bsz, seq, LANES), F32)],
        scratch_shapes=[pltpu.VMEM((CONV_TAIL + tm, CONV_CH), F32)],
        compiler_params=_cparams(("parallel", "arbitrary")),
        name="ssm_in",
    )(x, norm_w[None, :], mod, mod, w_perm, conv_w, conv_b[None, :], dt_bias_pad)


SSD_ROWS = 512
SSD_GW = D_INNER // SSM_GROUPS
SSD_HPG = SSM_HEADS // SSM_GROUPS
SSD_B_OFF = D_INNER
SSD_C_OFF = D_INNER + SSM_GROUPS * SSM_STATE


def _ssd_kernel(xbc_ref, z_ref, dt_ref, alog_ref, dskip_ref, nw_ref, ex_ref, o_ref,
                state_ref, xdt_ref, cum_ref, y_ref):
    rb, q = SSD_ROWS, SSM_CHUNK
    nchunk = rb // q

    @pl.when(pl.program_id(1) == 0)
    def _():
        state_ref[...] = jnp.zeros(state_ref.shape, F32)

    dt = dt_ref[0]
    a = dt * (-jnp.exp(alog_ref[...]))
    row = lax.broadcasted_iota(jnp.int32, (rb, rb), 0)
    col = lax.broadcasted_iota(jnp.int32, (rb, rb), 1)
    tril_bd = jnp.where(((row // q) == (col // q)) & (col <= row), 1.0, 0.0).astype(BF16)
    cum = _dot_exact_lhs(tril_bd, a)

    lrow = lax.broadcasted_iota(jnp.int32, (q, SSD_GW), 0)
    lcol = lax.broadcasted_iota(jnp.int32, (q, SSD_GW), 1) % q
    causal_t = lcol <= lrow
    eye_t = lcol == lrow
    half = SSD_GW // 2
    brow = lax.broadcasted_iota(jnp.int32, (half, half), 0) // q
    bcol = lax.broadcasted_iota(jnp.int32, (half, half), 1) // SSM_HEAD_DIM
    same_head = brow == bcol

    for g in range(SSM_GROUPS):
        xg = xbc_ref[0, :, g * SSD_GW:(g + 1) * SSD_GW]
        y_ref[g] = dskip_ref[:, g * SSD_GW:(g + 1) * SSD_GW] * xg
        xdt_ref[g] = xg * _dot_exact_rhs(dt, ex_ref[g])
        cum_ref[g] = _dot_exact_rhs(cum, ex_ref[g])

    def chunk(n, carry):
        rows = pl.ds(pl.multiple_of(n * q, q), q)
        for g in range(SSM_GROUPS):
            cum_c = cum_ref[g, rows, :]
            cum_s = jnp.sum(jnp.where(eye_t, cum_c, 0.0), axis=0, keepdims=True)
            decay_l = jnp.where(causal_t, jnp.exp(cum_c - cum_s), 0.0)
            bc = xbc_ref[0, rows, SSD_B_OFF + g * SSM_STATE:SSD_B_OFF + (g + 1) * SSM_STATE].astype(BF16)
            cc = xbc_ref[0, rows, SSD_C_OFF + g * SSM_STATE:SSD_C_OFF + (g + 1) * SSM_STATE].astype(BF16)
            cb_t = _dot_nt(cc, jnp.concatenate([bc] * SSD_HPG, axis=0))
            mat = (cb_t * decay_l).astype(BF16)
            xdt_c = xdt_ref[g, rows, :]
            xdt_b = xdt_c.astype(BF16)
            y_diag = []
            for s in range(2):
                blk = xdt_b[:, s * half:(s + 1) * half]
                bd = jnp.where(same_head, jnp.concatenate([blk] * (half // q), axis=0), 0.0).astype(BF16)
                y_diag.append(_dot(mat[:, s * half:(s + 1) * half], bd))
            y = jnp.concatenate(y_diag, axis=1)
            cum_end = cum_c[q - 1:q, :]
            st = state_ref[g]
            y = y + _dot(cc, st.astype(BF16)) * jnp.exp(cum_c)
            x_end = (xdt_c * jnp.exp(cum_end - cum_c)).astype(BF16)
            state_ref[g] = st * jnp.exp(cum_end) + _dot_tn(bc, x_end)
            y_ref[g, rows, :] += y
        return carry

    lax.fori_loop(0, nchunk, chunk, 0)

    for g in range(SSM_GROUPS):
        cols = slice(g * SSD_GW, (g + 1) * SSD_GW)
        y = y_ref[g] * z_ref[0, :, cols]
        y = y * lax.rsqrt(jnp.mean(y * y, axis=-1, keepdims=True) + NORM_EPS) * nw_ref[:, cols]
        o_ref[0, :, cols] = y.astype(o_ref.dtype)


def _ssd(xbc, z, dt, a_log_pad, d_skip_x, norm_w, expand):
    bsz, seq, _ = xbc.shape
    rb = SSD_ROWS
    tile = lambda width: pl.BlockSpec((1, rb, width), lambda b, i: (b, i, 0))
    return pl.pallas_call(
        _ssd_kernel,
        grid=(bsz, seq // rb),
        in_specs=[tile(CONV_CH), tile(D_INNER), tile(LANES), _const_spec((1, LANES)),
                  _const_spec((1, D_INNER)), _const_spec((1, D_INNER)), _const_spec(expand.shape)],
        out_specs=tile(D_INNER),
        out_shape=jax.ShapeDtypeStruct((bsz, seq, D_INNER), BF16),
        scratch_shapes=[pltpu.VMEM((SSM_GROUPS, SSM_STATE, SSD_GW), F32),
                        pltpu.VMEM((SSM_GROUPS, rb, SSD_GW), F32), pltpu.VMEM((SSM_GROUPS, rb, SSD_GW), F32),
                        pltpu.VMEM((SSM_GROUPS, rb, SSD_GW), F32)],
        compiler_params=_cparams(("parallel", "arbitrary")),
        name="ssd",
    )(xbc, z, dt, a_log_pad, d_skip_x, norm_w[None, :], expand)


def _pack_bf16_pairs(a):
    w = a.shape[1] // 2
    bits = lax.bitcast_convert_type(a.astype(jnp.bfloat16).astype(F32), jnp.uint32)
    packed = bits[:, w:] | (bits[:, :w] >> 16)
    return lax.bitcast_convert_type(packed, jnp.int32)


def _unpack_bf16_pairs(p):
    bits = lax.bitcast_convert_type(p, jnp.uint32)
    lo = lax.bitcast_convert_type(bits << 16, F32)
    hi = lax.bitcast_convert_type(bits & jnp.uint32(0xFFFF0000), F32)
    return jnp.concatenate([lo, hi], axis=1).astype(BF16)


def _ssm_out_route_kernel(x_ref, y_ref, g1_ref, w_ref, nw_ref, shift_ref, scale_ref, r_ref,
                          x3_ref, h_ref, gw_ref, sel_ref, cnt_ref):
    x3 = x_ref[0] + g1_ref[0] * _dot(y_ref[0], w_ref[...])
    x3_ref[0] = x3
    h = _modulated_norm(x3, nw_ref[...], shift_ref[0], scale_ref[0])
    h_ref[0] = _pack_bf16_pairs(h)
    h_hi, h_lo = _split_bf16(h, 2)
    r_hi, r_lo = _split_bf16(r_ref[...], 2)
    both = _dot(h_hi, jnp.concatenate([r_hi, r_lo], axis=1))
    logits = both[:, :LANES] + (both[:, LANES:] + _dot(h_lo, r_hi))
    lane = lax.broadcasted_iota(jnp.int32, logits.shape, 1)
    logits = jnp.where(lane < N_EXPERTS, logits, -jnp.inf)
    m1 = jnp.max(logits, axis=-1, keepdims=True)
    i1 = jnp.min(jnp.where(logits == m1, lane, LANES), axis=-1, keepdims=True)
    rest = jnp.where(lane == i1, -jnp.inf, logits)
    m2 = jnp.max(rest, axis=-1, keepdims=True)
    i2 = jnp.min(jnp.where(rest == m2, lane, LANES), axis=-1, keepdims=True)
    e2 = jnp.exp(m2 - m1)
    w1 = 1.0 / (1.0 + e2)
    w2 = e2 / (1.0 + e2)
    chosen = (lane == i1) | (lane == i2)
    gw_ref[0] = jnp.where(lane == i1, w1, 0.0) + jnp.where(lane == i2, w2, 0.0)
    sel = jnp.where(chosen, 1.0, 0.0)
    sel_ref[0] = sel.astype(BF16)
    cnt_ref[0] = jnp.sum(sel, axis=0, keepdims=True)


def _ssm_out_route(x, y, mod1, w_out, norm_w, mod2, router_pad):
    bsz, seq, d = x.shape
    tm = ROUTE_TM
    tiles = seq // tm
    _, _, gate1 = _mod_specs(2)
    shift2, scale2, _ = _mod_specs(2)
    tile = lambda width: pl.BlockSpec((1, tm, width), lambda b, i: (b, i, 0))
    return pl.pallas_call(
        _ssm_out_route_kernel,
        grid=(bsz, tiles),
        in_specs=[tile(d), tile(D_INNER), gate1, _const_spec(w_out.shape), _const_spec((1, d)), shift2, scale2,
                  _const_spec(router_pad.shape)],
        out_specs=[tile(d), tile(d // 2), tile(LANES), tile(LANES),
                   pl.BlockSpec((1, 1, LANES), lambda b, i: (b * tiles + i, 0, 0))],
        out_shape=[jax.ShapeDtypeStruct((bsz, seq, d), F32), jax.ShapeDtypeStruct((bsz, seq, d // 2), jnp.int32),
                   jax.ShapeDtypeStruct((bsz, seq, LANES), F32), jax.ShapeDtypeStruct((bsz, seq, LANES), BF16),
                   jax.ShapeDtypeStruct((bsz * tiles, 1, LANES), F32)],
        compiler_params=_cparams(("parallel", "parallel")),
        name="ssm_out_route",
    )(x, y, mod1, w_out, norm_w[None, :], mod2, mod2, router_pad)


ROUTE_TM = 512


def _moe_plan_kernel(gw_ref, sel_ref, base_ref, pos_ref, wts_ref):
    tm = gw_ref.shape[0]
    gw = gw_ref[...]
    sel = sel_ref[...]
    lane = lax.broadcasted_iota(jnp.int32, (tm, LANES), 1)
    chosen = sel.astype(F32) > 0.5
    row = lax.broadcasted_iota(jnp.int32, (tm, tm), 0)
    col = lax.broadcasted_iota(jnp.int32, (tm, tm), 1)
    before = jnp.where(col < row, 1.0, 0.0).astype(BF16)
    rank = _dot(before, sel)
    dest = rank + base_ref[0]
    first = jnp.min(jnp.where(chosen, lane, LANES), axis=-1, keepdims=True)
    last = jnp.max(jnp.where(chosen, lane, -1), axis=-1, keepdims=True)
    ones = jnp.ones((8, LANES), BF16)
    for slot, pick in enumerate((lane == first, lane == last)):
        parts = _split_bf16(jnp.where(pick, dest, 0.0), 3)
        pos = _dot_nt(ones, parts[0]) + (_dot_nt(ones, parts[1]) + _dot_nt(ones, parts[2]))
        pos_ref[slot] = pos.astype(jnp.int32)
    w_lo = jnp.sum(jnp.where(lane == first, gw, 0.0), axis=-1, keepdims=True)
    w_hi = jnp.sum(jnp.where(lane == last, gw, 0.0), axis=-1, keepdims=True)
    wts_ref[...] = jnp.where(lane == 0, w_lo, jnp.where(lane == 1, w_hi, 0.0))


def _moe_plan(gw, sel, base):
    n = gw.shape[0]
    tm = ROUTE_TM
    return pl.pallas_call(
        _moe_plan_kernel,
        grid=(n // tm,),
        in_specs=[pl.BlockSpec((tm, LANES), lambda i: (i, 0)), pl.BlockSpec((tm, LANES), lambda i: (i, 0)),
                  pl.BlockSpec((1, 1, LANES), lambda i: (i, 0, 0))],
        out_specs=[pl.BlockSpec((2, 8, tm), lambda i: (0, 0, i)), pl.BlockSpec((tm, LANES), lambda i: (i, 0))],
        out_shape=[jax.ShapeDtypeStruct((2, 8, n), jnp.int32), jax.ShapeDtypeStruct((n, LANES), F32)],
        compiler_params=_cparams(("parallel",)),
        name="moe_plan",
    )(gw, sel, base)


SC_WINDOW = 128


SC_CORES = 2
SC_SUBCORES = 16


def _sc_workers():
    return SC_CORES, SC_CORES * SC_SUBCORES


def _sc_dispatch(rows, pos):
    n, width = rows.shape
    ncores, workers = _sc_workers()
    per_worker = n // workers
    steps = per_worker // SC_WINDOW
    mesh = plsc.VectorSubcoreMesh(core_axis_name="c", subcore_axis_name="s")

    @functools.partial(
        pl.kernel, mesh=mesh, out_type=jax.ShapeDtypeStruct((2 * n, width), rows.dtype),
        scratch_types=[pltpu.VMEM((2, steps, SC_WINDOW), jnp.int32), pltpu.VMEM((SC_WINDOW, width), rows.dtype)],
        name="moe_dispatch")
    def run(rows_hbm, pos_hbm, out_hbm, idx_v, rows_v):
        wid = lax.axis_index("s") * ncores + lax.axis_index("c")
        pltpu.sync_copy(pos_hbm.at[wid], idx_v)
        for j in range(steps):
            pltpu.sync_copy(rows_hbm.at[pl.ds(wid * per_worker + j * SC_WINDOW, SC_WINDOW)], rows_v)
            pltpu.sync_copy(rows_v, out_hbm.at[idx_v.at[0, j]])
            pltpu.sync_copy(rows_v, out_hbm.at[idx_v.at[1, j]])

    pos_w = pos.reshape(2, workers, steps, SC_WINDOW).transpose(1, 0, 2, 3)
    return run(rows, pos_w)


SC_GATHER_WINDOW = 128


def _sc_gather(table, pos):
    _, width = table.shape
    n = pos.shape[1]
    ncores, workers = _sc_workers()
    per_worker = n // workers
    win = SC_GATHER_WINDOW
    steps = per_worker // win
    mesh = plsc.VectorSubcoreMesh(core_axis_name="c", subcore_axis_name="s")

    @functools.partial(
        pl.kernel, mesh=mesh, out_type=jax.ShapeDtypeStruct((2, n, width), table.dtype),
        scratch_types=[pltpu.VMEM((2, steps, win), jnp.int32), pltpu.VMEM((win, width), table.dtype)],
        name="moe_gather")
    def run(table_hbm, pos_hbm, out_hbm, idx_v, rows_v):
        wid = lax.axis_index("s") * ncores + lax.axis_index("c")
        pltpu.sync_copy(pos_hbm.at[wid], idx_v)
        for s in range(2):
            for j in range(steps):
                pltpu.sync_copy(table_hbm.at[idx_v.at[s, j]], rows_v)
                pltpu.sync_copy(rows_v, out_hbm.at[s, pl.ds(wid * per_worker + j * win, win)])

    pos_w = pos.reshape(2, workers, steps, win).transpose(1, 0, 2, 3)
    return run(table, pos_w)


MOE_TM = 512
MOE_TF = 896


def _moe_pairs(goff, n_rows):
    tiles = n_rows // MOE_TM
    steps = tiles + N_EXPERTS - 1
    first_row = jnp.arange(tiles, dtype=jnp.int32) * MOE_TM
    ends = goff[1:]
    e_lo = jnp.sum(first_row[:, None] >= ends[None, :], axis=1).astype(jnp.int32)
    e_hi = jnp.sum((first_row + (MOE_TM - 1))[:, None] >= ends[None, :], axis=1).astype(jnp.int32)
    count = e_hi - e_lo + 1
    start = jnp.cumsum(count) - count
    p = jnp.arange(steps, dtype=jnp.int32)
    tile = jnp.sum(start[None, :] <= p[:, None], axis=1).astype(jnp.int32) - 1
    expert = e_lo[tile] + (p - start[tile])
    valid = p < jnp.sum(count)
    return (jnp.where(valid, tile, tiles - 1).astype(jnp.int32),
            jnp.where(valid, expert, N_EXPERTS - 1).astype(jnp.int32), valid.astype(jnp.int32))


def _moe_group_kernel(tile_ref, exp_ref, valid_ref, goff_ref, x_ref, wg_ref, wu_ref, wd_ref, y_ref, act_ref):
    p = pl.program_id(0)
    tm = x_ref.shape[0]
    tile = tile_ref[p]
    expert = exp_ref[p]

    @pl.when((p == 0) | (tile != tile_ref[jnp.maximum(p - 1, 0)]))
    def _():
        y_ref[...] = jnp.zeros(y_ref.shape, y_ref.dtype)

    @pl.when(valid_ref[p] == 1)
    def _():
        x = _unpack_bf16_pairs(x_ref[...])
        for f in range(0, FFN_EXPERT, MOE_TF):
            act_ref[:, f:f + MOE_TF] = (_silu(_dot(x, wg_ref[0, :, f:f + MOE_TF]))
                                        * _dot(x, wu_ref[0, :, f:f + MOE_TF])).astype(BF16)
        out = _pack_bf16_pairs(_dot(act_ref[...], wd_ref[0]))
        rows = tile * tm + lax.broadcasted_iota(jnp.int32, (tm, 1), 0)
        mine = (rows >= goff_ref[expert]) & (rows < goff_ref[expert + 1])
        y_ref[...] = jnp.where(mine, out, y_ref[...])


def _moe_group(xs, goff, wg, wu, wd):
    n_rows, half = xs.shape
    d = 2 * half
    tm = MOE_TM
    tile, expert, valid = _moe_pairs(goff, n_rows)
    resident = lambda shape: pl.BlockSpec(shape, lambda p, t, e, v, g: (e[p], 0, 0), pipeline_mode=pl.Buffered(1))
    grid_spec = pltpu.PrefetchScalarGridSpec(
        num_scalar_prefetch=4,
        grid=(tile.shape[0],),
        in_specs=[pl.BlockSpec((tm, half), lambda p, t, e, v, g: (t[p], 0)),
                  resident((1, d, FFN_EXPERT)), resident((1, d, FFN_EXPERT)), resident((1, FFN_EXPERT, d))],
        out_specs=pl.BlockSpec((tm, half), lambda p, t, e, v, g: (t[p], 0)),
        scratch_shapes=[pltpu.VMEM((tm, FFN_EXPERT), BF16)],
    )
    return pl.pallas_call(
        _moe_group_kernel,
        grid_spec=grid_spec,
        out_shape=jax.ShapeDtypeStruct((n_rows, half), jnp.int32),
        compiler_params=_cparams(("arbitrary",)),
        name="moe_group",
    )(tile, expert, valid, goff, xs, wg, wu, wd)


def _moe_combine_kernel(x_ref, yg_ref, wts_ref, g_ref, fn_ref, o_ref):
    w = wts_ref[0]
    mix = (w[:, 0:1] * _unpack_bf16_pairs(yg_ref[0, 0]).astype(F32)
           + w[:, 1:2] * _unpack_bf16_pairs(yg_ref[1, 0]).astype(F32))
    x4 = x_ref[0] + g_ref[0] * mix
    y = x4 * lax.rsqrt(jnp.mean(x4 * x4, axis=-1, keepdims=True) + NORM_EPS)
    o_ref[0] = y * fn_ref[...]


def _moe_combine(x, yg, wts, mod, final_norm):
    bsz, seq, d = x.shape
    tm = 512
    _, _, gate = _mod_specs(2)
    tile = lambda width: pl.BlockSpec((1, tm, width), lambda b, i: (b, i, 0))
    return pl.pallas_call(
        _moe_combine_kernel,
        grid=(bsz, seq // tm),
        in_specs=[tile(d), pl.BlockSpec((2, 1, tm, d // 2), lambda b, i: (0, b, i, 0)), tile(LANES), gate,
                  _const_spec((1, d))],
        out_specs=tile(d),
        out_shape=jax.ShapeDtypeStruct((bsz, seq, d), F32),
        compiler_params=_cparams(("parallel", "parallel")),
        name="moe_combine",
    )(x, yg, wts, mod, final_norm[None, :])


def _pad_cols(a, width):
    return jnp.pad(a, ((0, 0), (0, width - a.shape[1])))


def _split(a, sizes):
    return jnp.split(a, [int(s) for s in np.cumsum(sizes)[:-1]], axis=-1)


def _prep_hyb_w_in(w):
    q_a, k_a, v_a, lr_a, r_a, q_b, kv_b, g_b = _split(w, HYB_SPLITS)
    return jnp.concatenate([q_a, k_a, v_a, r_a, _pad_cols(lr_a, LANES), _pad_cols(g_b, LANES), q_b, kv_b],
                           axis=1).astype(BF16)


def _prep_cmp(pe, w1, w2):
    eye = jnp.eye(NSA_KV_GROUPS, dtype=F32)
    half = CMP_BLOCK // 2
    w1r = w1.reshape(2, 2, half, NSA_HEAD_DIM, CMP_HIDDEN)
    w1x = jnp.einsum("jstdc,gh->jstgdhc", w1r, eye).reshape(2, 2, half * LANES, NSA_KV_GROUPS * CMP_HIDDEN)
    w2x = jnp.einsum("jcd,gh->jgchd", w2, eye).reshape(2, NSA_KV_GROUPS * CMP_HIDDEN, LANES)
    per = pe.reshape(2, 2, half, 1, NSA_HEAD_DIM)
    pe_rows = jnp.broadcast_to(per, (2, 2, half, NSA_KV_GROUPS, NSA_HEAD_DIM)).reshape(2, 2, 1, half * LANES)
    return pe_rows, w1x.astype(BF16), w2x.astype(BF16)


def _overlap_matrix(seq):
    n_rows = seq // CMP_STRIDE
    n = np.arange(n_rows)[:, None] * CMP_STRIDE
    s = np.arange(LANES)[None, :] * SEL_BLOCK
    ov = (n < s + SEL_BLOCK) & (n + CMP_BLOCK - 1 >= s) & (np.arange(LANES)[None, :] < seq // SEL_BLOCK)
    return jnp.asarray(ov, BF16)


def _head_expand():
    ex = np.zeros((SSM_GROUPS, LANES, SSD_GW), np.float32)
    for g in range(SSM_GROUPS):
        for j in range(SSD_HPG):
            ex[g, g * SSD_HPG + j, j * SSM_HEAD_DIM:(j + 1) * SSM_HEAD_DIM] = 1.0
    return jnp.asarray(ex, BF16)


def kernel(x, c, hyb_norm, hyb_mod_w, hyb_mod_b, hyb_w_in, gla_gk_up, gla_gk_bias, gla_out_norm, nsa_cmp_pe, nsa_cmp_w1, nsa_cmp_w2, hyb_w_out, dense_norm, dense_mod_w, dense_mod_b, dense_w_gu, dense_w_down, ssm_norm, ssm_mod_w, ssm_mod_b, ssm_w_in, ssm_conv_w, ssm_conv_b, ssm_dt_bias, ssm_a_log, ssm_d, ssm_gate_norm, ssm_w_out, moe_norm, moe_mod_w, moe_mod_b, moe_router, moe_w_gu, moe_w_down, final_norm):
    bsz, seq, d = x.shape
    mods = _adaln(c, (hyb_mod_w[0], dense_mod_w[0], ssm_mod_w[0], moe_mod_w[0]),
                  (hyb_mod_b[0], dense_mod_b[0], ssm_mod_b[0], moe_mod_b[0]))
    mods = mods.reshape(4, bsz, 1, 3 * d)

    gkup_pad = jnp.pad(gla_gk_up[0], ((0, LANES - GLA_LOWRANK), (0, 0))).astype(BF16)
    (gq, gk, la, gv, gr, nq, kcmp, vcmp, kslc, vslc, kwin, vwin, ng) = _hyb_in(
        x, hyb_norm[0], mods[0], _prep_hyb_w_in(hyb_w_in[0]), gkup_pad, gla_gk_bias[0])
    o_gla = _gla(gq, gk, la, gv, gr, gla_out_norm[0])
    pe_rows, w1x, w2x = _prep_cmp(nsa_cmp_pe[0], nsa_cmp_w1[0], nsa_cmp_w2[0])
    kc, vc = _nsa_compress(kcmp, vcmp, pe_rows, w1x, w2x)
    o_nsa = _nsa_attn(nq, ng, kc, vc, kslc, vslc, kwin, vwin, _overlap_matrix(seq))
    w_out = hyb_w_out[0].astype(BF16)
    n_gla = GLA_HEADS * GLA_DV
    w_gu = dense_w_gu[0].astype(BF16)
    x = _hyb_out_ffn(x, o_gla, o_nsa, mods[0], w_out[:n_gla], w_out[n_gla:], dense_norm[0], mods[1],
                     w_gu[:, :FFN_DENSE], w_gu[:, FFN_DENSE:], dense_w_down[0].astype(BF16))

    wz, wxbc, wdt = _split(ssm_w_in[0], SSM_SPLITS)
    w_ssm = jnp.concatenate([wz, wxbc, _pad_cols(wdt, LANES)], axis=1).astype(BF16)
    pad_heads = lambda a: jnp.pad(a, (0, LANES - SSM_HEADS))[None, :]
    z, xbc, dt = _ssm_in(x, ssm_norm[0], mods[2], w_ssm, ssm_conv_w[0], ssm_conv_b[0], pad_heads(ssm_dt_bias[0]))
    y = _ssd(xbc, z, dt, pad_heads(ssm_a_log[0]), jnp.repeat(ssm_d[0], SSM_HEAD_DIM)[None, :],
             ssm_gate_norm[0], _head_expand())
    x3, h4, gw, sel, cnt = _ssm_out_route(x, y, mods[2], ssm_w_out[0].astype(BF16), moe_norm[0], mods[3],
                                          _pad_cols(moe_router[0], LANES))
    n = bsz * seq
    cnt = cnt[:, 0, :]
    totals = jnp.sum(cnt, axis=0)
    goff_f = jnp.cumsum(totals) - totals
    base = (jnp.cumsum(cnt, axis=0) - cnt + goff_f[None, :])[:, None, :]
    goff = jnp.concatenate([goff_f[:N_EXPERTS], jnp.full((1,), 2.0 * n, F32)]).astype(jnp.int32)
    pos, wts = _moe_plan(gw.reshape(n, LANES), sel.reshape(n, LANES), base)
    pos = pos[:, 0, :]
    xs = _sc_dispatch(h4.reshape(n, d // 2), pos)
    w_gu = moe_w_gu[0].astype(BF16)
    ys = _moe_group(xs, goff, w_gu[:, :, :FFN_EXPERT], w_gu[:, :, FFN_EXPERT:], moe_w_down[0].astype(BF16))
    yg = _sc_gather(ys, pos)
    return _moe_combine(x3, yg.reshape(2, bsz, seq, d // 2), wts.reshape(bsz, seq, LANES), mods[3], final_norm)
```

```python
import functools

import jax
import jax.numpy as jnp
import numpy as np
from jax import lax
from jax.experimental import pallas as pl
from jax.experimental.pallas import tpu as pltpu
from jax.experimental.pallas import tpu_sc as plsc

F32 = jnp.float32
BF16 = jnp.bfloat16

D_MODEL = 1024
NORM_EPS = 1e-6
GLA_HEADS = 4
GLA_DV = D_MODEL // 8
GLA_DK = GLA_DV // 2
GLA_LOWRANK = 16
GLA_TAU = 16.0
GLA_CHUNK = 64
NSA_HEADS = 8
NSA_KV_GROUPS = 2
NSA_HEAD_DIM = D_MODEL // 16
CMP_BLOCK = 32
CMP_STRIDE = 16
CMP_HIDDEN = 4 * NSA_HEAD_DIM
SEL_BLOCK = 64
N_SELECT = 8
WINDOW = 512
D_INNER = 2 * D_MODEL
SSM_HEAD_DIM = 64
SSM_HEADS = D_INNER // SSM_HEAD_DIM
SSM_GROUPS = 4
SSM_STATE = 128
CONV_K = 4
SSM_CHUNK = 64
CONV_CH = D_INNER + 2 * SSM_GROUPS * SSM_STATE
FFN_DENSE = ((8 * D_MODEL // 3 + 127) // 128) * 128
N_EXPERTS = 8
FFN_EXPERT = 7 * D_MODEL // 2
HYB_SPLITS = (GLA_HEADS * GLA_DK, GLA_HEADS * GLA_DK, GLA_HEADS * GLA_DV, GLA_LOWRANK, GLA_HEADS * GLA_DV,
              NSA_HEADS * NSA_HEAD_DIM, 6 * NSA_KV_GROUPS * NSA_HEAD_DIM, 3 * NSA_HEADS)
SSM_SPLITS = (D_INNER, CONV_CH, SSM_HEADS)

LANES = 128
VMEM_LIMIT = 56 * 1024 * 1024

NEG_BIG = -1e30


def _cparams(sem):
    return pltpu.CompilerParams(dimension_semantics=sem, vmem_limit_bytes=VMEM_LIMIT)


def _dot(a, b):
    return jnp.dot(a, b, preferred_element_type=F32)


def _dot_nt(a, b):
    return lax.dot_general(a, b, (((1,), (1,)), ((), ())), preferred_element_type=F32)


def _dot_tn(a, b):
    return lax.dot_general(a, b, (((0,), (0,)), ((), ())), preferred_element_type=F32)


def _split_bf16(a, parts):
    out = []
    r = a
    for _ in range(parts):
        p = r.astype(BF16)
        out.append(p)
        r = r - p.astype(F32)
    return out


def _dot_exact_lhs(m, a, parts=3):
    width = a.shape[1]
    wide = _dot(m, jnp.concatenate(_split_bf16(a, parts), axis=1))
    acc = wide[:, :width]
    for i in range(1, parts):
        acc = acc + wide[:, i * width:(i + 1) * width]
    return acc


def _dot_exact_rhs(a, m, parts=3):
    acc = None
    for p in _split_bf16(a, parts):
        t = _dot(p, m)
        acc = t if acc is None else acc + t
    return acc


def _sigmoid(x):
    return 1.0 / (1.0 + jnp.exp(-x))


def _silu(x):
    return x * _sigmoid(x)


def _softplus(x):
    return jnp.maximum(x, 0.0) + jnp.log(1.0 + jnp.exp(-jnp.abs(x)))


def _gelu_tanh(x):
    return x * (0.5 * (1.0 + jnp.tanh(0.7978845608028654 * (x + 0.044715 * (x * x * x)))))


def _modulated_norm(x, nw, shift, scale):
    ms = jnp.mean(x * x, axis=-1, keepdims=True)
    y = x * lax.rsqrt(ms + NORM_EPS)
    return (y * nw) * (1.0 + scale) + shift


def _const_spec(shape):
    nd = len(shape)
    return pl.BlockSpec(shape, lambda *_: (0,) * nd, pipeline_mode=pl.Buffered(1))


def _adaln_kernel(c_ref, w0, w1, w2, w3, b0, b1, b2, b3, o_ref):
    sc = _silu(c_ref[...]).astype(BF16)
    for i, (w, b) in enumerate(((w0, b0), (w1, b1), (w2, b2), (w3, b3))):
        o_ref[i] = _dot(sc, w[...].astype(BF16)) + b[...]


def _adaln(c, ws, bs):
    bsz, d = c.shape
    n = ws[0].shape[1]
    tn = 512
    w_spec = pl.BlockSpec((d, tn), lambda j: (0, j))
    b_spec = pl.BlockSpec((1, tn), lambda j: (0, j))
    return pl.pallas_call(
        _adaln_kernel,
        grid=(n // tn,),
        in_specs=[pl.BlockSpec((bsz, d), lambda j: (0, 0))] + [w_spec] * 4 + [b_spec] * 4,
        out_specs=pl.BlockSpec((4, bsz, tn), lambda j: (0, 0, j)),
        out_shape=jax.ShapeDtypeStruct((4, bsz, n), F32),
        compiler_params=_cparams(("arbitrary",)),
        name="adaln",
    )(c, *ws, *[b[None, :] for b in bs])


def _mod_specs(tile_axes):
    def spec(part):
        if tile_axes == 2:
            return pl.BlockSpec((1, 1, D_MODEL), lambda b, i: (b, 0, part))
        return pl.BlockSpec((1, 1, D_MODEL), lambda b, i, j, k: (b, 0, part))
    return spec(0), spec(1), spec(2)


HYB_SEG = {
    "gq": (0, 256), "gk": (256, 256), "gv": (512, 512), "gr": (1024, 512), "lr_ng": (1536, 256),
    "nq": (1792, 512), "kv": (2304, 768),
}
HYB_COLS = 3072


def _hyb_in_kernel(x_ref, nw_ref, shift_ref, scale_ref, w_ref, gkup_ref, gkb_ref,
                   gq_ref, gk_ref, la_ref, gv_ref, gr_ref, nq_ref,
                   kcmp_ref, vcmp_ref, kslc_ref, vslc_ref, kwin_ref, vwin_ref, ng_ref):
    h = _modulated_norm(x_ref[0], nw_ref[...], shift_ref[0], scale_ref[0]).astype(BF16)

    def proj(name):
        off, width = HYB_SEG[name]
        return _dot(h, w_ref[:, off:off + width])

    def proj_pair(name, pair=0):
        off = HYB_SEG[name][0] + pair * 2 * LANES
        y = _dot(h, w_ref[:, off:off + 2 * LANES])
        return y[:, :LANES], y[:, LANES:]

    gq_ref[0] = proj("gq")
    gk_ref[0] = proj("gk")
    gv_ref[0] = proj("gv").astype(BF16)
    gr_ref[0] = proj("gr")
    lr, ng = proj_pair("lr_ng")
    z = _dot(lr.astype(BF16), gkup_ref[...]) + gkb_ref[...]
    la_ref[0] = -_softplus(-z) * (1.0 / GLA_TAU)
    ng_ref[0] = ng
    nq_ref[0] = proj("nq").astype(BF16)
    kcmp_ref[0], vcmp_ref[0] = proj_pair("kv", 0)
    kslc, vslc = proj_pair("kv", 1)
    kslc_ref[0] = kslc.T.astype(BF16)
    vslc_ref[0] = vslc.astype(BF16)
    kwin, vwin = proj_pair("kv", 2)
    kwin_ref[0] = kwin.T.astype(BF16)
    vwin_ref[0] = vwin.astype(BF16)


def _hyb_in(x, norm_w, mod, w_perm, gkup_pad, gk_bias):
    bsz, seq, d = x.shape
    tm = 512
    shift_s, scale_s, _ = _mod_specs(2)

    def out(width, dtype):
        return (pl.BlockSpec((1, tm, width), lambda b, i: (b, i, 0)),
                jax.ShapeDtypeStruct((bsz, seq, width), dtype))

    def out_t(dtype):
        return (pl.BlockSpec((1, LANES, tm), lambda b, i: (b, 0, i)),
                jax.ShapeDtypeStruct((bsz, LANES, seq), dtype))

    outs = [out(256, F32), out(256, F32), out(256, F32), out(512, BF16), out(512, F32), out(512, BF16),
            out(128, F32), out(128, F32), out_t(BF16), out(128, BF16), out_t(BF16), out(128, BF16),
            out(128, F32)]
    return pl.pallas_call(
        _hyb_in_kernel,
        grid=(bsz, seq // tm),
        in_specs=[pl.BlockSpec((1, tm, d), lambda b, i: (b, i, 0)),
                  _const_spec((1, d)), shift_s, scale_s,
                  _const_spec(w_perm.shape), _const_spec(gkup_pad.shape), _const_spec((1, 256))],
        out_specs=[o[0] for o in outs],
        out_shape=[o[1] for o in outs],
        compiler_params=_cparams(("parallel", "parallel")),
        name="hyb_in",
    )(x, norm_w[None, :], mod, mod, w_perm, gkup_pad, gk_bias[None, :])


GLA_ROWS = 512


def _gla_kernel(q_ref, k_ref, la_ref, v_ref, r_ref, nw_ref, o_ref, *, seq):
    rb = GLA_ROWS
    ncb = rb // GLA_CHUNK
    row = lax.broadcasted_iota(jnp.int32, (rb, rb), 0)
    col = lax.broadcasted_iota(jnp.int32, (rb, rb), 1)
    chunk_causal = ((row // GLA_CHUNK) == (col // GLA_CHUNK)) & (col <= row)
    tril_bd = jnp.where(chunk_causal, 1.0, 0.0).astype(BF16)
    lane = lax.broadcasted_iota(jnp.int32, (rb, LANES), 1)
    head_mask = (lane < GLA_DK, lane >= GLA_DK)
    nw = nw_ref[...]

    def body(i, st):
        r0 = pl.multiple_of(i * rb, rb)
        rows = pl.ds(r0, rb)
        q = q_ref[0, rows, :]
        k = k_ref[0, rows, :]
        la = la_ref[0, rows, :]
        b = _dot_exact_lhs(tril_bd, la)
        b3 = b.reshape(ncb, GLA_CHUNK, LANES)
        bend3 = b3[:, GLA_CHUNK - 1:GLA_CHUNK, :]
        bend = jnp.broadcast_to(bend3, (ncb, GLA_CHUNK, LANES)).reshape(rb, LANES)
        q_dec = (q * GLA_DK ** -0.5) * jnp.exp(b)
        k_dec = (k * jnp.exp(-b)).astype(BF16)
        k_end = k * jnp.exp(bend - b)
        decay = jnp.exp(bend3)

        qm, vh, o_intra, upd = [], [], [], []
        for h in range(2):
            qm_h = jnp.where(head_mask[h], q_dec, 0.0).astype(BF16)
            att = _dot_nt(qm_h, k_dec)
            att = jnp.where(chunk_causal, att, 0.0).astype(BF16)
            v_h = v_ref[0, rows, h * GLA_DV:(h + 1) * GLA_DV]
            o_intra.append(_dot(att, v_h))
            km_h = jnp.where(head_mask[h], k_end, 0.0).astype(BF16)
            upd.append([_dot_tn(v_h[n * GLA_CHUNK:(n + 1) * GLA_CHUNK], km_h[n * GLA_CHUNK:(n + 1) * GLA_CHUNK])
                        for n in range(ncb)])
            qm.append(qm_h)
            vh.append(v_h)

        prev = []
        for n in range(ncb):
            prev.append(st.astype(BF16))
            st = st * decay[n] + (upd[0][n] + upd[1][n])

        for h in range(2):
            o_inter = jnp.concatenate(
                [_dot_nt(qm[h][n * GLA_CHUNK:(n + 1) * GLA_CHUNK], prev[n]) for n in range(ncb)], axis=0)
            o = o_intra[h] + o_inter
            y = o * lax.rsqrt(jnp.mean(o * o, axis=-1, keepdims=True) + NORM_EPS) * nw
            r = r_ref[0, rows, h * GLA_DV:(h + 1) * GLA_DV]
            o_ref[0, rows, h * GLA_DV:(h + 1) * GLA_DV] = (y * _silu(r)).astype(o_ref.dtype)
        return st

    lax.fori_loop(0, seq // rb, body, jnp.zeros((GLA_DV, 2 * GLA_DK), F32))


def _gla(gq, gk, la, gv, gr, out_norm):
    bsz, seq, _ = gq.shape
    qk_spec = pl.BlockSpec((1, seq, 2 * GLA_DK), lambda b, p: (b, 0, p))
    v_spec = pl.BlockSpec((1, seq, 2 * GLA_DV), lambda b, p: (b, 0, p))
    return pl.pallas_call(
        functools.partial(_gla_kernel, seq=seq),
        grid=(bsz, GLA_HEADS // 2),
        in_specs=[qk_spec, qk_spec, qk_spec, v_spec, v_spec, _const_spec((1, GLA_DV))],
        out_specs=v_spec,
        out_shape=jax.ShapeDtypeStruct((bsz, seq, GLA_HEADS * GLA_DV), BF16),
        compiler_params=_cparams(("parallel", "parallel")),
        name="gla",
    )(gq, gk, la, gv, gr, out_norm[None, :])


def _nsa_compress_kernel(k_ref, v_ref, pe_ref, w1_ref, w2_ref, kc_ref, vc_ref, *, n_cmp):
    rows = k_ref.shape[1]
    rid = lax.broadcasted_iota(jnp.int32, (rows, LANES), 0)
    for j, (src, dst) in enumerate(((k_ref, kc_ref), (v_ref, vc_ref))):
        x = src[0]
        lo = _dot((x + pe_ref[j, 0]).astype(BF16), w1_ref[j, 0])
        hi = _dot((x + pe_ref[j, 1]).astype(BF16), w1_ref[j, 1])
        hpre = lo + pltpu.roll(hi, rows - 1, 0)
        out = jnp.where(rid < n_cmp, _dot(_gelu_tanh(hpre).astype(BF16), w2_ref[j]), 0.0)
        dst[0] = (out.T if j == 0 else out).astype(dst.dtype)


def _nsa_compress(kcmp, vcmp, pe_rows, w1x, w2x):
    bsz, seq, _ = kcmp.shape
    rows = seq // CMP_STRIDE
    width = CMP_STRIDE * LANES
    n_cmp = (seq - CMP_BLOCK) // CMP_STRIDE + 1
    x_spec = pl.BlockSpec((1, rows, width), lambda b: (b, 0, 0))
    return pl.pallas_call(
        functools.partial(_nsa_compress_kernel, n_cmp=n_cmp),
        grid=(bsz,),
        in_specs=[x_spec, x_spec, _const_spec(pe_rows.shape), _const_spec(w1x.shape), _const_spec(w2x.shape)],
        out_specs=[pl.BlockSpec((1, LANES, rows), lambda b: (b, 0, 0)),
                   pl.BlockSpec((1, rows, LANES), lambda b: (b, 0, 0))],
        out_shape=[jax.ShapeDtypeStruct((bsz, LANES, rows), BF16), jax.ShapeDtypeStruct((bsz, rows, LANES), BF16)],
        compiler_params=_cparams(("parallel",)),
        name="nsa_compress",
    )(kcmp.reshape(bsz, rows, width), vcmp.reshape(bsz, rows, width), pe_rows, w1x, w2x)


NSA_QT = 256
NSA_KT = 512
NSA_ROWS = NSA_HEADS * NSA_QT


NSA_MASK_PENALTY = 1e30
NSA_LANE_POS_HI, NSA_LANE_POS_LO, NSA_LANE_CMP_IDX, NSA_LANE_CMP_ONE = 64, 65, 66, 67


def _nsa_attn_kernel(q_ref, g_ref, kc_ref, vc_ref, ks_ref, vs_ref, kw_ref, vw_ref, ov_ref, kf_ref, cf_ref, o_ref,
                     m_ref, acc_ref, *, seq):
    qt, kt = NSA_QT, NSA_KT
    heads_per_group = NSA_HEADS // NSA_KV_GROUPS
    q0 = pl.program_id(1) * qt
    t = q0 + lax.broadcasted_iota(jnp.int32, (qt, 1), 0)
    lane = lax.broadcasted_iota(jnp.int32, (qt, LANES), 1)
    group_lanes = (lane < NSA_HEAD_DIM, lane >= NSA_HEAD_DIM)
    slopes = [2.0 ** (-(h + 1)) for h in range(NSA_HEADS)]

    qa = q_ref[0].astype(F32) * NSA_HEAD_DIM ** -0.5
    q_rows = []
    for h in range(NSA_HEADS):
        g = h // heads_per_group
        blk = qa[:, (h // 2) * LANES:(h // 2 + 1) * LANES]
        if h % 2 != g:
            blk = pltpu.roll(blk, NSA_HEAD_DIM, 1)
        q_rows.append(jnp.where(group_lanes[g], blk, 0.0).astype(BF16))
    q_ext = jnp.concatenate(q_rows, axis=0)

    def alibi_lanes(h):
        s = slopes[h]
        return jnp.where(lane == NSA_LANE_POS_HI, SEL_BLOCK * s,
                         jnp.where(lane == NSA_LANE_POS_LO, s,
                                   jnp.where(lane == NSA_LANE_CMP_IDX, CMP_STRIDE * s,
                                             jnp.where(lane == NSA_LANE_CMP_ONE, (CMP_BLOCK - 1) / 2.0 * s, 0.0))))

    q_pos = [alibi_lanes(h) for h in range(NSA_HEADS)]
    q_full = jnp.concatenate([q_ext, jnp.concatenate(q_pos, axis=0).astype(BF16)], axis=1)

    def softmax_parts(s, mask):
        s = jnp.where(mask, s, -jnp.inf)
        m = jnp.max(s, axis=-1, keepdims=True)
        m = jnp.where(m == -jnp.inf, 0.0, m)
        p = jnp.exp(s - m)
        return p, jnp.sum(p, axis=-1, keepdims=True)

    n_rows = kc_ref.shape[2]
    cidx = lax.broadcasted_iota(jnp.int32, (1, n_rows), 1)
    cmp_mask = cidx * CMP_STRIDE + (CMP_BLOCK - 1) <= t
    s_all = _dot(q_full, jnp.concatenate([kc_ref[0], cf_ref[...]], axis=0))
    p_list, psum = [], [None] * NSA_KV_GROUPS
    for h in range(NSA_HEADS):
        g = h // heads_per_group
        p, l = softmax_parts(s_all[h * qt:(h + 1) * qt], cmp_mask)
        p = p / jnp.maximum(l, 1e-20)
        p_list.append(p.astype(BF16))
        psum[g] = p if psum[g] is None else psum[g] + p
    o_cmp = _dot(jnp.concatenate(p_list, axis=0), vc_ref[0])

    n_blk = LANES // 2
    tq = q0 + lax.broadcasted_iota(jnp.int32, (n_blk, qt), 1)
    bidx = lax.broadcasted_iota(jnp.int32, (n_blk, qt), 0)
    bidx_f = bidx.astype(F32)
    blk_t = tq // SEL_BLOCK
    forced = (bidx == 0) | (bidx == blk_t) | (bidx == blk_t - 1)
    future = bidx * SEL_BLOCK > tq
    q_sel_rows = []
    for g in range(NSA_KV_GROUPS):
        imp = _dot_exact_rhs(psum[g], ov_ref[...]).T[:n_blk]
        v = jnp.where(future, -1.0, jnp.where(forced, 3e38, imp))
        chosen = jnp.zeros((n_blk, qt), F32)
        for _ in range(min(N_SELECT, seq // SEL_BLOCK)):
            m = jnp.max(v, axis=0, keepdims=True)
            idx = jnp.min(jnp.where(v == m, bidx_f, float(LANES)), axis=0, keepdims=True)
            pick = (bidx_f == idx) & (m >= 0.0)
            chosen = jnp.where(pick, 1.0, chosen)
            v = jnp.where(pick, -1.0, v)
        penalty = jnp.concatenate([(chosen - 1.0) * NSA_MASK_PENALTY, jnp.zeros((n_blk, qt), F32)], axis=0).T
        for r in range(heads_per_group):
            q_sel_rows.append((penalty + q_pos[g * heads_per_group + r]).astype(BF16))
    q_full_sel = jnp.concatenate([q_ext, jnp.concatenate(q_sel_rows, axis=0)], axis=1)

    vlane = lax.broadcasted_iota(jnp.int32, (1, LANES), 1)
    own_lanes = (vlane < NSA_HEAD_DIM, vlane >= NSA_HEAD_DIM)

    def weighted_values(p_rows, v):
        v = v.astype(F32)
        outs = []
        for g in range(NSA_KV_GROUPS):
            pg = jnp.concatenate(p_rows[g * heads_per_group:(g + 1) * heads_per_group], axis=0)
            outs.append(_dot(pg, jnp.where(own_lanes[g], v, 1.0).astype(BF16)))
        return jnp.concatenate(outs, axis=0)

    def normalised(acc):
        return acc / jnp.maximum(pltpu.roll(acc, NSA_HEAD_DIM, 1), 1e-20)

    m_ref[...] = jnp.full(m_ref.shape, -jnp.inf, F32)
    acc_ref[...] = jnp.zeros(acc_ref.shape, F32)
    kpos = lax.broadcasted_iota(jnp.int32, (1, kt), 1)

    def sel_tile(j, causal):
        k0 = pl.multiple_of(j * kt, kt)
        ks = jnp.concatenate([ks_ref[0, :, pl.ds(k0, kt)], kf_ref[:, pl.ds(k0, kt)]], axis=0)
        s_t = _dot(q_full_sel, ks)
        visible = (k0 + kpos) <= t
        m_all = m_ref[...]
        p_rows, m_rows, alpha_rows = [], [], []
        for h in range(NSA_HEADS):
            rows = slice(h * qt, (h + 1) * qt)
            s = s_t[rows]
            if causal:
                s = jnp.where(visible, s, -jnp.inf)
            m_old = m_all[rows]
            m_new = jnp.maximum(m_old, jnp.max(s, axis=-1, keepdims=True))
            m_safe = jnp.where(m_new == -jnp.inf, 0.0, m_new)
            p_rows.append(jnp.exp(s - jnp.concatenate([m_safe] * (kt // LANES), axis=1)).astype(BF16))
            alpha_rows.append(jnp.exp(m_old - m_safe))
            m_rows.append(m_new)
        m_ref[...] = jnp.concatenate(m_rows, axis=0)
        acc_ref[...] = (jnp.concatenate(alpha_rows, axis=0) * acc_ref[...]
                        + weighted_values(p_rows, vs_ref[0, pl.ds(k0, kt), :]))

    def sel_past_tile(j, carry):
        sel_tile(j, False)
        return carry

    last = q0 // kt
    lax.fori_loop(0, last, sel_past_tile, 0)
    sel_tile(last, True)
    o_slc = normalised(acc_ref[...])

    wk = WINDOW + qt
    w0 = pl.multiple_of(jnp.maximum(q0 - WINDOW, 0), qt)
    kw = jnp.concatenate([kw_ref[0, :, pl.ds(w0, wk)], kf_ref[:, pl.ds(w0, wk)]], axis=0)
    wdist = t - (w0 + lax.broadcasted_iota(jnp.int32, (1, wk), 1))
    wmask = (wdist >= 0) & (wdist < WINDOW)
    s_all = _dot(q_full, kw)
    p_rows = []
    for h in range(NSA_HEADS):
        s = jnp.where(wmask, s_all[h * qt:(h + 1) * qt], -jnp.inf)
        m = jnp.max(s, axis=-1, keepdims=True)
        p_rows.append(jnp.exp(s - jnp.where(m == -jnp.inf, 0.0, m)).astype(BF16))
    o_win = normalised(weighted_values(p_rows, vw_ref[0, pl.ds(w0, wk), :]))

    gates = _sigmoid(g_ref[0])
    o_heads = []
    for h in range(NSA_HEADS):
        rows = slice(h * qt, (h + 1) * qt)
        o_heads.append(gates[:, 3 * h:3 * h + 1] * o_cmp[rows] + gates[:, 3 * h + 1:3 * h + 2] * o_slc[rows]
                       + gates[:, 3 * h + 2:3 * h + 3] * o_win[rows])
    for c in range(NSA_HEADS // 2):
        g = (2 * c) // heads_per_group
        a, b = o_heads[2 * c], o_heads[2 * c + 1]
        if g == 0:
            blk = jnp.where(group_lanes[0], a, pltpu.roll(b, NSA_HEAD_DIM, 1))
        else:
            blk = jnp.where(group_lanes[0], pltpu.roll(a, NSA_HEAD_DIM, 1), b)
        o_ref[0, :, c * LANES:(c + 1) * LANES] = blk.astype(o_ref.dtype)


def _nsa_position_features(seq):
    assert seq // SEL_BLOCK <= LANES // 2
    pos = np.arange(seq)
    kf = np.zeros((seq, LANES), np.float32)
    kf[pos, pos // SEL_BLOCK] = 1.0
    kf[:, NSA_LANE_POS_HI] = pos // SEL_BLOCK
    kf[:, NSA_LANE_POS_LO] = pos % SEL_BLOCK
    n_rows = seq // CMP_STRIDE
    cf = np.zeros((n_rows, LANES), np.float32)
    cf[:, NSA_LANE_CMP_IDX] = np.arange(n_rows)
    cf[:, NSA_LANE_CMP_ONE] = 1.0
    return jnp.asarray(kf.T, BF16), jnp.asarray(cf.T, BF16)


def _nsa_attn(nq, ng, kc, vc, kslc, vslc, kwin, vwin, overlap):
    bsz, seq, _ = nq.shape
    n_rows = vc.shape[1]
    kfeat, cfeat = _nsa_position_features(seq)
    full = lambda rows: pl.BlockSpec((1, rows, LANES), lambda b, i: (b, 0, 0))
    full_t = lambda cols: pl.BlockSpec((1, LANES, cols), lambda b, i: (b, 0, 0))
    return pl.pallas_call(
        functools.partial(_nsa_attn_kernel, seq=seq),
        grid=(bsz, seq // NSA_QT),
        in_specs=[pl.BlockSpec((1, NSA_QT, NSA_HEADS * NSA_HEAD_DIM), lambda b, i: (b, i, 0)),
                  pl.BlockSpec((1, NSA_QT, LANES), lambda b, i: (b, i, 0)),
                  full_t(n_rows), full(n_rows), full_t(seq), full(seq), full_t(seq), full(seq),
                  _const_spec(overlap.shape), _const_spec(kfeat.shape), _const_spec(cfeat.shape)],
        out_specs=pl.BlockSpec((1, NSA_QT, NSA_HEADS * NSA_HEAD_DIM), lambda b, i: (b, i, 0)),
        out_shape=jax.ShapeDtypeStruct((bsz, seq, NSA_HEADS * NSA_HEAD_DIM), BF16),
        scratch_shapes=[pltpu.VMEM((NSA_ROWS, LANES), F32), pltpu.VMEM((NSA_ROWS, LANES), F32)],
        compiler_params=_cparams(("parallel", "arbitrary")),
        name="nsa_attn",
    )(nq, ng, kc, vc, kslc, vslc, kwin, vwin, overlap, kfeat, cfeat)


FFN_CHUNK = 1408


def _hyb_out_ffn_kernel(x_ref, oa_ref, ob_ref, g1_ref, wa_ref, wb_ref,
                        nw_ref, shift_ref, scale_ref, g2_ref, wg_ref, wu_ref, wd_ref, o_ref, act_ref):
    mix = _dot(oa_ref[0], wa_ref[...]) + _dot(ob_ref[0], wb_ref[...])
    x1 = x_ref[0] + g1_ref[0] * mix
    h = _modulated_norm(x1, nw_ref[...], shift_ref[0], scale_ref[0]).astype(BF16)
    for f in range(0, FFN_DENSE, FFN_CHUNK):
        gate = _dot(h, wg_ref[:, f:f + FFN_CHUNK])
        up = _dot(h, wu_ref[:, f:f + FFN_CHUNK])
        act_ref[:, f:f + FFN_CHUNK] = (_silu(gate) * up).astype(BF16)
    o_ref[0] = x1 + g2_ref[0] * _dot(act_ref[...], wd_ref[...])


def _hyb_out_ffn(x, o_gla, o_nsa, mod1, wa, wb, norm_w, mod2, wg, wu, wd):
    bsz, seq, d = x.shape
    tm = 512
    _, _, gate1 = _mod_specs(2)
    shift2, scale2, gate2 = _mod_specs(2)
    tile = lambda width: pl.BlockSpec((1, tm, width), lambda b, i: (b, i, 0))
    return pl.pallas_call(
        _hyb_out_ffn_kernel,
        grid=(bsz, seq // tm),
        in_specs=[tile(d), tile(o_gla.shape[-1]), tile(o_nsa.shape[-1]), gate1,
                  _const_spec(wa.shape), _const_spec(wb.shape), _const_spec((1, d)), shift2, scale2, gate2,
                  _const_spec(wg.shape), _const_spec(wu.shape), _const_spec(wd.shape)],
        out_specs=tile(d),
        out_shape=jax.ShapeDtypeStruct((bsz, seq, d), F32),
        scratch_shapes=[pltpu.VMEM((tm, FFN_DENSE), BF16)],
        compiler_params=_cparams(("parallel", "parallel")),
        name="hyb_out_ffn",
    )(x, o_gla, o_nsa, mod1, wa, wb, norm_w[None, :], mod2, mod2, mod2, wg, wu, wd)


SSM_DT_OFF = D_INNER + CONV_CH
SSM_COLS = SSM_DT_OFF + LANES
SSM_IN_CHUNK = 512


CONV_TAIL = 8


def _ssm_in_kernel(x_ref, nw_ref, shift_ref, scale_ref, w_ref, cw_ref, cb_ref, dtb_ref,
                   z_ref, xbc_ref, dt_ref, xb_ref):
    tm = x_ref.shape[1]

    @pl.when(pl.program_id(1) == 0)
    def _():
        xb_ref[0:CONV_TAIL, :] = jnp.zeros((CONV_TAIL, CONV_CH), F32)

    h = _modulated_norm(x_ref[0], nw_ref[...], shift_ref[0], scale_ref[0]).astype(BF16)
    for c in range(0, CONV_CH, SSM_IN_CHUNK):
        xb_ref[CONV_TAIL:, c:c + SSM_IN_CHUNK] = _dot(h, w_ref[:, D_INNER + c:D_INNER + c + SSM_IN_CHUNK])
    dt_ref[0] = _softplus(_dot(h, w_ref[:, SSM_DT_OFF:SSM_COLS]) + dtb_ref[...])

    def conv_chunk(c):
        cols = slice(c, c + SSM_IN_CHUNK)
        acc = cb_ref[:, cols]
        for j in range(CONV_K):
            acc = acc + xb_ref[CONV_TAIL - j:CONV_TAIL - j + tm, cols] * cw_ref[CONV_K - 1 - j:CONV_K - j, cols]
        xbc_ref[0, :, cols] = _silu(acc)

    def gate_chunk(c):
        z_ref[0, :, c:c + SSM_IN_CHUNK] = _silu(_dot(h, w_ref[:, c:c + SSM_IN_CHUNK]))

    conv_chunks = list(range(0, CONV_CH, SSM_IN_CHUNK))
    gate_chunks = list(range(0, D_INNER, SSM_IN_CHUNK))
    for i in range(max(len(conv_chunks), len(gate_chunks))):
        if i < len(gate_chunks):
            gate_chunk(gate_chunks[i])
        if i < len(conv_chunks):
            conv_chunk(conv_chunks[i])
    xb_ref[0:CONV_TAIL, :] = xb_ref[tm:tm + CONV_TAIL, :]


def _ssm_in(x, norm_w, mod, w_perm, conv_w, conv_b, dt_bias_pad):
    bsz, seq, d = x.shape
    tm = 512
    shift_s, scale_s, _ = _mod_specs(2)
    tile = lambda width: pl.BlockSpec((1, tm, width), lambda b, i: (b, i, 0))
    return pl.pallas_call(
        _ssm_in_kernel,
        grid=(bsz, seq // tm),
        in_specs=[tile(d), _const_spec((1, d)), shift_s, scale_s, _const_spec(w_perm.shape),
                  _const_spec((CONV_K, CONV_CH)), _const_spec((1, CONV_CH)), _const_spec((1, LANES))],
        out_specs=[tile(D_INNER), tile(CONV_CH), tile(LANES)],
        out_shape=[jax.ShapeDtypeStruct((bsz, seq, D_INNER), F32),
                   jax.ShapeDtypeStruct((bsz, seq, CONV_CH), F32),
                   jax.ShapeDtypeStruct((bsz, seq, LANES), F32)],
        scratch_shapes=[pltpu.VMEM((CONV_TAIL + tm, CONV_CH), F32)],
        compiler_params=_cparams(("parallel", "arbitrary")),
        name="ssm_in",
    )(x, norm_w[None, :], mod, mod, w_perm, conv_w, conv_b[None, :], dt_bias_pad)


SSD_ROWS = 512
SSD_GW = D_INNER // SSM_GROUPS
SSD_HPG = SSM_HEADS // SSM_GROUPS
SSD_B_OFF = D_INNER
SSD_C_OFF = D_INNER + SSM_GROUPS * SSM_STATE
SSD_SPLIT_PARTS = 3


def _ssd_kernel(xbc_ref, z_ref, dt_ref, alog_ref, dskip_ref, nw_ref, ex_ref, o_ref,
                state_ref, xdt_ref, cum_ref, y_ref):
    rb, q = SSD_ROWS, SSM_CHUNK
    nchunk = rb // q

    @pl.when(pl.program_id(1) == 0)
    def _():
        state_ref[...] = jnp.zeros(state_ref.shape, F32)

    dt = dt_ref[0]
    a = dt * (-jnp.exp(alog_ref[...]))
    row = lax.broadcasted_iota(jnp.int32, (rb, rb), 0)
    col = lax.broadcasted_iota(jnp.int32, (rb, rb), 1)
    tril_bd = jnp.where(((row // q) == (col // q)) & (col <= row), 1.0, 0.0).astype(BF16)
    cum = _dot_exact_lhs(tril_bd, a)

    head_lane = lax.broadcasted_iota(jnp.int32, (rb, LANES), 1) < SSM_HEADS

    def lane_parts(v):
        parts = _split_bf16(jnp.where(head_lane, v, 0.0), SSD_SPLIT_PARTS)
        packed = parts[0].astype(F32)
        for i in range(1, SSD_SPLIT_PARTS):
            packed = packed + pltpu.roll(parts[i].astype(F32), i * SSM_HEADS, 1)
        return packed.astype(BF16)

    dt_parts = lane_parts(dt)
    cum_parts = lane_parts(cum)

    lrow = lax.broadcasted_iota(jnp.int32, (q, SSD_GW), 0)
    lcol = lax.broadcasted_iota(jnp.int32, (q, SSD_GW), 1) % q
    causal_t = lcol <= lrow
    eye_t = lcol == lrow
    half = SSD_GW // 2
    brow = lax.broadcasted_iota(jnp.int32, (half, half), 0) // q
    bcol = lax.broadcasted_iota(jnp.int32, (half, half), 1) // SSM_HEAD_DIM
    same_head = brow == bcol

    for g in range(SSM_GROUPS):
        xg = xbc_ref[0, :, g * SSD_GW:(g + 1) * SSD_GW]
        y_ref[g] = dskip_ref[:, g * SSD_GW:(g + 1) * SSD_GW] * xg
        xdt_ref[g] = xg * _dot(dt_parts, ex_ref[g])
        cum_ref[g] = _dot(cum_parts, ex_ref[g])

    def chunk(n, carry):
        rows = pl.ds(pl.multiple_of(n * q, q), q)
        for g in range(SSM_GROUPS):
            cum_c = cum_ref[g, rows, :]
            cum_s = jnp.sum(jnp.where(eye_t, cum_c, 0.0), axis=0, keepdims=True)
            decay_l = jnp.where(causal_t, jnp.exp(cum_c - cum_s), 0.0)
            bc = xbc_ref[0, rows, SSD_B_OFF + g * SSM_STATE:SSD_B_OFF + (g + 1) * SSM_STATE].astype(BF16)
            cc = xbc_ref[0, rows, SSD_C_OFF + g * SSM_STATE:SSD_C_OFF + (g + 1) * SSM_STATE].astype(BF16)
            cb_t = _dot_nt(cc, jnp.concatenate([bc] * SSD_HPG, axis=0))
            mat = (cb_t * decay_l).astype(BF16)
            xdt_c = xdt_ref[g, rows, :]
            xdt_b = xdt_c.astype(BF16)
            y_diag = []
            for s in range(2):
                blk = xdt_b[:, s * half:(s + 1) * half]
                bd = jnp.where(same_head, jnp.concatenate([blk] * (half // q), axis=0), 0.0).astype(BF16)
                y_diag.append(_dot(mat[:, s * half:(s + 1) * half], bd))
            y = jnp.concatenate(y_diag, axis=1)
            cum_end = cum_c[q - 1:q, :]
            st = state_ref[g]
            y = y + _dot(cc, st.astype(BF16)) * jnp.exp(cum_c)
            x_end = (xdt_c * jnp.exp(cum_end - cum_c)).astype(BF16)
            state_ref[g] = st * jnp.exp(cum_end) + _dot_tn(bc, x_end)
            y_ref[g, rows, :] += y
        return carry

    lax.fori_loop(0, nchunk, chunk, 0)

    for g in range(SSM_GROUPS):
        cols = slice(g * SSD_GW, (g + 1) * SSD_GW)
        y = y_ref[g] * z_ref[0, :, cols]
        y = y * lax.rsqrt(jnp.mean(y * y, axis=-1, keepdims=True) + NORM_EPS) * nw_ref[:, cols]
        o_ref[0, :, cols] = y.astype(o_ref.dtype)


def _ssd(xbc, z, dt, a_log_pad, d_skip_x, norm_w, expand):
    bsz, seq, _ = xbc.shape
    rb = SSD_ROWS
    tile = lambda width: pl.BlockSpec((1, rb, width), lambda b, i: (b, i, 0))
    return pl.pallas_call(
        _ssd_kernel,
        grid=(bsz, seq // rb),
        in_specs=[tile(CONV_CH), tile(D_INNER), tile(LANES), _const_spec((1, LANES)),
                  _const_spec((1, D_INNER)), _const_spec((1, D_INNER)), _const_spec(expand.shape)],
        out_specs=tile(D_INNER),
        out_shape=jax.ShapeDtypeStruct((bsz, seq, D_INNER), BF16),
        scratch_shapes=[pltpu.VMEM((SSM_GROUPS, SSM_STATE, SSD_GW), F32),
                        pltpu.VMEM((SSM_GROUPS, rb, SSD_GW), F32), pltpu.VMEM((SSM_GROUPS, rb, SSD_GW), F32),
                        pltpu.VMEM((SSM_GROUPS, rb, SSD_GW), F32)],
        compiler_params=_cparams(("parallel", "arbitrary")),
        name="ssd",
    )(xbc, z, dt, a_log_pad, d_skip_x, norm_w[None, :], expand)


def _pack_bf16_pairs(a):
    w = a.shape[1] // 2
    bits = lax.bitcast_convert_type(a.astype(jnp.bfloat16).astype(F32), jnp.uint32)
    packed = bits[:, w:] | (bits[:, :w] >> 16)
    return lax.bitcast_convert_type(packed, jnp.int32)


def _unpack_bf16_pairs(p):
    bits = lax.bitcast_convert_type(p, jnp.uint32)
    lo = lax.bitcast_convert_type(bits << 16, F32)
    hi = lax.bitcast_convert_type(bits & jnp.uint32(0xFFFF0000), F32)
    return jnp.concatenate([lo, hi], axis=1).astype(BF16)


def _ssm_out_route_kernel(x_ref, y_ref, g1_ref, w_ref, nw_ref, shift_ref, scale_ref, r_ref,
                          x3_ref, h_ref, gw_ref, sel_ref, cnt_ref):
    x3 = x_ref[0] + g1_ref[0] * _dot(y_ref[0], w_ref[...])
    x3_ref[0] = x3
    h = _modulated_norm(x3, nw_ref[...], shift_ref[0], scale_ref[0])
    h_ref[0] = _pack_bf16_pairs(h)
    h_hi, h_lo = _split_bf16(h, 2)
    r_hi, r_lo = _split_bf16(r_ref[...], 2)
    both = _dot(h_hi, jnp.concatenate([r_hi, r_lo], axis=1))
    logits = both[:, :LANES] + (both[:, LANES:] + _dot(h_lo, r_hi))
    lane = lax.broadcasted_iota(jnp.int32, logits.shape, 1)
    logits = jnp.where(lane < N_EXPERTS, logits, -jnp.inf)
    m1 = jnp.max(logits, axis=-1, keepdims=True)
    i1 = jnp.min(jnp.where(logits == m1, lane, LANES), axis=-1, keepdims=True)
    rest = jnp.where(lane == i1, -jnp.inf, logits)
    m2 = jnp.max(rest, axis=-1, keepdims=True)
    i2 = jnp.min(jnp.where(rest == m2, lane, LANES), axis=-1, keepdims=True)
    e2 = jnp.exp(m2 - m1)
    w1 = 1.0 / (1.0 + e2)
    w2 = e2 / (1.0 + e2)
    chosen = (lane == i1) | (lane == i2)
    gw_ref[0] = jnp.where(lane == i1, w1, 0.0) + jnp.where(lane == i2, w2, 0.0)
    sel = jnp.where(chosen, 1.0, 0.0)
    sel_ref[0] = sel.astype(BF16)
    cnt_ref[0] = jnp.sum(sel, axis=0, keepdims=True)


def _ssm_out_route(x, y, mod1, w_out, norm_w, mod2, router_pad):
    bsz, seq, d = x.shape
    tm = ROUTE_TM
    tiles = seq // tm
    _, _, gate1 = _mod_specs(2)
    shift2, scale2, _ = _mod_specs(2)
    tile = lambda width: pl.BlockSpec((1, tm, width), lambda b, i: (b, i, 0))
    return pl.pallas_call(
        _ssm_out_route_kernel,
        grid=(bsz, tiles),
        in_specs=[tile(d), tile(D_INNER), gate1, _const_spec(w_out.shape), _const_spec((1, d)), shift2, scale2,
                  _const_spec(router_pad.shape)],
        out_specs=[tile(d), tile(d // 2), tile(LANES), tile(LANES),
                   pl.BlockSpec((1, 1, LANES), lambda b, i: (b * tiles + i, 0, 0))],
        out_shape=[jax.ShapeDtypeStruct((bsz, seq, d), F32), jax.ShapeDtypeStruct((bsz, seq, d // 2), jnp.int32),
                   jax.ShapeDtypeStruct((bsz, seq, LANES), F32), jax.ShapeDtypeStruct((bsz, seq, LANES), BF16),
                   jax.ShapeDtypeStruct((bsz * tiles, 1, LANES), F32)],
        compiler_params=_cparams(("parallel", "parallel")),
        name="ssm_out_route",
    )(x, y, mod1, w_out, norm_w[None, :], mod2, mod2, router_pad)


ROUTE_TM = 512


def _moe_plan_kernel(gw_ref, sel_ref, base_ref, pos_ref, wts_ref):
    tm = gw_ref.shape[0]
    gw = gw_ref[...]
    sel = sel_ref[...]
    lane = lax.broadcasted_iota(jnp.int32, (tm, LANES), 1)
    chosen = sel.astype(F32) > 0.5
    row = lax.broadcasted_iota(jnp.int32, (tm, tm), 0)
    col = lax.broadcasted_iota(jnp.int32, (tm, tm), 1)
    before = jnp.where(col < row, 1.0, 0.0).astype(BF16)
    rank = _dot(before, sel)
    dest = rank + base_ref[0]
    first = jnp.min(jnp.where(chosen, lane, LANES), axis=-1, keepdims=True)
    last = jnp.max(jnp.where(chosen, lane, -1), axis=-1, keepdims=True)
    ones = jnp.ones((8, LANES), BF16)
    for slot, pick in enumerate((lane == first, lane == last)):
        parts = _split_bf16(jnp.where(pick, dest, 0.0), 3)
        pos = _dot_nt(ones, parts[0]) + (_dot_nt(ones, parts[1]) + _dot_nt(ones, parts[2]))
        pos_ref[slot] = pos.astype(jnp.int32)
    w_lo = jnp.sum(jnp.where(lane == first, gw, 0.0), axis=-1, keepdims=True)
    w_hi = jnp.sum(jnp.where(lane == last, gw, 0.0), axis=-1, keepdims=True)
    wts_ref[...] = jnp.where(lane == 0, w_lo, jnp.where(lane == 1, w_hi, 0.0))


def _moe_plan(gw, sel, base):
    n = gw.shape[0]
    tm = ROUTE_TM
    return pl.pallas_call(
        _moe_plan_kernel,
        grid=(n // tm,),
        in_specs=[pl.BlockSpec((tm, LANES), lambda i: (i, 0)), pl.BlockSpec((tm, LANES), lambda i: (i, 0)),
                  pl.BlockSpec((1, 1, LANES), lambda i: (i, 0, 0))],
        out_specs=[pl.BlockSpec((2, 8, tm), lambda i: (0, 0, i)), pl.BlockSpec((tm, LANES), lambda i: (i, 0))],
        out_shape=[jax.ShapeDtypeStruct((2, 8, n), jnp.int32), jax.ShapeDtypeStruct((n, LANES), F32)],
        compiler_params=_cparams(("parallel",)),
        name="moe_plan",
    )(gw, sel, base)


SC_WINDOW = 128


SC_CORES = 2
SC_SUBCORES = 16


def _sc_workers():
    return SC_CORES, SC_CORES * SC_SUBCORES


def _sc_dispatch(rows, pos):
    n, width = rows.shape
    ncores, workers = _sc_workers()
    per_worker = n // workers
    steps = per_worker // SC_WINDOW
    mesh = plsc.VectorSubcoreMesh(core_axis_name="c", subcore_axis_name="s")

    @functools.partial(
        pl.kernel, mesh=mesh, out_type=jax.ShapeDtypeStruct((2 * n, width), rows.dtype),
        scratch_types=[pltpu.VMEM((2, steps, SC_WINDOW), jnp.int32), pltpu.VMEM((SC_WINDOW, width), rows.dtype)],
        name="moe_dispatch")
    def run(rows_hbm, pos_hbm, out_hbm, idx_v, rows_v):
        wid = lax.axis_index("s") * ncores + lax.axis_index("c")
        pltpu.sync_copy(pos_hbm.at[wid], idx_v)
        for j in range(steps):
            pltpu.sync_copy(rows_hbm.at[pl.ds(wid * per_worker + j * SC_WINDOW, SC_WINDOW)], rows_v)
            pltpu.sync_copy(rows_v, out_hbm.at[idx_v.at[0, j]])
            pltpu.sync_copy(rows_v, out_hbm.at[idx_v.at[1, j]])

    pos_w = pos.reshape(2, workers, steps, SC_WINDOW).transpose(1, 0, 2, 3)
    return run(rows, pos_w)


SC_GATHER_WINDOW = 128


def _sc_gather(table, pos):
    _, width = table.shape
    n = pos.shape[1]
    ncores, workers = _sc_workers()
    per_worker = n // workers
    win = SC_GATHER_WINDOW
    steps = per_worker // win
    mesh = plsc.VectorSubcoreMesh(core_axis_name="c", subcore_axis_name="s")

    @functools.partial(
        pl.kernel, mesh=mesh, out_type=jax.ShapeDtypeStruct((2, n, width), table.dtype),
        scratch_types=[pltpu.VMEM((2, steps, win), jnp.int32), pltpu.VMEM((win, width), table.dtype)],
        name="moe_gather")
    def run(table_hbm, pos_hbm, out_hbm, idx_v, rows_v):
        wid = lax.axis_index("s") * ncores + lax.axis_index("c")
        pltpu.sync_copy(pos_hbm.at[wid], idx_v)
        for s in range(2):
            for j in range(steps):
                pltpu.sync_copy(table_hbm.at[idx_v.at[s, j]], rows_v)
                pltpu.sync_copy(rows_v, out_hbm.at[s, pl.ds(wid * per_worker + j * win, win)])

    pos_w = pos.reshape(2, workers, steps, win).transpose(1, 0, 2, 3)
    return run(table, pos_w)


MOE_TM = 512
MOE_TF = 896


def _moe_pairs(goff, n_rows):
    tiles = n_rows // MOE_TM
    steps = tiles + N_EXPERTS - 1
    first_row = jnp.arange(tiles, dtype=jnp.int32) * MOE_TM
    ends = goff[1:]
    e_lo = jnp.sum(first_row[:, None] >= ends[None, :], axis=1).astype(jnp.int32)
    e_hi = jnp.sum((first_row + (MOE_TM - 1))[:, None] >= ends[None, :], axis=1).astype(jnp.int32)
    count = e_hi - e_lo + 1
    start = jnp.cumsum(count) - count
    p = jnp.arange(steps, dtype=jnp.int32)
    tile = jnp.sum(start[None, :] <= p[:, None], axis=1).astype(jnp.int32) - 1
    expert = e_lo[tile] + (p - start[tile])
    valid = p < jnp.sum(count)
    return (jnp.where(valid, tile, tiles - 1).astype(jnp.int32),
            jnp.where(valid, expert, N_EXPERTS - 1).astype(jnp.int32), valid.astype(jnp.int32))


def _moe_group_kernel(tile_ref, exp_ref, valid_ref, goff_ref, x_ref, wg_ref, wu_ref, wd_ref, y_ref, act_ref):
    p = pl.program_id(0)
    tm = x_ref.shape[0]
    tile = tile_ref[p]
    expert = exp_ref[p]

    @pl.when((p == 0) | (tile != tile_ref[jnp.maximum(p - 1, 0)]))
    def _():
        y_ref[...] = jnp.zeros(y_ref.shape, y_ref.dtype)

    @pl.when(valid_ref[p] == 1)
    def _():
        x = _unpack_bf16_pairs(x_ref[...])
        for f in range(0, FFN_EXPERT, MOE_TF):
            act_ref[:, f:f + MOE_TF] = (_silu(_dot(x, wg_ref[0, :, f:f + MOE_TF]))
                                        * _dot(x, wu_ref[0, :, f:f + MOE_TF])).astype(BF16)
        out = _pack_bf16_pairs(_dot(act_ref[...], wd_ref[0]))
        rows = tile * tm + lax.broadcasted_iota(jnp.int32, (tm, 1), 0)
        mine = (rows >= goff_ref[expert]) & (rows < goff_ref[expert + 1])
        y_ref[...] = jnp.where(mine, out, y_ref[...])


def _moe_group(xs, goff, wg, wu, wd):
    n_rows, half = xs.shape
    d = 2 * half
    tm = MOE_TM
    tile, expert, valid = _moe_pairs(goff, n_rows)
    resident = lambda shape: pl.BlockSpec(shape, lambda p, t, e, v, g: (e[p], 0, 0), pipeline_mode=pl.Buffered(1))
    grid_spec = pltpu.PrefetchScalarGridSpec(
        num_scalar_prefetch=4,
        grid=(tile.shape[0],),
        in_specs=[pl.BlockSpec((tm, half), lambda p, t, e, v, g: (t[p], 0)),
                  resident((1, d, FFN_EXPERT)), resident((1, d, FFN_EXPERT)), resident((1, FFN_EXPERT, d))],
        out_specs=pl.BlockSpec((tm, half), lambda p, t, e, v, g: (t[p], 0)),
        scratch_shapes=[pltpu.VMEM((tm, FFN_EXPERT), BF16)],
    )
    return pl.pallas_call(
        _moe_group_kernel,
        grid_spec=grid_spec,
        out_shape=jax.ShapeDtypeStruct((n_rows, half), jnp.int32),
        compiler_params=_cparams(("arbitrary",)),
        name="moe_group",
    )(tile, expert, valid, goff, xs, wg, wu, wd)


def _moe_combine_kernel(x_ref, yg_ref, wts_ref, g_ref, fn_ref, o_ref):
    w = wts_ref[0]
    mix = (w[:, 0:1] * _unpack_bf16_pairs(yg_ref[0, 0]).astype(F32)
           + w[:, 1:2] * _unpack_bf16_pairs(yg_ref[1, 0]).astype(F32))
    x4 = x_ref[0] + g_ref[0] * mix
    y = x4 * lax.rsqrt(jnp.mean(x4 * x4, axis=-1, keepdims=True) + NORM_EPS)
    o_ref[0] = y * fn_ref[...]


def _moe_combine(x, yg, wts, mod, final_norm):
    bsz, seq, d = x.shape
    tm = 512
    _, _, gate = _mod_specs(2)
    tile = lambda width: pl.BlockSpec((1, tm, width), lambda b, i: (b, i, 0))
    return pl.pallas_call(
        _moe_combine_kernel,
        grid=(bsz, seq // tm),
        in_specs=[tile(d), pl.BlockSpec((2, 1, tm, d // 2), lambda b, i: (0, b, i, 0)), tile(LANES), gate,
                  _const_spec((1, d))],
        out_specs=tile(d),
        out_shape=jax.ShapeDtypeStruct((bsz, seq, d), F32),
        compiler_params=_cparams(("parallel", "parallel")),
        name="moe_combine",
    )(x, yg, wts, mod, final_norm[None, :])


def _pad_cols(a, width):
    return jnp.pad(a, ((0, 0), (0, width - a.shape[1])))


def _split(a, sizes):
    return jnp.split(a, [int(s) for s in np.cumsum(sizes)[:-1]], axis=-1)


def _prep_hyb_w_in(w):
    q_a, k_a, v_a, lr_a, r_a, q_b, kv_b, g_b = _split(w, HYB_SPLITS)
    return jnp.concatenate([q_a, k_a, v_a, r_a, _pad_cols(lr_a, LANES), _pad_cols(g_b, LANES), q_b, kv_b],
                           axis=1).astype(BF16)


def _prep_cmp(pe, w1, w2):
    eye = jnp.eye(NSA_KV_GROUPS, dtype=F32)
    half = CMP_BLOCK // 2
    w1r = w1.reshape(2, 2, half, NSA_HEAD_DIM, CMP_HIDDEN)
    w1x = jnp.einsum("jstdc,gh->jstgdhc", w1r, eye).reshape(2, 2, half * LANES, NSA_KV_GROUPS * CMP_HIDDEN)
    w2x = jnp.einsum("jcd,gh->jgchd", w2, eye).reshape(2, NSA_KV_GROUPS * CMP_HIDDEN, LANES)
    per = pe.reshape(2, 2, half, 1, NSA_HEAD_DIM)
    pe_rows = jnp.broadcast_to(per, (2, 2, half, NSA_KV_GROUPS, NSA_HEAD_DIM)).reshape(2, 2, 1, half * LANES)
    return pe_rows, w1x.astype(BF16), w2x.astype(BF16)


def _overlap_matrix(seq):
    n_rows = seq // CMP_STRIDE
    n = np.arange(n_rows)[:, None] * CMP_STRIDE
    s = np.arange(LANES)[None, :] * SEL_BLOCK
    ov = (n < s + SEL_BLOCK) & (n + CMP_BLOCK - 1 >= s) & (np.arange(LANES)[None, :] < seq // SEL_BLOCK)
    return jnp.asarray(ov, BF16)


def _head_expand():
    ex = np.zeros((SSM_GROUPS, LANES, SSD_GW), np.float32)
    for g in range(SSM_GROUPS):
        for j in range(SSD_HPG):
            for part in range(SSD_SPLIT_PARTS):
                ex[g, part * SSM_HEADS + g * SSD_HPG + j, j * SSM_HEAD_DIM:(j + 1) * SSM_HEAD_DIM] = 1.0
    return jnp.asarray(ex, BF16)


def kernel(x, c, hyb_norm, hyb_mod_w, hyb_mod_b, hyb_w_in, gla_gk_up, gla_gk_bias, gla_out_norm, nsa_cmp_pe, nsa_cmp_w1, nsa_cmp_w2, hyb_w_out, dense_norm, dense_mod_w, dense_mod_b, dense_w_gu, dense_w_down, ssm_norm, ssm_mod_w, ssm_mod_b, ssm_w_in, ssm_conv_w, ssm_conv_b, ssm_dt_bias, ssm_a_log, ssm_d, ssm_gate_norm, ssm_w_out, moe_norm, moe_mod_w, moe_mod_b, moe_router, moe_w_gu, moe_w_down, final_norm):
    bsz, seq, d = x.shape
    mods = _adaln(c, (hyb_mod_w[0], dense_mod_w[0], ssm_mod_w[0], moe_mod_w[0]),
                  (hyb_mod_b[0], dense_mod_b[0], ssm_mod_b[0], moe_mod_b[0]))
    mods = mods.reshape(4, bsz, 1, 3 * d)

    gkup_pad = jnp.pad(gla_gk_up[0], ((0, LANES - GLA_LOWRANK), (0, 0))).astype(BF16)
    (gq, gk, la, gv, gr, nq, kcmp, vcmp, kslc, vslc, kwin, vwin, ng) = _hyb_in(
        x, hyb_norm[0], mods[0], _prep_hyb_w_in(hyb_w_in[0]), gkup_pad, gla_gk_bias[0])
    o_gla = _gla(gq, gk, la, gv, gr, gla_out_norm[0])
    pe_rows, w1x, w2x = _prep_cmp(nsa_cmp_pe[0], nsa_cmp_w1[0], nsa_cmp_w2[0])
    kc, vc = _nsa_compress(kcmp, vcmp, pe_rows, w1x, w2x)
    o_nsa = _nsa_attn(nq, ng, kc, vc, kslc, vslc, kwin, vwin, _overlap_matrix(seq))
    w_out = hyb_w_out[0].astype(BF16)
    n_gla = GLA_HEADS * GLA_DV
    w_gu = dense_w_gu[0].astype(BF16)
    x = _hyb_out_ffn(x, o_gla, o_nsa, mods[0], w_out[:n_gla], w_out[n_gla:], dense_norm[0], mods[1],
                     w_gu[:, :FFN_DENSE], w_gu[:, FFN_DENSE:], dense_w_down[0].astype(BF16))

    wz, wxbc, wdt = _split(ssm_w_in[0], SSM_SPLITS)
    w_ssm = jnp.concatenate([wz, wxbc, _pad_cols(wdt, LANES)], axis=1).astype(BF16)
    pad_heads = lambda a: jnp.pad(a, (0, LANES - SSM_HEADS))[None, :]
    z, xbc, dt = _ssm_in(x, ssm_norm[0], mods[2], w_ssm, ssm_conv_w[0], ssm_conv_b[0], pad_heads(ssm_dt_bias[0]))
    y = _ssd(xbc, z, dt, pad_heads(ssm_a_log[0]), jnp.repeat(ssm_d[0], SSM_HEAD_DIM)[None, :],
             ssm_gate_norm[0], _head_expand())
    x3, h4, gw, sel, cnt = _ssm_out_route(x, y, mods[2], ssm_w_out[0].astype(BF16), moe_norm[0], mods[3],
                                          _pad_cols(moe_router[0], LANES))
    n = bsz * seq
    cnt = cnt[:, 0, :]
    totals = jnp.sum(cnt, axis=0)
    goff_f = jnp.cumsum(totals) - totals
    base = (jnp.cumsum(cnt, axis=0) - cnt + goff_f[None, :])[:, None, :]
    goff = jnp.concatenate([goff_f[:N_EXPERTS], jnp.full((1,), 2.0 * n, F32)]).astype(jnp.int32)
    pos, wts = _moe_plan(gw.reshape(n, LANES), sel.reshape(n, LANES), base)
    pos = pos[:, 0, :]
    xs = _sc_dispatch(h4.reshape(n, d // 2), pos)
    w_gu = moe_w_gu[0].astype(BF16)
    ys = _moe_group(xs, goff, w_gu[:, :, :FFN_EXPERT], w_gu[:, :, FFN_EXPERT:], moe_w_down[0].astype(BF16))
    yg = _sc_gather(ys, pos)
    return _moe_combine(x3, yg.reshape(2, bsz, seq, d // 2), wts.reshape(bsz, seq, LANES), mods[3], final_norm)
```

```python
import functools

import jax
import jax.numpy as jnp
import numpy as np
from jax import lax
from jax.experimental import pallas as pl
from jax.experimental.pallas import tpu as pltpu
from jax.experimental.pallas import tpu_sc as plsc

F32 = jnp.float32
BF16 = jnp.bfloat16

D_MODEL = 1024
NORM_EPS = 1e-6
GLA_HEADS = 4
GLA_DV = D_MODEL // 8
GLA_DK = GLA_DV // 2
GLA_LOWRANK = 16
GLA_TAU = 16.0
GLA_CHUNK = 64
NSA_HEADS = 8
NSA_KV_GROUPS = 2
NSA_HEAD_DIM = D_MODEL // 16
CMP_BLOCK = 32
CMP_STRIDE = 16
CMP_HIDDEN = 4 * NSA_HEAD_DIM
SEL_BLOCK = 64
N_SELECT = 8
WINDOW = 512
D_INNER = 2 * D_MODEL
SSM_HEAD_DIM = 64
SSM_HEADS = D_INNER // SSM_HEAD_DIM
SSM_GROUPS = 4
SSM_STATE = 128
CONV_K = 4
SSM_CHUNK = 64
CONV_CH = D_INNER + 2 * SSM_GROUPS * SSM_STATE
FFN_DENSE = ((8 * D_MODEL // 3 + 127) // 128) * 128
N_EXPERTS = 8
FFN_EXPERT = 7 * D_MODEL // 2
HYB_SPLITS = (GLA_HEADS * GLA_DK, GLA_HEADS * GLA_DK, GLA_HEADS * GLA_DV, GLA_LOWRANK, GLA_HEADS * GLA_DV,
              NSA_HEADS * NSA_HEAD_DIM, 6 * NSA_KV_GROUPS * NSA_HEAD_DIM, 3 * NSA_HEADS)
SSM_SPLITS = (D_INNER, CONV_CH, SSM_HEADS)

LANES = 128
VMEM_LIMIT = 56 * 1024 * 1024

NEG_BIG = -1e30


def _cparams(sem):
    return pltpu.CompilerParams(dimension_semantics=sem, vmem_limit_bytes=VMEM_LIMIT)


def _dot(a, b):
    return jnp.dot(a, b, preferred_element_type=F32)


def _dot_nt(a, b):
    return lax.dot_general(a, b, (((1,), (1,)), ((), ())), preferred_element_type=F32)


def _dot_tn(a, b):
    return lax.dot_general(a, b, (((0,), (0,)), ((), ())), preferred_element_type=F32)


def _split_bf16(a, parts):
    out = []
    r = a
    for _ in range(parts):
        p = r.astype(BF16)
        out.append(p)
        r = r - p.astype(F32)
    return out


def _dot_exact_lhs(m, a, parts=3):
    width = a.shape[1]
    wide = _dot(m, jnp.concatenate(_split_bf16(a, parts), axis=1))
    acc = wide[:, :width]
    for i in range(1, parts):
        acc = acc + wide[:, i * width:(i + 1) * width]
    return acc


def _dot_exact_rhs(a, m, parts=3):
    acc = None
    for p in _split_bf16(a, parts):
        t = _dot(p, m)
        acc = t if acc is None else acc + t
    return acc


def _sigmoid(x):
    return 1.0 / (1.0 + jnp.exp(-x))


def _silu(x):
    return x * _sigmoid(x)


def _softplus(x):
    return jnp.maximum(x, 0.0) + jnp.log(1.0 + jnp.exp(-jnp.abs(x)))


def _gelu_tanh(x):
    return x * (0.5 * (1.0 + jnp.tanh(0.7978845608028654 * (x + 0.044715 * (x * x * x)))))


def _modulated_norm(x, nw, shift, scale):
    ms = jnp.mean(x * x, axis=-1, keepdims=True)
    y = x * lax.rsqrt(ms + NORM_EPS)
    return (y * nw) * (1.0 + scale) + shift


def _const_spec(shape):
    nd = len(shape)
    return pl.BlockSpec(shape, lambda *_: (0,) * nd, pipeline_mode=pl.Buffered(1))


def _adaln_kernel(c_ref, w0, w1, w2, w3, b0, b1, b2, b3, o_ref):
    sc = _silu(c_ref[...]).astype(BF16)
    for i, (w, b) in enumerate(((w0, b0), (w1, b1), (w2, b2), (w3, b3))):
        o_ref[i] = _dot(sc, w[...].astype(BF16)) + b[...]


def _adaln(c, ws, bs):
    bsz, d = c.shape
    n = ws[0].shape[1]
    tn = 512
    w_spec = pl.BlockSpec((d, tn), lambda j: (0, j))
    b_spec = pl.BlockSpec((1, tn), lambda j: (0, j))
    return pl.pallas_call(
        _adaln_kernel,
        grid=(n // tn,),
        in_specs=[pl.BlockSpec((bsz, d), lambda j: (0, 0))] + [w_spec] * 4 + [b_spec] * 4,
        out_specs=pl.BlockSpec((4, bsz, tn), lambda j: (0, 0, j)),
        out_shape=jax.ShapeDtypeStruct((4, bsz, n), F32),
        compiler_params=_cparams(("arbitrary",)),
        name="adaln",
    )(c, *ws, *[b[None, :] for b in bs])


def _mod_specs(tile_axes):
    def spec(part):
        if tile_axes == 2:
            return pl.BlockSpec((1, 1, D_MODEL), lambda b, i: (b, 0, part))
        return pl.BlockSpec((1, 1, D_MODEL), lambda b, i, j, k: (b, 0, part))
    return spec(0), spec(1), spec(2)


HYB_SEG = {
    "gq": (0, 256), "gk": (256, 256), "gv": (512, 512), "gr": (1024, 512), "lr_ng": (1536, 256),
    "nq": (1792, 512), "kv": (2304, 768),
}
HYB_COLS = 3072


def _hyb_in_kernel(x_ref, nw_ref, shift_ref, scale_ref, w_ref, gkup_ref, gkb_ref,
                   gq_ref, gk_ref, la_ref, gv_ref, gr_ref, nq_ref,
                   kcmp_ref, vcmp_ref, kslc_ref, vslc_ref, kwin_ref, vwin_ref, ng_ref):
    h = _modulated_norm(x_ref[0], nw_ref[...], shift_ref[0], scale_ref[0]).astype(BF16)

    def proj(name):
        off, width = HYB_SEG[name]
        return _dot(h, w_ref[:, off:off + width])

    def proj_pair(name, pair=0):
        off = HYB_SEG[name][0] + pair * 2 * LANES
        y = _dot(h, w_ref[:, off:off + 2 * LANES])
        return y[:, :LANES], y[:, LANES:]

    gq_ref[0] = proj("gq")
    gk_ref[0] = proj("gk")
    gv_ref[0] = proj("gv").astype(BF16)
    gr_ref[0] = proj("gr")
    lr, ng = proj_pair("lr_ng")
    z = _dot(lr.astype(BF16), gkup_ref[...]) + gkb_ref[...]
    la_ref[0] = -_softplus(-z) * (1.0 / GLA_TAU)
    ng_ref[0] = ng
    nq_ref[0] = proj("nq").astype(BF16)
    kcmp_ref[0], vcmp_ref[0] = proj_pair("kv", 0)
    kslc, vslc = proj_pair("kv", 1)
    kslc_ref[0] = kslc.T.astype(BF16)
    vslc_ref[0] = vslc.astype(BF16)
    kwin, vwin = proj_pair("kv", 2)
    kwin_ref[0] = kwin.T.astype(BF16)
    vwin_ref[0] = vwin.astype(BF16)


def _hyb_in(x, norm_w, mod, w_perm, gkup_pad, gk_bias):
    bsz, seq, d = x.shape
    tm = 512
    shift_s, scale_s, _ = _mod_specs(2)

    def out(width, dtype):
        return (pl.BlockSpec((1, tm, width), lambda b, i: (b, i, 0)),
                jax.ShapeDtypeStruct((bsz, seq, width), dtype))

    def out_t(dtype):
        return (pl.BlockSpec((1, LANES, tm), lambda b, i: (b, 0, i)),
                jax.ShapeDtypeStruct((bsz, LANES, seq), dtype))

    outs = [out(256, F32), out(256, F32), out(256, F32), out(512, BF16), out(512, F32), out(512, BF16),
            out(128, F32), out(128, F32), out_t(BF16), out(128, BF16), out_t(BF16), out(128, BF16),
            out(128, F32)]
    return pl.pallas_call(
        _hyb_in_kernel,
        grid=(bsz, seq // tm),
        in_specs=[pl.BlockSpec((1, tm, d), lambda b, i: (b, i, 0)),
                  _const_spec((1, d)), shift_s, scale_s,
                  _const_spec(w_perm.shape), _const_spec(gkup_pad.shape), _const_spec((1, 256))],
        out_specs=[o[0] for o in outs],
        out_shape=[o[1] for o in outs],
        compiler_params=_cparams(("parallel", "parallel")),
        name="hyb_in",
    )(x, norm_w[None, :], mod, mod, w_perm, gkup_pad, gk_bias[None, :])


GLA_ROWS = 512


def _gla_kernel(q_ref, k_ref, la_ref, v_ref, r_ref, nw_ref, o_ref, *, seq):
    rb = GLA_ROWS
    ncb = rb // GLA_CHUNK
    row = lax.broadcasted_iota(jnp.int32, (rb, rb), 0)
    col = lax.broadcasted_iota(jnp.int32, (rb, rb), 1)
    chunk_causal = ((row // GLA_CHUNK) == (col // GLA_CHUNK)) & (col <= row)
    tril_bd = jnp.where(chunk_causal, 1.0, 0.0).astype(BF16)
    lane = lax.broadcasted_iota(jnp.int32, (rb, LANES), 1)
    head_mask = (lane < GLA_DK, lane >= GLA_DK)
    nw = nw_ref[...]

    def body(i, st):
        r0 = pl.multiple_of(i * rb, rb)
        rows = pl.ds(r0, rb)
        q = q_ref[0, rows, :]
        k = k_ref[0, rows, :]
        la = la_ref[0, rows, :]
        b = _dot_exact_lhs(tril_bd, la)
        b3 = b.reshape(ncb, GLA_CHUNK, LANES)
        bend3 = b3[:, GLA_CHUNK - 1:GLA_CHUNK, :]
        bend = jnp.broadcast_to(bend3, (ncb, GLA_CHUNK, LANES)).reshape(rb, LANES)
        q_dec = (q * GLA_DK ** -0.5) * jnp.exp(b)
        k_dec = (k * jnp.exp(-b)).astype(BF16)
        k_end = k * jnp.exp(bend - b)
        decay = jnp.exp(bend3)

        qm, vh, o_intra, upd = [], [], [], []
        for h in range(2):
            qm_h = jnp.where(head_mask[h], q_dec, 0.0).astype(BF16)
            att = _dot_nt(qm_h, k_dec)
            att = jnp.where(chunk_causal, att, 0.0).astype(BF16)
            v_h = v_ref[0, rows, h * GLA_DV:(h + 1) * GLA_DV]
            o_intra.append(_dot(att, v_h))
            km_h = jnp.where(head_mask[h], k_end, 0.0).astype(BF16)
            upd.append([_dot_tn(v_h[n * GLA_CHUNK:(n + 1) * GLA_CHUNK], km_h[n * GLA_CHUNK:(n + 1) * GLA_CHUNK])
                        for n in range(ncb)])
            qm.append(qm_h)
            vh.append(v_h)

        prev = []
        for n in range(ncb):
            prev.append(st.astype(BF16))
            st = st * decay[n] + (upd[0][n] + upd[1][n])

        for h in range(2):
            o_inter = jnp.concatenate(
                [_dot_nt(qm[h][n * GLA_CHUNK:(n + 1) * GLA_CHUNK], prev[n]) for n in range(ncb)], axis=0)
            o = o_intra[h] + o_inter
            y = o * lax.rsqrt(jnp.mean(o * o, axis=-1, keepdims=True) + NORM_EPS) * nw
            r = r_ref[0, rows, h * GLA_DV:(h + 1) * GLA_DV]
            o_ref[0, rows, h * GLA_DV:(h + 1) * GLA_DV] = (y * _silu(r)).astype(o_ref.dtype)
        return st

    lax.fori_loop(0, seq // rb, body, jnp.zeros((GLA_DV, 2 * GLA_DK), F32))


def _gla(gq, gk, la, gv, gr, out_norm):
    bsz, seq, _ = gq.shape
    qk_spec = pl.BlockSpec((1, seq, 2 * GLA_DK), lambda b, p: (b, 0, p))
    v_spec = pl.BlockSpec((1, seq, 2 * GLA_DV), lambda b, p: (b, 0, p))
    return pl.pallas_call(
        functools.partial(_gla_kernel, seq=seq),
        grid=(bsz, GLA_HEADS // 2),
        in_specs=[qk_spec, qk_spec, qk_spec, v_spec, v_spec, _const_spec((1, GLA_DV))],
        out_specs=v_spec,
        out_shape=jax.ShapeDtypeStruct((bsz, seq, GLA_HEADS * GLA_DV), BF16),
        compiler_params=_cparams(("parallel", "parallel")),
        name="gla",
    )(gq, gk, la, gv, gr, out_norm[None, :])


def _nsa_compress_kernel(k_ref, v_ref, pe_ref, w1_ref, w2_ref, kc_ref, vc_ref, *, n_cmp):
    rows = k_ref.shape[1]
    rid = lax.broadcasted_iota(jnp.int32, (rows, LANES), 0)
    for j, (src, dst) in enumerate(((k_ref, kc_ref), (v_ref, vc_ref))):
        x = src[0]
        lo = _dot((x + pe_ref[j, 0]).astype(BF16), w1_ref[j, 0])
        hi = _dot((x + pe_ref[j, 1]).astype(BF16), w1_ref[j, 1])
        hpre = lo + pltpu.roll(hi, rows - 1, 0)
        out = jnp.where(rid < n_cmp, _dot(_gelu_tanh(hpre).astype(BF16), w2_ref[j]), 0.0)
        dst[0] = (out.T if j == 0 else out).astype(dst.dtype)


def _nsa_compress(kcmp, vcmp, pe_rows, w1x, w2x):
    bsz, seq, _ = kcmp.shape
    rows = seq // CMP_STRIDE
    width = CMP_STRIDE * LANES
    n_cmp = (seq - CMP_BLOCK) // CMP_STRIDE + 1
    x_spec = pl.BlockSpec((1, rows, width), lambda b: (b, 0, 0))
    return pl.pallas_call(
        functools.partial(_nsa_compress_kernel, n_cmp=n_cmp),
        grid=(bsz,),
        in_specs=[x_spec, x_spec, _const_spec(pe_rows.shape), _const_spec(w1x.shape), _const_spec(w2x.shape)],
        out_specs=[pl.BlockSpec((1, LANES, rows), lambda b: (b, 0, 0)),
                   pl.BlockSpec((1, rows, LANES), lambda b: (b, 0, 0))],
        out_shape=[jax.ShapeDtypeStruct((bsz, LANES, rows), BF16), jax.ShapeDtypeStruct((bsz, rows, LANES), BF16)],
        compiler_params=_cparams(("parallel",)),
        name="nsa_compress",
    )(kcmp.reshape(bsz, rows, width), vcmp.reshape(bsz, rows, width), pe_rows, w1x, w2x)


NSA_QT = 256
NSA_KT = 512
NSA_ROWS = NSA_HEADS * NSA_QT


NSA_MASK_PENALTY = 1e30
NSA_LANE_POS_HI, NSA_LANE_POS_LO, NSA_LANE_CMP_IDX, NSA_LANE_CMP_ONE = 64, 65, 66, 67


def _nsa_attn_kernel(q_ref, g_ref, kc_ref, vc_ref, ks_ref, vs_ref, kw_ref, vw_ref, ov_ref, kf_ref, cf_ref, o_ref,
                     m_ref, acc_ref, *, seq):
    qt, kt = NSA_QT, NSA_KT
    heads_per_group = NSA_HEADS // NSA_KV_GROUPS
    q0 = pl.program_id(1) * qt
    t = q0 + lax.broadcasted_iota(jnp.int32, (qt, 1), 0)
    lane = lax.broadcasted_iota(jnp.int32, (qt, LANES), 1)
    group_lanes = (lane < NSA_HEAD_DIM, lane >= NSA_HEAD_DIM)
    slopes = [2.0 ** (-(h + 1)) for h in range(NSA_HEADS)]

    qa = q_ref[0].astype(F32) * NSA_HEAD_DIM ** -0.5
    q_rows = []
    for h in range(NSA_HEADS):
        g = h // heads_per_group
        blk = qa[:, (h // 2) * LANES:(h // 2 + 1) * LANES]
        if h % 2 != g:
            blk = pltpu.roll(blk, NSA_HEAD_DIM, 1)
        q_rows.append(jnp.where(group_lanes[g], blk, 0.0).astype(BF16))
    q_ext = jnp.concatenate(q_rows, axis=0)

    def alibi_lanes(h):
        s = slopes[h]
        return jnp.where(lane == NSA_LANE_POS_HI, SEL_BLOCK * s,
                         jnp.where(lane == NSA_LANE_POS_LO, s,
                                   jnp.where(lane == NSA_LANE_CMP_IDX, CMP_STRIDE * s,
                                             jnp.where(lane == NSA_LANE_CMP_ONE, (CMP_BLOCK - 1) / 2.0 * s, 0.0))))

    q_pos = [alibi_lanes(h) for h in range(NSA_HEADS)]
    q_full = jnp.concatenate([q_ext, jnp.concatenate(q_pos, axis=0).astype(BF16)], axis=1)

    def softmax_parts(s, mask):
        s = jnp.where(mask, s, -jnp.inf)
        m = jnp.max(s, axis=-1, keepdims=True)
        m = jnp.where(m == -jnp.inf, 0.0, m)
        p = jnp.exp(s - m)
        return p, jnp.sum(p, axis=-1, keepdims=True)

    n_rows = kc_ref.shape[2]
    cidx = lax.broadcasted_iota(jnp.int32, (1, n_rows), 1)
    cmp_mask = cidx * CMP_STRIDE + (CMP_BLOCK - 1) <= t
    s_all = _dot(q_full, jnp.concatenate([kc_ref[0], cf_ref[...]], axis=0))
    p_list, psum = [], [None] * NSA_KV_GROUPS
    for h in range(NSA_HEADS):
        g = h // heads_per_group
        p, l = softmax_parts(s_all[h * qt:(h + 1) * qt], cmp_mask)
        p = p / jnp.maximum(l, 1e-20)
        p_list.append(p.astype(BF16))
        psum[g] = p if psum[g] is None else psum[g] + p
    o_cmp = _dot(jnp.concatenate(p_list, axis=0), vc_ref[0])

    n_blk = LANES // 2
    tq = q0 + lax.broadcasted_iota(jnp.int32, (n_blk, qt), 1)
    bidx = lax.broadcasted_iota(jnp.int32, (n_blk, qt), 0)
    bidx_f = bidx.astype(F32)
    blk_t = tq // SEL_BLOCK
    forced = (bidx == 0) | (bidx == blk_t) | (bidx == blk_t - 1)
    future = bidx * SEL_BLOCK > tq
    q_sel_rows = []
    for g in range(NSA_KV_GROUPS):
        imp = _dot_exact_rhs(psum[g], ov_ref[...]).T[:n_blk]
        v = jnp.where(future, -1.0, jnp.where(forced, 3e38, imp))
        chosen = jnp.zeros((n_blk, qt), F32)
        for _ in range(min(N_SELECT, seq // SEL_BLOCK)):
            m = jnp.max(v, axis=0, keepdims=True)
            idx = jnp.min(jnp.where(v == m, bidx_f, float(LANES)), axis=0, keepdims=True)
            pick = (bidx_f == idx) & (m >= 0.0)
            chosen = jnp.where(pick, 1.0, chosen)
            v = jnp.where(pick, -1.0, v)
        penalty = jnp.concatenate([(chosen - 1.0) * NSA_MASK_PENALTY, jnp.zeros((n_blk, qt), F32)], axis=0).T
        for r in range(heads_per_group):
            q_sel_rows.append((penalty + q_pos[g * heads_per_group + r]).astype(BF16))
    q_full_sel = jnp.concatenate([q_ext, jnp.concatenate(q_sel_rows, axis=0)], axis=1)

    vlane = lax.broadcasted_iota(jnp.int32, (1, LANES), 1)
    own_lanes = (vlane < NSA_HEAD_DIM, vlane >= NSA_HEAD_DIM)

    def weighted_values(p_rows, v):
        v = v.astype(F32)
        outs = []
        for g in range(NSA_KV_GROUPS):
            pg = jnp.concatenate(p_rows[g * heads_per_group:(g + 1) * heads_per_group], axis=0)
            outs.append(_dot(pg, jnp.where(own_lanes[g], v, 1.0).astype(BF16)))
        return jnp.concatenate(outs, axis=0)

    def normalised(acc):
        return acc / jnp.maximum(pltpu.roll(acc, NSA_HEAD_DIM, 1), 1e-20)

    m_ref[...] = jnp.full(m_ref.shape, -jnp.inf, F32)
    acc_ref[...] = jnp.zeros(acc_ref.shape, F32)
    kpos = lax.broadcasted_iota(jnp.int32, (1, kt), 1)

    def sel_tile(j, causal):
        k0 = pl.multiple_of(j * kt, kt)
        ks = jnp.concatenate([ks_ref[0, :, pl.ds(k0, kt)], kf_ref[:, pl.ds(k0, kt)]], axis=0)
        s_t = _dot(q_full_sel, ks)
        visible = (k0 + kpos) <= t
        m_all = m_ref[...]
        p_rows, m_rows, alpha_rows = [], [], []
        for h in range(NSA_HEADS):
            rows = slice(h * qt, (h + 1) * qt)
            s = s_t[rows]
            if causal:
                s = jnp.where(visible, s, -jnp.inf)
            m_old = m_all[rows]
            m_new = jnp.maximum(m_old, jnp.max(s, axis=-1, keepdims=True))
            m_safe = jnp.where(m_new == -jnp.inf, 0.0, m_new)
            p_rows.append(jnp.exp(s - jnp.concatenate([m_safe] * (kt // LANES), axis=1)).astype(BF16))
            alpha_rows.append(jnp.exp(m_old - m_safe))
            m_rows.append(m_new)
        m_ref[...] = jnp.concatenate(m_rows, axis=0)
        acc_ref[...] = (jnp.concatenate(alpha_rows, axis=0) * acc_ref[...]
                        + weighted_values(p_rows, vs_ref[0, pl.ds(k0, kt), :]))

    def sel_past_tile(j, carry):
        sel_tile(j, False)
        return carry

    last = q0 // kt
    lax.fori_loop(0, last, sel_past_tile, 0)
    sel_tile(last, True)
    o_slc = normalised(acc_ref[...])

    wk = WINDOW + qt
    w0 = pl.multiple_of(jnp.maximum(q0 - WINDOW, 0), qt)
    kw = jnp.concatenate([kw_ref[0, :, pl.ds(w0, wk)], kf_ref[:, pl.ds(w0, wk)]], axis=0)
    wdist = t - (w0 + lax.broadcasted_iota(jnp.int32, (1, wk), 1))
    wmask = (wdist >= 0) & (wdist < WINDOW)
    s_all = _dot(q_full, kw)
    p_rows = []
    for h in range(NSA_HEADS):
        s = jnp.where(wmask, s_all[h * qt:(h + 1) * qt], -jnp.inf)
        m = jnp.max(s, axis=-1, keepdims=True)
        p_rows.append(jnp.exp(s - jnp.where(m == -jnp.inf, 0.0, m)).astype(BF16))
    o_win = normalised(weighted_values(p_rows, vw_ref[0, pl.ds(w0, wk), :]))

    gates = _sigmoid(g_ref[0])
    o_heads = []
    for h in range(NSA_HEADS):
        rows = slice(h * qt, (h + 1) * qt)
        o_heads.append(gates[:, 3 * h:3 * h + 1] * o_cmp[rows] + gates[:, 3 * h + 1:3 * h + 2] * o_slc[rows]
                       + gates[:, 3 * h + 2:3 * h + 3] * o_win[rows])
    for c in range(NSA_HEADS // 2):
        g = (2 * c) // heads_per_group
        a, b = o_heads[2 * c], o_heads[2 * c + 1]
        if g == 0:
            blk = jnp.where(group_lanes[0], a, pltpu.roll(b, NSA_HEAD_DIM, 1))
        else:
            blk = jnp.where(group_lanes[0], pltpu.roll(a, NSA_HEAD_DIM, 1), b)
        o_ref[0, :, c * LANES:(c + 1) * LANES] = blk.astype(o_ref.dtype)


def _nsa_position_features(seq):
    assert seq // SEL_BLOCK <= LANES // 2
    pos = np.arange(seq)
    kf = np.zeros((seq, LANES), np.float32)
    kf[pos, pos // SEL_BLOCK] = 1.0
    kf[:, NSA_LANE_POS_HI] = pos // SEL_BLOCK
    kf[:, NSA_LANE_POS_LO] = pos % SEL_BLOCK
    n_rows = seq // CMP_STRIDE
    cf = np.zeros((n_rows, LANES), np.float32)
    cf[:, NSA_LANE_CMP_IDX] = np.arange(n_rows)
    cf[:, NSA_LANE_CMP_ONE] = 1.0
    return jnp.asarray(kf.T, BF16), jnp.asarray(cf.T, BF16)


def _nsa_attn(nq, ng, kc, vc, kslc, vslc, kwin, vwin, overlap):
    bsz, seq, _ = nq.shape
    n_rows = vc.shape[1]
    kfeat, cfeat = _nsa_position_features(seq)
    full = lambda rows: pl.BlockSpec((1, rows, LANES), lambda b, i: (b, 0, 0))
    full_t = lambda cols: pl.BlockSpec((1, LANES, cols), lambda b, i: (b, 0, 0))
    return pl.pallas_call(
        functools.partial(_nsa_attn_kernel, seq=seq),
        grid=(bsz, seq // NSA_QT),
        in_specs=[pl.BlockSpec((1, NSA_QT, NSA_HEADS * NSA_HEAD_DIM), lambda b, i: (b, i, 0)),
                  pl.BlockSpec((1, NSA_QT, LANES), lambda b, i: (b, i, 0)),
                  full_t(n_rows), full(n_rows), full_t(seq), full(seq), full_t(seq), full(seq),
                  _const_spec(overlap.shape), _const_spec(kfeat.shape), _const_spec(cfeat.shape)],
        out_specs=pl.BlockSpec((1, NSA_QT, NSA_HEADS * NSA_HEAD_DIM), lambda b, i: (b, i, 0)),
        out_shape=jax.ShapeDtypeStruct((bsz, seq, NSA_HEADS * NSA_HEAD_DIM), BF16),
        scratch_shapes=[pltpu.VMEM((NSA_ROWS, LANES), F32), pltpu.VMEM((NSA_ROWS, LANES), F32)],
        compiler_params=_cparams(("parallel", "arbitrary")),
        name="nsa_attn",
    )(nq, ng, kc, vc, kslc, vslc, kwin, vwin, overlap, kfeat, cfeat)


FFN_CHUNK = 1408


def _hyb_out_ffn_kernel(x_ref, oa_ref, ob_ref, g1_ref, wo_ref,
                        nw_ref, shift_ref, scale_ref, g2_ref, wgu_ref, wd_ref, o_ref, act_ref):
    n_a = oa_ref.shape[2]
    mix = _dot(oa_ref[0], wo_ref[:n_a, :]) + _dot(ob_ref[0], wo_ref[n_a:, :])
    x1 = x_ref[0] + g1_ref[0] * mix
    h = _modulated_norm(x1, nw_ref[...], shift_ref[0], scale_ref[0]).astype(BF16)
    for f in range(0, FFN_DENSE, FFN_CHUNK):
        gate = _dot(h, wgu_ref[:, f:f + FFN_CHUNK])
        up = _dot(h, wgu_ref[:, FFN_DENSE + f:FFN_DENSE + f + FFN_CHUNK])
        act_ref[:, f:f + FFN_CHUNK] = (_silu(gate) * up).astype(BF16)
    o_ref[0] = x1 + g2_ref[0] * _dot(act_ref[...], wd_ref[...])


def _hyb_out_ffn(x, o_gla, o_nsa, mod1, w_out, norm_w, mod2, w_gu, wd):
    bsz, seq, d = x.shape
    tm = 512
    _, _, gate1 = _mod_specs(2)
    shift2, scale2, gate2 = _mod_specs(2)
    tile = lambda width: pl.BlockSpec((1, tm, width), lambda b, i: (b, i, 0))
    return pl.pallas_call(
        _hyb_out_ffn_kernel,
        grid=(bsz, seq // tm),
        in_specs=[tile(d), tile(o_gla.shape[-1]), tile(o_nsa.shape[-1]), gate1,
                  _const_spec(w_out.shape), _const_spec((1, d)), shift2, scale2, gate2,
                  _const_spec(w_gu.shape), _const_spec(wd.shape)],
        out_specs=tile(d),
        out_shape=jax.ShapeDtypeStruct((bsz, seq, d), F32),
        scratch_shapes=[pltpu.VMEM((tm, FFN_DENSE), BF16)],
        compiler_params=_cparams(("parallel", "parallel")),
        name="hyb_out_ffn",
    )(x, o_gla, o_nsa, mod1, w_out, norm_w[None, :], mod2, mod2, mod2, w_gu, wd)


SSM_DT_OFF = D_INNER + CONV_CH
SSM_COLS = SSM_DT_OFF + LANES
SSM_IN_CHUNK = 256


CONV_TAIL = 8


def _ssm_in_kernel(x_ref, nw_ref, shift_ref, scale_ref, w_ref, cw_ref, cb_ref, dtb_ref,
                   z_ref, xbc_ref, dt_ref, tail_ref):
    tm = x_ref.shape[1]

    @pl.when(pl.program_id(1) == 0)
    def _():
        tail_ref[...] = jnp.zeros(tail_ref.shape, F32)

    h = _modulated_norm(x_ref[0], nw_ref[...], shift_ref[0], scale_ref[0]).astype(BF16)
    for c in range(0, D_INNER, SSM_IN_CHUNK):
        z_ref[0, :, c:c + SSM_IN_CHUNK] = _silu(_dot(h, w_ref[:, c:c + SSM_IN_CHUNK]))
    for c in range(0, CONV_CH, SSM_IN_CHUNK):
        cols = slice(c, c + SSM_IN_CHUNK)
        x = _dot(h, w_ref[:, D_INNER + c:D_INNER + c + SSM_IN_CHUNK])
        tail = tail_ref[:, cols]
        acc = x * cw_ref[CONV_K - 1:CONV_K, cols] + cb_ref[:, cols]
        for j in range(1, CONV_K):
            xs = jnp.concatenate([tail[CONV_TAIL - j:], x[:tm - j]], axis=0)
            acc = acc + xs * cw_ref[CONV_K - 1 - j:CONV_K - j, cols]
        xbc_ref[0, :, cols] = _silu(acc)
        tail_ref[:, cols] = x[tm - CONV_TAIL:]
    dt_ref[0] = _softplus(_dot(h, w_ref[:, SSM_DT_OFF:SSM_COLS]) + dtb_ref[...])


def _ssm_in(x, norm_w, mod, w_perm, conv_w, conv_b, dt_bias_pad):
    bsz, seq, d = x.shape
    tm = 512
    shift_s, scale_s, _ = _mod_specs(2)
    tile = lambda width: pl.BlockSpec((1, tm, width), lambda b, i: (b, i, 0))
    return pl.pallas_call(
        _ssm_in_kernel,
        grid=(bsz, seq // tm),
        in_specs=[tile(d), _const_spec((1, d)), shift_s, scale_s, _const_spec(w_perm.shape),
                  _const_spec((CONV_K, CONV_CH)), _const_spec((1, CONV_CH)), _const_spec((1, LANES))],
        out_specs=[tile(D_INNER), tile(CONV_CH), tile(LANES)],
        out_shape=[jax.ShapeDtypeStruct((bsz, seq, D_INNER), F32),
                   jax.ShapeDtypeStruct((bsz, seq, CONV_CH), F32),
                   jax.ShapeDtypeStruct((bsz, seq, LANES), F32)],
        scratch_shapes=[pltpu.VMEM((CONV_TAIL, CONV_CH), F32)],
        compiler_params=_cparams(("parallel", "arbitrary")),
        name="ssm_in",
    )(x, norm_w[None, :], mod, mod, w_perm, conv_w, conv_b[None, :], dt_bias_pad)


SSD_ROWS = 512
SSD_GW = D_INNER // SSM_GROUPS
SSD_HPG = SSM_HEADS // SSM_GROUPS
SSD_B_OFF = D_INNER
SSD_C_OFF = D_INNER + SSM_GROUPS * SSM_STATE
SSD_SPLIT_PARTS = 3


def _ssd_kernel(xbc_ref, z_ref, dt_ref, alog_ref, dskip_ref, nw_ref, ex_ref, o_ref,
                state_ref, xdt_ref, cum_ref, y_ref):
    rb, q = SSD_ROWS, SSM_CHUNK
    nchunk = rb // q

    @pl.when(pl.program_id(1) == 0)
    def _():
        state_ref[...] = jnp.zeros(state_ref.shape, F32)

    dt = dt_ref[0]
    a = dt * (-jnp.exp(alog_ref[...]))
    row = lax.broadcasted_iota(jnp.int32, (rb, rb), 0)
    col = lax.broadcasted_iota(jnp.int32, (rb, rb), 1)
    tril_bd = jnp.where(((row // q) == (col // q)) & (col <= row), 1.0, 0.0).astype(BF16)
    cum = _dot_exact_lhs(tril_bd, a)

    head_lane = lax.broadcasted_iota(jnp.int32, (rb, LANES), 1) < SSM_HEADS

    def lane_parts(v):
        parts = _split_bf16(jnp.where(head_lane, v, 0.0), SSD_SPLIT_PARTS)
        packed = parts[0].astype(F32)
        for i in range(1, SSD_SPLIT_PARTS):
            packed = packed + pltpu.roll(parts[i].astype(F32), i * SSM_HEADS, 1)
        return packed.astype(BF16)

    dt_parts = lane_parts(dt)
    cum_parts = lane_parts(cum)

    lrow = lax.broadcasted_iota(jnp.int32, (q, SSD_GW), 0)
    lcol = lax.broadcasted_iota(jnp.int32, (q, SSD_GW), 1) % q
    causal_t = lcol <= lrow
    eye_t = lcol == lrow
    half = SSD_GW // 2
    brow = lax.broadcasted_iota(jnp.int32, (half, half), 0) // q
    bcol = lax.broadcasted_iota(jnp.int32, (half, half), 1) // SSM_HEAD_DIM
    same_head = brow == bcol

    for g in range(SSM_GROUPS):
        xg = xbc_ref[0, :, g * SSD_GW:(g + 1) * SSD_GW]
        y_ref[g] = dskip_ref[:, g * SSD_GW:(g + 1) * SSD_GW] * xg
        xdt_ref[g] = xg * _dot(dt_parts, ex_ref[g])
        cum_ref[g] = _dot(cum_parts, ex_ref[g])

    def chunk(n, carry):
        rows = pl.ds(pl.multiple_of(n * q, q), q)
        for g in range(SSM_GROUPS):
            cum_c = cum_ref[g, rows, :]
            cum_s = jnp.sum(jnp.where(eye_t, cum_c, 0.0), axis=0, keepdims=True)
            decay_l = jnp.where(causal_t, jnp.exp(cum_c - cum_s), 0.0)
            bc = xbc_ref[0, rows, SSD_B_OFF + g * SSM_STATE:SSD_B_OFF + (g + 1) * SSM_STATE].astype(BF16)
            cc = xbc_ref[0, rows, SSD_C_OFF + g * SSM_STATE:SSD_C_OFF + (g + 1) * SSM_STATE].astype(BF16)
            cb_t = _dot_nt(cc, jnp.concatenate([bc] * SSD_HPG, axis=0))
            mat = (cb_t * decay_l).astype(BF16)
            xdt_c = xdt_ref[g, rows, :]
            xdt_b = xdt_c.astype(BF16)
            y_diag = []
            for s in range(2):
                blk = xdt_b[:, s * half:(s + 1) * half]
                bd = jnp.where(same_head, jnp.concatenate([blk] * (half // q), axis=0), 0.0).astype(BF16)
                y_diag.append(_dot(mat[:, s * half:(s + 1) * half], bd))
            y = jnp.concatenate(y_diag, axis=1)
            cum_end = cum_c[q - 1:q, :]
            st = state_ref[g]
            y = y + _dot(cc, st.astype(BF16)) * jnp.exp(cum_c)
            x_end = (xdt_c * jnp.exp(cum_end - cum_c)).astype(BF16)
            state_ref[g] = st * jnp.exp(cum_end) + _dot_tn(bc, x_end)
            y_ref[g, rows, :] += y
        return carry

    lax.fori_loop(0, nchunk, chunk, 0)

    for g in range(SSM_GROUPS):
        cols = slice(g * SSD_GW, (g + 1) * SSD_GW)
        y = y_ref[g] * z_ref[0, :, cols]
        y = y * lax.rsqrt(jnp.mean(y * y, axis=-1, keepdims=True) + NORM_EPS) * nw_ref[:, cols]
        o_ref[0, :, cols] = y.astype(o_ref.dtype)


def _ssd(xbc, z, dt, a_log_pad, d_skip_x, norm_w, expand):
    bsz, seq, _ = xbc.shape
    rb = SSD_ROWS
    tile = lambda width: pl.BlockSpec((1, rb, width), lambda b, i: (b, i, 0))
    return pl.pallas_call(
        _ssd_kernel,
        grid=(bsz, seq // rb),
        in_specs=[tile(CONV_CH), tile(D_INNER), tile(LANES), _const_spec((1, LANES)),
                  _const_spec((1, D_INNER)), _const_spec((1, D_INNER)), _const_spec(expand.shape)],
        out_specs=tile(D_INNER),
        out_shape=jax.ShapeDtypeStruct((bsz, seq, D_INNER), BF16),
        scratch_shapes=[pltpu.VMEM((SSM_GROUPS, SSM_STATE, SSD_GW), F32),
                        pltpu.VMEM((SSM_GROUPS, rb, SSD_GW), F32), pltpu.VMEM((SSM_GROUPS, rb, SSD_GW), F32),
                        pltpu.VMEM((SSM_GROUPS, rb, SSD_GW), F32)],
        compiler_params=_cparams(("parallel", "arbitrary")),
        name="ssd",
    )(xbc, z, dt, a_log_pad, d_skip_x, norm_w[None, :], expand)


def _pack_bf16_pairs(a):
    w = a.shape[1] // 2
    bits = lax.bitcast_convert_type(a.astype(jnp.bfloat16).astype(F32), jnp.uint32)
    packed = bits[:, w:] | (bits[:, :w] >> 16)
    return lax.bitcast_convert_type(packed, jnp.int32)


def _unpack_bf16_pairs(p):
    bits = lax.bitcast_convert_type(p, jnp.uint32)
    lo = lax.bitcast_convert_type(bits << 16, F32)
    hi = lax.bitcast_convert_type(bits & jnp.uint32(0xFFFF0000), F32)
    return jnp.concatenate([lo, hi], axis=1).astype(BF16)


def _ssm_out_route_kernel(x_ref, y_ref, g1_ref, w_ref, nw_ref, shift_ref, scale_ref, r_ref,
                          x3_ref, h_ref, gw_ref, sel_ref, cnt_ref):
    x3 = x_ref[0] + g1_ref[0] * _dot(y_ref[0], w_ref[...])
    x3_ref[0] = x3
    h = _modulated_norm(x3, nw_ref[...], shift_ref[0], scale_ref[0])
    h_ref[0] = _pack_bf16_pairs(h)
    h_hi, h_lo = _split_bf16(h, 2)
    r_hi, r_lo = _split_bf16(r_ref[...], 2)
    both = _dot(h_hi, jnp.concatenate([r_hi, r_lo], axis=1))
    logits = both[:, :LANES] + (both[:, LANES:] + _dot(h_lo, r_hi))
    lane = lax.broadcasted_iota(jnp.int32, logits.shape, 1)
    logits = jnp.where(lane < N_EXPERTS, logits, -jnp.inf)
    m1 = jnp.max(logits, axis=-1, keepdims=True)
    i1 = jnp.min(jnp.where(logits == m1, lane, LANES), axis=-1, keepdims=True)
    rest = jnp.where(lane == i1, -jnp.inf, logits)
    m2 = jnp.max(rest, axis=-1, keepdims=True)
    i2 = jnp.min(jnp.where(rest == m2, lane, LANES), axis=-1, keepdims=True)
    e2 = jnp.exp(m2 - m1)
    w1 = 1.0 / (1.0 + e2)
    w2 = e2 / (1.0 + e2)
    chosen = (lane == i1) | (lane == i2)
    gw_ref[0] = jnp.where(lane == i1, w1, 0.0) + jnp.where(lane == i2, w2, 0.0)
    sel = jnp.where(chosen, 1.0, 0.0)
    sel_ref[0] = sel.astype(BF16)
    cnt_ref[0] = jnp.sum(sel, axis=0, keepdims=True)


def _ssm_out_route(x, y, mod1, w_out, norm_w, mod2, router_pad):
    bsz, seq, d = x.shape
    tm = ROUTE_TM
    tiles = seq // tm
    _, _, gate1 = _mod_specs(2)
    shift2, scale2, _ = _mod_specs(2)
    tile = lambda width: pl.BlockSpec((1, tm, width), lambda b, i: (b, i, 0))
    return pl.pallas_call(
        _ssm_out_route_kernel,
        grid=(bsz, tiles),
        in_specs=[tile(d), tile(D_INNER), gate1, _const_spec(w_out.shape), _const_spec((1, d)), shift2, scale2,
                  _const_spec(router_pad.shape)],
        out_specs=[tile(d), tile(d // 2), tile(LANES), tile(LANES),
                   pl.BlockSpec((1, 1, LANES), lambda b, i: (b * tiles + i, 0, 0))],
        out_shape=[jax.ShapeDtypeStruct((bsz, seq, d), F32), jax.ShapeDtypeStruct((bsz, seq, d // 2), jnp.int32),
                   jax.ShapeDtypeStruct((bsz, seq, LANES), F32), jax.ShapeDtypeStruct((bsz, seq, LANES), BF16),
                   jax.ShapeDtypeStruct((bsz * tiles, 1, LANES), F32)],
        compiler_params=_cparams(("parallel", "parallel")),
        name="ssm_out_route",
    )(x, y, mod1, w_out, norm_w[None, :], mod2, mod2, router_pad)


ROUTE_TM = 512


def _moe_plan_kernel(gw_ref, sel_ref, base_ref, pos_ref, wts_ref):
    tm = gw_ref.shape[0]
    gw = gw_ref[...]
    sel = sel_ref[...]
    lane = lax.broadcasted_iota(jnp.int32, (tm, LANES), 1)
    chosen = sel.astype(F32) > 0.5
    row = lax.broadcasted_iota(jnp.int32, (tm, tm), 0)
    col = lax.broadcasted_iota(jnp.int32, (tm, tm), 1)
    before = jnp.where(col < row, 1.0, 0.0).astype(BF16)
    rank = _dot(before, sel)
    dest = rank + base_ref[0]
    first = jnp.min(jnp.where(chosen, lane, LANES), axis=-1, keepdims=True)
    last = jnp.max(jnp.where(chosen, lane, -1), axis=-1, keepdims=True)
    ones = jnp.ones((8, LANES), BF16)
    for slot, pick in enumerate((lane == first, lane == last)):
        parts = _split_bf16(jnp.where(pick, dest, 0.0), 3)
        pos = _dot_nt(ones, parts[0]) + (_dot_nt(ones, parts[1]) + _dot_nt(ones, parts[2]))
        pos_ref[slot] = pos.astype(jnp.int32)
    w_lo = jnp.sum(jnp.where(lane == first, gw, 0.0), axis=-1, keepdims=True)
    w_hi = jnp.sum(jnp.where(lane == last, gw, 0.0), axis=-1, keepdims=True)
    wts_ref[...] = jnp.where(lane == 0, w_lo, jnp.where(lane == 1, w_hi, 0.0))


def _moe_plan(gw, sel, base):
    n = gw.shape[0]
    tm = ROUTE_TM
    return pl.pallas_call(
        _moe_plan_kernel,
        grid=(n // tm,),
        in_specs=[pl.BlockSpec((tm, LANES), lambda i: (i, 0)), pl.BlockSpec((tm, LANES), lambda i: (i, 0)),
                  pl.BlockSpec((1, 1, LANES), lambda i: (i, 0, 0))],
        out_specs=[pl.BlockSpec((2, 8, tm), lambda i: (0, 0, i)), pl.BlockSpec((tm, LANES), lambda i: (i, 0))],
        out_shape=[jax.ShapeDtypeStruct((2, 8, n), jnp.int32), jax.ShapeDtypeStruct((n, LANES), F32)],
        compiler_params=_cparams(("parallel",)),
        name="moe_plan",
    )(gw, sel, base)


SC_WINDOW = 128


SC_CORES = 2
SC_SUBCORES = 16


def _sc_workers():
    return SC_CORES, SC_CORES * SC_SUBCORES


def _sc_dispatch(rows, pos):
    n, width = rows.shape
    ncores, workers = _sc_workers()
    per_worker = n // workers
    steps = per_worker // SC_WINDOW
    mesh = plsc.VectorSubcoreMesh(core_axis_name="c", subcore_axis_name="s")

    @functools.partial(
        pl.kernel, mesh=mesh, out_type=jax.ShapeDtypeStruct((2 * n, width), rows.dtype),
        scratch_types=[pltpu.VMEM((2, steps, SC_WINDOW), jnp.int32), pltpu.VMEM((SC_WINDOW, width), rows.dtype)],
        name="moe_dispatch")
    def run(rows_hbm, pos_hbm, out_hbm, idx_v, rows_v):
        wid = lax.axis_index("s") * ncores + lax.axis_index("c")
        pltpu.sync_copy(pos_hbm.at[wid], idx_v)
        for j in range(steps):
            pltpu.sync_copy(rows_hbm.at[pl.ds(wid * per_worker + j * SC_WINDOW, SC_WINDOW)], rows_v)
            pltpu.sync_copy(rows_v, out_hbm.at[idx_v.at[0, j]])
            pltpu.sync_copy(rows_v, out_hbm.at[idx_v.at[1, j]])

    pos_w = pos.reshape(2, workers, steps, SC_WINDOW).transpose(1, 0, 2, 3)
    return run(rows, pos_w)


SC_GATHER_WINDOW = 128


def _sc_gather(table, pos):
    _, width = table.shape
    n = pos.shape[1]
    ncores, workers = _sc_workers()
    per_worker = n // workers
    win = SC_GATHER_WINDOW
    steps = per_worker // win
    mesh = plsc.VectorSubcoreMesh(core_axis_name="c", subcore_axis_name="s")

    @functools.partial(
        pl.kernel, mesh=mesh, out_type=jax.ShapeDtypeStruct((2, n, width), table.dtype),
        scratch_types=[pltpu.VMEM((2, steps, win), jnp.int32), pltpu.VMEM((win, width), table.dtype)],
        name="moe_gather")
    def run(table_hbm, pos_hbm, out_hbm, idx_v, rows_v):
        wid = lax.axis_index("s") * ncores + lax.axis_index("c")
        pltpu.sync_copy(pos_hbm.at[wid], idx_v)
        for s in range(2):
            for j in range(steps):
                pltpu.sync_copy(table_hbm.at[idx_v.at[s, j]], rows_v)
                pltpu.sync_copy(rows_v, out_hbm.at[s, pl.ds(wid * per_worker + j * win, win)])

    pos_w = pos.reshape(2, workers, steps, win).transpose(1, 0, 2, 3)
    return run(table, pos_w)


MOE_TM = 512
MOE_TF = 896


def _moe_pairs(goff, n_rows):
    tiles = n_rows // MOE_TM
    steps = tiles + N_EXPERTS - 1
    first_row = jnp.arange(tiles, dtype=jnp.int32) * MOE_TM
    ends = goff[1:]
    e_lo = jnp.sum(first_row[:, None] >= ends[None, :], axis=1).astype(jnp.int32)
    e_hi = jnp.sum((first_row + (MOE_TM - 1))[:, None] >= ends[None, :], axis=1).astype(jnp.int32)
    count = e_hi - e_lo + 1
    start = jnp.cumsum(count) - count
    p = jnp.arange(steps, dtype=jnp.int32)
    tile = jnp.sum(start[None, :] <= p[:, None], axis=1).astype(jnp.int32) - 1
    expert = e_lo[tile] + (p - start[tile])
    valid = p < jnp.sum(count)
    return (jnp.where(valid, tile, tiles - 1).astype(jnp.int32),
            jnp.where(valid, expert, N_EXPERTS - 1).astype(jnp.int32), valid.astype(jnp.int32))


def _moe_group_kernel(tile_ref, exp_ref, valid_ref, goff_ref, x_ref, wg_ref, wu_ref, wd_ref, y_ref, act_ref):
    p = pl.program_id(0)
    tm = x_ref.shape[0]
    tile = tile_ref[p]
    expert = exp_ref[p]

    @pl.when((p == 0) | (tile != tile_ref[jnp.maximum(p - 1, 0)]))
    def _():
        y_ref[...] = jnp.zeros(y_ref.shape, y_ref.dtype)

    @pl.when(valid_ref[p] == 1)
    def _():
        x = _unpack_bf16_pairs(x_ref[...])
        for f in range(0, FFN_EXPERT, MOE_TF):
            act_ref[:, f:f + MOE_TF] = (_silu(_dot(x, wg_ref[0, :, f:f + MOE_TF]))
                                        * _dot(x, wu_ref[0, :, f:f + MOE_TF])).astype(BF16)
        out = _pack_bf16_pairs(_dot(act_ref[...], wd_ref[0]))
        rows = tile * tm + lax.broadcasted_iota(jnp.int32, (tm, 1), 0)
        mine = (rows >= goff_ref[expert]) & (rows < goff_ref[expert + 1])
        y_ref[...] = jnp.where(mine, out, y_ref[...])


def _moe_group(xs, goff, w_gu, wd):
    n_rows, half = xs.shape
    d = 2 * half
    tm = MOE_TM
    tile, expert, valid = _moe_pairs(goff, n_rows)

    def resident(shape, col_block=0):
        return pl.BlockSpec(shape, lambda p, t, e, v, g: (e[p], 0, col_block), pipeline_mode=pl.Buffered(1))

    grid_spec = pltpu.PrefetchScalarGridSpec(
        num_scalar_prefetch=4,
        grid=(tile.shape[0],),
        in_specs=[pl.BlockSpec((tm, half), lambda p, t, e, v, g: (t[p], 0)),
                  resident((1, d, FFN_EXPERT), 0), resident((1, d, FFN_EXPERT), 1), resident((1, FFN_EXPERT, d))],
        out_specs=pl.BlockSpec((tm, half), lambda p, t, e, v, g: (t[p], 0)),
        scratch_shapes=[pltpu.VMEM((tm, FFN_EXPERT), BF16)],
    )
    return pl.pallas_call(
        _moe_group_kernel,
        grid_spec=grid_spec,
        out_shape=jax.ShapeDtypeStruct((n_rows, half), jnp.int32),
        compiler_params=_cparams(("arbitrary",)),
        name="moe_group",
    )(tile, expert, valid, goff, xs, w_gu, w_gu, wd)


def _moe_combine_kernel(x_ref, yg_ref, wts_ref, g_ref, fn_ref, o_ref):
    w = wts_ref[0]
    mix = (w[:, 0:1] * _unpack_bf16_pairs(yg_ref[0, 0]).astype(F32)
           + w[:, 1:2] * _unpack_bf16_pairs(yg_ref[1, 0]).astype(F32))
    x4 = x_ref[0] + g_ref[0] * mix
    y = x4 * lax.rsqrt(jnp.mean(x4 * x4, axis=-1, keepdims=True) + NORM_EPS)
    o_ref[0] = y * fn_ref[...]


def _moe_combine(x, yg, wts, mod, final_norm):
    bsz, seq, d = x.shape
    tm = 512
    _, _, gate = _mod_specs(2)
    tile = lambda width: pl.BlockSpec((1, tm, width), lambda b, i: (b, i, 0))
    return pl.pallas_call(
        _moe_combine_kernel,
        grid=(bsz, seq // tm),
        in_specs=[tile(d), pl.BlockSpec((2, 1, tm, d // 2), lambda b, i: (0, b, i, 0)), tile(LANES), gate,
                  _const_spec((1, d))],
        out_specs=tile(d),
        out_shape=jax.ShapeDtypeStruct((bsz, seq, d), F32),
        compiler_params=_cparams(("parallel", "parallel")),
        name="moe_combine",
    )(x, yg, wts, mod, final_norm[None, :])


def _pad_cols(a, width):
    return jnp.pad(a, ((0, 0), (0, width - a.shape[1])))


def _split(a, sizes):
    return jnp.split(a, [int(s) for s in np.cumsum(sizes)[:-1]], axis=-1)


def _prep_hyb_w_in(w):
    q_a, k_a, v_a, lr_a, r_a, q_b, kv_b, g_b = _split(w, HYB_SPLITS)
    return jnp.concatenate([q_a, k_a, v_a, r_a, _pad_cols(lr_a, LANES), _pad_cols(g_b, LANES), q_b, kv_b],
                           axis=1).astype(BF16)


def _prep_cmp(pe, w1, w2):
    eye = jnp.eye(NSA_KV_GROUPS, dtype=F32)
    half = CMP_BLOCK // 2
    w1r = w1.reshape(2, 2, half, NSA_HEAD_DIM, CMP_HIDDEN)
    w1x = jnp.einsum("jstdc,gh->jstgdhc", w1r, eye).reshape(2, 2, half * LANES, NSA_KV_GROUPS * CMP_HIDDEN)
    w2x = jnp.einsum("jcd,gh->jgchd", w2, eye).reshape(2, NSA_KV_GROUPS * CMP_HIDDEN, LANES)
    per = pe.reshape(2, 2, half, 1, NSA_HEAD_DIM)
    pe_rows = jnp.broadcast_to(per, (2, 2, half, NSA_KV_GROUPS, NSA_HEAD_DIM)).reshape(2, 2, 1, half * LANES)
    return pe_rows, w1x.astype(BF16), w2x.astype(BF16)


def _overlap_matrix(seq):
    n_rows = seq // CMP_STRIDE
    n = np.arange(n_rows)[:, None] * CMP_STRIDE
    s = np.arange(LANES)[None, :] * SEL_BLOCK
    ov = (n < s + SEL_BLOCK) & (n + CMP_BLOCK - 1 >= s) & (np.arange(LANES)[None, :] < seq // SEL_BLOCK)
    return jnp.asarray(ov, BF16)


def _head_expand():
    ex = np.zeros((SSM_GROUPS, LANES, SSD_GW), np.float32)
    for g in range(SSM_GROUPS):
        for j in range(SSD_HPG):
            for part in range(SSD_SPLIT_PARTS):
                ex[g, part * SSM_HEADS + g * SSD_HPG + j, j * SSM_HEAD_DIM:(j + 1) * SSM_HEAD_DIM] = 1.0
    return jnp.asarray(ex, BF16)


def kernel(x, c, hyb_norm, hyb_mod_w, hyb_mod_b, hyb_w_in, gla_gk_up, gla_gk_bias, gla_out_norm, nsa_cmp_pe, nsa_cmp_w1, nsa_cmp_w2, hyb_w_out, dense_norm, dense_mod_w, dense_mod_b, dense_w_gu, dense_w_down, ssm_norm, ssm_mod_w, ssm_mod_b, ssm_w_in, ssm_conv_w, ssm_conv_b, ssm_dt_bias, ssm_a_log, ssm_d, ssm_gate_norm, ssm_w_out, moe_norm, moe_mod_w, moe_mod_b, moe_router, moe_w_gu, moe_w_down, final_norm):
    bsz, seq, d = x.shape
    mods = _adaln(c, (hyb_mod_w[0], dense_mod_w[0], ssm_mod_w[0], moe_mod_w[0]),
                  (hyb_mod_b[0], dense_mod_b[0], ssm_mod_b[0], moe_mod_b[0]))
    mods = mods.reshape(4, bsz, 1, 3 * d)

    gkup_pad = jnp.pad(gla_gk_up[0], ((0, LANES - GLA_LOWRANK), (0, 0))).astype(BF16)
    (gq, gk, la, gv, gr, nq, kcmp, vcmp, kslc, vslc, kwin, vwin, ng) = _hyb_in(
        x, hyb_norm[0], mods[0], _prep_hyb_w_in(hyb_w_in[0]), gkup_pad, gla_gk_bias[0])
    o_gla = _gla(gq, gk, la, gv, gr, gla_out_norm[0])
    pe_rows, w1x, w2x = _prep_cmp(nsa_cmp_pe[0], nsa_cmp_w1[0], nsa_cmp_w2[0])
    kc, vc = _nsa_compress(kcmp, vcmp, pe_rows, w1x, w2x)
    o_nsa = _nsa_attn(nq, ng, kc, vc, kslc, vslc, kwin, vwin, _overlap_matrix(seq))
    x = _hyb_out_ffn(x, o_gla, o_nsa, mods[0], hyb_w_out[0].astype(BF16), dense_norm[0], mods[1],
                     dense_w_gu[0].astype(BF16), dense_w_down[0].astype(BF16))

    wz, wxbc, wdt = _split(ssm_w_in[0], SSM_SPLITS)
    w_ssm = jnp.concatenate([wz, wxbc, _pad_cols(wdt, LANES)], axis=1).astype(BF16)
    pad_heads = lambda a: jnp.pad(a, (0, LANES - SSM_HEADS))[None, :]
    z, xbc, dt = _ssm_in(x, ssm_norm[0], mods[2], w_ssm, ssm_conv_w[0], ssm_conv_b[0], pad_heads(ssm_dt_bias[0]))
    y = _ssd(xbc, z, dt, pad_heads(ssm_a_log[0]), jnp.repeat(ssm_d[0], SSM_HEAD_DIM)[None, :],
             ssm_gate_norm[0], _head_expand())
    x3, h4, gw, sel, cnt = _ssm_out_route(x, y, mods[2], ssm_w_out[0].astype(BF16), moe_norm[0], mods[3],
                                          _pad_cols(moe_router[0], LANES))
    n = bsz * seq
    cnt = cnt[:, 0, :]
    totals = jnp.sum(cnt, axis=0)
    goff_f = jnp.cumsum(totals) - totals
    base = (jnp.cumsum(cnt, axis=0) - cnt + goff_f[None, :])[:, None, :]
    goff = jnp.concatenate([goff_f[:N_EXPERTS], jnp.full((1,), 2.0 * n, F32)]).astype(jnp.int32)
    pos, wts = _moe_plan(gw.reshape(n, LANES), sel.reshape(n, LANES), base)
    pos = pos[:, 0, :]
    xs = _sc_dispatch(h4.reshape(n, d // 2), pos)
    ys = _moe_group(xs, goff, moe_w_gu[0].astype(BF16), moe_w_down[0].astype(BF16))
    yg = _sc_gather(ys, pos)
    return _moe_combine(x3, yg.reshape(2, bsz, seq, d // 2), wts.reshape(bsz, seq, LANES), mods[3], final_norm)
```

```python
import functools

import jax
import jax.numpy as jnp
import numpy as np
from jax import lax
from jax.experimental import pallas as pl
from jax.experimental.pallas import tpu as pltpu
from jax.experimental.pallas import tpu_sc as plsc

F32 = jnp.float32
BF16 = jnp.bfloat16

D_MODEL = 1024
NORM_EPS = 1e-6
GLA_HEADS = 4
GLA_DV = D_MODEL // 8
GLA_DK = GLA_DV // 2
GLA_LOWRANK = 16
GLA_TAU = 16.0
GLA_CHUNK = 64
NSA_HEADS = 8
NSA_KV_GROUPS = 2
NSA_HEAD_DIM = D_MODEL // 16
CMP_BLOCK = 32
CMP_STRIDE = 16
CMP_HIDDEN = 4 * NSA_HEAD_DIM
SEL_BLOCK = 64
N_SELECT = 8
WINDOW = 512
D_INNER = 2 * D_MODEL
SSM_HEAD_DIM = 64
SSM_HEADS = D_INNER // SSM_HEAD_DIM
SSM_GROUPS = 4
SSM_STATE = 128
CONV_K = 4
SSM_CHUNK = 64
CONV_CH = D_INNER + 2 * SSM_GROUPS * SSM_STATE
FFN_DENSE = ((8 * D_MODEL // 3 + 127) // 128) * 128
N_EXPERTS = 8
FFN_EXPERT = 7 * D_MODEL // 2
HYB_SPLITS = (GLA_HEADS * GLA_DK, GLA_HEADS * GLA_DK, GLA_HEADS * GLA_DV, GLA_LOWRANK, GLA_HEADS * GLA_DV,
              NSA_HEADS * NSA_HEAD_DIM, 6 * NSA_KV_GROUPS * NSA_HEAD_DIM, 3 * NSA_HEADS)
SSM_SPLITS = (D_INNER, CONV_CH, SSM_HEADS)

LANES = 128
VMEM_LIMIT = 56 * 1024 * 1024

NEG_BIG = -1e30


def _cparams(sem, vmem_limit=VMEM_LIMIT):
    return pltpu.CompilerParams(dimension_semantics=sem, vmem_limit_bytes=vmem_limit)


def _dot(a, b):
    return jnp.dot(a, b, preferred_element_type=F32)


def _dot_nt(a, b):
    return lax.dot_general(a, b, (((1,), (1,)), ((), ())), preferred_element_type=F32)


def _dot_tn(a, b):
    return lax.dot_general(a, b, (((0,), (0,)), ((), ())), preferred_element_type=F32)


def _split_bf16(a, parts):
    out = []
    r = a
    for _ in range(parts):
        p = r.astype(BF16)
        out.append(p)
        r = r - p.astype(F32)
    return out


def _dot_exact_lhs(m, a, parts=3):
    width = a.shape[1]
    wide = _dot(m, jnp.concatenate(_split_bf16(a, parts), axis=1))
    acc = wide[:, :width]
    for i in range(1, parts):
        acc = acc + wide[:, i * width:(i + 1) * width]
    return acc


def _dot_exact_rhs(a, m, parts=3):
    acc = None
    for p in _split_bf16(a, parts):
        t = _dot(p, m)
        acc = t if acc is None else acc + t
    return acc


def _sigmoid(x):
    return 1.0 / (1.0 + jnp.exp(-x))


def _silu(x):
    return x * _sigmoid(x)


def _softplus(x):
    return jnp.maximum(x, 0.0) + jnp.log(1.0 + jnp.exp(-jnp.abs(x)))


def _gelu_tanh(x):
    return x * (0.5 * (1.0 + jnp.tanh(0.7978845608028654 * (x + 0.044715 * (x * x * x)))))


def _modulated_norm(x, nw, shift, scale):
    ms = jnp.mean(x * x, axis=-1, keepdims=True)
    y = x * lax.rsqrt(ms + NORM_EPS)
    return (y * nw) * (1.0 + scale) + shift


def _const_spec(shape):
    nd = len(shape)
    return pl.BlockSpec(shape, lambda *_: (0,) * nd, pipeline_mode=pl.Buffered(1))


def _adaln_kernel(c_ref, w0, w1, w2, w3, b0, b1, b2, b3, o_ref):
    sc = _silu(c_ref[...]).astype(BF16)
    for i, (w, b) in enumerate(((w0, b0), (w1, b1), (w2, b2), (w3, b3))):
        o_ref[i] = _dot(sc, w[...].astype(BF16)) + b[...]


def _adaln(c, ws, bs):
    bsz, d = c.shape
    n = ws[0].shape[1]
    tn = 512
    w_spec = pl.BlockSpec((d, tn), lambda j: (0, j))
    b_spec = pl.BlockSpec((1, tn), lambda j: (0, j))
    return pl.pallas_call(
        _adaln_kernel,
        grid=(n // tn,),
        in_specs=[pl.BlockSpec((bsz, d), lambda j: (0, 0))] + [w_spec] * 4 + [b_spec] * 4,
        out_specs=pl.BlockSpec((4, bsz, tn), lambda j: (0, 0, j)),
        out_shape=jax.ShapeDtypeStruct((4, bsz, n), F32),
        compiler_params=_cparams(("arbitrary",)),
        name="adaln",
    )(c, *ws, *[b[None, :] for b in bs])


def _mod_specs(tile_axes):
    def spec(part):
        if tile_axes == 2:
            return pl.BlockSpec((1, 1, D_MODEL), lambda b, i: (b, 0, part))
        return pl.BlockSpec((1, 1, D_MODEL), lambda b, i, j, k: (b, 0, part))
    return spec(0), spec(1), spec(2)


HYB_SEG = {
    "gq": (0, 256), "gk": (256, 256), "gv": (512, 512), "gr": (1024, 512), "lr_ng": (1536, 256),
    "nq": (1792, 512), "kv": (2304, 768),
}
HYB_COLS = 3072


def _hyb_in_kernel(x_ref, nw_ref, shift_ref, scale_ref, w_ref, gkup_ref, gkb_ref,
                   gq_ref, gk_ref, la_ref, gv_ref, gr_ref, nq_ref,
                   kcmp_ref, vcmp_ref, kslc_ref, vslc_ref, kwin_ref, vwin_ref, ng_ref):
    h = _modulated_norm(x_ref[0], nw_ref[...], shift_ref[0], scale_ref[0]).astype(BF16)

    def proj(name):
        off, width = HYB_SEG[name]
        return _dot(h, w_ref[:, off:off + width])

    def proj_pair(name, pair=0):
        off = HYB_SEG[name][0] + pair * 2 * LANES
        y = _dot(h, w_ref[:, off:off + 2 * LANES])
        return y[:, :LANES], y[:, LANES:]

    gq_ref[0] = proj("gq")
    gk_ref[0] = proj("gk")
    gv_ref[0] = proj("gv").astype(BF16)
    gr_ref[0] = proj("gr")
    lr, ng = proj_pair("lr_ng")
    z = _dot(lr.astype(BF16), gkup_ref[...]) + gkb_ref[...]
    la_ref[0] = -_softplus(-z) * (1.0 / GLA_TAU)
    ng_ref[0] = ng
    nq_ref[0] = proj("nq").astype(BF16)
    kcmp_ref[0], vcmp_ref[0] = proj_pair("kv", 0)
    kslc, vslc = proj_pair("kv", 1)
    kslc_ref[0] = kslc.T.astype(BF16)
    vslc_ref[0] = vslc.astype(BF16)
    kwin, vwin = proj_pair("kv", 2)
    kwin_ref[0] = kwin.T.astype(BF16)
    vwin_ref[0] = vwin.astype(BF16)


def _hyb_in(x, norm_w, mod, w_perm, gkup_pad, gk_bias):
    bsz, seq, d = x.shape
    tm = 512
    shift_s, scale_s, _ = _mod_specs(2)

    def out(width, dtype):
        return (pl.BlockSpec((1, tm, width), lambda b, i: (b, i, 0)),
                jax.ShapeDtypeStruct((bsz, seq, width), dtype))

    def out_t(dtype):
        return (pl.BlockSpec((1, LANES, tm), lambda b, i: (b, 0, i)),
                jax.ShapeDtypeStruct((bsz, LANES, seq), dtype))

    outs = [out(256, F32), out(256, F32), out(256, F32), out(512, BF16), out(512, F32), out(512, BF16),
            out(128, F32), out(128, F32), out_t(BF16), out(128, BF16), out_t(BF16), out(128, BF16),
            out(128, F32)]
    return pl.pallas_call(
        _hyb_in_kernel,
        grid=(bsz, seq // tm),
        in_specs=[pl.BlockSpec((1, tm, d), lambda b, i: (b, i, 0)),
                  _const_spec((1, d)), shift_s, scale_s,
                  _const_spec(w_perm.shape), _const_spec(gkup_pad.shape), _const_spec((1, 256))],
        out_specs=[o[0] for o in outs],
        out_shape=[o[1] for o in outs],
        compiler_params=_cparams(("parallel", "parallel")),
        name="hyb_in",
    )(x, norm_w[None, :], mod, mod, w_perm, gkup_pad, gk_bias[None, :])


GLA_ROWS = 512


def _gla_kernel(q_ref, k_ref, la_ref, v_ref, r_ref, nw_ref, o_ref, *, seq):
    rb = GLA_ROWS
    ncb = rb // GLA_CHUNK
    row = lax.broadcasted_iota(jnp.int32, (rb, rb), 0)
    col = lax.broadcasted_iota(jnp.int32, (rb, rb), 1)
    chunk_causal = ((row // GLA_CHUNK) == (col // GLA_CHUNK)) & (col <= row)
    tril_bd = jnp.where(chunk_causal, 1.0, 0.0).astype(BF16)
    lane = lax.broadcasted_iota(jnp.int32, (rb, LANES), 1)
    head_mask = (lane < GLA_DK, lane >= GLA_DK)
    nw = nw_ref[...]

    def body(i, st):
        r0 = pl.multiple_of(i * rb, rb)
        rows = pl.ds(r0, rb)
        q = q_ref[0, rows, :]
        k = k_ref[0, rows, :]
        la = la_ref[0, rows, :]
        b = _dot_exact_lhs(tril_bd, la)
        b3 = b.reshape(ncb, GLA_CHUNK, LANES)
        bend3 = b3[:, GLA_CHUNK - 1:GLA_CHUNK, :]
        bend = jnp.broadcast_to(bend3, (ncb, GLA_CHUNK, LANES)).reshape(rb, LANES)
        q_dec = (q * GLA_DK ** -0.5) * jnp.exp(b)
        k_dec = (k * jnp.exp(-b)).astype(BF16)
        k_end = k * jnp.exp(bend - b)
        decay = jnp.exp(bend3)

        qm, vh, o_intra, upd = [], [], [], []
        for h in range(2):
            qm_h = jnp.where(head_mask[h], q_dec, 0.0).astype(BF16)
            att = _dot_nt(qm_h, k_dec)
            att = jnp.where(chunk_causal, att, 0.0).astype(BF16)
            v_h = v_ref[0, rows, h * GLA_DV:(h + 1) * GLA_DV]
            o_intra.append(_dot(att, v_h))
            km_h = jnp.where(head_mask[h], k_end, 0.0).astype(BF16)
            upd.append([_dot_tn(v_h[n * GLA_CHUNK:(n + 1) * GLA_CHUNK], km_h[n * GLA_CHUNK:(n + 1) * GLA_CHUNK])
                        for n in range(ncb)])
            qm.append(qm_h)
            vh.append(v_h)

        prev = []
        for n in range(ncb):
            prev.append(st.astype(BF16))
            st = st * decay[n] + (upd[0][n] + upd[1][n])

        for h in range(2):
            o_inter = jnp.concatenate(
                [_dot_nt(qm[h][n * GLA_CHUNK:(n + 1) * GLA_CHUNK], prev[n]) for n in range(ncb)], axis=0)
            o = o_intra[h] + o_inter
            y = o * lax.rsqrt(jnp.mean(o * o, axis=-1, keepdims=True) + NORM_EPS) * nw
            r = r_ref[0, rows, h * GLA_DV:(h + 1) * GLA_DV]
            o_ref[0, rows, h * GLA_DV:(h + 1) * GLA_DV] = (y * _silu(r)).astype(o_ref.dtype)
        return st

    lax.fori_loop(0, seq // rb, body, jnp.zeros((GLA_DV, 2 * GLA_DK), F32))


def _gla(gq, gk, la, gv, gr, out_norm):
    bsz, seq, _ = gq.shape
    qk_spec = pl.BlockSpec((1, seq, 2 * GLA_DK), lambda b, p: (b, 0, p))
    v_spec = pl.BlockSpec((1, seq, 2 * GLA_DV), lambda b, p: (b, 0, p))
    return pl.pallas_call(
        functools.partial(_gla_kernel, seq=seq),
        grid=(bsz, GLA_HEADS // 2),
        in_specs=[qk_spec, qk_spec, qk_spec, v_spec, v_spec, _const_spec((1, GLA_DV))],
        out_specs=v_spec,
        out_shape=jax.ShapeDtypeStruct((bsz, seq, GLA_HEADS * GLA_DV), BF16),
        compiler_params=_cparams(("parallel", "parallel")),
        name="gla",
    )(gq, gk, la, gv, gr, out_norm[None, :])


def _nsa_compress_kernel(k_ref, v_ref, pe_ref, w1_ref, w2_ref, kc_ref, vc_ref, *, n_cmp):
    rows = k_ref.shape[1]
    rid = lax.broadcasted_iota(jnp.int32, (rows, LANES), 0)
    for j, (src, dst) in enumerate(((k_ref, kc_ref), (v_ref, vc_ref))):
        x = src[0]
        lo = _dot((x + pe_ref[j, 0]).astype(BF16), w1_ref[j, 0])
        hi = _dot((x + pe_ref[j, 1]).astype(BF16), w1_ref[j, 1])
        hpre = lo + pltpu.roll(hi, rows - 1, 0)
        out = jnp.where(rid < n_cmp, _dot(_gelu_tanh(hpre).astype(BF16), w2_ref[j]), 0.0)
        dst[0] = (out.T if j == 0 else out).astype(dst.dtype)


def _nsa_compress(kcmp, vcmp, pe_rows, w1x, w2x):
    bsz, seq, _ = kcmp.shape
    rows = seq // CMP_STRIDE
    width = CMP_STRIDE * LANES
    n_cmp = (seq - CMP_BLOCK) // CMP_STRIDE + 1
    x_spec = pl.BlockSpec((1, rows, width), lambda b: (b, 0, 0))
    return pl.pallas_call(
        functools.partial(_nsa_compress_kernel, n_cmp=n_cmp),
        grid=(bsz,),
        in_specs=[x_spec, x_spec, _const_spec(pe_rows.shape), _const_spec(w1x.shape), _const_spec(w2x.shape)],
        out_specs=[pl.BlockSpec((1, LANES, rows), lambda b: (b, 0, 0)),
                   pl.BlockSpec((1, rows, LANES), lambda b: (b, 0, 0))],
        out_shape=[jax.ShapeDtypeStruct((bsz, LANES, rows), BF16), jax.ShapeDtypeStruct((bsz, rows, LANES), BF16)],
        compiler_params=_cparams(("parallel",)),
        name="nsa_compress",
    )(kcmp.reshape(bsz, rows, width), vcmp.reshape(bsz, rows, width), pe_rows, w1x, w2x)


NSA_QT = 256
NSA_KT = 512
NSA_ROWS = NSA_HEADS * NSA_QT


NSA_MASK_PENALTY = 1e30
NSA_LANE_POS_HI, NSA_LANE_POS_LO, NSA_LANE_CMP_IDX, NSA_LANE_CMP_ONE = 64, 65, 66, 67


def _nsa_attn_kernel(q_ref, g_ref, kc_ref, vc_ref, ks_ref, vs_ref, kw_ref, vw_ref, ov_ref, kf_ref, cf_ref, o_ref,
                     m_ref, acc_ref, *, seq):
    qt, kt = NSA_QT, NSA_KT
    heads_per_group = NSA_HEADS // NSA_KV_GROUPS
    q0 = pl.program_id(1) * qt
    t = q0 + lax.broadcasted_iota(jnp.int32, (qt, 1), 0)
    lane = lax.broadcasted_iota(jnp.int32, (qt, LANES), 1)
    group_lanes = (lane < NSA_HEAD_DIM, lane >= NSA_HEAD_DIM)
    slopes = [2.0 ** (-(h + 1)) for h in range(NSA_HEADS)]

    qa = q_ref[0].astype(F32) * NSA_HEAD_DIM ** -0.5
    q_rows = []
    for h in range(NSA_HEADS):
        g = h // heads_per_group
        blk = qa[:, (h // 2) * LANES:(h // 2 + 1) * LANES]
        if h % 2 != g:
            blk = pltpu.roll(blk, NSA_HEAD_DIM, 1)
        q_rows.append(jnp.where(group_lanes[g], blk, 0.0).astype(BF16))
    q_ext = jnp.concatenate(q_rows, axis=0)

    def alibi_lanes(h):
        s = slopes[h]
        return jnp.where(lane == NSA_LANE_POS_HI, SEL_BLOCK * s,
                         jnp.where(lane == NSA_LANE_POS_LO, s,
                                   jnp.where(lane == NSA_LANE_CMP_IDX, CMP_STRIDE * s,
                                             jnp.where(lane == NSA_LANE_CMP_ONE, (CMP_BLOCK - 1) / 2.0 * s, 0.0))))

    q_pos = [alibi_lanes(h) for h in range(NSA_HEADS)]
    q_full = jnp.concatenate([q_ext, jnp.concatenate(q_pos, axis=0).astype(BF16)], axis=1)

    def softmax_parts(s, mask):
        s = jnp.where(mask, s, -jnp.inf)
        m = jnp.max(s, axis=-1, keepdims=True)
        m = jnp.where(m == -jnp.inf, 0.0, m)
        p = jnp.exp(s - m)
        return p, jnp.sum(p, axis=-1, keepdims=True)

    n_rows = kc_ref.shape[2]
    cidx = lax.broadcasted_iota(jnp.int32, (1, n_rows), 1)
    cmp_mask = cidx * CMP_STRIDE + (CMP_BLOCK - 1) <= t
    s_all = _dot(q_full, jnp.concatenate([kc_ref[0], cf_ref[...]], axis=0))
    p_list, psum = [], [None] * NSA_KV_GROUPS
    for h in range(NSA_HEADS):
        g = h // heads_per_group
        p, l = softmax_parts(s_all[h * qt:(h + 1) * qt], cmp_mask)
        p = p / jnp.maximum(l, 1e-20)
        p_list.append(p.astype(BF16))
        psum[g] = p if psum[g] is None else psum[g] + p
    o_cmp = _dot(jnp.concatenate(p_list, axis=0), vc_ref[0])

    n_blk = LANES // 2
    tq = q0 + lax.broadcasted_iota(jnp.int32, (n_blk, qt), 1)
    bidx = lax.broadcasted_iota(jnp.int32, (n_blk, qt), 0)
    bidx_f = bidx.astype(F32)
    blk_t = tq // SEL_BLOCK
    forced = (bidx == 0) | (bidx == blk_t) | (bidx == blk_t - 1)
    future = bidx * SEL_BLOCK > tq
    q_sel_rows = []
    for g in range(NSA_KV_GROUPS):
        imp = _dot_exact_rhs(psum[g], ov_ref[...]).T[:n_blk]
        v = jnp.where(future, -1.0, jnp.where(forced, 3e38, imp))
        chosen = jnp.zeros((n_blk, qt), F32)
        for _ in range(min(N_SELECT, seq // SEL_BLOCK)):
            m = jnp.max(v, axis=0, keepdims=True)
            idx = jnp.min(jnp.where(v == m, bidx_f, float(LANES)), axis=0, keepdims=True)
            pick = (bidx_f == idx) & (m >= 0.0)
            chosen = jnp.where(pick, 1.0, chosen)
            v = jnp.where(pick, -1.0, v)
        penalty = jnp.concatenate([(chosen - 1.0) * NSA_MASK_PENALTY, jnp.zeros((n_blk, qt), F32)], axis=0).T
        for r in range(heads_per_group):
            q_sel_rows.append((penalty + q_pos[g * heads_per_group + r]).astype(BF16))
    q_full_sel = jnp.concatenate([q_ext, jnp.concatenate(q_sel_rows, axis=0)], axis=1)

    vlane = lax.broadcasted_iota(jnp.int32, (1, LANES), 1)
    own_lanes = (vlane < NSA_HEAD_DIM, vlane >= NSA_HEAD_DIM)

    def weighted_values(p_rows, v):
        v = v.astype(F32)
        outs = []
        for g in range(NSA_KV_GROUPS):
            pg = jnp.concatenate(p_rows[g * heads_per_group:(g + 1) * heads_per_group], axis=0)
            outs.append(_dot(pg, jnp.where(own_lanes[g], v, 1.0).astype(BF16)))
        return jnp.concatenate(outs, axis=0)

    def normalised(acc):
        return acc / jnp.maximum(pltpu.roll(acc, NSA_HEAD_DIM, 1), 1e-20)

    m_ref[...] = jnp.full(m_ref.shape, -jnp.inf, F32)
    acc_ref[...] = jnp.zeros(acc_ref.shape, F32)
    def sel_tile(k0, width, causal):
        ks = jnp.concatenate([ks_ref[0, :, pl.ds(k0, width)], kf_ref[:, pl.ds(k0, width)]], axis=0)
        s_t = _dot(q_full_sel, ks)
        visible = (k0 + lax.broadcasted_iota(jnp.int32, (1, width), 1)) <= t
        m_all = m_ref[...]
        p_rows, m_rows, alpha_rows = [], [], []
        for h in range(NSA_HEADS):
            rows = slice(h * qt, (h + 1) * qt)
            s = s_t[rows]
            if causal:
                s = jnp.where(visible, s, -jnp.inf)
            m_old = m_all[rows]
            m_new = jnp.maximum(m_old, jnp.max(s, axis=-1, keepdims=True))
            m_safe = jnp.where(m_new == -jnp.inf, 0.0, m_new)
            p_rows.append(jnp.exp(s - jnp.concatenate([m_safe] * (width // LANES), axis=1)).astype(BF16))
            alpha_rows.append(jnp.exp(m_old - m_safe))
            m_rows.append(m_new)
        m_ref[...] = jnp.concatenate(m_rows, axis=0)
        acc_ref[...] = (jnp.concatenate(alpha_rows, axis=0) * acc_ref[...]
                        + weighted_values(p_rows, vs_ref[0, pl.ds(k0, width), :]))

    def sel_past_tile(j, carry):
        sel_tile(pl.multiple_of(j * kt, kt), kt, False)
        return carry

    last = q0 // kt
    lax.fori_loop(0, last, sel_past_tile, 0)
    sel_tile(pl.multiple_of(last * kt, kt), kt, True)
    o_slc = normalised(acc_ref[...])

    wk = WINDOW + qt
    w0 = pl.multiple_of(jnp.maximum(q0 - WINDOW, 0), qt)
    kw = jnp.concatenate([kw_ref[0, :, pl.ds(w0, wk)], kf_ref[:, pl.ds(w0, wk)]], axis=0)
    wdist = t - (w0 + lax.broadcasted_iota(jnp.int32, (1, wk), 1))
    wmask = (wdist >= 0) & (wdist < WINDOW)
    s_all = _dot(q_full, kw)
    p_rows = []
    for h in range(NSA_HEADS):
        s = jnp.where(wmask, s_all[h * qt:(h + 1) * qt], -jnp.inf)
        m = jnp.max(s, axis=-1, keepdims=True)
        p_rows.append(jnp.exp(s - jnp.where(m == -jnp.inf, 0.0, m)).astype(BF16))
    o_win = normalised(weighted_values(p_rows, vw_ref[0, pl.ds(w0, wk), :]))

    gates = _sigmoid(g_ref[0])
    o_heads = []
    for h in range(NSA_HEADS):
        rows = slice(h * qt, (h + 1) * qt)
        o_heads.append(gates[:, 3 * h:3 * h + 1] * o_cmp[rows] + gates[:, 3 * h + 1:3 * h + 2] * o_slc[rows]
                       + gates[:, 3 * h + 2:3 * h + 3] * o_win[rows])
    for c in range(NSA_HEADS // 2):
        g = (2 * c) // heads_per_group
        a, b = o_heads[2 * c], o_heads[2 * c + 1]
        if g == 0:
            blk = jnp.where(group_lanes[0], a, pltpu.roll(b, NSA_HEAD_DIM, 1))
        else:
            blk = jnp.where(group_lanes[0], pltpu.roll(a, NSA_HEAD_DIM, 1), b)
        o_ref[0, :, c * LANES:(c + 1) * LANES] = blk.astype(o_ref.dtype)


def _nsa_position_features(seq):
    assert seq // SEL_BLOCK <= LANES // 2
    pos = np.arange(seq)
    kf = np.zeros((seq, LANES), np.float32)
    kf[pos, pos // SEL_BLOCK] = 1.0
    kf[:, NSA_LANE_POS_HI] = pos // SEL_BLOCK
    kf[:, NSA_LANE_POS_LO] = pos % SEL_BLOCK
    n_rows = seq // CMP_STRIDE
    cf = np.zeros((n_rows, LANES), np.float32)
    cf[:, NSA_LANE_CMP_IDX] = np.arange(n_rows)
    cf[:, NSA_LANE_CMP_ONE] = 1.0
    return jnp.asarray(kf.T, BF16), jnp.asarray(cf.T, BF16)


def _nsa_attn(nq, ng, kc, vc, kslc, vslc, kwin, vwin, overlap):
    bsz, seq, _ = nq.shape
    n_rows = vc.shape[1]
    kfeat, cfeat = _nsa_position_features(seq)
    full = lambda rows: pl.BlockSpec((1, rows, LANES), lambda b, i: (b, 0, 0))
    full_t = lambda cols: pl.BlockSpec((1, LANES, cols), lambda b, i: (b, 0, 0))
    return pl.pallas_call(
        functools.partial(_nsa_attn_kernel, seq=seq),
        grid=(bsz, seq // NSA_QT),
        in_specs=[pl.BlockSpec((1, NSA_QT, NSA_HEADS * NSA_HEAD_DIM), lambda b, i: (b, i, 0)),
                  pl.BlockSpec((1, NSA_QT, LANES), lambda b, i: (b, i, 0)),
                  full_t(n_rows), full(n_rows), full_t(seq), full(seq), full_t(seq), full(seq),
                  _const_spec(overlap.shape), _const_spec(kfeat.shape), _const_spec(cfeat.shape)],
        out_specs=pl.BlockSpec((1, NSA_QT, NSA_HEADS * NSA_HEAD_DIM), lambda b, i: (b, i, 0)),
        out_shape=jax.ShapeDtypeStruct((bsz, seq, NSA_HEADS * NSA_HEAD_DIM), BF16),
        scratch_shapes=[pltpu.VMEM((NSA_ROWS, LANES), F32), pltpu.VMEM((NSA_ROWS, LANES), F32)],
        compiler_params=_cparams(("parallel", "arbitrary")),
        name="nsa_attn",
    )(nq, ng, kc, vc, kslc, vslc, kwin, vwin, overlap, kfeat, cfeat)


FFN_CHUNK = 1408


def _hyb_out_ffn_kernel(x_ref, oa_ref, ob_ref, g1_ref, wo_ref,
                        nw_ref, shift_ref, scale_ref, g2_ref, wgu_ref, wd_ref, o_ref, act_ref):
    n_a = oa_ref.shape[2]
    mix = _dot(oa_ref[0], wo_ref[:n_a, :]) + _dot(ob_ref[0], wo_ref[n_a:, :])
    x1 = x_ref[0] + g1_ref[0] * mix
    h = _modulated_norm(x1, nw_ref[...], shift_ref[0], scale_ref[0]).astype(BF16)
    for f in range(0, FFN_DENSE, FFN_CHUNK):
        gate = _dot(h, wgu_ref[:, f:f + FFN_CHUNK])
        up = _dot(h, wgu_ref[:, FFN_DENSE + f:FFN_DENSE + f + FFN_CHUNK])
        act_ref[:, f:f + FFN_CHUNK] = (_silu(gate) * up).astype(BF16)
    o_ref[0] = x1 + g2_ref[0] * _dot(act_ref[...], wd_ref[...])


def _hyb_out_ffn(x, o_gla, o_nsa, mod1, w_out, norm_w, mod2, w_gu, wd):
    bsz, seq, d = x.shape
    tm = 512
    _, _, gate1 = _mod_specs(2)
    shift2, scale2, gate2 = _mod_specs(2)
    tile = lambda width: pl.BlockSpec((1, tm, width), lambda b, i: (b, i, 0))
    return pl.pallas_call(
        _hyb_out_ffn_kernel,
        grid=(bsz, seq // tm),
        in_specs=[tile(d), tile(o_gla.shape[-1]), tile(o_nsa.shape[-1]), gate1,
                  _const_spec(w_out.shape), _const_spec((1, d)), shift2, scale2, gate2,
                  _const_spec(w_gu.shape), _const_spec(wd.shape)],
        out_specs=tile(d),
        out_shape=jax.ShapeDtypeStruct((bsz, seq, d), F32),
        scratch_shapes=[pltpu.VMEM((tm, FFN_DENSE), BF16)],
        compiler_params=_cparams(("parallel", "parallel")),
        name="hyb_out_ffn",
    )(x, o_gla, o_nsa, mod1, w_out, norm_w[None, :], mod2, mod2, mod2, w_gu, wd)


SSM_DT_OFF = D_INNER + CONV_CH
SSM_COLS = SSM_DT_OFF + LANES
SSM_IN_CHUNK = 256


CONV_TAIL = 8


def _ssm_in_kernel(x_ref, nw_ref, shift_ref, scale_ref, w_ref, cw_ref, cb_ref, dtb_ref,
                   z_ref, xbc_ref, dt_ref, tail_ref):
    tm = x_ref.shape[1]

    @pl.when(pl.program_id(1) == 0)
    def _():
        tail_ref[...] = jnp.zeros(tail_ref.shape, F32)

    h = _modulated_norm(x_ref[0], nw_ref[...], shift_ref[0], scale_ref[0]).astype(BF16)
    for c in range(0, D_INNER, SSM_IN_CHUNK):
        z_ref[0, :, c:c + SSM_IN_CHUNK] = _silu(_dot(h, w_ref[:, c:c + SSM_IN_CHUNK]))
    for c in range(0, CONV_CH, SSM_IN_CHUNK):
        cols = slice(c, c + SSM_IN_CHUNK)
        x = _dot(h, w_ref[:, D_INNER + c:D_INNER + c + SSM_IN_CHUNK])
        tail = tail_ref[:, cols]
        acc = x * cw_ref[CONV_K - 1:CONV_K, cols] + cb_ref[:, cols]
        for j in range(1, CONV_K):
            xs = jnp.concatenate([tail[CONV_TAIL - j:], x[:tm - j]], axis=0)
            acc = acc + xs * cw_ref[CONV_K - 1 - j:CONV_K - j, cols]
        xbc_ref[0, :, cols] = _silu(acc)
        tail_ref[:, cols] = x[tm - CONV_TAIL:]
    dt_ref[0] = _softplus(_dot(h, w_ref[:, SSM_DT_OFF:SSM_COLS]) + dtb_ref[...])


def _ssm_in(x, norm_w, mod, w_perm, conv_w, conv_b, dt_bias_pad):
    bsz, seq, d = x.shape
    tm = 512
    shift_s, scale_s, _ = _mod_specs(2)
    tile = lambda width: pl.BlockSpec((1, tm, width), lambda b, i: (b, i, 0))
    return pl.pallas_call(
        _ssm_in_kernel,
        grid=(bsz, seq // tm),
        in_specs=[tile(d), _const_spec((1, d)), shift_s, scale_s, _const_spec(w_perm.shape),
                  _const_spec((CONV_K, CONV_CH)), _const_spec((1, CONV_CH)), _const_spec((1, LANES))],
        out_specs=[tile(D_INNER), tile(CONV_CH), tile(LANES)],
        out_shape=[jax.ShapeDtypeStruct((bsz, seq, D_INNER), F32),
                   jax.ShapeDtypeStruct((bsz, seq, CONV_CH), F32),
                   jax.ShapeDtypeStruct((bsz, seq, LANES), F32)],
        scratch_shapes=[pltpu.VMEM((CONV_TAIL, CONV_CH), F32)],
        compiler_params=_cparams(("parallel", "arbitrary")),
        name="ssm_in",
    )(x, norm_w[None, :], mod, mod, w_perm, conv_w, conv_b[None, :], dt_bias_pad)


SSD_ROWS = 512
SSD_GW = D_INNER // SSM_GROUPS
SSD_HPG = SSM_HEADS // SSM_GROUPS
SSD_B_OFF = D_INNER
SSD_C_OFF = D_INNER + SSM_GROUPS * SSM_STATE
SSD_SPLIT_PARTS = 3


def _ssd_kernel(xbc_ref, z_ref, dt_ref, alog_ref, dskip_ref, nw_ref, ex_ref, o_ref,
                state_ref, xdt_ref, cum_ref, y_ref):
    rb, q = SSD_ROWS, SSM_CHUNK
    nchunk = rb // q

    @pl.when(pl.program_id(1) == 0)
    def _():
        state_ref[...] = jnp.zeros(state_ref.shape, F32)

    dt = dt_ref[0]
    a = dt * (-jnp.exp(alog_ref[...]))
    row = lax.broadcasted_iota(jnp.int32, (rb, rb), 0)
    col = lax.broadcasted_iota(jnp.int32, (rb, rb), 1)
    tril_bd = jnp.where(((row // q) == (col // q)) & (col <= row), 1.0, 0.0).astype(BF16)
    cum = _dot_exact_lhs(tril_bd, a)

    head_lane = lax.broadcasted_iota(jnp.int32, (rb, LANES), 1) < SSM_HEADS

    def lane_parts(v):
        parts = _split_bf16(jnp.where(head_lane, v, 0.0), SSD_SPLIT_PARTS)
        packed = parts[0].astype(F32)
        for i in range(1, SSD_SPLIT_PARTS):
            packed = packed + pltpu.roll(parts[i].astype(F32), i * SSM_HEADS, 1)
        return packed.astype(BF16)

    dt_parts = lane_parts(dt)
    cum_parts = lane_parts(cum)

    lrow = lax.broadcasted_iota(jnp.int32, (q, SSD_GW), 0)
    lcol = lax.broadcasted_iota(jnp.int32, (q, SSD_GW), 1) % q
    causal_t = lcol <= lrow
    eye_t = lcol == lrow
    half = SSD_GW // 2
    brow = lax.broadcasted_iota(jnp.int32, (half, half), 0) // q
    bcol = lax.broadcasted_iota(jnp.int32, (half, half), 1) // SSM_HEAD_DIM
    same_head = brow == bcol

    for g in range(SSM_GROUPS):
        xg = xbc_ref[0, :, g * SSD_GW:(g + 1) * SSD_GW]
        y_ref[g] = dskip_ref[:, g * SSD_GW:(g + 1) * SSD_GW] * xg
        xdt_ref[g] = xg * _dot(dt_parts, ex_ref[g])
        cum_ref[g] = _dot(cum_parts, ex_ref[g])

    def chunk(n, carry):
        rows = pl.ds(pl.multiple_of(n * q, q), q)
        for g in range(SSM_GROUPS):
            cum_c = cum_ref[g, rows, :]
            cum_s = jnp.sum(jnp.where(eye_t, cum_c, 0.0), axis=0, keepdims=True)
            decay_l = jnp.where(causal_t, jnp.exp(cum_c - cum_s), 0.0)
            bc = xbc_ref[0, rows, SSD_B_OFF + g * SSM_STATE:SSD_B_OFF + (g + 1) * SSM_STATE].astype(BF16)
            cc = xbc_ref[0, rows, SSD_C_OFF + g * SSM_STATE:SSD_C_OFF + (g + 1) * SSM_STATE].astype(BF16)
            cb_t = _dot_nt(cc, jnp.concatenate([bc] * SSD_HPG, axis=0))
            mat = (cb_t * decay_l).astype(BF16)
            xdt_c = xdt_ref[g, rows, :]
            xdt_b = xdt_c.astype(BF16)
            y_diag = []
            for s in range(2):
                blk = xdt_b[:, s * half:(s + 1) * half]
                bd = jnp.where(same_head, jnp.concatenate([blk] * (half // q), axis=0), 0.0).astype(BF16)
                y_diag.append(_dot(mat[:, s * half:(s + 1) * half], bd))
            y = jnp.concatenate(y_diag, axis=1)
            cum_end = cum_c[q - 1:q, :]
            st = state_ref[g]
            y = y + _dot(cc, st.astype(BF16)) * jnp.exp(cum_c)
            x_end = (xdt_c * jnp.exp(cum_end - cum_c)).astype(BF16)
            state_ref[g] = st * jnp.exp(cum_end) + _dot_tn(bc, x_end)
            y_ref[g, rows, :] += y
        return carry

    lax.fori_loop(0, nchunk, chunk, 0)

    for g in range(SSM_GROUPS):
        cols = slice(g * SSD_GW, (g + 1) * SSD_GW)
        y = y_ref[g] * z_ref[0, :, cols]
        y = y * lax.rsqrt(jnp.mean(y * y, axis=-1, keepdims=True) + NORM_EPS) * nw_ref[:, cols]
        o_ref[0, :, cols] = y.astype(o_ref.dtype)


def _ssd(xbc, z, dt, a_log_pad, d_skip_x, norm_w, expand):
    bsz, seq, _ = xbc.shape
    rb = SSD_ROWS
    tile = lambda width: pl.BlockSpec((1, rb, width), lambda b, i: (b, i, 0))
    return pl.pallas_call(
        _ssd_kernel,
        grid=(bsz, seq // rb),
        in_specs=[tile(CONV_CH), tile(D_INNER), tile(LANES), _const_spec((1, LANES)),
                  _const_spec((1, D_INNER)), _const_spec((1, D_INNER)), _const_spec(expand.shape)],
        out_specs=tile(D_INNER),
        out_shape=jax.ShapeDtypeStruct((bsz, seq, D_INNER), BF16),
        scratch_shapes=[pltpu.VMEM((SSM_GROUPS, SSM_STATE, SSD_GW), F32),
                        pltpu.VMEM((SSM_GROUPS, rb, SSD_GW), F32), pltpu.VMEM((SSM_GROUPS, rb, SSD_GW), F32),
                        pltpu.VMEM((SSM_GROUPS, rb, SSD_GW), F32)],
        compiler_params=_cparams(("parallel", "arbitrary")),
        name="ssd",
    )(xbc, z, dt, a_log_pad, d_skip_x, norm_w[None, :], expand)


def _pack_bf16_pairs(a):
    w = a.shape[1] // 2
    bits = lax.bitcast_convert_type(a.astype(jnp.bfloat16).astype(F32), jnp.uint32)
    packed = bits[:, w:] | (bits[:, :w] >> 16)
    return lax.bitcast_convert_type(packed, jnp.int32)


def _unpack_bf16_pairs(p):
    bits = lax.bitcast_convert_type(p, jnp.uint32)
    lo = lax.bitcast_convert_type(bits << 16, F32)
    hi = lax.bitcast_convert_type(bits & jnp.uint32(0xFFFF0000), F32)
    return jnp.concatenate([lo, hi], axis=1).astype(BF16)


def _ssm_out_route_kernel(x_ref, y_ref, g1_ref, w_ref, nw_ref, shift_ref, scale_ref, r_ref,
                          x3_ref, h_ref, gw_ref, sel_ref, cnt_ref):
    x3 = x_ref[0] + g1_ref[0] * _dot(y_ref[0], w_ref[...])
    x3_ref[0] = x3
    h = _modulated_norm(x3, nw_ref[...], shift_ref[0], scale_ref[0])
    h_ref[0] = _pack_bf16_pairs(h)
    h_hi, h_lo = _split_bf16(h, 2)
    r_hi, r_lo = _split_bf16(r_ref[...], 2)
    both = _dot(h_hi, jnp.concatenate([r_hi, r_lo], axis=1))
    logits = both[:, :LANES] + (both[:, LANES:] + _dot(h_lo, r_hi))
    lane = lax.broadcasted_iota(jnp.int32, logits.shape, 1)
    logits = jnp.where(lane < N_EXPERTS, logits, -jnp.inf)
    m1 = jnp.max(logits, axis=-1, keepdims=True)
    i1 = jnp.min(jnp.where(logits == m1, lane, LANES), axis=-1, keepdims=True)
    rest = jnp.where(lane == i1, -jnp.inf, logits)
    m2 = jnp.max(rest, axis=-1, keepdims=True)
    i2 = jnp.min(jnp.where(rest == m2, lane, LANES), axis=-1, keepdims=True)
    e2 = jnp.exp(m2 - m1)
    w1 = 1.0 / (1.0 + e2)
    w2 = e2 / (1.0 + e2)
    chosen = (lane == i1) | (lane == i2)
    gw_ref[0] = jnp.where(lane == i1, w1, 0.0) + jnp.where(lane == i2, w2, 0.0)
    sel = jnp.where(chosen, 1.0, 0.0)
    sel_ref[0] = sel.astype(BF16)
    cnt_ref[0] = jnp.sum(sel, axis=0, keepdims=True)


def _ssm_out_route(x, y, mod1, w_out, norm_w, mod2, router_pad):
    bsz, seq, d = x.shape
    tm = ROUTE_TM
    tiles = seq // tm
    _, _, gate1 = _mod_specs(2)
    shift2, scale2, _ = _mod_specs(2)
    tile = lambda width: pl.BlockSpec((1, tm, width), lambda b, i: (b, i, 0))
    return pl.pallas_call(
        _ssm_out_route_kernel,
        grid=(bsz, tiles),
        in_specs=[tile(d), tile(D_INNER), gate1, _const_spec(w_out.shape), _const_spec((1, d)), shift2, scale2,
                  _const_spec(router_pad.shape)],
        out_specs=[tile(d), tile(d // 2), tile(LANES), tile(LANES),
                   pl.BlockSpec((1, 1, LANES), lambda b, i: (b * tiles + i, 0, 0))],
        out_shape=[jax.ShapeDtypeStruct((bsz, seq, d), F32), jax.ShapeDtypeStruct((bsz, seq, d // 2), jnp.int32),
                   jax.ShapeDtypeStruct((bsz, seq, LANES), F32), jax.ShapeDtypeStruct((bsz, seq, LANES), BF16),
                   jax.ShapeDtypeStruct((bsz * tiles, 1, LANES), F32)],
        compiler_params=_cparams(("parallel", "parallel")),
        name="ssm_out_route",
    )(x, y, mod1, w_out, norm_w[None, :], mod2, mod2, router_pad)


ROUTE_TM = 512


def _moe_plan_kernel(gw_ref, sel_ref, base_ref, pos_ref, wts_ref):
    tm = gw_ref.shape[0]
    gw = gw_ref[...]
    sel = sel_ref[...]
    lane = lax.broadcasted_iota(jnp.int32, (tm, LANES), 1)
    chosen = sel.astype(F32) > 0.5
    row = lax.broadcasted_iota(jnp.int32, (tm, tm), 0)
    col = lax.broadcasted_iota(jnp.int32, (tm, tm), 1)
    before = jnp.where(col < row, 1.0, 0.0).astype(BF16)
    rank = _dot(before, sel)
    dest = rank + base_ref[0]
    first = jnp.min(jnp.where(chosen, lane, LANES), axis=-1, keepdims=True)
    last = jnp.max(jnp.where(chosen, lane, -1), axis=-1, keepdims=True)
    ones = jnp.ones((8, LANES), BF16)
    for slot, pick in enumerate((lane == first, lane == last)):
        parts = _split_bf16(jnp.where(pick, dest, 0.0), 3)
        pos = _dot_nt(ones, parts[0]) + (_dot_nt(ones, parts[1]) + _dot_nt(ones, parts[2]))
        pos_ref[slot] = pos.astype(jnp.int32)
    w_lo = jnp.sum(jnp.where(lane == first, gw, 0.0), axis=-1, keepdims=True)
    w_hi = jnp.sum(jnp.where(lane == last, gw, 0.0), axis=-1, keepdims=True)
    wts_ref[...] = jnp.where(lane == 0, w_lo, jnp.where(lane == 1, w_hi, 0.0))


def _moe_plan(gw, sel, base):
    n = gw.shape[0]
    tm = ROUTE_TM
    return pl.pallas_call(
        _moe_plan_kernel,
        grid=(n // tm,),
        in_specs=[pl.BlockSpec((tm, LANES), lambda i: (i, 0)), pl.BlockSpec((tm, LANES), lambda i: (i, 0)),
                  pl.BlockSpec((1, 1, LANES), lambda i: (i, 0, 0))],
        out_specs=[pl.BlockSpec((2, 8, tm), lambda i: (0, 0, i)), pl.BlockSpec((tm, LANES), lambda i: (i, 0))],
        out_shape=[jax.ShapeDtypeStruct((2, 8, n), jnp.int32), jax.ShapeDtypeStruct((n, LANES), F32)],
        compiler_params=_cparams(("parallel",)),
        name="moe_plan",
    )(gw, sel, base)


SC_WINDOW = 128


SC_CORES = 2
SC_SUBCORES = 16


def _sc_workers():
    return SC_CORES, SC_CORES * SC_SUBCORES


def _sc_dispatch(rows, pos):
    n, width = rows.shape
    ncores, workers = _sc_workers()
    per_worker = n // workers
    steps = per_worker // SC_WINDOW
    mesh = plsc.VectorSubcoreMesh(core_axis_name="c", subcore_axis_name="s")

    @functools.partial(
        pl.kernel, mesh=mesh, out_type=jax.ShapeDtypeStruct((2 * n, width), rows.dtype),
        scratch_types=[pltpu.VMEM((2, steps, SC_WINDOW), jnp.int32), pltpu.VMEM((SC_WINDOW, width), rows.dtype)],
        name="moe_dispatch")
    def run(rows_hbm, pos_hbm, out_hbm, idx_v, rows_v):
        wid = lax.axis_index("s") * ncores + lax.axis_index("c")
        pltpu.sync_copy(pos_hbm.at[wid], idx_v)
        for j in range(steps):
            pltpu.sync_copy(rows_hbm.at[pl.ds(wid * per_worker + j * SC_WINDOW, SC_WINDOW)], rows_v)
            pltpu.sync_copy(rows_v, out_hbm.at[idx_v.at[0, j]])
            pltpu.sync_copy(rows_v, out_hbm.at[idx_v.at[1, j]])

    pos_w = pos.reshape(2, workers, steps, SC_WINDOW).transpose(1, 0, 2, 3)
    return run(rows, pos_w)


SC_GATHER_WINDOW = 128


def _sc_gather(table, pos):
    _, width = table.shape
    n = pos.shape[1]
    ncores, workers = _sc_workers()
    per_worker = n // workers
    win = SC_GATHER_WINDOW
    steps = per_worker // win
    mesh = plsc.VectorSubcoreMesh(core_axis_name="c", subcore_axis_name="s")

    @functools.partial(
        pl.kernel, mesh=mesh, out_type=jax.ShapeDtypeStruct((2, n, width), table.dtype),
        scratch_types=[pltpu.VMEM((2, steps, win), jnp.int32), pltpu.VMEM((win, width), table.dtype)],
        name="moe_gather")
    def run(table_hbm, pos_hbm, out_hbm, idx_v, rows_v):
        wid = lax.axis_index("s") * ncores + lax.axis_index("c")
        pltpu.sync_copy(pos_hbm.at[wid], idx_v)
        for s in range(2):
            for j in range(steps):
                pltpu.sync_copy(table_hbm.at[idx_v.at[s, j]], rows_v)
                pltpu.sync_copy(rows_v, out_hbm.at[s, pl.ds(wid * per_worker + j * win, win)])

    pos_w = pos.reshape(2, workers, steps, win).transpose(1, 0, 2, 3)
    return run(table, pos_w)


MOE_TM = 512
MOE_TF = 896
MOE_VMEM_LIMIT = 62 * 1024 * 1024


def _moe_pairs(goff, n_rows):
    tiles = n_rows // MOE_TM
    steps = tiles + N_EXPERTS - 1
    first_row = jnp.arange(tiles, dtype=jnp.int32) * MOE_TM
    ends = goff[1:]
    e_lo = jnp.sum(first_row[:, None] >= ends[None, :], axis=1).astype(jnp.int32)
    e_hi = jnp.sum((first_row + (MOE_TM - 1))[:, None] >= ends[None, :], axis=1).astype(jnp.int32)
    count = e_hi - e_lo + 1
    start = jnp.cumsum(count) - count
    p = jnp.arange(steps, dtype=jnp.int32)
    tile = jnp.sum(start[None, :] <= p[:, None], axis=1).astype(jnp.int32) - 1
    expert = e_lo[tile] + (p - start[tile])
    valid = p < jnp.sum(count)
    return (jnp.where(valid, tile, tiles - 1).astype(jnp.int32),
            jnp.where(valid, expert, N_EXPERTS - 1).astype(jnp.int32), valid.astype(jnp.int32))


def _moe_group_kernel(tile_ref, exp_ref, valid_ref, goff_ref, x_ref, wg_ref, wu_ref, wd_ref, y_ref, act_ref):
    p = pl.program_id(0)
    tm = x_ref.shape[0]
    tile = tile_ref[p]
    expert = exp_ref[p]

    @pl.when((p == 0) | (tile != tile_ref[jnp.maximum(p - 1, 0)]))
    def _():
        y_ref[...] = jnp.zeros(y_ref.shape, y_ref.dtype)

    @pl.when(valid_ref[p] == 1)
    def _():
        x = _unpack_bf16_pairs(x_ref[...])
        for f in range(0, FFN_EXPERT, MOE_TF):
            act_ref[:, f:f + MOE_TF] = (_silu(_dot(x, wg_ref[0, :, f:f + MOE_TF]))
                                        * _dot(x, wu_ref[0, :, f:f + MOE_TF])).astype(BF16)
        out = _pack_bf16_pairs(_dot(act_ref[...], wd_ref[0]))
        rows = tile * tm + lax.broadcasted_iota(jnp.int32, (tm, 1), 0)
        mine = (rows >= goff_ref[expert]) & (rows < goff_ref[expert + 1])
        y_ref[...] = jnp.where(mine, out, y_ref[...])


def _moe_group(xs, goff, w_gu, wd):
    n_rows, half = xs.shape
    d = 2 * half
    tm = MOE_TM
    tile, expert, valid = _moe_pairs(goff, n_rows)

    def resident(shape, col_block=0):
        return pl.BlockSpec(shape, lambda p, t, e, v, g: (e[p], 0, col_block))

    grid_spec = pltpu.PrefetchScalarGridSpec(
        num_scalar_prefetch=4,
        grid=(tile.shape[0],),
        in_specs=[pl.BlockSpec((tm, half), lambda p, t, e, v, g: (t[p], 0)),
                  resident((1, d, FFN_EXPERT), 0), resident((1, d, FFN_EXPERT), 1), resident((1, FFN_EXPERT, d))],
        out_specs=pl.BlockSpec((tm, half), lambda p, t, e, v, g: (t[p], 0)),
        scratch_shapes=[pltpu.VMEM((tm, FFN_EXPERT), BF16)],
    )
    return pl.pallas_call(
        _moe_group_kernel,
        grid_spec=grid_spec,
        out_shape=jax.ShapeDtypeStruct((n_rows, half), jnp.int32),
        compiler_params=_cparams(("arbitrary",), MOE_VMEM_LIMIT),
        name="moe_group",
    )(tile, expert, valid, goff, xs, w_gu, w_gu, wd)


def _moe_combine_kernel(x_ref, yg_ref, wts_ref, g_ref, fn_ref, o_ref):
    w = wts_ref[0]
    mix = (w[:, 0:1] * _unpack_bf16_pairs(yg_ref[0, 0]).astype(F32)
           + w[:, 1:2] * _unpack_bf16_pairs(yg_ref[1, 0]).astype(F32))
    x4 = x_ref[0] + g_ref[0] * mix
    y = x4 * lax.rsqrt(jnp.mean(x4 * x4, axis=-1, keepdims=True) + NORM_EPS)
    o_ref[0] = y * fn_ref[...]


def _moe_combine(x, yg, wts, mod, final_norm):
    bsz, seq, d = x.shape
    tm = 512
    _, _, gate = _mod_specs(2)
    tile = lambda width: pl.BlockSpec((1, tm, width), lambda b, i: (b, i, 0))
    return pl.pallas_call(
        _moe_combine_kernel,
        grid=(bsz, seq // tm),
        in_specs=[tile(d), pl.BlockSpec((2, 1, tm, d // 2), lambda b, i: (0, b, i, 0)), tile(LANES), gate,
                  _const_spec((1, d))],
        out_specs=tile(d),
        out_shape=jax.ShapeDtypeStruct((bsz, seq, d), F32),
        compiler_params=_cparams(("parallel", "parallel")),
        name="moe_combine",
    )(x, yg, wts, mod, final_norm[None, :])


def _pad_cols(a, width):
    return jnp.pad(a, ((0, 0), (0, width - a.shape[1])))


def _split(a, sizes):
    return jnp.split(a, [int(s) for s in np.cumsum(sizes)[:-1]], axis=-1)


def _prep_hyb_w_in(w):
    q_a, k_a, v_a, lr_a, r_a, q_b, kv_b, g_b = _split(w, HYB_SPLITS)
    return jnp.concatenate([q_a, k_a, v_a, r_a, _pad_cols(lr_a, LANES), _pad_cols(g_b, LANES), q_b, kv_b],
                           axis=1).astype(BF16)


def _prep_cmp(pe, w1, w2):
    eye = jnp.eye(NSA_KV_GROUPS, dtype=F32)
    half = CMP_BLOCK // 2
    w1r = w1.reshape(2, 2, half, NSA_HEAD_DIM, CMP_HIDDEN)
    w1x = jnp.einsum("jstdc,gh->jstgdhc", w1r, eye).reshape(2, 2, half * LANES, NSA_KV_GROUPS * CMP_HIDDEN)
    w2x = jnp.einsum("jcd,gh->jgchd", w2, eye).reshape(2, NSA_KV_GROUPS * CMP_HIDDEN, LANES)
    per = pe.reshape(2, 2, half, 1, NSA_HEAD_DIM)
    pe_rows = jnp.broadcast_to(per, (2, 2, half, NSA_KV_GROUPS, NSA_HEAD_DIM)).reshape(2, 2, 1, half * LANES)
    return pe_rows, w1x.astype(BF16), w2x.astype(BF16)


def _overlap_matrix(seq):
    n_rows = seq // CMP_STRIDE
    n = np.arange(n_rows)[:, None] * CMP_STRIDE
    s = np.arange(LANES)[None, :] * SEL_BLOCK
    ov = (n < s + SEL_BLOCK) & (n + CMP_BLOCK - 1 >= s) & (np.arange(LANES)[None, :] < seq // SEL_BLOCK)
    return jnp.asarray(ov, BF16)


def _head_expand():
    ex = np.zeros((SSM_GROUPS, LANES, SSD_GW), np.float32)
    for g in range(SSM_GROUPS):
        for j in range(SSD_HPG):
            for part in range(SSD_SPLIT_PARTS):
                ex[g, part * SSM_HEADS + g * SSD_HPG + j, j * SSM_HEAD_DIM:(j + 1) * SSM_HEAD_DIM] = 1.0
    return jnp.asarray(ex, BF16)


def kernel(x, c, hyb_norm, hyb_mod_w, hyb_mod_b, hyb_w_in, gla_gk_up, gla_gk_bias, gla_out_norm, nsa_cmp_pe, nsa_cmp_w1, nsa_cmp_w2, hyb_w_out, dense_norm, dense_mod_w, dense_mod_b, dense_w_gu, dense_w_down, ssm_norm, ssm_mod_w, ssm_mod_b, ssm_w_in, ssm_conv_w, ssm_conv_b, ssm_dt_bias, ssm_a_log, ssm_d, ssm_gate_norm, ssm_w_out, moe_norm, moe_mod_w, moe_mod_b, moe_router, moe_w_gu, moe_w_down, final_norm):
    bsz, seq, d = x.shape
    mods = _adaln(c, (hyb_mod_w[0], dense_mod_w[0], ssm_mod_w[0], moe_mod_w[0]),
                  (hyb_mod_b[0], dense_mod_b[0], ssm_mod_b[0], moe_mod_b[0]))
    mods = mods.reshape(4, bsz, 1, 3 * d)

    gkup_pad = jnp.pad(gla_gk_up[0], ((0, LANES - GLA_LOWRANK), (0, 0))).astype(BF16)
    (gq, gk, la, gv, gr, nq, kcmp, vcmp, kslc, vslc, kwin, vwin, ng) = _hyb_in(
        x, hyb_norm[0], mods[0], _prep_hyb_w_in(hyb_w_in[0]), gkup_pad, gla_gk_bias[0])
    o_gla = _gla(gq, gk, la, gv, gr, gla_out_norm[0])
    pe_rows, w1x, w2x = _prep_cmp(nsa_cmp_pe[0], nsa_cmp_w1[0], nsa_cmp_w2[0])
    kc, vc = _nsa_compress(kcmp, vcmp, pe_rows, w1x, w2x)
    o_nsa = _nsa_attn(nq, ng, kc, vc, kslc, vslc, kwin, vwin, _overlap_matrix(seq))
    x = _hyb_out_ffn(x, o_gla, o_nsa, mods[0], hyb_w_out[0].astype(BF16), dense_norm[0], mods[1],
                     dense_w_gu[0].astype(BF16), dense_w_down[0].astype(BF16))

    wz, wxbc, wdt = _split(ssm_w_in[0], SSM_SPLITS)
    w_ssm = jnp.concatenate([wz, wxbc, _pad_cols(wdt, LANES)], axis=1).astype(BF16)
    pad_heads = lambda a: jnp.pad(a, (0, LANES - SSM_HEADS))[None, :]
    z, xbc, dt = _ssm_in(x, ssm_norm[0], mods[2], w_ssm, ssm_conv_w[0], ssm_conv_b[0], pad_heads(ssm_dt_bias[0]))
    y = _ssd(xbc, z, dt, pad_heads(ssm_a_log[0]), jnp.repeat(ssm_d[0], SSM_HEAD_DIM)[None, :],
             ssm_gate_norm[0], _head_expand())
    x3, h4, gw, sel, cnt = _ssm_out_route(x, y, mods[2], ssm_w_out[0].astype(BF16), moe_norm[0], mods[3],
                                          _pad_cols(moe_router[0], LANES))
    n = bsz * seq
    cnt = cnt[:, 0, :]
    totals = jnp.sum(cnt, axis=0)
    goff_f = jnp.cumsum(totals) - totals
    base = (jnp.cumsum(cnt, axis=0) - cnt + goff_f[None, :])[:, None, :]
    goff = jnp.concatenate([goff_f[:N_EXPERTS], jnp.full((1,), 2.0 * n, F32)]).astype(jnp.int32)
    pos, wts = _moe_plan(gw.reshape(n, LANES), sel.reshape(n, LANES), base)
    pos = pos[:, 0, :]
    xs = _sc_dispatch(h4.reshape(n, d // 2), pos)
    ys = _moe_group(xs, goff, moe_w_gu[0].astype(BF16), moe_w_down[0].astype(BF16))
    yg = _sc_gather(ys, pos)
    return _moe_combine(x3, yg.reshape(2, bsz, seq, d // 2), wts.reshape(bsz, seq, LANES), mods[3], final_norm)
```

```python
import functools

import jax
import jax.numpy as jnp
import numpy as np
from jax import lax
from jax.experimental import pallas as pl
from jax.experimental.pallas import tpu as pltpu
from jax.experimental.pallas import tpu_sc as plsc

F32 = jnp.float32
BF16 = jnp.bfloat16

D_MODEL = 1024
NORM_EPS = 1e-6
GLA_HEADS = 4
GLA_DV = D_MODEL // 8
GLA_DK = GLA_DV // 2
GLA_LOWRANK = 16
GLA_TAU = 16.0
GLA_CHUNK = 64
NSA_HEADS = 8
NSA_KV_GROUPS = 2
NSA_HEAD_DIM = D_MODEL // 16
CMP_BLOCK = 32
CMP_STRIDE = 16
CMP_HIDDEN = 4 * NSA_HEAD_DIM
SEL_BLOCK = 64
N_SELECT = 8
WINDOW = 512
D_INNER = 2 * D_MODEL
SSM_HEAD_DIM = 64
SSM_HEADS = D_INNER // SSM_HEAD_DIM
SSM_GROUPS = 4
SSM_STATE = 128
CONV_K = 4
SSM_CHUNK = 64
CONV_CH = D_INNER + 2 * SSM_GROUPS * SSM_STATE
FFN_DENSE = ((8 * D_MODEL // 3 + 127) // 128) * 128
N_EXPERTS = 8
FFN_EXPERT = 7 * D_MODEL // 2
HYB_SPLITS = (GLA_HEADS * GLA_DK, GLA_HEADS * GLA_DK, GLA_HEADS * GLA_DV, GLA_LOWRANK, GLA_HEADS * GLA_DV,
              NSA_HEADS * NSA_HEAD_DIM, 6 * NSA_KV_GROUPS * NSA_HEAD_DIM, 3 * NSA_HEADS)
SSM_SPLITS = (D_INNER, CONV_CH, SSM_HEADS)

LANES = 128
VMEM_LIMIT = 56 * 1024 * 1024


def _cparams(sem, vmem_limit=VMEM_LIMIT):
    return pltpu.CompilerParams(dimension_semantics=sem, vmem_limit_bytes=vmem_limit)


def _dot(a, b):
    return jnp.dot(a, b, preferred_element_type=F32)


def _dot_nt(a, b):
    return lax.dot_general(a, b, (((1,), (1,)), ((), ())), preferred_element_type=F32)


def _dot_tn(a, b):
    return lax.dot_general(a, b, (((0,), (0,)), ((), ())), preferred_element_type=F32)


def _split_bf16(a, parts):
    out = []
    r = a
    for _ in range(parts):
        p = r.astype(BF16)
        out.append(p)
        r = r - p.astype(F32)
    return out


def _dot_exact_lhs(m, a, parts=3):
    width = a.shape[1]
    wide = _dot(m, jnp.concatenate(_split_bf16(a, parts), axis=1))
    acc = wide[:, :width]
    for i in range(1, parts):
        acc = acc + wide[:, i * width:(i + 1) * width]
    return acc


def _dot_exact_rhs(a, m, parts=3):
    acc = None
    for p in _split_bf16(a, parts):
        t = _dot(p, m)
        acc = t if acc is None else acc + t
    return acc


def _sigmoid(x):
    return 1.0 / (1.0 + jnp.exp(-x))


def _silu(x):
    return x * _sigmoid(x)


def _softplus(x):
    return jnp.maximum(x, 0.0) + jnp.log(1.0 + jnp.exp(-jnp.abs(x)))


def _gelu_tanh(x):
    return x * (0.5 * (1.0 + jnp.tanh(0.7978845608028654 * (x + 0.044715 * (x * x * x)))))


def _modulated_norm(x, nw, shift, scale):
    ms = jnp.mean(x * x, axis=-1, keepdims=True)
    y = x * lax.rsqrt(ms + NORM_EPS)
    return (y * nw) * (1.0 + scale) + shift


def _const_spec(shape):
    nd = len(shape)
    return pl.BlockSpec(shape, lambda *_: (0,) * nd, pipeline_mode=pl.Buffered(1))


def _adaln_kernel(c_ref, w0, w1, w2, w3, b0, b1, b2, b3, o_ref):
    sc = _silu(c_ref[...]).astype(BF16)
    for i, (w, b) in enumerate(((w0, b0), (w1, b1), (w2, b2), (w3, b3))):
        o_ref[i] = _dot(sc, w[...].astype(BF16)) + b[...]


def _adaln(c, ws, bs):
    bsz, d = c.shape
    n = ws[0].shape[1]
    tn = 512
    w_spec = pl.BlockSpec((d, tn), lambda j: (0, j))
    b_spec = pl.BlockSpec((1, tn), lambda j: (0, j))
    return pl.pallas_call(
        _adaln_kernel,
        grid=(n // tn,),
        in_specs=[pl.BlockSpec((bsz, d), lambda j: (0, 0))] + [w_spec] * 4 + [b_spec] * 4,
        out_specs=pl.BlockSpec((4, bsz, tn), lambda j: (0, 0, j)),
        out_shape=jax.ShapeDtypeStruct((4, bsz, n), F32),
        compiler_params=_cparams(("arbitrary",)),
        name="adaln",
    )(c, *ws, *[b[None, :] for b in bs])


def _mod_specs(tile_axes):
    def spec(part):
        if tile_axes == 2:
            return pl.BlockSpec((1, 1, D_MODEL), lambda b, i: (b, 0, part))
        return pl.BlockSpec((1, 1, D_MODEL), lambda b, i, j, k: (b, 0, part))
    return spec(0), spec(1), spec(2)


HYB_SEG = {
    "gq": (0, 256), "gk": (256, 256), "gv": (512, 512), "gr": (1024, 512), "lr_ng": (1536, 256),
    "nq": (1792, 512), "kv": (2304, 768),
}


def _hyb_in_kernel(x_ref, nw_ref, shift_ref, scale_ref, w_ref, gkup_ref, gkb_ref,
                   gq_ref, gk_ref, la_ref, gv_ref, gr_ref, nq_ref,
                   kcmp_ref, vcmp_ref, kslc_ref, vslc_ref, kwin_ref, vwin_ref, ng_ref):
    h = _modulated_norm(x_ref[0], nw_ref[...], shift_ref[0], scale_ref[0]).astype(BF16)

    def proj(name):
        off, width = HYB_SEG[name]
        return _dot(h, w_ref[:, off:off + width])

    def proj_pair(name, pair=0):
        off = HYB_SEG[name][0] + pair * 2 * LANES
        y = _dot(h, w_ref[:, off:off + 2 * LANES])
        return y[:, :LANES], y[:, LANES:]

    gq_ref[0] = proj("gq")
    gk_ref[0] = proj("gk")
    gv_ref[0] = proj("gv").astype(BF16)
    gr_ref[0] = proj("gr")
    lr, ng = proj_pair("lr_ng")
    z = _dot(lr.astype(BF16), gkup_ref[...]) + gkb_ref[...]
    la_ref[0] = -_softplus(-z) * (1.0 / GLA_TAU)
    ng_ref[0] = ng
    nq_ref[0] = proj("nq").astype(BF16)
    kcmp_ref[0], vcmp_ref[0] = proj_pair("kv", 0)
    kslc, vslc = proj_pair("kv", 1)
    kslc_ref[0] = kslc.T.astype(BF16)
    vslc_ref[0] = vslc.astype(BF16)
    kwin, vwin = proj_pair("kv", 2)
    kwin_ref[0] = kwin.T.astype(BF16)
    vwin_ref[0] = vwin.astype(BF16)


def _hyb_in(x, norm_w, mod, w_perm, gkup_pad, gk_bias):
    bsz, seq, d = x.shape
    tm = 512
    shift_s, scale_s, _ = _mod_specs(2)

    def out(width, dtype):
        return (pl.BlockSpec((1, tm, width), lambda b, i: (b, i, 0)),
                jax.ShapeDtypeStruct((bsz, seq, width), dtype))

    def out_t(dtype):
        return (pl.BlockSpec((1, LANES, tm), lambda b, i: (b, 0, i)),
                jax.ShapeDtypeStruct((bsz, LANES, seq), dtype))

    outs = [out(256, F32), out(256, F32), out(256, F32), out(512, BF16), out(512, F32), out(512, BF16),
            out(128, F32), out(128, F32), out_t(BF16), out(128, BF16), out_t(BF16), out(128, BF16),
            out(128, F32)]
    return pl.pallas_call(
        _hyb_in_kernel,
        grid=(bsz, seq // tm),
        in_specs=[pl.BlockSpec((1, tm, d), lambda b, i: (b, i, 0)),
                  _const_spec((1, d)), shift_s, scale_s,
                  _const_spec(w_perm.shape), _const_spec(gkup_pad.shape), _const_spec((1, 256))],
        out_specs=[o[0] for o in outs],
        out_shape=[o[1] for o in outs],
        compiler_params=_cparams(("parallel", "parallel")),
        name="hyb_in",
    )(x, norm_w[None, :], mod, mod, w_perm, gkup_pad, gk_bias[None, :])


GLA_ROWS = 512


def _gla_kernel(q_ref, k_ref, la_ref, v_ref, r_ref, nw_ref, o_ref, *, seq):
    rb = GLA_ROWS
    ncb = rb // GLA_CHUNK
    row = lax.broadcasted_iota(jnp.int32, (rb, rb), 0)
    col = lax.broadcasted_iota(jnp.int32, (rb, rb), 1)
    chunk_causal = ((row // GLA_CHUNK) == (col // GLA_CHUNK)) & (col <= row)
    tril_bd = jnp.where(chunk_causal, 1.0, 0.0).astype(BF16)
    lane = lax.broadcasted_iota(jnp.int32, (rb, LANES), 1)
    head_mask = (lane < GLA_DK, lane >= GLA_DK)
    nw = nw_ref[...]

    def body(i, st):
        r0 = pl.multiple_of(i * rb, rb)
        rows = pl.ds(r0, rb)
        q = q_ref[0, rows, :]
        k = k_ref[0, rows, :]
        la = la_ref[0, rows, :]
        b = _dot_exact_lhs(tril_bd, la)
        b3 = b.reshape(ncb, GLA_CHUNK, LANES)
        bend3 = b3[:, GLA_CHUNK - 1:GLA_CHUNK, :]
        bend = jnp.broadcast_to(bend3, (ncb, GLA_CHUNK, LANES)).reshape(rb, LANES)
        q_dec = (q * GLA_DK ** -0.5) * jnp.exp(b)
        k_dec = (k * jnp.exp(-b)).astype(BF16)
        k_end = k * jnp.exp(bend - b)
        decay = jnp.exp(bend3)

        qm, vh, o_intra, upd = [], [], [], []
        for h in range(2):
            qm_h = jnp.where(head_mask[h], q_dec, 0.0).astype(BF16)
            att = _dot_nt(qm_h, k_dec)
            att = jnp.where(chunk_causal, att, 0.0).astype(BF16)
            v_h = v_ref[0, rows, h * GLA_DV:(h + 1) * GLA_DV]
            o_intra.append(_dot(att, v_h))
            km_h = jnp.where(head_mask[h], k_end, 0.0).astype(BF16)
            upd.append([_dot_tn(v_h[n * GLA_CHUNK:(n + 1) * GLA_CHUNK], km_h[n * GLA_CHUNK:(n + 1) * GLA_CHUNK])
                        for n in range(ncb)])
            qm.append(qm_h)
            vh.append(v_h)

        prev = []
        for n in range(ncb):
            prev.append(st.astype(BF16))
            st = st * decay[n] + (upd[0][n] + upd[1][n])

        for h in range(2):
            o_inter = jnp.concatenate(
                [_dot_nt(qm[h][n * GLA_CHUNK:(n + 1) * GLA_CHUNK], prev[n]) for n in range(ncb)], axis=0)
            o = o_intra[h] + o_inter
            y = o * lax.rsqrt(jnp.mean(o * o, axis=-1, keepdims=True) + NORM_EPS) * nw
            r = r_ref[0, rows, h * GLA_DV:(h + 1) * GLA_DV]
            o_ref[0, rows, h * GLA_DV:(h + 1) * GLA_DV] = (y * _silu(r)).astype(o_ref.dtype)
        return st

    lax.fori_loop(0, seq // rb, body, jnp.zeros((GLA_DV, 2 * GLA_DK), F32))


def _gla(gq, gk, la, gv, gr, out_norm):
    bsz, seq, _ = gq.shape
    qk_spec = pl.BlockSpec((1, seq, 2 * GLA_DK), lambda b, p: (b, 0, p))
    v_spec = pl.BlockSpec((1, seq, 2 * GLA_DV), lambda b, p: (b, 0, p))
    return pl.pallas_call(
        functools.partial(_gla_kernel, seq=seq),
        grid=(bsz, GLA_HEADS // 2),
        in_specs=[qk_spec, qk_spec, qk_spec, v_spec, v_spec, _const_spec((1, GLA_DV))],
        out_specs=v_spec,
        out_shape=jax.ShapeDtypeStruct((bsz, seq, GLA_HEADS * GLA_DV), BF16),
        compiler_params=_cparams(("parallel", "parallel")),
        name="gla",
    )(gq, gk, la, gv, gr, out_norm[None, :])


def _nsa_compress_kernel(k_ref, v_ref, pe_ref, w1_ref, w2_ref, kc_ref, vc_ref, *, n_cmp):
    rows = k_ref.shape[1]
    rid = lax.broadcasted_iota(jnp.int32, (rows, LANES), 0)
    for j, (src, dst) in enumerate(((k_ref, kc_ref), (v_ref, vc_ref))):
        x = src[0]
        lo = _dot((x + pe_ref[j, 0]).astype(BF16), w1_ref[j, 0])
        hi = _dot((x + pe_ref[j, 1]).astype(BF16), w1_ref[j, 1])
        hpre = lo + pltpu.roll(hi, rows - 1, 0)
        out = jnp.where(rid < n_cmp, _dot(_gelu_tanh(hpre).astype(BF16), w2_ref[j]), 0.0)
        dst[0] = (out.T if j == 0 else out).astype(dst.dtype)


def _nsa_compress(kcmp, vcmp, pe_rows, w1x, w2x):
    bsz, seq, _ = kcmp.shape
    rows = seq // CMP_STRIDE
    width = CMP_STRIDE * LANES
    n_cmp = (seq - CMP_BLOCK) // CMP_STRIDE + 1
    x_spec = pl.BlockSpec((1, rows, width), lambda b: (b, 0, 0))
    return pl.pallas_call(
        functools.partial(_nsa_compress_kernel, n_cmp=n_cmp),
        grid=(bsz,),
        in_specs=[x_spec, x_spec, _const_spec(pe_rows.shape), _const_spec(w1x.shape), _const_spec(w2x.shape)],
        out_specs=[pl.BlockSpec((1, LANES, rows), lambda b: (b, 0, 0)),
                   pl.BlockSpec((1, rows, LANES), lambda b: (b, 0, 0))],
        out_shape=[jax.ShapeDtypeStruct((bsz, LANES, rows), BF16), jax.ShapeDtypeStruct((bsz, rows, LANES), BF16)],
        compiler_params=_cparams(("parallel",)),
        name="nsa_compress",
    )(kcmp.reshape(bsz, rows, width), vcmp.reshape(bsz, rows, width), pe_rows, w1x, w2x)


NSA_QT = 256
NSA_KT = 512
NSA_ROWS = NSA_HEADS * NSA_QT


NSA_MASK_PENALTY = 1e30
NSA_LANE_POS_HI, NSA_LANE_POS_LO, NSA_LANE_CMP_IDX, NSA_LANE_CMP_ONE = 64, 65, 66, 67


def _nsa_attn_kernel(q_ref, g_ref, kc_ref, vc_ref, ks_ref, vs_ref, kw_ref, vw_ref, ov_ref, kf_ref, cf_ref, o_ref,
                     m_ref, acc_ref, *, seq):
    qt, kt = NSA_QT, NSA_KT
    heads_per_group = NSA_HEADS // NSA_KV_GROUPS
    q0 = pl.program_id(1) * qt
    t = q0 + lax.broadcasted_iota(jnp.int32, (qt, 1), 0)
    lane = lax.broadcasted_iota(jnp.int32, (qt, LANES), 1)
    group_lanes = (lane < NSA_HEAD_DIM, lane >= NSA_HEAD_DIM)
    slopes = [2.0 ** (-(h + 1)) for h in range(NSA_HEADS)]

    qa = q_ref[0].astype(F32) * NSA_HEAD_DIM ** -0.5
    q_rows = []
    for h in range(NSA_HEADS):
        g = h // heads_per_group
        blk = qa[:, (h // 2) * LANES:(h // 2 + 1) * LANES]
        if h % 2 != g:
            blk = pltpu.roll(blk, NSA_HEAD_DIM, 1)
        q_rows.append(jnp.where(group_lanes[g], blk, 0.0).astype(BF16))
    q_ext = jnp.concatenate(q_rows, axis=0)

    def alibi_lanes(h):
        s = slopes[h]
        return jnp.where(lane == NSA_LANE_POS_HI, SEL_BLOCK * s,
                         jnp.where(lane == NSA_LANE_POS_LO, s,
                                   jnp.where(lane == NSA_LANE_CMP_IDX, CMP_STRIDE * s,
                                             jnp.where(lane == NSA_LANE_CMP_ONE, (CMP_BLOCK - 1) / 2.0 * s, 0.0))))

    q_pos = [alibi_lanes(h) for h in range(NSA_HEADS)]
    q_full = jnp.concatenate([q_ext, jnp.concatenate(q_pos, axis=0).astype(BF16)], axis=1)

    def softmax_parts(s, mask):
        s = jnp.where(mask, s, -jnp.inf)
        m = jnp.max(s, axis=-1, keepdims=True)
        m = jnp.where(m == -jnp.inf, 0.0, m)
        p = jnp.exp(s - m)
        return p, jnp.sum(p, axis=-1, keepdims=True)

    n_rows = kc_ref.shape[2]
    cidx = lax.broadcasted_iota(jnp.int32, (1, n_rows), 1)
    cmp_mask = cidx * CMP_STRIDE + (CMP_BLOCK - 1) <= t
    s_all = _dot(q_full, jnp.concatenate([kc_ref[0], cf_ref[...]], axis=0))
    p_list, psum = [], [None] * NSA_KV_GROUPS
    for h in range(NSA_HEADS):
        g = h // heads_per_group
        p, l = softmax_parts(s_all[h * qt:(h + 1) * qt], cmp_mask)
        p = p / jnp.maximum(l, 1e-20)
        p_list.append(p.astype(BF16))
        psum[g] = p if psum[g] is None else psum[g] + p
    o_cmp = _dot(jnp.concatenate(p_list, axis=0), vc_ref[0])

    n_blk = LANES // 2
    tq = q0 + lax.broadcasted_iota(jnp.int32, (n_blk, qt), 1)
    bidx = lax.broadcasted_iota(jnp.int32, (n_blk, qt), 0)
    bidx_f = bidx.astype(F32)
    blk_t = tq // SEL_BLOCK
    forced = (bidx == 0) | (bidx == blk_t) | (bidx == blk_t - 1)
    future = bidx * SEL_BLOCK > tq
    q_sel_rows = []
    for g in range(NSA_KV_GROUPS):
        imp = _dot_exact_rhs(psum[g], ov_ref[...]).T[:n_blk]
        v = jnp.where(future, -1.0, jnp.where(forced, 3e38, imp))
        chosen = jnp.zeros((n_blk, qt), F32)
        for _ in range(min(N_SELECT, seq // SEL_BLOCK)):
            m = jnp.max(v, axis=0, keepdims=True)
            idx = jnp.min(jnp.where(v == m, bidx_f, float(LANES)), axis=0, keepdims=True)
            pick = (bidx_f == idx) & (m >= 0.0)
            chosen = jnp.where(pick, 1.0, chosen)
            v = jnp.where(pick, -1.0, v)
        penalty = jnp.concatenate([(chosen - 1.0) * NSA_MASK_PENALTY, jnp.zeros((n_blk, qt), F32)], axis=0).T
        for r in range(heads_per_group):
            q_sel_rows.append((penalty + q_pos[g * heads_per_group + r]).astype(BF16))
    q_full_sel = jnp.concatenate([q_ext, jnp.concatenate(q_sel_rows, axis=0)], axis=1)

    vlane = lax.broadcasted_iota(jnp.int32, (1, LANES), 1)
    own_lanes = (vlane < NSA_HEAD_DIM, vlane >= NSA_HEAD_DIM)

    def weighted_values(p_rows, v):
        v = v.astype(F32)
        outs = []
        for g in range(NSA_KV_GROUPS):
            pg = jnp.concatenate(p_rows[g * heads_per_group:(g + 1) * heads_per_group], axis=0)
            outs.append(_dot(pg, jnp.where(own_lanes[g], v, 1.0).astype(BF16)))
        return jnp.concatenate(outs, axis=0)

    def normalised(acc):
        return acc / jnp.maximum(pltpu.roll(acc, NSA_HEAD_DIM, 1), 1e-20)

    m_ref[...] = jnp.full(m_ref.shape, -jnp.inf, F32)
    acc_ref[...] = jnp.zeros(acc_ref.shape, F32)
    def sel_tile(k0, width, causal):
        ks = jnp.concatenate([ks_ref[0, :, pl.ds(k0, width)], kf_ref[:, pl.ds(k0, width)]], axis=0)
        s_t = _dot(q_full_sel, ks)
        visible = (k0 + lax.broadcasted_iota(jnp.int32, (1, width), 1)) <= t
        m_all = m_ref[...]
        p_rows, m_rows, alpha_rows = [], [], []
        for h in range(NSA_HEADS):
            rows = slice(h * qt, (h + 1) * qt)
            s = s_t[rows]
            if causal:
                s = jnp.where(visible, s, -jnp.inf)
            m_old = m_all[rows]
            m_new = jnp.maximum(m_old, jnp.max(s, axis=-1, keepdims=True))
            m_safe = jnp.where(m_new == -jnp.inf, 0.0, m_new)
            p_rows.append(jnp.exp(s - jnp.concatenate([m_safe] * (width // LANES), axis=1)).astype(BF16))
            alpha_rows.append(jnp.exp(m_old - m_safe))
            m_rows.append(m_new)
        m_ref[...] = jnp.concatenate(m_rows, axis=0)
        acc_ref[...] = (jnp.concatenate(alpha_rows, axis=0) * acc_ref[...]
                        + weighted_values(p_rows, vs_ref[0, pl.ds(k0, width), :]))

    def sel_past_tile(j, carry):
        sel_tile(pl.multiple_of(j * kt, kt), kt, False)
        return carry

    last = q0 // kt
    lax.fori_loop(0, last, sel_past_tile, 0)
    sel_tile(pl.multiple_of(last * kt, kt), kt, True)
    o_slc = normalised(acc_ref[...])

    wk = WINDOW + qt
    w0 = pl.multiple_of(jnp.maximum(q0 - WINDOW, 0), qt)
    kw = jnp.concatenate([kw_ref[0, :, pl.ds(w0, wk)], kf_ref[:, pl.ds(w0, wk)]], axis=0)
    wdist = t - (w0 + lax.broadcasted_iota(jnp.int32, (1, wk), 1))
    wmask = (wdist >= 0) & (wdist < WINDOW)
    s_all = _dot(q_full, kw)
    p_rows = []
    for h in range(NSA_HEADS):
        s = jnp.where(wmask, s_all[h * qt:(h + 1) * qt], -jnp.inf)
        m = jnp.max(s, axis=-1, keepdims=True)
        p_rows.append(jnp.exp(s - jnp.where(m == -jnp.inf, 0.0, m)).astype(BF16))
    o_win = normalised(weighted_values(p_rows, vw_ref[0, pl.ds(w0, wk), :]))

    gates = _sigmoid(g_ref[0])
    o_heads = []
    for h in range(NSA_HEADS):
        rows = slice(h * qt, (h + 1) * qt)
        o_heads.append(gates[:, 3 * h:3 * h + 1] * o_cmp[rows] + gates[:, 3 * h + 1:3 * h + 2] * o_slc[rows]
                       + gates[:, 3 * h + 2:3 * h + 3] * o_win[rows])
    for c in range(NSA_HEADS // 2):
        g = (2 * c) // heads_per_group
        a, b = o_heads[2 * c], o_heads[2 * c + 1]
        if g == 0:
            blk = jnp.where(group_lanes[0], a, pltpu.roll(b, NSA_HEAD_DIM, 1))
        else:
            blk = jnp.where(group_lanes[0], pltpu.roll(a, NSA_HEAD_DIM, 1), b)
        o_ref[0, :, c * LANES:(c + 1) * LANES] = blk.astype(o_ref.dtype)


def _nsa_position_features(seq):
    assert seq // SEL_BLOCK <= LANES // 2
    pos = np.arange(seq)
    kf = np.zeros((seq, LANES), np.float32)
    kf[pos, pos // SEL_BLOCK] = 1.0
    kf[:, NSA_LANE_POS_HI] = pos // SEL_BLOCK
    kf[:, NSA_LANE_POS_LO] = pos % SEL_BLOCK
    n_rows = seq // CMP_STRIDE
    cf = np.zeros((n_rows, LANES), np.float32)
    cf[:, NSA_LANE_CMP_IDX] = np.arange(n_rows)
    cf[:, NSA_LANE_CMP_ONE] = 1.0
    return jnp.asarray(kf.T, BF16), jnp.asarray(cf.T, BF16)


def _nsa_attn(nq, ng, kc, vc, kslc, vslc, kwin, vwin, overlap):
    bsz, seq, _ = nq.shape
    n_rows = vc.shape[1]
    kfeat, cfeat = _nsa_position_features(seq)
    full = lambda rows: pl.BlockSpec((1, rows, LANES), lambda b, i: (b, 0, 0))
    full_t = lambda cols: pl.BlockSpec((1, LANES, cols), lambda b, i: (b, 0, 0))
    return pl.pallas_call(
        functools.partial(_nsa_attn_kernel, seq=seq),
        grid=(bsz, seq // NSA_QT),
        in_specs=[pl.BlockSpec((1, NSA_QT, NSA_HEADS * NSA_HEAD_DIM), lambda b, i: (b, i, 0)),
                  pl.BlockSpec((1, NSA_QT, LANES), lambda b, i: (b, i, 0)),
                  full_t(n_rows), full(n_rows), full_t(seq), full(seq), full_t(seq), full(seq),
                  _const_spec(overlap.shape), _const_spec(kfeat.shape), _const_spec(cfeat.shape)],
        out_specs=pl.BlockSpec((1, NSA_QT, NSA_HEADS * NSA_HEAD_DIM), lambda b, i: (b, i, 0)),
        out_shape=jax.ShapeDtypeStruct((bsz, seq, NSA_HEADS * NSA_HEAD_DIM), BF16),
        scratch_shapes=[pltpu.VMEM((NSA_ROWS, LANES), F32), pltpu.VMEM((NSA_ROWS, LANES), F32)],
        compiler_params=_cparams(("parallel", "arbitrary")),
        name="nsa_attn",
    )(nq, ng, kc, vc, kslc, vslc, kwin, vwin, overlap, kfeat, cfeat)


FFN_CHUNK = 1408


def _hyb_out_ffn_kernel(x_ref, oa_ref, ob_ref, g1_ref, wo_ref,
                        nw_ref, shift_ref, scale_ref, g2_ref, wgu_ref, wd_ref, o_ref, act_ref):
    n_a = oa_ref.shape[2]
    mix = _dot(oa_ref[0], wo_ref[:n_a, :]) + _dot(ob_ref[0], wo_ref[n_a:, :])
    x1 = x_ref[0] + g1_ref[0] * mix
    h = _modulated_norm(x1, nw_ref[...], shift_ref[0], scale_ref[0]).astype(BF16)
    for f in range(0, FFN_DENSE, FFN_CHUNK):
        gate = _dot(h, wgu_ref[:, f:f + FFN_CHUNK])
        up = _dot(h, wgu_ref[:, FFN_DENSE + f:FFN_DENSE + f + FFN_CHUNK])
        act_ref[:, f:f + FFN_CHUNK] = (_silu(gate) * up).astype(BF16)
    o_ref[0] = x1 + g2_ref[0] * _dot(act_ref[...], wd_ref[...])


def _hyb_out_ffn(x, o_gla, o_nsa, mod1, w_out, norm_w, mod2, w_gu, wd):
    bsz, seq, d = x.shape
    tm = 512
    _, _, gate1 = _mod_specs(2)
    shift2, scale2, gate2 = _mod_specs(2)
    tile = lambda width: pl.BlockSpec((1, tm, width), lambda b, i: (b, i, 0))
    return pl.pallas_call(
        _hyb_out_ffn_kernel,
        grid=(bsz, seq // tm),
        in_specs=[tile(d), tile(o_gla.shape[-1]), tile(o_nsa.shape[-1]), gate1,
                  _const_spec(w_out.shape), _const_spec((1, d)), shift2, scale2, gate2,
                  _const_spec(w_gu.shape), _const_spec(wd.shape)],
        out_specs=tile(d),
        out_shape=jax.ShapeDtypeStruct((bsz, seq, d), F32),
        scratch_shapes=[pltpu.VMEM((tm, FFN_DENSE), BF16)],
        compiler_params=_cparams(("parallel", "parallel")),
        name="hyb_out_ffn",
    )(x, o_gla, o_nsa, mod1, w_out, norm_w[None, :], mod2, mod2, mod2, w_gu, wd)


SSM_DT_OFF = D_INNER + CONV_CH
SSM_COLS = SSM_DT_OFF + LANES
SSM_IN_CHUNK = 256


CONV_TAIL = 8


def _ssm_in_kernel(x_ref, nw_ref, shift_ref, scale_ref, w_ref, cw_ref, cb_ref, dtb_ref,
                   z_ref, xbc_ref, dt_ref, tail_ref):
    tm = x_ref.shape[1]

    @pl.when(pl.program_id(1) == 0)
    def _():
        tail_ref[...] = jnp.zeros(tail_ref.shape, F32)

    h = _modulated_norm(x_ref[0], nw_ref[...], shift_ref[0], scale_ref[0]).astype(BF16)
    for c in range(0, D_INNER, SSM_IN_CHUNK):
        z_ref[0, :, c:c + SSM_IN_CHUNK] = _silu(_dot(h, w_ref[:, c:c + SSM_IN_CHUNK]))
    for c in range(0, CONV_CH, SSM_IN_CHUNK):
        cols = slice(c, c + SSM_IN_CHUNK)
        x = _dot(h, w_ref[:, D_INNER + c:D_INNER + c + SSM_IN_CHUNK])
        tail = tail_ref[:, cols]
        acc = x * cw_ref[CONV_K - 1:CONV_K, cols] + cb_ref[:, cols]
        for j in range(1, CONV_K):
            xs = jnp.concatenate([tail[CONV_TAIL - j:], x[:tm - j]], axis=0)
            acc = acc + xs * cw_ref[CONV_K - 1 - j:CONV_K - j, cols]
        xbc_ref[0, :, cols] = _silu(acc)
        tail_ref[:, cols] = x[tm - CONV_TAIL:]
    dt_ref[0] = _softplus(_dot(h, w_ref[:, SSM_DT_OFF:SSM_COLS]) + dtb_ref[...])


def _ssm_in(x, norm_w, mod, w_perm, conv_w, conv_b, dt_bias_pad):
    bsz, seq, d = x.shape
    tm = 512
    shift_s, scale_s, _ = _mod_specs(2)
    tile = lambda width: pl.BlockSpec((1, tm, width), lambda b, i: (b, i, 0))
    return pl.pallas_call(
        _ssm_in_kernel,
        grid=(bsz, seq // tm),
        in_specs=[tile(d), _const_spec((1, d)), shift_s, scale_s, _const_spec(w_perm.shape),
                  _const_spec((CONV_K, CONV_CH)), _const_spec((1, CONV_CH)), _const_spec((1, LANES))],
        out_specs=[tile(D_INNER), tile(CONV_CH), tile(LANES)],
        out_shape=[jax.ShapeDtypeStruct((bsz, seq, D_INNER), F32),
                   jax.ShapeDtypeStruct((bsz, seq, CONV_CH), F32),
                   jax.ShapeDtypeStruct((bsz, seq, LANES), F32)],
        scratch_shapes=[pltpu.VMEM((CONV_TAIL, CONV_CH), F32)],
        compiler_params=_cparams(("parallel", "arbitrary")),
        name="ssm_in",
    )(x, norm_w[None, :], mod, mod, w_perm, conv_w, conv_b[None, :], dt_bias_pad)


SSD_ROWS = 512
SSD_GW = D_INNER // SSM_GROUPS
SSD_HPG = SSM_HEADS // SSM_GROUPS
SSD_B_OFF = D_INNER
SSD_C_OFF = D_INNER + SSM_GROUPS * SSM_STATE
SSD_SPLIT_PARTS = 3


def _ssd_kernel(xbc_ref, z_ref, dt_ref, alog_ref, dskip_ref, nw_ref, ex_ref, o_ref,
                state_ref, xdt_ref, cum_ref, y_ref):
    rb, q = SSD_ROWS, SSM_CHUNK
    nchunk = rb // q

    @pl.when(pl.program_id(1) == 0)
    def _():
        state_ref[...] = jnp.zeros(state_ref.shape, F32)

    dt = dt_ref[0]
    a = dt * (-jnp.exp(alog_ref[...]))
    row = lax.broadcasted_iota(jnp.int32, (rb, rb), 0)
    col = lax.broadcasted_iota(jnp.int32, (rb, rb), 1)
    tril_bd = jnp.where(((row // q) == (col // q)) & (col <= row), 1.0, 0.0).astype(BF16)
    cum = _dot_exact_lhs(tril_bd, a)

    head_lane = lax.broadcasted_iota(jnp.int32, (rb, LANES), 1) < SSM_HEADS

    def lane_parts(v):
        parts = _split_bf16(jnp.where(head_lane, v, 0.0), SSD_SPLIT_PARTS)
        packed = parts[0].astype(F32)
        for i in range(1, SSD_SPLIT_PARTS):
            packed = packed + pltpu.roll(parts[i].astype(F32), i * SSM_HEADS, 1)
        return packed.astype(BF16)

    dt_parts = lane_parts(dt)
    cum_parts = lane_parts(cum)

    lrow = lax.broadcasted_iota(jnp.int32, (q, SSD_GW), 0)
    lcol = lax.broadcasted_iota(jnp.int32, (q, SSD_GW), 1) % q
    causal_t = lcol <= lrow
    eye_t = lcol == lrow
    half = SSD_GW // 2
    brow = lax.broadcasted_iota(jnp.int32, (half, half), 0) // q
    bcol = lax.broadcasted_iota(jnp.int32, (half, half), 1) // SSM_HEAD_DIM
    same_head = brow == bcol

    for g in range(SSM_GROUPS):
        xg = xbc_ref[0, :, g * SSD_GW:(g + 1) * SSD_GW]
        y_ref[g] = dskip_ref[:, g * SSD_GW:(g + 1) * SSD_GW] * xg
        xdt_ref[g] = xg * _dot(dt_parts, ex_ref[g])
        cum_ref[g] = _dot(cum_parts, ex_ref[g])

    def chunk(n, carry):
        rows = pl.ds(pl.multiple_of(n * q, q), q)
        for g in range(SSM_GROUPS):
            cum_c = cum_ref[g, rows, :]
            cum_s = jnp.sum(jnp.where(eye_t, cum_c, 0.0), axis=0, keepdims=True)
            decay_l = jnp.where(causal_t, jnp.exp(cum_c - cum_s), 0.0)
            bc = xbc_ref[0, rows, SSD_B_OFF + g * SSM_STATE:SSD_B_OFF + (g + 1) * SSM_STATE].astype(BF16)
            cc = xbc_ref[0, rows, SSD_C_OFF + g * SSM_STATE:SSD_C_OFF + (g + 1) * SSM_STATE].astype(BF16)
            cb_t = _dot_nt(cc, jnp.concatenate([bc] * SSD_HPG, axis=0))
            mat = (cb_t * decay_l).astype(BF16)
            xdt_c = xdt_ref[g, rows, :]
            xdt_b = xdt_c.astype(BF16)
            y_diag = []
            for s in range(2):
                blk = xdt_b[:, s * half:(s + 1) * half]
                bd = jnp.where(same_head, jnp.concatenate([blk] * (half // q), axis=0), 0.0).astype(BF16)
                y_diag.append(_dot(mat[:, s * half:(s + 1) * half], bd))
            y = jnp.concatenate(y_diag, axis=1)
            cum_end = cum_c[q - 1:q, :]
            st = state_ref[g]
            y = y + _dot(cc, st.astype(BF16)) * jnp.exp(cum_c)
            x_end = (xdt_c * jnp.exp(cum_end - cum_c)).astype(BF16)
            state_ref[g] = st * jnp.exp(cum_end) + _dot_tn(bc, x_end)
            y_ref[g, rows, :] += y
        return carry

    lax.fori_loop(0, nchunk, chunk, 0)

    for g in range(SSM_GROUPS):
        cols = slice(g * SSD_GW, (g + 1) * SSD_GW)
        y = y_ref[g] * z_ref[0, :, cols]
        y = y * lax.rsqrt(jnp.mean(y * y, axis=-1, keepdims=True) + NORM_EPS) * nw_ref[:, cols]
        o_ref[0, :, cols] = y.astype(o_ref.dtype)


def _ssd(xbc, z, dt, a_log_pad, d_skip_x, norm_w, expand):
    bsz, seq, _ = xbc.shape
    rb = SSD_ROWS
    tile = lambda width: pl.BlockSpec((1, rb, width), lambda b, i: (b, i, 0))
    return pl.pallas_call(
        _ssd_kernel,
        grid=(bsz, seq // rb),
        in_specs=[tile(CONV_CH), tile(D_INNER), tile(LANES), _const_spec((1, LANES)),
                  _const_spec((1, D_INNER)), _const_spec((1, D_INNER)), _const_spec(expand.shape)],
        out_specs=tile(D_INNER),
        out_shape=jax.ShapeDtypeStruct((bsz, seq, D_INNER), BF16),
        scratch_shapes=[pltpu.VMEM((SSM_GROUPS, SSM_STATE, SSD_GW), F32),
                        pltpu.VMEM((SSM_GROUPS, rb, SSD_GW), F32), pltpu.VMEM((SSM_GROUPS, rb, SSD_GW), F32),
                        pltpu.VMEM((SSM_GROUPS, rb, SSD_GW), F32)],
        compiler_params=_cparams(("parallel", "arbitrary")),
        name="ssd",
    )(xbc, z, dt, a_log_pad, d_skip_x, norm_w[None, :], expand)


ROUTE_TM = 512


def _pack_bf16_pairs(a):
    w = a.shape[1] // 2
    bits = lax.bitcast_convert_type(a.astype(jnp.bfloat16).astype(F32), jnp.uint32)
    packed = bits[:, w:] | (bits[:, :w] >> 16)
    return lax.bitcast_convert_type(packed, jnp.int32)


def _unpack_bf16_pairs(p):
    bits = lax.bitcast_convert_type(p, jnp.uint32)
    lo = lax.bitcast_convert_type(bits << 16, F32)
    hi = lax.bitcast_convert_type(bits & jnp.uint32(0xFFFF0000), F32)
    return jnp.concatenate([lo, hi], axis=1).astype(BF16)


def _ssm_out_route_kernel(x_ref, y_ref, g1_ref, w_ref, nw_ref, shift_ref, scale_ref, r_ref,
                          x3_ref, h_ref, gw_ref, sel_ref, cnt_ref):
    x3 = x_ref[0] + g1_ref[0] * _dot(y_ref[0], w_ref[...])
    x3_ref[0] = x3
    h = _modulated_norm(x3, nw_ref[...], shift_ref[0], scale_ref[0])
    h_ref[0] = _pack_bf16_pairs(h)
    h_hi, h_lo = _split_bf16(h, 2)
    r_hi, r_lo = _split_bf16(r_ref[...], 2)
    both = _dot(h_hi, jnp.concatenate([r_hi, r_lo], axis=1))
    logits = both[:, :LANES] + (both[:, LANES:] + _dot(h_lo, r_hi))
    lane = lax.broadcasted_iota(jnp.int32, logits.shape, 1)
    logits = jnp.where(lane < N_EXPERTS, logits, -jnp.inf)
    m1 = jnp.max(logits, axis=-1, keepdims=True)
    i1 = jnp.min(jnp.where(logits == m1, lane, LANES), axis=-1, keepdims=True)
    rest = jnp.where(lane == i1, -jnp.inf, logits)
    m2 = jnp.max(rest, axis=-1, keepdims=True)
    i2 = jnp.min(jnp.where(rest == m2, lane, LANES), axis=-1, keepdims=True)
    e2 = jnp.exp(m2 - m1)
    w1 = 1.0 / (1.0 + e2)
    w2 = e2 / (1.0 + e2)
    chosen = (lane == i1) | (lane == i2)
    gw_ref[0] = jnp.where(lane == i1, w1, 0.0) + jnp.where(lane == i2, w2, 0.0)
    sel = jnp.where(chosen, 1.0, 0.0)
    sel_ref[0] = sel.astype(BF16)
    cnt_ref[0] = jnp.sum(sel, axis=0, keepdims=True)


def _ssm_out_route(x, y, mod1, w_out, norm_w, mod2, router_pad):
    bsz, seq, d = x.shape
    tm = ROUTE_TM
    tiles = seq // tm
    _, _, gate1 = _mod_specs(2)
    shift2, scale2, _ = _mod_specs(2)
    tile = lambda width: pl.BlockSpec((1, tm, width), lambda b, i: (b, i, 0))
    return pl.pallas_call(
        _ssm_out_route_kernel,
        grid=(bsz, tiles),
        in_specs=[tile(d), tile(D_INNER), gate1, _const_spec(w_out.shape), _const_spec((1, d)), shift2, scale2,
                  _const_spec(router_pad.shape)],
        out_specs=[tile(d), tile(d // 2), tile(LANES), tile(LANES),
                   pl.BlockSpec((1, 1, LANES), lambda b, i: (b * tiles + i, 0, 0))],
        out_shape=[jax.ShapeDtypeStruct((bsz, seq, d), F32), jax.ShapeDtypeStruct((bsz, seq, d // 2), jnp.int32),
                   jax.ShapeDtypeStruct((bsz, seq, LANES), F32), jax.ShapeDtypeStruct((bsz, seq, LANES), BF16),
                   jax.ShapeDtypeStruct((bsz * tiles, 1, LANES), F32)],
        compiler_params=_cparams(("parallel", "parallel")),
        name="ssm_out_route",
    )(x, y, mod1, w_out, norm_w[None, :], mod2, mod2, router_pad)


def _moe_plan_kernel(gw_ref, sel_ref, base_ref, pos_ref, wts_ref):
    tm = gw_ref.shape[0]
    gw = gw_ref[...]
    sel = sel_ref[...]
    lane = lax.broadcasted_iota(jnp.int32, (tm, LANES), 1)
    chosen = sel.astype(F32) > 0.5
    row = lax.broadcasted_iota(jnp.int32, (tm, tm), 0)
    col = lax.broadcasted_iota(jnp.int32, (tm, tm), 1)
    before = jnp.where(col < row, 1.0, 0.0).astype(BF16)
    rank = _dot(before, sel)
    dest = rank + base_ref[0]
    first = jnp.min(jnp.where(chosen, lane, LANES), axis=-1, keepdims=True)
    last = jnp.max(jnp.where(chosen, lane, -1), axis=-1, keepdims=True)
    ones = jnp.ones((8, LANES), BF16)
    for slot, pick in enumerate((lane == first, lane == last)):
        parts = _split_bf16(jnp.where(pick, dest, 0.0), 3)
        pos = _dot_nt(ones, parts[0]) + (_dot_nt(ones, parts[1]) + _dot_nt(ones, parts[2]))
        pos_ref[slot] = pos.astype(jnp.int32)
    w_lo = jnp.sum(jnp.where(lane == first, gw, 0.0), axis=-1, keepdims=True)
    w_hi = jnp.sum(jnp.where(lane == last, gw, 0.0), axis=-1, keepdims=True)
    wts_ref[...] = jnp.where(lane == 0, w_lo, jnp.where(lane == 1, w_hi, 0.0))


def _moe_plan(gw, sel, base):
    n = gw.shape[0]
    tm = ROUTE_TM
    return pl.pallas_call(
        _moe_plan_kernel,
        grid=(n // tm,),
        in_specs=[pl.BlockSpec((tm, LANES), lambda i: (i, 0)), pl.BlockSpec((tm, LANES), lambda i: (i, 0)),
                  pl.BlockSpec((1, 1, LANES), lambda i: (i, 0, 0))],
        out_specs=[pl.BlockSpec((2, 8, tm), lambda i: (0, 0, i)), pl.BlockSpec((tm, LANES), lambda i: (i, 0))],
        out_shape=[jax.ShapeDtypeStruct((2, 8, n), jnp.int32), jax.ShapeDtypeStruct((n, LANES), F32)],
        compiler_params=_cparams(("parallel",)),
        name="moe_plan",
    )(gw, sel, base)


SC_WINDOW = 128


SC_CORES = 2
SC_SUBCORES = 16


def _sc_workers():
    return SC_CORES, SC_CORES * SC_SUBCORES


def _sc_dispatch(rows, pos):
    n, width = rows.shape
    ncores, workers = _sc_workers()
    per_worker = n // workers
    steps = per_worker // SC_WINDOW
    mesh = plsc.VectorSubcoreMesh(core_axis_name="c", subcore_axis_name="s")

    @functools.partial(
        pl.kernel, mesh=mesh, out_type=jax.ShapeDtypeStruct((2 * n, width), rows.dtype),
        scratch_types=[pltpu.VMEM((2, steps, SC_WINDOW), jnp.int32), pltpu.VMEM((SC_WINDOW, width), rows.dtype)],
        name="moe_dispatch")
    def run(rows_hbm, pos_hbm, out_hbm, idx_v, rows_v):
        wid = lax.axis_index("s") * ncores + lax.axis_index("c")
        pltpu.sync_copy(pos_hbm.at[wid], idx_v)
        for j in range(steps):
            pltpu.sync_copy(rows_hbm.at[pl.ds(wid * per_worker + j * SC_WINDOW, SC_WINDOW)], rows_v)
            pltpu.sync_copy(rows_v, out_hbm.at[idx_v.at[0, j]])
            pltpu.sync_copy(rows_v, out_hbm.at[idx_v.at[1, j]])

    pos_w = pos.reshape(2, workers, steps, SC_WINDOW).transpose(1, 0, 2, 3)
    return run(rows, pos_w)


SC_GATHER_WINDOW = 128


def _sc_gather(table, pos):
    _, width = table.shape
    n = pos.shape[1]
    ncores, workers = _sc_workers()
    per_worker = n // workers
    win = SC_GATHER_WINDOW
    steps = per_worker // win
    mesh = plsc.VectorSubcoreMesh(core_axis_name="c", subcore_axis_name="s")

    @functools.partial(
        pl.kernel, mesh=mesh, out_type=jax.ShapeDtypeStruct((2, n, width), table.dtype),
        scratch_types=[pltpu.VMEM((2, steps, win), jnp.int32), pltpu.VMEM((win, width), table.dtype)],
        name="moe_gather")
    def run(table_hbm, pos_hbm, out_hbm, idx_v, rows_v):
        wid = lax.axis_index("s") * ncores + lax.axis_index("c")
        pltpu.sync_copy(pos_hbm.at[wid], idx_v)
        for s in range(2):
            for j in range(steps):
                pltpu.sync_copy(table_hbm.at[idx_v.at[s, j]], rows_v)
                pltpu.sync_copy(rows_v, out_hbm.at[s, pl.ds(wid * per_worker + j * win, win)])

    pos_w = pos.reshape(2, workers, steps, win).transpose(1, 0, 2, 3)
    return run(table, pos_w)


MOE_TM = 512
MOE_TF = 896
MOE_VMEM_LIMIT = 62 * 1024 * 1024


def _moe_pairs(goff, n_rows):
    tiles = n_rows // MOE_TM
    steps = tiles + N_EXPERTS - 1
    first_row = jnp.arange(tiles, dtype=jnp.int32) * MOE_TM
    ends = goff[1:]
    e_lo = jnp.sum(first_row[:, None] >= ends[None, :], axis=1).astype(jnp.int32)
    e_hi = jnp.sum((first_row + (MOE_TM - 1))[:, None] >= ends[None, :], axis=1).astype(jnp.int32)
    count = e_hi - e_lo + 1
    start = jnp.cumsum(count) - count
    p = jnp.arange(steps, dtype=jnp.int32)
    tile = jnp.sum(start[None, :] <= p[:, None], axis=1).astype(jnp.int32) - 1
    expert = e_lo[tile] + (p - start[tile])
    valid = p < jnp.sum(count)
    return (jnp.where(valid, tile, tiles - 1).astype(jnp.int32),
            jnp.where(valid, expert, N_EXPERTS - 1).astype(jnp.int32), valid.astype(jnp.int32))


def _moe_group_kernel(tile_ref, exp_ref, valid_ref, goff_ref, x_ref, wg_ref, wu_ref, wd_ref, y_ref, act_ref):
    p = pl.program_id(0)
    tm = x_ref.shape[0]
    tile = tile_ref[p]
    expert = exp_ref[p]

    @pl.when((p == 0) | (tile != tile_ref[jnp.maximum(p - 1, 0)]))
    def _():
        y_ref[...] = jnp.zeros(y_ref.shape, y_ref.dtype)

    @pl.when(valid_ref[p] == 1)
    def _():
        x = _unpack_bf16_pairs(x_ref[...])
        for f in range(0, FFN_EXPERT, MOE_TF):
            act_ref[:, f:f + MOE_TF] = (_silu(_dot(x, wg_ref[0, :, f:f + MOE_TF]))
                                        * _dot(x, wu_ref[0, :, f:f + MOE_TF])).astype(BF16)
        out = _pack_bf16_pairs(_dot(act_ref[...], wd_ref[0]))
        rows = tile * tm + lax.broadcasted_iota(jnp.int32, (tm, 1), 0)
        mine = (rows >= goff_ref[expert]) & (rows < goff_ref[expert + 1])
        y_ref[...] = jnp.where(mine, out, y_ref[...])


def _moe_group(xs, goff, w_gu, wd):
    n_rows, half = xs.shape
    d = 2 * half
    tm = MOE_TM
    tile, expert, valid = _moe_pairs(goff, n_rows)

    def resident(shape, col_block=0):
        return pl.BlockSpec(shape, lambda p, t, e, v, g: (e[p], 0, col_block))

    grid_spec = pltpu.PrefetchScalarGridSpec(
        num_scalar_prefetch=4,
        grid=(tile.shape[0],),
        in_specs=[pl.BlockSpec((tm, half), lambda p, t, e, v, g: (t[p], 0)),
                  resident((1, d, FFN_EXPERT), 0), resident((1, d, FFN_EXPERT), 1), resident((1, FFN_EXPERT, d))],
        out_specs=pl.BlockSpec((tm, half), lambda p, t, e, v, g: (t[p], 0)),
        scratch_shapes=[pltpu.VMEM((tm, FFN_EXPERT), BF16)],
    )
    return pl.pallas_call(
        _moe_group_kernel,
        grid_spec=grid_spec,
        out_shape=jax.ShapeDtypeStruct((n_rows, half), jnp.int32),
        compiler_params=_cparams(("arbitrary",), MOE_VMEM_LIMIT),
        name="moe_group",
    )(tile, expert, valid, goff, xs, w_gu, w_gu, wd)


def _moe_combine_kernel(x_ref, yg_ref, wts_ref, g_ref, fn_ref, o_ref):
    w = wts_ref[0]
    mix = (w[:, 0:1] * _unpack_bf16_pairs(yg_ref[0, 0]).astype(F32)
           + w[:, 1:2] * _unpack_bf16_pairs(yg_ref[1, 0]).astype(F32))
    x4 = x_ref[0] + g_ref[0] * mix
    y = x4 * lax.rsqrt(jnp.mean(x4 * x4, axis=-1, keepdims=True) + NORM_EPS)
    o_ref[0] = y * fn_ref[...]


def _moe_combine(x, yg, wts, mod, final_norm):
    bsz, seq, d = x.shape
    tm = 512
    _, _, gate = _mod_specs(2)
    tile = lambda width: pl.BlockSpec((1, tm, width), lambda b, i: (b, i, 0))
    return pl.pallas_call(
        _moe_combine_kernel,
        grid=(bsz, seq // tm),
        in_specs=[tile(d), pl.BlockSpec((2, 1, tm, d // 2), lambda b, i: (0, b, i, 0)), tile(LANES), gate,
                  _const_spec((1, d))],
        out_specs=tile(d),
        out_shape=jax.ShapeDtypeStruct((bsz, seq, d), F32),
        compiler_params=_cparams(("parallel", "parallel")),
        name="moe_combine",
    )(x, yg, wts, mod, final_norm[None, :])


def _pad_cols(a, width):
    return jnp.pad(a, ((0, 0), (0, width - a.shape[1])))


def _split(a, sizes):
    return jnp.split(a, [int(s) for s in np.cumsum(sizes)[:-1]], axis=-1)


def _prep_hyb_w_in(w):
    q_a, k_a, v_a, lr_a, r_a, q_b, kv_b, g_b = _split(w, HYB_SPLITS)
    return jnp.concatenate([q_a, k_a, v_a, r_a, _pad_cols(lr_a, LANES), _pad_cols(g_b, LANES), q_b, kv_b],
                           axis=1).astype(BF16)


def _prep_cmp(pe, w1, w2):
    eye = jnp.eye(NSA_KV_GROUPS, dtype=F32)
    half = CMP_BLOCK // 2
    w1r = w1.reshape(2, 2, half, NSA_HEAD_DIM, CMP_HIDDEN)
    w1x = jnp.einsum("jstdc,gh->jstgdhc", w1r, eye).reshape(2, 2, half * LANES, NSA_KV_GROUPS * CMP_HIDDEN)
    w2x = jnp.einsum("jcd,gh->jgchd", w2, eye).reshape(2, NSA_KV_GROUPS * CMP_HIDDEN, LANES)
    per = pe.reshape(2, 2, half, 1, NSA_HEAD_DIM)
    pe_rows = jnp.broadcast_to(per, (2, 2, half, NSA_KV_GROUPS, NSA_HEAD_DIM)).reshape(2, 2, 1, half * LANES)
    return pe_rows, w1x.astype(BF16), w2x.astype(BF16)


def _overlap_matrix(seq):
    n_rows = seq // CMP_STRIDE
    n = np.arange(n_rows)[:, None] * CMP_STRIDE
    s = np.arange(LANES)[None, :] * SEL_BLOCK
    ov = (n < s + SEL_BLOCK) & (n + CMP_BLOCK - 1 >= s) & (np.arange(LANES)[None, :] < seq // SEL_BLOCK)
    return jnp.asarray(ov, BF16)


def _head_expand():
    ex = np.zeros((SSM_GROUPS, LANES, SSD_GW), np.float32)
    for g in range(SSM_GROUPS):
        for j in range(SSD_HPG):
            for part in range(SSD_SPLIT_PARTS):
                ex[g, part * SSM_HEADS + g * SSD_HPG + j, j * SSM_HEAD_DIM:(j + 1) * SSM_HEAD_DIM] = 1.0
    return jnp.asarray(ex, BF16)


def kernel(x, c, hyb_norm, hyb_mod_w, hyb_mod_b, hyb_w_in, gla_gk_up, gla_gk_bias, gla_out_norm, nsa_cmp_pe, nsa_cmp_w1, nsa_cmp_w2, hyb_w_out, dense_norm, dense_mod_w, dense_mod_b, dense_w_gu, dense_w_down, ssm_norm, ssm_mod_w, ssm_mod_b, ssm_w_in, ssm_conv_w, ssm_conv_b, ssm_dt_bias, ssm_a_log, ssm_d, ssm_gate_norm, ssm_w_out, moe_norm, moe_mod_w, moe_mod_b, moe_router, moe_w_gu, moe_w_down, final_norm):
    bsz, seq, d = x.shape
    mods = _adaln(c, (hyb_mod_w[0], dense_mod_w[0], ssm_mod_w[0], moe_mod_w[0]),
                  (hyb_mod_b[0], dense_mod_b[0], ssm_mod_b[0], moe_mod_b[0]))
    mods = mods.reshape(4, bsz, 1, 3 * d)

    gkup_pad = jnp.pad(gla_gk_up[0], ((0, LANES - GLA_LOWRANK), (0, 0))).astype(BF16)
    (gq, gk, la, gv, gr, nq, kcmp, vcmp, kslc, vslc, kwin, vwin, ng) = _hyb_in(
        x, hyb_norm[0], mods[0], _prep_hyb_w_in(hyb_w_in[0]), gkup_pad, gla_gk_bias[0])
    o_gla = _gla(gq, gk, la, gv, gr, gla_out_norm[0])
    pe_rows, w1x, w2x = _prep_cmp(nsa_cmp_pe[0], nsa_cmp_w1[0], nsa_cmp_w2[0])
    kc, vc = _nsa_compress(kcmp, vcmp, pe_rows, w1x, w2x)
    o_nsa = _nsa_attn(nq, ng, kc, vc, kslc, vslc, kwin, vwin, _overlap_matrix(seq))
    x = _hyb_out_ffn(x, o_gla, o_nsa, mods[0], hyb_w_out[0].astype(BF16), dense_norm[0], mods[1],
                     dense_w_gu[0].astype(BF16), dense_w_down[0].astype(BF16))

    wz, wxbc, wdt = _split(ssm_w_in[0], SSM_SPLITS)
    w_ssm = jnp.concatenate([wz, wxbc, _pad_cols(wdt, LANES)], axis=1).astype(BF16)
    pad_heads = lambda a: jnp.pad(a, (0, LANES - SSM_HEADS))[None, :]
    z, xbc, dt = _ssm_in(x, ssm_norm[0], mods[2], w_ssm, ssm_conv_w[0], ssm_conv_b[0], pad_heads(ssm_dt_bias[0]))
    y = _ssd(xbc, z, dt, pad_heads(ssm_a_log[0]), jnp.repeat(ssm_d[0], SSM_HEAD_DIM)[None, :],
             ssm_gate_norm[0], _head_expand())
    x3, h4, gw, sel, cnt = _ssm_out_route(x, y, mods[2], ssm_w_out[0].astype(BF16), moe_norm[0], mods[3],
                                          _pad_cols(moe_router[0], LANES))
    n = bsz * seq
    cnt = cnt[:, 0, :]
    totals = jnp.sum(cnt, axis=0)
    goff_f = jnp.cumsum(totals) - totals
    base = (jnp.cumsum(cnt, axis=0) - cnt + goff_f[None, :])[:, None, :]
    goff = jnp.concatenate([goff_f[:N_EXPERTS], jnp.full((1,), 2.0 * n, F32)]).astype(jnp.int32)
    pos, wts = _moe_plan(gw.reshape(n, LANES), sel.reshape(n, LANES), base)
    pos = pos[:, 0, :]
    xs = _sc_dispatch(h4.reshape(n, d // 2), pos)
    ys = _moe_group(xs, goff, moe_w_gu[0].astype(BF16), moe_w_down[0].astype(BF16))
    yg = _sc_gather(ys, pos)
    return _moe_combine(x3, yg.reshape(2, bsz, seq, d // 2), wts.reshape(bsz, seq, LANES), mods[3], final_norm)
```

```python
import functools

import jax
import jax.numpy as jnp
import numpy as np
from jax import lax
from jax.experimental import pallas as pl
from jax.experimental.pallas import tpu as pltpu
from jax.experimental.pallas import tpu_sc as plsc

F32 = jnp.float32
BF16 = jnp.bfloat16

D_MODEL = 1024
NORM_EPS = 1e-6
GLA_HEADS = 4
GLA_DV = D_MODEL // 8
GLA_DK = GLA_DV // 2
GLA_LOWRANK = 16
GLA_TAU = 16.0
GLA_CHUNK = 64
NSA_HEADS = 8
NSA_KV_GROUPS = 2
NSA_HEAD_DIM = D_MODEL // 16
CMP_BLOCK = 32
CMP_STRIDE = 16
CMP_HIDDEN = 4 * NSA_HEAD_DIM
SEL_BLOCK = 64
N_SELECT = 8
WINDOW = 512
D_INNER = 2 * D_MODEL
SSM_HEAD_DIM = 64
SSM_HEADS = D_INNER // SSM_HEAD_DIM
SSM_GROUPS = 4
SSM_STATE = 128
CONV_K = 4
SSM_CHUNK = 64
CONV_CH = D_INNER + 2 * SSM_GROUPS * SSM_STATE
FFN_DENSE = ((8 * D_MODEL // 3 + 127) // 128) * 128
N_EXPERTS = 8
FFN_EXPERT = 7 * D_MODEL // 2
HYB_SPLITS = (GLA_HEADS * GLA_DK, GLA_HEADS * GLA_DK, GLA_HEADS * GLA_DV, GLA_LOWRANK, GLA_HEADS * GLA_DV,
              NSA_HEADS * NSA_HEAD_DIM, 6 * NSA_KV_GROUPS * NSA_HEAD_DIM, 3 * NSA_HEADS)
SSM_SPLITS = (D_INNER, CONV_CH, SSM_HEADS)

LANES = 128
VMEM_LIMIT = 56 * 1024 * 1024


def _cparams(sem, vmem_limit=VMEM_LIMIT):
    return pltpu.CompilerParams(dimension_semantics=sem, vmem_limit_bytes=vmem_limit)


def _dot(a, b):
    return jnp.dot(a, b, preferred_element_type=F32)


def _dot_nt(a, b):
    return lax.dot_general(a, b, (((1,), (1,)), ((), ())), preferred_element_type=F32)


def _dot_tn(a, b):
    return lax.dot_general(a, b, (((0,), (0,)), ((), ())), preferred_element_type=F32)


def _split_bf16(a, parts):
    out = []
    r = a
    for _ in range(parts):
        p = r.astype(BF16)
        out.append(p)
        r = r - p.astype(F32)
    return out


def _dot_exact_lhs(m, a, parts=3):
    width = a.shape[1]
    wide = _dot(m, jnp.concatenate(_split_bf16(a, parts), axis=1))
    acc = wide[:, :width]
    for i in range(1, parts):
        acc = acc + wide[:, i * width:(i + 1) * width]
    return acc


def _dot_exact_rhs(a, m, parts=3):
    acc = None
    for p in _split_bf16(a, parts):
        t = _dot(p, m)
        acc = t if acc is None else acc + t
    return acc


def _sigmoid(x):
    return 1.0 / (1.0 + jnp.exp(-x))


def _silu(x):
    return x * _sigmoid(x)


def _softplus(x):
    return jnp.maximum(x, 0.0) + jnp.log(1.0 + jnp.exp(-jnp.abs(x)))


def _gelu_tanh(x):
    return x * (0.5 * (1.0 + jnp.tanh(0.7978845608028654 * (x + 0.044715 * (x * x * x)))))


def _modulated_norm(x, nw, shift, scale):
    ms = jnp.mean(x * x, axis=-1, keepdims=True)
    y = x * lax.rsqrt(ms + NORM_EPS)
    return (y * nw) * (1.0 + scale) + shift


def _const_spec(shape):
    nd = len(shape)
    return pl.BlockSpec(shape, lambda *_: (0,) * nd, pipeline_mode=pl.Buffered(1))


def _adaln_kernel(c_ref, w0, w1, w2, w3, b0, b1, b2, b3, o_ref):
    sc = _silu(c_ref[...]).astype(BF16)
    for i, (w, b) in enumerate(((w0, b0), (w1, b1), (w2, b2), (w3, b3))):
        o_ref[i] = _dot(sc, w[...].astype(BF16)) + b[...]


def _adaln(c, ws, bs):
    bsz, d = c.shape
    n = ws[0].shape[1]
    tn = 512
    w_spec = pl.BlockSpec((d, tn), lambda j: (0, j))
    b_spec = pl.BlockSpec((1, tn), lambda j: (0, j))
    return pl.pallas_call(
        _adaln_kernel,
        grid=(n // tn,),
        in_specs=[pl.BlockSpec((bsz, d), lambda j: (0, 0))] + [w_spec] * 4 + [b_spec] * 4,
        out_specs=pl.BlockSpec((4, bsz, tn), lambda j: (0, 0, j)),
        out_shape=jax.ShapeDtypeStruct((4, bsz, n), F32),
        compiler_params=_cparams(("arbitrary",)),
        name="adaln",
    )(c, *ws, *[b[None, :] for b in bs])


def _mod_specs(tile_axes):
    def spec(part):
        if tile_axes == 2:
            return pl.BlockSpec((1, 1, D_MODEL), lambda b, i: (b, 0, part))
        return pl.BlockSpec((1, 1, D_MODEL), lambda b, i, j, k: (b, 0, part))
    return spec(0), spec(1), spec(2)


HYB_SEG = {
    "gq": (0, 256), "gk": (256, 256), "gv": (512, 512), "gr": (1024, 512), "lr_ng": (1536, 256),
    "nq": (1792, 512), "kv": (2304, 768),
}


def _hyb_in_kernel(x_ref, nw_ref, shift_ref, scale_ref, w_ref, gkup_ref, gkb_ref,
                   gq_ref, gk_ref, la_ref, gv_ref, gr_ref, nq_ref,
                   kcmp_ref, vcmp_ref, kslc_ref, vslc_ref, kwin_ref, vwin_ref, ng_ref):
    h = _modulated_norm(x_ref[0], nw_ref[...], shift_ref[0], scale_ref[0]).astype(BF16)

    def proj(name):
        off, width = HYB_SEG[name]
        return _dot(h, w_ref[:, off:off + width])

    def proj_pair(name, pair=0):
        off = HYB_SEG[name][0] + pair * 2 * LANES
        y = _dot(h, w_ref[:, off:off + 2 * LANES])
        return y[:, :LANES], y[:, LANES:]

    gq_ref[0] = proj("gq")
    gk_ref[0] = proj("gk")
    gv_ref[0] = proj("gv").astype(BF16)
    gr_ref[0] = proj("gr")
    lr, ng = proj_pair("lr_ng")
    z = _dot(lr.astype(BF16), gkup_ref[...]) + gkb_ref[...]
    la_ref[0] = -_softplus(-z) * (1.0 / GLA_TAU)
    ng_ref[0] = ng
    nq_ref[0] = proj("nq").astype(BF16)
    kcmp_ref[0], vcmp_ref[0] = proj_pair("kv", 0)
    kslc, vslc = proj_pair("kv", 1)
    kslc_ref[0] = kslc.T.astype(BF16)
    vslc_ref[0] = vslc.astype(BF16)
    kwin, vwin = proj_pair("kv", 2)
    kwin_ref[0] = kwin.T.astype(BF16)
    vwin_ref[0] = vwin.astype(BF16)


def _hyb_in(x, norm_w, mod, w_perm, gkup_pad, gk_bias):
    bsz, seq, d = x.shape
    tm = 512
    shift_s, scale_s, _ = _mod_specs(2)

    def out(width, dtype):
        return (pl.BlockSpec((1, tm, width), lambda b, i: (b, i, 0)),
                jax.ShapeDtypeStruct((bsz, seq, width), dtype))

    def out_t(dtype):
        return (pl.BlockSpec((1, LANES, tm), lambda b, i: (b, 0, i)),
                jax.ShapeDtypeStruct((bsz, LANES, seq), dtype))

    outs = [out(256, F32), out(256, F32), out(256, F32), out(512, BF16), out(512, F32), out(512, BF16),
            out(128, F32), out(128, F32), out_t(BF16), out(128, BF16), out_t(BF16), out(128, BF16),
            out(128, F32)]
    return pl.pallas_call(
        _hyb_in_kernel,
        grid=(bsz, seq // tm),
        in_specs=[pl.BlockSpec((1, tm, d), lambda b, i: (b, i, 0)),
                  _const_spec((1, d)), shift_s, scale_s,
                  _const_spec(w_perm.shape), _const_spec(gkup_pad.shape), _const_spec((1, 256))],
        out_specs=[o[0] for o in outs],
        out_shape=[o[1] for o in outs],
        compiler_params=_cparams(("parallel", "parallel")),
        name="hyb_in",
    )(x, norm_w[None, :], mod, mod, w_perm, gkup_pad, gk_bias[None, :])


GLA_ROWS = 512


def _gla_kernel(q_ref, k_ref, la_ref, v_ref, r_ref, nw_ref, o_ref, *, seq):
    rb = GLA_ROWS
    ncb = rb // GLA_CHUNK
    row = lax.broadcasted_iota(jnp.int32, (rb, rb), 0)
    col = lax.broadcasted_iota(jnp.int32, (rb, rb), 1)
    chunk_causal = ((row // GLA_CHUNK) == (col // GLA_CHUNK)) & (col <= row)
    tril_bd = jnp.where(chunk_causal, 1.0, 0.0).astype(BF16)
    lane = lax.broadcasted_iota(jnp.int32, (rb, LANES), 1)
    head_mask = (lane < GLA_DK, lane >= GLA_DK)
    nw = nw_ref[...]

    def body(i, st):
        r0 = pl.multiple_of(i * rb, rb)
        rows = pl.ds(r0, rb)
        q = q_ref[0, rows, :]
        k = k_ref[0, rows, :]
        la = la_ref[0, rows, :]
        b = _dot_exact_lhs(tril_bd, la)
        b3 = b.reshape(ncb, GLA_CHUNK, LANES)
        bend3 = b3[:, GLA_CHUNK - 1:GLA_CHUNK, :]
        bend = jnp.broadcast_to(bend3, (ncb, GLA_CHUNK, LANES)).reshape(rb, LANES)
        q_dec = (q * GLA_DK ** -0.5) * jnp.exp(b)
        k_dec = (k * jnp.exp(-b)).astype(BF16)
        k_end = k * jnp.exp(bend - b)
        decay = jnp.exp(bend3)

        qm, vh, o_intra, upd = [], [], [], []
        for h in range(2):
            qm_h = jnp.where(head_mask[h], q_dec, 0.0).astype(BF16)
            att = _dot_nt(qm_h, k_dec)
            att = jnp.where(chunk_causal, att, 0.0).astype(BF16)
            v_h = v_ref[0, rows, h * GLA_DV:(h + 1) * GLA_DV]
            o_intra.append(_dot(att, v_h))
            km_h = jnp.where(head_mask[h], k_end, 0.0).astype(BF16)
            upd.append([_dot_tn(v_h[n * GLA_CHUNK:(n + 1) * GLA_CHUNK], km_h[n * GLA_CHUNK:(n + 1) * GLA_CHUNK])
                        for n in range(ncb)])
            qm.append(qm_h)
            vh.append(v_h)

        prev = []
        for n in range(ncb):
            prev.append(st.astype(BF16))
            st = st * decay[n] + (upd[0][n] + upd[1][n])

        for h in range(2):
            o_inter = jnp.concatenate(
                [_dot_nt(qm[h][n * GLA_CHUNK:(n + 1) * GLA_CHUNK], prev[n]) for n in range(ncb)], axis=0)
            o = o_intra[h] + o_inter
            y = o * lax.rsqrt(jnp.mean(o * o, axis=-1, keepdims=True) + NORM_EPS) * nw
            r = r_ref[0, rows, h * GLA_DV:(h + 1) * GLA_DV]
            o_ref[0, rows, h * GLA_DV:(h + 1) * GLA_DV] = (y * _silu(r)).astype(o_ref.dtype)
        return st

    lax.fori_loop(0, seq // rb, body, jnp.zeros((GLA_DV, 2 * GLA_DK), F32))


def _gla(gq, gk, la, gv, gr, out_norm):
    bsz, seq, _ = gq.shape
    qk_spec = pl.BlockSpec((1, seq, 2 * GLA_DK), lambda b, p: (b, 0, p))
    v_spec = pl.BlockSpec((1, seq, 2 * GLA_DV), lambda b, p: (b, 0, p))
    return pl.pallas_call(
        functools.partial(_gla_kernel, seq=seq),
        grid=(bsz, GLA_HEADS // 2),
        in_specs=[qk_spec, qk_spec, qk_spec, v_spec, v_spec, _const_spec((1, GLA_DV))],
        out_specs=v_spec,
        out_shape=jax.ShapeDtypeStruct((bsz, seq, GLA_HEADS * GLA_DV), BF16),
        compiler_params=_cparams(("parallel", "parallel")),
        name="gla",
    )(gq, gk, la, gv, gr, out_norm[None, :])


def _nsa_compress_kernel(k_ref, v_ref, pe_ref, w1_ref, w2_ref, kc_ref, vc_ref, *, n_cmp):
    rows = k_ref.shape[1]
    rid = lax.broadcasted_iota(jnp.int32, (rows, LANES), 0)
    for j, (src, dst) in enumerate(((k_ref, kc_ref), (v_ref, vc_ref))):
        x = src[0]
        lo = _dot((x + pe_ref[j, 0]).astype(BF16), w1_ref[j, 0])
        hi = _dot((x + pe_ref[j, 1]).astype(BF16), w1_ref[j, 1])
        hpre = lo + pltpu.roll(hi, rows - 1, 0)
        out = jnp.where(rid < n_cmp, _dot(_gelu_tanh(hpre).astype(BF16), w2_ref[j]), 0.0)
        dst[0] = (out.T if j == 0 else out).astype(dst.dtype)


def _nsa_compress(kcmp, vcmp, pe_rows, w1x, w2x):
    bsz, seq, _ = kcmp.shape
    rows = seq // CMP_STRIDE
    width = CMP_STRIDE * LANES
    n_cmp = (seq - CMP_BLOCK) // CMP_STRIDE + 1
    x_spec = pl.BlockSpec((1, rows, width), lambda b: (b, 0, 0))
    return pl.pallas_call(
        functools.partial(_nsa_compress_kernel, n_cmp=n_cmp),
        grid=(bsz,),
        in_specs=[x_spec, x_spec, _const_spec(pe_rows.shape), _const_spec(w1x.shape), _const_spec(w2x.shape)],
        out_specs=[pl.BlockSpec((1, LANES, rows), lambda b: (b, 0, 0)),
                   pl.BlockSpec((1, rows, LANES), lambda b: (b, 0, 0))],
        out_shape=[jax.ShapeDtypeStruct((bsz, LANES, rows), BF16), jax.ShapeDtypeStruct((bsz, rows, LANES), BF16)],
        compiler_params=_cparams(("parallel",)),
        name="nsa_compress",
    )(kcmp.reshape(bsz, rows, width), vcmp.reshape(bsz, rows, width), pe_rows, w1x, w2x)


NSA_QT = 256
NSA_KT = 512
NSA_ROWS = NSA_HEADS * NSA_QT


NSA_MASK_PENALTY = 1e30
NSA_LANE_POS_HI, NSA_LANE_POS_LO, NSA_LANE_CMP_IDX, NSA_LANE_CMP_ONE = 64, 65, 66, 67


def _nsa_attn_kernel(q_ref, g_ref, kc_ref, vc_ref, ks_ref, vs_ref, kw_ref, vw_ref, ov_ref, kf_ref, cf_ref, o_ref,
                     m_ref, acc_ref, *, seq):
    qt, kt = NSA_QT, NSA_KT
    heads_per_group = NSA_HEADS // NSA_KV_GROUPS
    q0 = pl.program_id(1) * qt
    t = q0 + lax.broadcasted_iota(jnp.int32, (qt, 1), 0)
    lane = lax.broadcasted_iota(jnp.int32, (qt, LANES), 1)
    group_lanes = (lane < NSA_HEAD_DIM, lane >= NSA_HEAD_DIM)
    slopes = [2.0 ** (-(h + 1)) for h in range(NSA_HEADS)]

    qa = q_ref[0].astype(F32) * NSA_HEAD_DIM ** -0.5
    q_rows = []
    for h in range(NSA_HEADS):
        g = h // heads_per_group
        blk = qa[:, (h // 2) * LANES:(h // 2 + 1) * LANES]
        if h % 2 != g:
            blk = pltpu.roll(blk, NSA_HEAD_DIM, 1)
        q_rows.append(jnp.where(group_lanes[g], blk, 0.0).astype(BF16))
    q_ext = jnp.concatenate(q_rows, axis=0)

    def alibi_lanes(h):
        s = slopes[h]
        return jnp.where(lane == NSA_LANE_POS_HI, SEL_BLOCK * s,
                         jnp.where(lane == NSA_LANE_POS_LO, s,
                                   jnp.where(lane == NSA_LANE_CMP_IDX, CMP_STRIDE * s,
                                             jnp.where(lane == NSA_LANE_CMP_ONE, (CMP_BLOCK - 1) / 2.0 * s, 0.0))))

    q_pos = [alibi_lanes(h) for h in range(NSA_HEADS)]
    q_full = jnp.concatenate([q_ext, jnp.concatenate(q_pos, axis=0).astype(BF16)], axis=1)

    def softmax_parts(s, mask):
        s = jnp.where(mask, s, -jnp.inf)
        m = jnp.max(s, axis=-1, keepdims=True)
        m = jnp.where(m == -jnp.inf, 0.0, m)
        p = jnp.exp(s - m)
        return p, jnp.sum(p, axis=-1, keepdims=True)

    n_rows = kc_ref.shape[2]
    cidx = lax.broadcasted_iota(jnp.int32, (1, n_rows), 1)
    cmp_mask = cidx * CMP_STRIDE + (CMP_BLOCK - 1) <= t
    s_all = _dot(q_full, jnp.concatenate([kc_ref[0], cf_ref[...]], axis=0))
    p_list, psum = [], [None] * NSA_KV_GROUPS
    for h in range(NSA_HEADS):
        g = h // heads_per_group
        p, l = softmax_parts(s_all[h * qt:(h + 1) * qt], cmp_mask)
        p = p / jnp.maximum(l, 1e-20)
        p_list.append(p.astype(BF16))
        psum[g] = p if psum[g] is None else psum[g] + p
    o_cmp = _dot(jnp.concatenate(p_list, axis=0), vc_ref[0])

    n_blk = LANES // 2
    tq = q0 + lax.broadcasted_iota(jnp.int32, (n_blk, qt), 1)
    bidx = lax.broadcasted_iota(jnp.int32, (n_blk, qt), 0)
    bidx_f = bidx.astype(F32)
    blk_t = tq // SEL_BLOCK
    forced = (bidx == 0) | (bidx == blk_t) | (bidx == blk_t - 1)
    future = bidx * SEL_BLOCK > tq
    q_sel_rows = []
    for g in range(NSA_KV_GROUPS):
        imp = _dot_exact_rhs(psum[g], ov_ref[...]).T[:n_blk]
        v = jnp.where(future, -1.0, jnp.where(forced, 3e38, imp))
        chosen = jnp.zeros((n_blk, qt), F32)
        for _ in range(min(N_SELECT, seq // SEL_BLOCK)):
            m = jnp.max(v, axis=0, keepdims=True)
            idx = jnp.min(jnp.where(v == m, bidx_f, float(LANES)), axis=0, keepdims=True)
            pick = (bidx_f == idx) & (m >= 0.0)
            chosen = jnp.where(pick, 1.0, chosen)
            v = jnp.where(pick, -1.0, v)
        penalty = jnp.concatenate([(chosen - 1.0) * NSA_MASK_PENALTY, jnp.zeros((n_blk, qt), F32)], axis=0).T
        for r in range(heads_per_group):
            q_sel_rows.append((penalty + q_pos[g * heads_per_group + r]).astype(BF16))
    q_full_sel = jnp.concatenate([q_ext, jnp.concatenate(q_sel_rows, axis=0)], axis=1)

    vlane = lax.broadcasted_iota(jnp.int32, (1, LANES), 1)
    own_lanes = (vlane < NSA_HEAD_DIM, vlane >= NSA_HEAD_DIM)

    def weighted_values(p_rows, v):
        v = v.astype(F32)
        outs = []
        for g in range(NSA_KV_GROUPS):
            pg = jnp.concatenate(p_rows[g * heads_per_group:(g + 1) * heads_per_group], axis=0)
            outs.append(_dot(pg, jnp.where(own_lanes[g], v, 1.0).astype(BF16)))
        return jnp.concatenate(outs, axis=0)

    def normalised(acc):
        return acc / jnp.maximum(pltpu.roll(acc, NSA_HEAD_DIM, 1), 1e-20)

    m_ref[...] = jnp.full(m_ref.shape, -jnp.inf, F32)
    acc_ref[...] = jnp.zeros(acc_ref.shape, F32)
    def sel_tile(k0, width, causal):
        ks = jnp.concatenate([ks_ref[0, :, pl.ds(k0, width)], kf_ref[:, pl.ds(k0, width)]], axis=0)
        s_t = _dot(q_full_sel, ks)
        visible = (k0 + lax.broadcasted_iota(jnp.int32, (1, width), 1)) <= t
        m_all = m_ref[...]
        p_rows, m_rows, alpha_rows = [], [], []
        for h in range(NSA_HEADS):
            rows = slice(h * qt, (h + 1) * qt)
            s = s_t[rows]
            if causal:
                s = jnp.where(visible, s, -jnp.inf)
            m_old = m_all[rows]
            m_new = jnp.maximum(m_old, jnp.max(s, axis=-1, keepdims=True))
            m_safe = jnp.where(m_new == -jnp.inf, 0.0, m_new)
            p_rows.append(jnp.exp(s - jnp.concatenate([m_safe] * (width // LANES), axis=1)).astype(BF16))
            alpha_rows.append(jnp.exp(m_old - m_safe))
            m_rows.append(m_new)
        m_ref[...] = jnp.concatenate(m_rows, axis=0)
        acc_ref[...] = (jnp.concatenate(alpha_rows, axis=0) * acc_ref[...]
                        + weighted_values(p_rows, vs_ref[0, pl.ds(k0, width), :]))

    def sel_past_tile(j, carry):
        sel_tile(pl.multiple_of(j * kt, kt), kt, False)
        return carry

    last = q0 // kt
    lax.fori_loop(0, last, sel_past_tile, 0)
    sel_tile(pl.multiple_of(last * kt, kt), kt, True)
    o_slc = normalised(acc_ref[...])

    wk = WINDOW + qt
    w0 = pl.multiple_of(jnp.maximum(q0 - WINDOW, 0), qt)
    kw = jnp.concatenate([kw_ref[0, :, pl.ds(w0, wk)], kf_ref[:, pl.ds(w0, wk)]], axis=0)
    wdist = t - (w0 + lax.broadcasted_iota(jnp.int32, (1, wk), 1))
    wmask = (wdist >= 0) & (wdist < WINDOW)
    s_all = _dot(q_full, kw)
    p_rows = []
    for h in range(NSA_HEADS):
        s = jnp.where(wmask, s_all[h * qt:(h + 1) * qt], -jnp.inf)
        m = jnp.max(s, axis=-1, keepdims=True)
        p_rows.append(jnp.exp(s - jnp.where(m == -jnp.inf, 0.0, m)).astype(BF16))
    o_win = normalised(weighted_values(p_rows, vw_ref[0, pl.ds(w0, wk), :]))

    gates = _sigmoid(g_ref[0])
    o_heads = []
    for h in range(NSA_HEADS):
        rows = slice(h * qt, (h + 1) * qt)
        o_heads.append(gates[:, 3 * h:3 * h + 1] * o_cmp[rows] + gates[:, 3 * h + 1:3 * h + 2] * o_slc[rows]
                       + gates[:, 3 * h + 2:3 * h + 3] * o_win[rows])
    for c in range(NSA_HEADS // 2):
        g = (2 * c) // heads_per_group
        a, b = o_heads[2 * c], o_heads[2 * c + 1]
        if g == 0:
            blk = jnp.where(group_lanes[0], a, pltpu.roll(b, NSA_HEAD_DIM, 1))
        else:
            blk = jnp.where(group_lanes[0], pltpu.roll(a, NSA_HEAD_DIM, 1), b)
        o_ref[0, :, c * LANES:(c + 1) * LANES] = blk.astype(o_ref.dtype)


def _nsa_position_features(seq):
    assert seq // SEL_BLOCK <= LANES // 2
    pos = np.arange(seq)
    kf = np.zeros((seq, LANES), np.float32)
    kf[pos, pos // SEL_BLOCK] = 1.0
    kf[:, NSA_LANE_POS_HI] = pos // SEL_BLOCK
    kf[:, NSA_LANE_POS_LO] = pos % SEL_BLOCK
    n_rows = seq // CMP_STRIDE
    cf = np.zeros((n_rows, LANES), np.float32)
    cf[:, NSA_LANE_CMP_IDX] = np.arange(n_rows)
    cf[:, NSA_LANE_CMP_ONE] = 1.0
    return jnp.asarray(kf.T, BF16), jnp.asarray(cf.T, BF16)


def _nsa_attn(nq, ng, kc, vc, kslc, vslc, kwin, vwin, overlap):
    bsz, seq, _ = nq.shape
    n_rows = vc.shape[1]
    kfeat, cfeat = _nsa_position_features(seq)
    full = lambda rows: pl.BlockSpec((1, rows, LANES), lambda b, i: (b, 0, 0))
    full_t = lambda cols: pl.BlockSpec((1, LANES, cols), lambda b, i: (b, 0, 0))
    return pl.pallas_call(
        functools.partial(_nsa_attn_kernel, seq=seq),
        grid=(bsz, seq // NSA_QT),
        in_specs=[pl.BlockSpec((1, NSA_QT, NSA_HEADS * NSA_HEAD_DIM), lambda b, i: (b, i, 0)),
                  pl.BlockSpec((1, NSA_QT, LANES), lambda b, i: (b, i, 0)),
                  full_t(n_rows), full(n_rows), full_t(seq), full(seq), full_t(seq), full(seq),
                  _const_spec(overlap.shape), _const_spec(kfeat.shape), _const_spec(cfeat.shape)],
        out_specs=pl.BlockSpec((1, NSA_QT, NSA_HEADS * NSA_HEAD_DIM), lambda b, i: (b, i, 0)),
        out_shape=jax.ShapeDtypeStruct((bsz, seq, NSA_HEADS * NSA_HEAD_DIM), BF16),
        scratch_shapes=[pltpu.VMEM((NSA_ROWS, LANES), F32), pltpu.VMEM((NSA_ROWS, LANES), F32)],
        compiler_params=_cparams(("parallel", "arbitrary")),
        name="nsa_attn",
    )(nq, ng, kc, vc, kslc, vslc, kwin, vwin, overlap, kfeat, cfeat)


MXU_COLS = 256


def _column_chunks(total, chunk):
    assert total % MXU_COLS == 0 and chunk % MXU_COLS == 0
    return [(f, min(chunk, total - f)) for f in range(0, total, chunk)]


FFN_CHUNK = 1536


def _hyb_out_ffn_kernel(x_ref, oa_ref, ob_ref, g1_ref, wo_ref,
                        nw_ref, shift_ref, scale_ref, g2_ref, wgu_ref, wd_ref, o_ref, act_ref):
    n_a = oa_ref.shape[2]
    mix = _dot(oa_ref[0], wo_ref[:n_a, :]) + _dot(ob_ref[0], wo_ref[n_a:, :])
    x1 = x_ref[0] + g1_ref[0] * mix
    h = _modulated_norm(x1, nw_ref[...], shift_ref[0], scale_ref[0]).astype(BF16)
    for f, width in _column_chunks(FFN_DENSE, FFN_CHUNK):
        gate = _dot(h, wgu_ref[:, f:f + width])
        up = _dot(h, wgu_ref[:, FFN_DENSE + f:FFN_DENSE + f + width])
        act_ref[:, f:f + width] = (_silu(gate) * up).astype(BF16)
    o_ref[0] = x1 + g2_ref[0] * _dot(act_ref[...], wd_ref[...])


def _hyb_out_ffn(x, o_gla, o_nsa, mod1, w_out, norm_w, mod2, w_gu, wd):
    bsz, seq, d = x.shape
    tm = 512
    _, _, gate1 = _mod_specs(2)
    shift2, scale2, gate2 = _mod_specs(2)
    tile = lambda width: pl.BlockSpec((1, tm, width), lambda b, i: (b, i, 0))
    return pl.pallas_call(
        _hyb_out_ffn_kernel,
        grid=(bsz, seq // tm),
        in_specs=[tile(d), tile(o_gla.shape[-1]), tile(o_nsa.shape[-1]), gate1,
                  _const_spec(w_out.shape), _const_spec((1, d)), shift2, scale2, gate2,
                  _const_spec(w_gu.shape), _const_spec(wd.shape)],
        out_specs=tile(d),
        out_shape=jax.ShapeDtypeStruct((bsz, seq, d), F32),
        scratch_shapes=[pltpu.VMEM((tm, FFN_DENSE), BF16)],
        compiler_params=_cparams(("parallel", "parallel")),
        name="hyb_out_ffn",
    )(x, o_gla, o_nsa, mod1, w_out, norm_w[None, :], mod2, mod2, mod2, w_gu, wd)


SSM_DT_OFF = D_INNER + CONV_CH
SSM_COLS = SSM_DT_OFF + LANES
SSM_IN_CHUNK = 256


CONV_TAIL = 8


def _ssm_in_kernel(x_ref, nw_ref, shift_ref, scale_ref, w_ref, cw_ref, cb_ref, dtb_ref,
                   z_ref, xbc_ref, dt_ref, tail_ref):
    tm = x_ref.shape[1]

    @pl.when(pl.program_id(1) == 0)
    def _():
        tail_ref[...] = jnp.zeros(tail_ref.shape, F32)

    h = _modulated_norm(x_ref[0], nw_ref[...], shift_ref[0], scale_ref[0]).astype(BF16)
    for c in range(0, D_INNER, SSM_IN_CHUNK):
        z_ref[0, :, c:c + SSM_IN_CHUNK] = _silu(_dot(h, w_ref[:, c:c + SSM_IN_CHUNK]))
    for c in range(0, CONV_CH, SSM_IN_CHUNK):
        cols = slice(c, c + SSM_IN_CHUNK)
        x = _dot(h, w_ref[:, D_INNER + c:D_INNER + c + SSM_IN_CHUNK])
        tail = tail_ref[:, cols]
        acc = x * cw_ref[CONV_K - 1:CONV_K, cols] + cb_ref[:, cols]
        for j in range(1, CONV_K):
            xs = jnp.concatenate([tail[CONV_TAIL - j:], x[:tm - j]], axis=0)
            acc = acc + xs * cw_ref[CONV_K - 1 - j:CONV_K - j, cols]
        xbc_ref[0, :, cols] = _silu(acc)
        tail_ref[:, cols] = x[tm - CONV_TAIL:]
    dt_ref[0] = _softplus(_dot(h, w_ref[:, SSM_DT_OFF:SSM_COLS]) + dtb_ref[...])


def _ssm_in(x, norm_w, mod, w_perm, conv_w, conv_b, dt_bias_pad):
    bsz, seq, d = x.shape
    tm = 512
    shift_s, scale_s, _ = _mod_specs(2)
    tile = lambda width: pl.BlockSpec((1, tm, width), lambda b, i: (b, i, 0))
    return pl.pallas_call(
        _ssm_in_kernel,
        grid=(bsz, seq // tm),
        in_specs=[tile(d), _const_spec((1, d)), shift_s, scale_s, _const_spec(w_perm.shape),
                  _const_spec((CONV_K, CONV_CH)), _const_spec((1, CONV_CH)), _const_spec((1, LANES))],
        out_specs=[tile(D_INNER), tile(CONV_CH), tile(LANES)],
        out_shape=[jax.ShapeDtypeStruct((bsz, seq, D_INNER), F32),
                   jax.ShapeDtypeStruct((bsz, seq, CONV_CH), F32),
                   jax.ShapeDtypeStruct((bsz, seq, LANES), F32)],
        scratch_shapes=[pltpu.VMEM((CONV_TAIL, CONV_CH), F32)],
        compiler_params=_cparams(("parallel", "arbitrary")),
        name="ssm_in",
    )(x, norm_w[None, :], mod, mod, w_perm, conv_w, conv_b[None, :], dt_bias_pad)


SSD_ROWS = 512
SSD_GW = D_INNER // SSM_GROUPS
SSD_HPG = SSM_HEADS // SSM_GROUPS
SSD_B_OFF = D_INNER
SSD_C_OFF = D_INNER + SSM_GROUPS * SSM_STATE
SSD_SPLIT_PARTS = 3


def _ssd_kernel(xbc_ref, z_ref, dt_ref, alog_ref, dskip_ref, nw_ref, ex_ref, o_ref,
                state_ref, xdt_ref, cum_ref, y_ref):
    rb, q = SSD_ROWS, SSM_CHUNK
    nchunk = rb // q

    @pl.when(pl.program_id(1) == 0)
    def _():
        state_ref[...] = jnp.zeros(state_ref.shape, F32)

    dt = dt_ref[0]
    a = dt * (-jnp.exp(alog_ref[...]))
    row = lax.broadcasted_iota(jnp.int32, (rb, rb), 0)
    col = lax.broadcasted_iota(jnp.int32, (rb, rb), 1)
    tril_bd = jnp.where(((row // q) == (col // q)) & (col <= row), 1.0, 0.0).astype(BF16)
    cum = _dot_exact_lhs(tril_bd, a)

    head_lane = lax.broadcasted_iota(jnp.int32, (rb, LANES), 1) < SSM_HEADS

    def lane_parts(v):
        parts = _split_bf16(jnp.where(head_lane, v, 0.0), SSD_SPLIT_PARTS)
        packed = parts[0].astype(F32)
        for i in range(1, SSD_SPLIT_PARTS):
            packed = packed + pltpu.roll(parts[i].astype(F32), i * SSM_HEADS, 1)
        return packed.astype(BF16)

    dt_parts = lane_parts(dt)
    cum_parts = lane_parts(cum)

    lrow = lax.broadcasted_iota(jnp.int32, (q, SSD_GW), 0)
    lcol = lax.broadcasted_iota(jnp.int32, (q, SSD_GW), 1) % q
    causal_t = lcol <= lrow
    eye_t = lcol == lrow
    half = SSD_GW // 2
    brow = lax.broadcasted_iota(jnp.int32, (half, half), 0) // q
    bcol = lax.broadcasted_iota(jnp.int32, (half, half), 1) // SSM_HEAD_DIM
    same_head = brow == bcol

    for g in range(SSM_GROUPS):
        xg = xbc_ref[0, :, g * SSD_GW:(g + 1) * SSD_GW]
        y_ref[g] = dskip_ref[:, g * SSD_GW:(g + 1) * SSD_GW] * xg
        xdt_ref[g] = xg * _dot(dt_parts, ex_ref[g])
        cum_ref[g] = _dot(cum_parts, ex_ref[g])

    def chunk(n, carry):
        rows = pl.ds(pl.multiple_of(n * q, q), q)
        for g in range(SSM_GROUPS):
            cum_c = cum_ref[g, rows, :]
            cum_s = jnp.sum(jnp.where(eye_t, cum_c, 0.0), axis=0, keepdims=True)
            decay_l = jnp.where(causal_t, jnp.exp(cum_c - cum_s), 0.0)
            bc = xbc_ref[0, rows, SSD_B_OFF + g * SSM_STATE:SSD_B_OFF + (g + 1) * SSM_STATE].astype(BF16)
            cc = xbc_ref[0, rows, SSD_C_OFF + g * SSM_STATE:SSD_C_OFF + (g + 1) * SSM_STATE].astype(BF16)
            cb_t = _dot_nt(cc, jnp.concatenate([bc] * SSD_HPG, axis=0))
            mat = (cb_t * decay_l).astype(BF16)
            xdt_c = xdt_ref[g, rows, :]
            xdt_b = xdt_c.astype(BF16)
            y_diag = []
            for s in range(2):
                blk = xdt_b[:, s * half:(s + 1) * half]
                bd = jnp.where(same_head, jnp.concatenate([blk] * (half // q), axis=0), 0.0).astype(BF16)
                y_diag.append(_dot(mat[:, s * half:(s + 1) * half], bd))
            y = jnp.concatenate(y_diag, axis=1)
            cum_end = cum_c[q - 1:q, :]
            st = state_ref[g]
            y = y + _dot(cc, st.astype(BF16)) * jnp.exp(cum_c)
            x_end = (xdt_c * jnp.exp(cum_end - cum_c)).astype(BF16)
            state_ref[g] = st * jnp.exp(cum_end) + _dot_tn(bc, x_end)
            y_ref[g, rows, :] += y
        return carry

    lax.fori_loop(0, nchunk, chunk, 0)

    for g in range(SSM_GROUPS):
        cols = slice(g * SSD_GW, (g + 1) * SSD_GW)
        y = y_ref[g] * z_ref[0, :, cols]
        y = y * lax.rsqrt(jnp.mean(y * y, axis=-1, keepdims=True) + NORM_EPS) * nw_ref[:, cols]
        o_ref[0, :, cols] = y.astype(o_ref.dtype)


def _ssd(xbc, z, dt, a_log_pad, d_skip_x, norm_w, expand):
    bsz, seq, _ = xbc.shape
    rb = SSD_ROWS
    tile = lambda width: pl.BlockSpec((1, rb, width), lambda b, i: (b, i, 0))
    return pl.pallas_call(
        _ssd_kernel,
        grid=(bsz, seq // rb),
        in_specs=[tile(CONV_CH), tile(D_INNER), tile(LANES), _const_spec((1, LANES)),
                  _const_spec((1, D_INNER)), _const_spec((1, D_INNER)), _const_spec(expand.shape)],
        out_specs=tile(D_INNER),
        out_shape=jax.ShapeDtypeStruct((bsz, seq, D_INNER), BF16),
        scratch_shapes=[pltpu.VMEM((SSM_GROUPS, SSM_STATE, SSD_GW), F32),
                        pltpu.VMEM((SSM_GROUPS, rb, SSD_GW), F32), pltpu.VMEM((SSM_GROUPS, rb, SSD_GW), F32),
                        pltpu.VMEM((SSM_GROUPS, rb, SSD_GW), F32)],
        compiler_params=_cparams(("parallel", "arbitrary")),
        name="ssd",
    )(xbc, z, dt, a_log_pad, d_skip_x, norm_w[None, :], expand)


ROUTE_TM = 512


def _pack_bf16_pairs(a):
    w = a.shape[1] // 2
    bits = lax.bitcast_convert_type(a.astype(jnp.bfloat16).astype(F32), jnp.uint32)
    packed = bits[:, w:] | (bits[:, :w] >> 16)
    return lax.bitcast_convert_type(packed, jnp.int32)


def _unpack_bf16_pairs(p):
    bits = lax.bitcast_convert_type(p, jnp.uint32)
    lo = lax.bitcast_convert_type(bits << 16, F32)
    hi = lax.bitcast_convert_type(bits & jnp.uint32(0xFFFF0000), F32)
    return jnp.concatenate([lo, hi], axis=1).astype(BF16)


def _ssm_out_route_kernel(x_ref, y_ref, g1_ref, w_ref, nw_ref, shift_ref, scale_ref, r_ref,
                          x3_ref, h_ref, gw_ref, sel_ref, cnt_ref):
    x3 = x_ref[0] + g1_ref[0] * _dot(y_ref[0], w_ref[...])
    x3_ref[0] = x3
    h = _modulated_norm(x3, nw_ref[...], shift_ref[0], scale_ref[0])
    h_ref[0] = _pack_bf16_pairs(h)
    h_hi, h_lo = _split_bf16(h, 2)
    r_hi, r_lo = _split_bf16(r_ref[...], 2)
    both = _dot(h_hi, jnp.concatenate([r_hi, r_lo], axis=1))
    logits = both[:, :LANES] + (both[:, LANES:] + _dot(h_lo, r_hi))
    lane = lax.broadcasted_iota(jnp.int32, logits.shape, 1)
    logits = jnp.where(lane < N_EXPERTS, logits, -jnp.inf)
    m1 = jnp.max(logits, axis=-1, keepdims=True)
    i1 = jnp.min(jnp.where(logits == m1, lane, LANES), axis=-1, keepdims=True)
    rest = jnp.where(lane == i1, -jnp.inf, logits)
    m2 = jnp.max(rest, axis=-1, keepdims=True)
    i2 = jnp.min(jnp.where(rest == m2, lane, LANES), axis=-1, keepdims=True)
    e2 = jnp.exp(m2 - m1)
    w1 = 1.0 / (1.0 + e2)
    w2 = e2 / (1.0 + e2)
    chosen = (lane == i1) | (lane == i2)
    gw_ref[0] = jnp.where(lane == i1, w1, 0.0) + jnp.where(lane == i2, w2, 0.0)
    sel = jnp.where(chosen, 1.0, 0.0)
    sel_ref[0] = sel.astype(BF16)
    cnt_ref[0] = jnp.sum(sel, axis=0, keepdims=True)


def _ssm_out_route(x, y, mod1, w_out, norm_w, mod2, router_pad):
    bsz, seq, d = x.shape
    tm = ROUTE_TM
    tiles = seq // tm
    _, _, gate1 = _mod_specs(2)
    shift2, scale2, _ = _mod_specs(2)
    tile = lambda width: pl.BlockSpec((1, tm, width), lambda b, i: (b, i, 0))
    return pl.pallas_call(
        _ssm_out_route_kernel,
        grid=(bsz, tiles),
        in_specs=[tile(d), tile(D_INNER), gate1, _const_spec(w_out.shape), _const_spec((1, d)), shift2, scale2,
                  _const_spec(router_pad.shape)],
        out_specs=[tile(d), tile(d // 2), tile(LANES), tile(LANES),
                   pl.BlockSpec((1, 1, LANES), lambda b, i: (b * tiles + i, 0, 0))],
        out_shape=[jax.ShapeDtypeStruct((bsz, seq, d), F32), jax.ShapeDtypeStruct((bsz, seq, d // 2), jnp.int32),
                   jax.ShapeDtypeStruct((bsz, seq, LANES), F32), jax.ShapeDtypeStruct((bsz, seq, LANES), BF16),
                   jax.ShapeDtypeStruct((bsz * tiles, 1, LANES), F32)],
        compiler_params=_cparams(("parallel", "parallel")),
        name="ssm_out_route",
    )(x, y, mod1, w_out, norm_w[None, :], mod2, mod2, router_pad)


def _moe_plan_kernel(gw_ref, sel_ref, base_ref, pos_ref, wts_ref):
    tm = gw_ref.shape[0]
    gw = gw_ref[...]
    sel = sel_ref[...]
    lane = lax.broadcasted_iota(jnp.int32, (tm, LANES), 1)
    chosen = sel.astype(F32) > 0.5
    row = lax.broadcasted_iota(jnp.int32, (tm, tm), 0)
    col = lax.broadcasted_iota(jnp.int32, (tm, tm), 1)
    before = jnp.where(col < row, 1.0, 0.0).astype(BF16)
    rank = _dot(before, sel)
    dest = rank + base_ref[0]
    first = jnp.min(jnp.where(chosen, lane, LANES), axis=-1, keepdims=True)
    last = jnp.max(jnp.where(chosen, lane, -1), axis=-1, keepdims=True)
    ones = jnp.ones((8, LANES), BF16)
    for slot, pick in enumerate((lane == first, lane == last)):
        parts = _split_bf16(jnp.where(pick, dest, 0.0), 3)
        pos = _dot_nt(ones, parts[0]) + (_dot_nt(ones, parts[1]) + _dot_nt(ones, parts[2]))
        pos_ref[slot] = pos.astype(jnp.int32)
    w_lo = jnp.sum(jnp.where(lane == first, gw, 0.0), axis=-1, keepdims=True)
    w_hi = jnp.sum(jnp.where(lane == last, gw, 0.0), axis=-1, keepdims=True)
    wts_ref[...] = jnp.where(lane == 0, w_lo, jnp.where(lane == 1, w_hi, 0.0))


def _moe_plan(gw, sel, base):
    n = gw.shape[0]
    tm = ROUTE_TM
    return pl.pallas_call(
        _moe_plan_kernel,
        grid=(n // tm,),
        in_specs=[pl.BlockSpec((tm, LANES), lambda i: (i, 0)), pl.BlockSpec((tm, LANES), lambda i: (i, 0)),
                  pl.BlockSpec((1, 1, LANES), lambda i: (i, 0, 0))],
        out_specs=[pl.BlockSpec((2, 8, tm), lambda i: (0, 0, i)), pl.BlockSpec((tm, LANES), lambda i: (i, 0))],
        out_shape=[jax.ShapeDtypeStruct((2, 8, n), jnp.int32), jax.ShapeDtypeStruct((n, LANES), F32)],
        compiler_params=_cparams(("parallel",)),
        name="moe_plan",
    )(gw, sel, base)


SC_WINDOW = 128


SC_CORES = 2
SC_SUBCORES = 16


def _sc_workers():
    return SC_CORES, SC_CORES * SC_SUBCORES


def _sc_dispatch(rows, pos):
    n, width = rows.shape
    ncores, workers = _sc_workers()
    per_worker = n // workers
    steps = per_worker // SC_WINDOW
    mesh = plsc.VectorSubcoreMesh(core_axis_name="c", subcore_axis_name="s")

    @functools.partial(
        pl.kernel, mesh=mesh, out_type=jax.ShapeDtypeStruct((2 * n, width), rows.dtype),
        scratch_types=[pltpu.VMEM((2, steps, SC_WINDOW), jnp.int32), pltpu.VMEM((SC_WINDOW, width), rows.dtype)],
        name="moe_dispatch")
    def run(rows_hbm, pos_hbm, out_hbm, idx_v, rows_v):
        wid = lax.axis_index("s") * ncores + lax.axis_index("c")
        pltpu.sync_copy(pos_hbm.at[wid], idx_v)
        for j in range(steps):
            pltpu.sync_copy(rows_hbm.at[pl.ds(wid * per_worker + j * SC_WINDOW, SC_WINDOW)], rows_v)
            pltpu.sync_copy(rows_v, out_hbm.at[idx_v.at[0, j]])
            pltpu.sync_copy(rows_v, out_hbm.at[idx_v.at[1, j]])

    pos_w = pos.reshape(2, workers, steps, SC_WINDOW).transpose(1, 0, 2, 3)
    return run(rows, pos_w)


SC_GATHER_WINDOW = 128


def _sc_gather(table, pos):
    _, width = table.shape
    n = pos.shape[1]
    ncores, workers = _sc_workers()
    per_worker = n // workers
    win = SC_GATHER_WINDOW
    steps = per_worker // win
    mesh = plsc.VectorSubcoreMesh(core_axis_name="c", subcore_axis_name="s")

    @functools.partial(
        pl.kernel, mesh=mesh, out_type=jax.ShapeDtypeStruct((2, n, width), table.dtype),
        scratch_types=[pltpu.VMEM((2, steps, win), jnp.int32), pltpu.VMEM((win, width), table.dtype)],
        name="moe_gather")
    def run(table_hbm, pos_hbm, out_hbm, idx_v, rows_v):
        wid = lax.axis_index("s") * ncores + lax.axis_index("c")
        pltpu.sync_copy(pos_hbm.at[wid], idx_v)
        for s in range(2):
            for j in range(steps):
                pltpu.sync_copy(table_hbm.at[idx_v.at[s, j]], rows_v)
                pltpu.sync_copy(rows_v, out_hbm.at[s, pl.ds(wid * per_worker + j * win, win)])

    pos_w = pos.reshape(2, workers, steps, win).transpose(1, 0, 2, 3)
    return run(table, pos_w)


MOE_TM = 512
MOE_TF = 1024
MOE_VMEM_LIMIT = 62 * 1024 * 1024


def _moe_pairs(goff, n_rows):
    tiles = n_rows // MOE_TM
    steps = tiles + N_EXPERTS - 1
    first_row = jnp.arange(tiles, dtype=jnp.int32) * MOE_TM
    ends = goff[1:]
    e_lo = jnp.sum(first_row[:, None] >= ends[None, :], axis=1).astype(jnp.int32)
    e_hi = jnp.sum((first_row + (MOE_TM - 1))[:, None] >= ends[None, :], axis=1).astype(jnp.int32)
    count = e_hi - e_lo + 1
    start = jnp.cumsum(count) - count
    p = jnp.arange(steps, dtype=jnp.int32)
    tile = jnp.sum(start[None, :] <= p[:, None], axis=1).astype(jnp.int32) - 1
    expert = e_lo[tile] + (p - start[tile])
    valid = p < jnp.sum(count)
    return (jnp.where(valid, tile, tiles - 1).astype(jnp.int32),
            jnp.where(valid, expert, N_EXPERTS - 1).astype(jnp.int32), valid.astype(jnp.int32))


def _moe_group_kernel(tile_ref, exp_ref, valid_ref, goff_ref, x_ref, wg_ref, wu_ref, wd_ref, y_ref, act_ref):
    p = pl.program_id(0)
    tm = x_ref.shape[0]
    tile = tile_ref[p]
    expert = exp_ref[p]

    @pl.when((p == 0) | (tile != tile_ref[jnp.maximum(p - 1, 0)]))
    def _():
        y_ref[...] = jnp.zeros(y_ref.shape, y_ref.dtype)

    @pl.when(valid_ref[p] == 1)
    def _():
        x = _unpack_bf16_pairs(x_ref[...])
        for f, width in _column_chunks(FFN_EXPERT, MOE_TF):
            act_ref[:, f:f + width] = (_silu(_dot(x, wg_ref[0, :, f:f + width]))
                                       * _dot(x, wu_ref[0, :, f:f + width])).astype(BF16)
        out = _pack_bf16_pairs(_dot(act_ref[...], wd_ref[0]))
        rows = tile * tm + lax.broadcasted_iota(jnp.int32, (tm, 1), 0)
        mine = (rows >= goff_ref[expert]) & (rows < goff_ref[expert + 1])
        y_ref[...] = jnp.where(mine, out, y_ref[...])


def _moe_group(xs, goff, w_gu, wd):
    n_rows, half = xs.shape
    d = 2 * half
    tm = MOE_TM
    tile, expert, valid = _moe_pairs(goff, n_rows)

    def resident(shape, col_block=0):
        return pl.BlockSpec(shape, lambda p, t, e, v, g: (e[p], 0, col_block))

    grid_spec = pltpu.PrefetchScalarGridSpec(
        num_scalar_prefetch=4,
        grid=(tile.shape[0],),
        in_specs=[pl.BlockSpec((tm, half), lambda p, t, e, v, g: (t[p], 0)),
                  resident((1, d, FFN_EXPERT), 0), resident((1, d, FFN_EXPERT), 1), resident((1, FFN_EXPERT, d))],
        out_specs=pl.BlockSpec((tm, half), lambda p, t, e, v, g: (t[p], 0)),
        scratch_shapes=[pltpu.VMEM((tm, FFN_EXPERT), BF16)],
    )
    return pl.pallas_call(
        _moe_group_kernel,
        grid_spec=grid_spec,
        out_shape=jax.ShapeDtypeStruct((n_rows, half), jnp.int32),
        compiler_params=_cparams(("arbitrary",), MOE_VMEM_LIMIT),
        name="moe_group",
    )(tile, expert, valid, goff, xs, w_gu, w_gu, wd)


def _moe_combine_kernel(x_ref, yg_ref, wts_ref, g_ref, fn_ref, o_ref):
    w = wts_ref[0]
    mix = (w[:, 0:1] * _unpack_bf16_pairs(yg_ref[0, 0]).astype(F32)
           + w[:, 1:2] * _unpack_bf16_pairs(yg_ref[1, 0]).astype(F32))
    x4 = x_ref[0] + g_ref[0] * mix
    y = x4 * lax.rsqrt(jnp.mean(x4 * x4, axis=-1, keepdims=True) + NORM_EPS)
    o_ref[0] = y * fn_ref[...]


def _moe_combine(x, yg, wts, mod, final_norm):
    bsz, seq, d = x.shape
    tm = 512
    _, _, gate = _mod_specs(2)
    tile = lambda width: pl.BlockSpec((1, tm, width), lambda b, i: (b, i, 0))
    return pl.pallas_call(
        _moe_combine_kernel,
        grid=(bsz, seq // tm),
        in_specs=[tile(d), pl.BlockSpec((2, 1, tm, d // 2), lambda b, i: (0, b, i, 0)), tile(LANES), gate,
                  _const_spec((1, d))],
        out_specs=tile(d),
        out_shape=jax.ShapeDtypeStruct((bsz, seq, d), F32),
        compiler_params=_cparams(("parallel", "parallel")),
        name="moe_combine",
    )(x, yg, wts, mod, final_norm[None, :])


def _pad_cols(a, width):
    return jnp.pad(a, ((0, 0), (0, width - a.shape[1])))


def _split(a, sizes):
    return jnp.split(a, [int(s) for s in np.cumsum(sizes)[:-1]], axis=-1)


def _prep_hyb_w_in(w):
    q_a, k_a, v_a, lr_a, r_a, q_b, kv_b, g_b = _split(w, HYB_SPLITS)
    return jnp.concatenate([q_a, k_a, v_a, r_a, _pad_cols(lr_a, LANES), _pad_cols(g_b, LANES), q_b, kv_b],
                           axis=1).astype(BF16)


def _prep_cmp(pe, w1, w2):
    eye = jnp.eye(NSA_KV_GROUPS, dtype=F32)
    half = CMP_BLOCK // 2
    w1r = w1.reshape(2, 2, half, NSA_HEAD_DIM, CMP_HIDDEN)
    w1x = jnp.einsum("jstdc,gh->jstgdhc", w1r, eye).reshape(2, 2, half * LANES, NSA_KV_GROUPS * CMP_HIDDEN)
    w2x = jnp.einsum("jcd,gh->jgchd", w2, eye).reshape(2, NSA_KV_GROUPS * CMP_HIDDEN, LANES)
    per = pe.reshape(2, 2, half, 1, NSA_HEAD_DIM)
    pe_rows = jnp.broadcast_to(per, (2, 2, half, NSA_KV_GROUPS, NSA_HEAD_DIM)).reshape(2, 2, 1, half * LANES)
    return pe_rows, w1x.astype(BF16), w2x.astype(BF16)


def _overlap_matrix(seq):
    n_rows = seq // CMP_STRIDE
    n = np.arange(n_rows)[:, None] * CMP_STRIDE
    s = np.arange(LANES)[None, :] * SEL_BLOCK
    ov = (n < s + SEL_BLOCK) & (n + CMP_BLOCK - 1 >= s) & (np.arange(LANES)[None, :] < seq // SEL_BLOCK)
    return jnp.asarray(ov, BF16)


def _head_expand():
    ex = np.zeros((SSM_GROUPS, LANES, SSD_GW), np.float32)
    for g in range(SSM_GROUPS):
        for j in range(SSD_HPG):
            for part in range(SSD_SPLIT_PARTS):
                ex[g, part * SSM_HEADS + g * SSD_HPG + j, j * SSM_HEAD_DIM:(j + 1) * SSM_HEAD_DIM] = 1.0
    return jnp.asarray(ex, BF16)


def kernel(x, c, hyb_norm, hyb_mod_w, hyb_mod_b, hyb_w_in, gla_gk_up, gla_gk_bias, gla_out_norm, nsa_cmp_pe, nsa_cmp_w1, nsa_cmp_w2, hyb_w_out, dense_norm, dense_mod_w, dense_mod_b, dense_w_gu, dense_w_down, ssm_norm, ssm_mod_w, ssm_mod_b, ssm_w_in, ssm_conv_w, ssm_conv_b, ssm_dt_bias, ssm_a_log, ssm_d, ssm_gate_norm, ssm_w_out, moe_norm, moe_mod_w, moe_mod_b, moe_router, moe_w_gu, moe_w_down, final_norm):
    bsz, seq, d = x.shape
    mods = _adaln(c, (hyb_mod_w[0], dense_mod_w[0], ssm_mod_w[0], moe_mod_w[0]),
                  (hyb_mod_b[0], dense_mod_b[0], ssm_mod_b[0], moe_mod_b[0]))
    mods = mods.reshape(4, bsz, 1, 3 * d)

    gkup_pad = jnp.pad(gla_gk_up[0], ((0, LANES - GLA_LOWRANK), (0, 0))).astype(BF16)
    (gq, gk, la, gv, gr, nq, kcmp, vcmp, kslc, vslc, kwin, vwin, ng) = _hyb_in(
        x, hyb_norm[0], mods[0], _prep_hyb_w_in(hyb_w_in[0]), gkup_pad, gla_gk_bias[0])
    o_gla = _gla(gq, gk, la, gv, gr, gla_out_norm[0])
    pe_rows, w1x, w2x = _prep_cmp(nsa_cmp_pe[0], nsa_cmp_w1[0], nsa_cmp_w2[0])
    kc, vc = _nsa_compress(kcmp, vcmp, pe_rows, w1x, w2x)
    o_nsa = _nsa_attn(nq, ng, kc, vc, kslc, vslc, kwin, vwin, _overlap_matrix(seq))
    x = _hyb_out_ffn(x, o_gla, o_nsa, mods[0], hyb_w_out[0].astype(BF16), dense_norm[0], mods[1],
                     dense_w_gu[0].astype(BF16), dense_w_down[0].astype(BF16))

    wz, wxbc, wdt = _split(ssm_w_in[0], SSM_SPLITS)
    w_ssm = jnp.concatenate([wz, wxbc, _pad_cols(wdt, LANES)], axis=1).astype(BF16)
    pad_heads = lambda a: jnp.pad(a, (0, LANES - SSM_HEADS))[None, :]
    z, xbc, dt = _ssm_in(x, ssm_norm[0], mods[2], w_ssm, ssm_conv_w[0], ssm_conv_b[0], pad_heads(ssm_dt_bias[0]))
    y = _ssd(xbc, z, dt, pad_heads(ssm_a_log[0]), jnp.repeat(ssm_d[0], SSM_HEAD_DIM)[None, :],
             ssm_gate_norm[0], _head_expand())
    x3, h4, gw, sel, cnt = _ssm_out_route(x, y, mods[2], ssm_w_out[0].astype(BF16), moe_norm[0], mods[3],
                                          _pad_cols(moe_router[0], LANES))
    n = bsz * seq
    cnt = cnt[:, 0, :]
    totals = jnp.sum(cnt, axis=0)
    goff_f = jnp.cumsum(totals) - totals
    base = (jnp.cumsum(cnt, axis=0) - cnt + goff_f[None, :])[:, None, :]
    goff = jnp.concatenate([goff_f[:N_EXPERTS], jnp.full((1,), 2.0 * n, F32)]).astype(jnp.int32)
    pos, wts = _moe_plan(gw.reshape(n, LANES), sel.reshape(n, LANES), base)
    pos = pos[:, 0, :]
    xs = _sc_dispatch(h4.reshape(n, d // 2), pos)
    ys = _moe_group(xs, goff, moe_w_gu[0].astype(BF16), moe_w_down[0].astype(BF16))
    yg = _sc_gather(ys, pos)
    return _moe_combine(x3, yg.reshape(2, bsz, seq, d // 2), wts.reshape(bsz, seq, LANES), mods[3], final_norm)
```

```python
import functools

import jax
import jax.numpy as jnp
import numpy as np
from jax import lax
from jax.experimental import pallas as pl
from jax.experimental.pallas import tpu as pltpu
from jax.experimental.pallas import tpu_sc as plsc

F32 = jnp.float32
BF16 = jnp.bfloat16

D_MODEL = 1024
NORM_EPS = 1e-6
GLA_HEADS = 4
GLA_DV = D_MODEL // 8
GLA_DK = GLA_DV // 2
GLA_LOWRANK = 16
GLA_TAU = 16.0
GLA_CHUNK = 64
NSA_HEADS = 8
NSA_KV_GROUPS = 2
NSA_HEAD_DIM = D_MODEL // 16
CMP_BLOCK = 32
CMP_STRIDE = 16
CMP_HIDDEN = 4 * NSA_HEAD_DIM
SEL_BLOCK = 64
N_SELECT = 8
WINDOW = 512
D_INNER = 2 * D_MODEL
SSM_HEAD_DIM = 64
SSM_HEADS = D_INNER // SSM_HEAD_DIM
SSM_GROUPS = 4
SSM_STATE = 128
CONV_K = 4
SSM_CHUNK = 64
CONV_CH = D_INNER + 2 * SSM_GROUPS * SSM_STATE
FFN_DENSE = ((8 * D_MODEL // 3 + 127) // 128) * 128
N_EXPERTS = 8
FFN_EXPERT = 7 * D_MODEL // 2
HYB_SPLITS = (GLA_HEADS * GLA_DK, GLA_HEADS * GLA_DK, GLA_HEADS * GLA_DV, GLA_LOWRANK, GLA_HEADS * GLA_DV,
              NSA_HEADS * NSA_HEAD_DIM, 6 * NSA_KV_GROUPS * NSA_HEAD_DIM, 3 * NSA_HEADS)
SSM_SPLITS = (D_INNER, CONV_CH, SSM_HEADS)

LANES = 128
VMEM_LIMIT = 56 * 1024 * 1024


def _cparams(sem, vmem_limit=VMEM_LIMIT):
    return pltpu.CompilerParams(dimension_semantics=sem, vmem_limit_bytes=vmem_limit)


def _dot(a, b):
    return jnp.dot(a, b, preferred_element_type=F32)


def _dot_nt(a, b):
    return lax.dot_general(a, b, (((1,), (1,)), ((), ())), preferred_element_type=F32)


def _dot_tn(a, b):
    return lax.dot_general(a, b, (((0,), (0,)), ((), ())), preferred_element_type=F32)


def _split_bf16(a, parts):
    out = []
    r = a
    for _ in range(parts):
        p = r.astype(BF16)
        out.append(p)
        r = r - p.astype(F32)
    return out


def _dot_exact_lhs(m, a, parts=3):
    width = a.shape[1]
    wide = _dot(m, jnp.concatenate(_split_bf16(a, parts), axis=1))
    acc = wide[:, :width]
    for i in range(1, parts):
        acc = acc + wide[:, i * width:(i + 1) * width]
    return acc


def _dot_exact_rhs(a, m, parts=3):
    acc = None
    for p in _split_bf16(a, parts):
        t = _dot(p, m)
        acc = t if acc is None else acc + t
    return acc


def _sigmoid(x):
    return 1.0 / (1.0 + jnp.exp(-x))


def _silu(x):
    return x * _sigmoid(x)


def _softplus(x):
    return jnp.maximum(x, 0.0) + jnp.log(1.0 + jnp.exp(-jnp.abs(x)))


def _gelu_tanh(x):
    return x * (0.5 * (1.0 + jnp.tanh(0.7978845608028654 * (x + 0.044715 * (x * x * x)))))


def _modulated_norm(x, nw, shift, scale):
    ms = jnp.mean(x * x, axis=-1, keepdims=True)
    y = x * lax.rsqrt(ms + NORM_EPS)
    return (y * nw) * (1.0 + scale) + shift


def _const_spec(shape):
    nd = len(shape)
    return pl.BlockSpec(shape, lambda *_: (0,) * nd, pipeline_mode=pl.Buffered(1))


def _adaln_kernel(c_ref, w0, w1, w2, w3, b0, b1, b2, b3, o_ref):
    sc = _silu(c_ref[...]).astype(BF16)
    for i, (w, b) in enumerate(((w0, b0), (w1, b1), (w2, b2), (w3, b3))):
        o_ref[i] = _dot(sc, w[...].astype(BF16)) + b[...]


def _adaln(c, ws, bs):
    bsz, d = c.shape
    n = ws[0].shape[1]
    tn = 512
    w_spec = pl.BlockSpec((d, tn), lambda j: (0, j))
    b_spec = pl.BlockSpec((1, tn), lambda j: (0, j))
    return pl.pallas_call(
        _adaln_kernel,
        grid=(n // tn,),
        in_specs=[pl.BlockSpec((bsz, d), lambda j: (0, 0))] + [w_spec] * 4 + [b_spec] * 4,
        out_specs=pl.BlockSpec((4, bsz, tn), lambda j: (0, 0, j)),
        out_shape=jax.ShapeDtypeStruct((4, bsz, n), F32),
        compiler_params=_cparams(("arbitrary",)),
        name="adaln",
    )(c, *ws, *[b[None, :] for b in bs])


def _mod_specs(tile_axes):
    def spec(part):
        if tile_axes == 2:
            return pl.BlockSpec((1, 1, D_MODEL), lambda b, i: (b, 0, part))
        return pl.BlockSpec((1, 1, D_MODEL), lambda b, i, j, k: (b, 0, part))
    return spec(0), spec(1), spec(2)


HYB_SEG = {
    "gq": (0, 256), "gk": (256, 256), "gv": (512, 512), "gr": (1024, 512), "lr_ng": (1536, 256),
    "nq": (1792, 512), "kv": (2304, 768),
}


def _hyb_in_kernel(x_ref, nw_ref, shift_ref, scale_ref, w_ref, gkup_ref, gkb_ref,
                   gq_ref, gk_ref, la_ref, gv_ref, gr_ref, nq_ref,
                   kcmp_ref, vcmp_ref, kslc_ref, vslc_ref, kwin_ref, vwin_ref, ng_ref):
    h = _modulated_norm(x_ref[0], nw_ref[...], shift_ref[0], scale_ref[0]).astype(BF16)

    def proj(name):
        off, width = HYB_SEG[name]
        return _dot(h, w_ref[:, off:off + width])

    def proj_pair(name, pair=0):
        off = HYB_SEG[name][0] + pair * 2 * LANES
        y = _dot(h, w_ref[:, off:off + 2 * LANES])
        return y[:, :LANES], y[:, LANES:]

    gq_ref[0] = proj("gq")
    gk_ref[0] = proj("gk")
    gv_ref[0] = proj("gv").astype(BF16)
    gr_ref[0] = proj("gr")
    lr, ng = proj_pair("lr_ng")
    z = _dot(lr.astype(BF16), gkup_ref[...]) + gkb_ref[...]
    la_ref[0] = -_softplus(-z) * (1.0 / GLA_TAU)
    ng_ref[0] = ng
    nq_ref[0] = proj("nq").astype(BF16)
    kcmp_ref[0], vcmp_ref[0] = proj_pair("kv", 0)
    kslc, vslc = proj_pair("kv", 1)
    kslc_ref[0] = kslc.T.astype(BF16)
    vslc_ref[0] = vslc.astype(BF16)
    kwin, vwin = proj_pair("kv", 2)
    kwin_ref[0] = kwin.T.astype(BF16)
    vwin_ref[0] = vwin.astype(BF16)


def _hyb_in(x, norm_w, mod, w_perm, gkup_pad, gk_bias):
    bsz, seq, d = x.shape
    tm = 512
    shift_s, scale_s, _ = _mod_specs(2)

    def out(width, dtype):
        return (pl.BlockSpec((1, tm, width), lambda b, i: (b, i, 0)),
                jax.ShapeDtypeStruct((bsz, seq, width), dtype))

    def out_t(dtype):
        return (pl.BlockSpec((1, LANES, tm), lambda b, i: (b, 0, i)),
                jax.ShapeDtypeStruct((bsz, LANES, seq), dtype))

    outs = [out(256, F32), out(256, F32), out(256, F32), out(512, BF16), out(512, F32), out(512, BF16),
            out(128, F32), out(128, F32), out_t(BF16), out(128, BF16), out_t(BF16), out(128, BF16),
            out(128, F32)]
    return pl.pallas_call(
        _hyb_in_kernel,
        grid=(bsz, seq // tm),
        in_specs=[pl.BlockSpec((1, tm, d), lambda b, i: (b, i, 0)),
                  _const_spec((1, d)), shift_s, scale_s,
                  _const_spec(w_perm.shape), _const_spec(gkup_pad.shape), _const_spec((1, 256))],
        out_specs=[o[0] for o in outs],
        out_shape=[o[1] for o in outs],
        compiler_params=_cparams(("parallel", "parallel")),
        name="hyb_in",
    )(x, norm_w[None, :], mod, mod, w_perm, gkup_pad, gk_bias[None, :])


GLA_ROWS = 512


def _gla_kernel(q_ref, k_ref, la_ref, v_ref, r_ref, nw_ref, o_ref, *, seq):
    rb = GLA_ROWS
    ncb = rb // GLA_CHUNK
    row = lax.broadcasted_iota(jnp.int32, (rb, rb), 0)
    col = lax.broadcasted_iota(jnp.int32, (rb, rb), 1)
    chunk_causal = ((row // GLA_CHUNK) == (col // GLA_CHUNK)) & (col <= row)
    tril_bd = jnp.where(chunk_causal, 1.0, 0.0).astype(BF16)
    lane = lax.broadcasted_iota(jnp.int32, (rb, LANES), 1)
    head_mask = (lane < GLA_DK, lane >= GLA_DK)
    nw = nw_ref[...]

    def body(i, st):
        r0 = pl.multiple_of(i * rb, rb)
        rows = pl.ds(r0, rb)
        q = q_ref[0, rows, :]
        k = k_ref[0, rows, :]
        la = la_ref[0, rows, :]
        b = _dot_exact_lhs(tril_bd, la)
        b3 = b.reshape(ncb, GLA_CHUNK, LANES)
        bend3 = b3[:, GLA_CHUNK - 1:GLA_CHUNK, :]
        bend = jnp.broadcast_to(bend3, (ncb, GLA_CHUNK, LANES)).reshape(rb, LANES)
        q_dec = (q * GLA_DK ** -0.5) * jnp.exp(b)
        k_dec = (k * jnp.exp(-b)).astype(BF16)
        k_end = k * jnp.exp(bend - b)
        decay = jnp.exp(bend3)

        qm, vh, o_intra, upd = [], [], [], []
        for h in range(2):
            qm_h = jnp.where(head_mask[h], q_dec, 0.0).astype(BF16)
            att = _dot_nt(qm_h, k_dec)
            att = jnp.where(chunk_causal, att, 0.0).astype(BF16)
            v_h = v_ref[0, rows, h * GLA_DV:(h + 1) * GLA_DV]
            o_intra.append(_dot(att, v_h))
            km_h = jnp.where(head_mask[h], k_end, 0.0).astype(BF16)
            upd.append([_dot_tn(v_h[n * GLA_CHUNK:(n + 1) * GLA_CHUNK], km_h[n * GLA_CHUNK:(n + 1) * GLA_CHUNK])
                        for n in range(ncb)])
            qm.append(qm_h)
            vh.append(v_h)

        prev = []
        for n in range(ncb):
            prev.append(st.astype(BF16))
            st = st * decay[n] + (upd[0][n] + upd[1][n])

        for h in range(2):
            o_inter = jnp.concatenate(
                [_dot_nt(qm[h][n * GLA_CHUNK:(n + 1) * GLA_CHUNK], prev[n]) for n in range(ncb)], axis=0)
            o = o_intra[h] + o_inter
            y = o * lax.rsqrt(jnp.mean(o * o, axis=-1, keepdims=True) + NORM_EPS) * nw
            r = r_ref[0, rows, h * GLA_DV:(h + 1) * GLA_DV]
            o_ref[0, rows, h * GLA_DV:(h + 1) * GLA_DV] = (y * _silu(r)).astype(o_ref.dtype)
        return st

    lax.fori_loop(0, seq // rb, body, jnp.zeros((GLA_DV, 2 * GLA_DK), F32))


def _gla(gq, gk, la, gv, gr, out_norm):
    bsz, seq, _ = gq.shape
    qk_spec = pl.BlockSpec((1, seq, 2 * GLA_DK), lambda b, p: (b, 0, p))
    v_spec = pl.BlockSpec((1, seq, 2 * GLA_DV), lambda b, p: (b, 0, p))
    return pl.pallas_call(
        functools.partial(_gla_kernel, seq=seq),
        grid=(bsz, GLA_HEADS // 2),
        in_specs=[qk_spec, qk_spec, qk_spec, v_spec, v_spec, _const_spec((1, GLA_DV))],
        out_specs=v_spec,
        out_shape=jax.ShapeDtypeStruct((bsz, seq, GLA_HEADS * GLA_DV), BF16),
        compiler_params=_cparams(("parallel", "parallel")),
        name="gla",
    )(gq, gk, la, gv, gr, out_norm[None, :])


def _nsa_compress_kernel(k_ref, v_ref, pe_ref, w1_ref, w2_ref, kc_ref, vc_ref, *, n_cmp):
    rows = k_ref.shape[1]
    rid = lax.broadcasted_iota(jnp.int32, (rows, LANES), 0)
    for j, (src, dst) in enumerate(((k_ref, kc_ref), (v_ref, vc_ref))):
        x = src[0]
        lo = _dot((x + pe_ref[j, 0]).astype(BF16), w1_ref[j, 0])
        hi = _dot((x + pe_ref[j, 1]).astype(BF16), w1_ref[j, 1])
        hpre = lo + pltpu.roll(hi, rows - 1, 0)
        out = jnp.where(rid < n_cmp, _dot(_gelu_tanh(hpre).astype(BF16), w2_ref[j]), 0.0)
        dst[0] = (out.T if j == 0 else out).astype(dst.dtype)


def _nsa_compress(kcmp, vcmp, pe_rows, w1x, w2x):
    bsz, seq, _ = kcmp.shape
    rows = seq // CMP_STRIDE
    width = CMP_STRIDE * LANES
    n_cmp = (seq - CMP_BLOCK) // CMP_STRIDE + 1
    x_spec = pl.BlockSpec((1, rows, width), lambda b: (b, 0, 0))
    return pl.pallas_call(
        functools.partial(_nsa_compress_kernel, n_cmp=n_cmp),
        grid=(bsz,),
        in_specs=[x_spec, x_spec, _const_spec(pe_rows.shape), _const_spec(w1x.shape), _const_spec(w2x.shape)],
        out_specs=[pl.BlockSpec((1, LANES, rows), lambda b: (b, 0, 0)),
                   pl.BlockSpec((1, rows, LANES), lambda b: (b, 0, 0))],
        out_shape=[jax.ShapeDtypeStruct((bsz, LANES, rows), BF16), jax.ShapeDtypeStruct((bsz, rows, LANES), BF16)],
        compiler_params=_cparams(("parallel",)),
        name="nsa_compress",
    )(kcmp.reshape(bsz, rows, width), vcmp.reshape(bsz, rows, width), pe_rows, w1x, w2x)


NSA_QT = 256
NSA_KT = 512
NSA_ROWS = NSA_HEADS * NSA_QT


NSA_MASK_PENALTY = 1e30
NSA_LANE_POS_HI, NSA_LANE_POS_LO, NSA_LANE_CMP_IDX, NSA_LANE_CMP_ONE = 64, 65, 66, 67


def _nsa_attn_kernel(q_ref, g_ref, kc_ref, vc_ref, ks_ref, vs_ref, kw_ref, vw_ref, ov_ref, kf_ref, cf_ref, o_ref,
                     m_ref, acc_ref, *, seq):
    qt, kt = NSA_QT, NSA_KT
    heads_per_group = NSA_HEADS // NSA_KV_GROUPS
    q0 = pl.program_id(1) * qt
    t = q0 + lax.broadcasted_iota(jnp.int32, (qt, 1), 0)
    lane = lax.broadcasted_iota(jnp.int32, (qt, LANES), 1)
    group_lanes = (lane < NSA_HEAD_DIM, lane >= NSA_HEAD_DIM)
    slopes = [2.0 ** (-(h + 1)) for h in range(NSA_HEADS)]

    qa = q_ref[0].astype(F32) * NSA_HEAD_DIM ** -0.5
    q_rows = []
    for h in range(NSA_HEADS):
        g = h // heads_per_group
        blk = qa[:, (h // 2) * LANES:(h // 2 + 1) * LANES]
        if h % 2 != g:
            blk = pltpu.roll(blk, NSA_HEAD_DIM, 1)
        q_rows.append(jnp.where(group_lanes[g], blk, 0.0).astype(BF16))
    q_ext = jnp.concatenate(q_rows, axis=0)

    def alibi_lanes(h):
        s = slopes[h]
        return jnp.where(lane == NSA_LANE_POS_HI, SEL_BLOCK * s,
                         jnp.where(lane == NSA_LANE_POS_LO, s,
                                   jnp.where(lane == NSA_LANE_CMP_IDX, CMP_STRIDE * s,
                                             jnp.where(lane == NSA_LANE_CMP_ONE, (CMP_BLOCK - 1) / 2.0 * s, 0.0))))

    q_pos = [alibi_lanes(h) for h in range(NSA_HEADS)]
    q_full = jnp.concatenate([q_ext, jnp.concatenate(q_pos, axis=0).astype(BF16)], axis=1)

    def softmax_parts(s, mask):
        s = jnp.where(mask, s, -jnp.inf)
        m = jnp.max(s, axis=-1, keepdims=True)
        m = jnp.where(m == -jnp.inf, 0.0, m)
        p = jnp.exp(s - m)
        return p, jnp.sum(p, axis=-1, keepdims=True)

    n_rows = kc_ref.shape[2]
    cidx = lax.broadcasted_iota(jnp.int32, (1, n_rows), 1)
    cmp_mask = cidx * CMP_STRIDE + (CMP_BLOCK - 1) <= t
    s_all = _dot(q_full, jnp.concatenate([kc_ref[0], cf_ref[...]], axis=0))
    p_list, psum = [], [None] * NSA_KV_GROUPS
    for h in range(NSA_HEADS):
        g = h // heads_per_group
        p, l = softmax_parts(s_all[h * qt:(h + 1) * qt], cmp_mask)
        p = p / jnp.maximum(l, 1e-20)
        p_list.append(p.astype(BF16))
        psum[g] = p if psum[g] is None else psum[g] + p
    o_cmp = _dot(jnp.concatenate(p_list, axis=0), vc_ref[0])

    n_blk = LANES // 2
    tq = q0 + lax.broadcasted_iota(jnp.int32, (n_blk, qt), 1)
    bidx = lax.broadcasted_iota(jnp.int32, (n_blk, qt), 0)
    bidx_f = bidx.astype(F32)
    blk_t = tq // SEL_BLOCK
    forced = (bidx == 0) | (bidx == blk_t) | (bidx == blk_t - 1)
    future = bidx * SEL_BLOCK > tq
    q_sel_rows = []
    for g in range(NSA_KV_GROUPS):
        imp = _dot_exact_rhs(psum[g], ov_ref[...]).T[:n_blk]
        v = jnp.where(future, -1.0, jnp.where(forced, 3e38, imp))
        chosen = jnp.zeros((n_blk, qt), F32)
        for _ in range(min(N_SELECT, seq // SEL_BLOCK)):
            m = jnp.max(v, axis=0, keepdims=True)
            idx = jnp.min(jnp.where(v == m, bidx_f, float(LANES)), axis=0, keepdims=True)
            pick = (bidx_f == idx) & (m >= 0.0)
            chosen = jnp.where(pick, 1.0, chosen)
            v = jnp.where(pick, -1.0, v)
        penalty = jnp.concatenate([(chosen - 1.0) * NSA_MASK_PENALTY, jnp.zeros((n_blk, qt), F32)], axis=0).T
        for r in range(heads_per_group):
            q_sel_rows.append((penalty + q_pos[g * heads_per_group + r]).astype(BF16))
    q_full_sel = jnp.concatenate([q_ext, jnp.concatenate(q_sel_rows, axis=0)], axis=1)

    vlane = lax.broadcasted_iota(jnp.int32, (1, LANES), 1)
    own_lanes = (vlane < NSA_HEAD_DIM, vlane >= NSA_HEAD_DIM)

    def weighted_values(p_rows, v):
        v = v.astype(F32)
        outs = []
        for g in range(NSA_KV_GROUPS):
            pg = jnp.concatenate(p_rows[g * heads_per_group:(g + 1) * heads_per_group], axis=0)
            outs.append(_dot(pg, jnp.where(own_lanes[g], v, 1.0).astype(BF16)))
        return jnp.concatenate(outs, axis=0)

    def normalised(acc):
        return acc / jnp.maximum(pltpu.roll(acc, NSA_HEAD_DIM, 1), 1e-20)

    m_ref[...] = jnp.full(m_ref.shape, -jnp.inf, F32)
    acc_ref[...] = jnp.zeros(acc_ref.shape, F32)
    def sel_tile(k0, width, causal):
        ks = jnp.concatenate([ks_ref[0, :, pl.ds(k0, width)], kf_ref[:, pl.ds(k0, width)]], axis=0)
        s_t = _dot(q_full_sel, ks)
        visible = (k0 + lax.broadcasted_iota(jnp.int32, (1, width), 1)) <= t
        m_all = m_ref[...]
        p_rows, m_rows, alpha_rows = [], [], []
        for h in range(NSA_HEADS):
            rows = slice(h * qt, (h + 1) * qt)
            s = s_t[rows]
            if causal:
                s = jnp.where(visible, s, -jnp.inf)
            m_old = m_all[rows]
            m_new = jnp.maximum(m_old, jnp.max(s, axis=-1, keepdims=True))
            m_safe = jnp.where(m_new == -jnp.inf, 0.0, m_new)
            p_rows.append(jnp.exp(s - jnp.concatenate([m_safe] * (width // LANES), axis=1)).astype(BF16))
            alpha_rows.append(jnp.exp(m_old - m_safe))
            m_rows.append(m_new)
        m_ref[...] = jnp.concatenate(m_rows, axis=0)
        acc_ref[...] = (jnp.concatenate(alpha_rows, axis=0) * acc_ref[...]
                        + weighted_values(p_rows, vs_ref[0, pl.ds(k0, width), :]))

    def sel_past_tile(j, carry):
        sel_tile(pl.multiple_of(j * kt, kt), kt, False)
        return carry

    last = q0 // kt
    lax.fori_loop(0, last, sel_past_tile, 0)
    sel_tile(pl.multiple_of(last * kt, kt), kt, True)
    o_slc = normalised(acc_ref[...])

    wk = WINDOW + qt
    w0 = pl.multiple_of(jnp.maximum(q0 - WINDOW, 0), qt)
    kw = jnp.concatenate([kw_ref[0, :, pl.ds(w0, wk)], kf_ref[:, pl.ds(w0, wk)]], axis=0)
    wdist = t - (w0 + lax.broadcasted_iota(jnp.int32, (1, wk), 1))
    wmask = (wdist >= 0) & (wdist < WINDOW)
    s_all = _dot(q_full, kw)
    p_rows = []
    for h in range(NSA_HEADS):
        s = jnp.where(wmask, s_all[h * qt:(h + 1) * qt], -jnp.inf)
        m = jnp.max(s, axis=-1, keepdims=True)
        p_rows.append(jnp.exp(s - jnp.where(m == -jnp.inf, 0.0, m)).astype(BF16))
    o_win = normalised(weighted_values(p_rows, vw_ref[0, pl.ds(w0, wk), :]))

    gates = _sigmoid(g_ref[0])
    o_heads = []
    for h in range(NSA_HEADS):
        rows = slice(h * qt, (h + 1) * qt)
        o_heads.append(gates[:, 3 * h:3 * h + 1] * o_cmp[rows] + gates[:, 3 * h + 1:3 * h + 2] * o_slc[rows]
                       + gates[:, 3 * h + 2:3 * h + 3] * o_win[rows])
    for c in range(NSA_HEADS // 2):
        g = (2 * c) // heads_per_group
        a, b = o_heads[2 * c], o_heads[2 * c + 1]
        if g == 0:
            blk = jnp.where(group_lanes[0], a, pltpu.roll(b, NSA_HEAD_DIM, 1))
        else:
            blk = jnp.where(group_lanes[0], pltpu.roll(a, NSA_HEAD_DIM, 1), b)
        o_ref[0, :, c * LANES:(c + 1) * LANES] = blk.astype(o_ref.dtype)


def _nsa_position_features(seq):
    assert seq // SEL_BLOCK <= LANES // 2
    pos = np.arange(seq)
    kf = np.zeros((seq, LANES), np.float32)
    kf[pos, pos // SEL_BLOCK] = 1.0
    kf[:, NSA_LANE_POS_HI] = pos // SEL_BLOCK
    kf[:, NSA_LANE_POS_LO] = pos % SEL_BLOCK
    n_rows = seq // CMP_STRIDE
    cf = np.zeros((n_rows, LANES), np.float32)
    cf[:, NSA_LANE_CMP_IDX] = np.arange(n_rows)
    cf[:, NSA_LANE_CMP_ONE] = 1.0
    return jnp.asarray(kf.T, BF16), jnp.asarray(cf.T, BF16)


def _nsa_attn(nq, ng, kc, vc, kslc, vslc, kwin, vwin, overlap):
    bsz, seq, _ = nq.shape
    n_rows = vc.shape[1]
    kfeat, cfeat = _nsa_position_features(seq)
    full = lambda rows: pl.BlockSpec((1, rows, LANES), lambda b, i: (b, 0, 0))
    full_t = lambda cols: pl.BlockSpec((1, LANES, cols), lambda b, i: (b, 0, 0))
    return pl.pallas_call(
        functools.partial(_nsa_attn_kernel, seq=seq),
        grid=(bsz, seq // NSA_QT),
        in_specs=[pl.BlockSpec((1, NSA_QT, NSA_HEADS * NSA_HEAD_DIM), lambda b, i: (b, i, 0)),
                  pl.BlockSpec((1, NSA_QT, LANES), lambda b, i: (b, i, 0)),
                  full_t(n_rows), full(n_rows), full_t(seq), full(seq), full_t(seq), full(seq),
                  _const_spec(overlap.shape), _const_spec(kfeat.shape), _const_spec(cfeat.shape)],
        out_specs=pl.BlockSpec((1, NSA_QT, NSA_HEADS * NSA_HEAD_DIM), lambda b, i: (b, i, 0)),
        out_shape=jax.ShapeDtypeStruct((bsz, seq, NSA_HEADS * NSA_HEAD_DIM), BF16),
        scratch_shapes=[pltpu.VMEM((NSA_ROWS, LANES), F32), pltpu.VMEM((NSA_ROWS, LANES), F32)],
        compiler_params=_cparams(("parallel", "arbitrary")),
        name="nsa_attn",
    )(nq, ng, kc, vc, kslc, vslc, kwin, vwin, overlap, kfeat, cfeat)


MXU_COLS = 256


def _column_chunks(total, chunk):
    assert total % MXU_COLS == 0 and chunk % MXU_COLS == 0
    return [(f, min(chunk, total - f)) for f in range(0, total, chunk)]


FFN_CHUNK = 1536


def _hyb_out_ffn_kernel(x_ref, oa_ref, ob_ref, g1_ref, wo_ref,
                        nw_ref, shift_ref, scale_ref, g2_ref, wgu_ref, wd_ref, o_ref, act_ref):
    n_a = oa_ref.shape[2]
    mix = _dot(oa_ref[0], wo_ref[:n_a, :]) + _dot(ob_ref[0], wo_ref[n_a:, :])
    x1 = x_ref[0] + g1_ref[0] * mix
    h = _modulated_norm(x1, nw_ref[...], shift_ref[0], scale_ref[0]).astype(BF16)
    for f, width in _column_chunks(FFN_DENSE, FFN_CHUNK):
        gate = _dot(h, wgu_ref[:, f:f + width])
        up = _dot(h, wgu_ref[:, FFN_DENSE + f:FFN_DENSE + f + width])
        act_ref[:, f:f + width] = (_silu(gate) * up).astype(BF16)
    o_ref[0] = x1 + g2_ref[0] * _dot(act_ref[...], wd_ref[...])


def _hyb_out_ffn(x, o_gla, o_nsa, mod1, w_out, norm_w, mod2, w_gu, wd):
    bsz, seq, d = x.shape
    tm = 1024
    _, _, gate1 = _mod_specs(2)
    shift2, scale2, gate2 = _mod_specs(2)
    tile = lambda width: pl.BlockSpec((1, tm, width), lambda b, i: (b, i, 0))
    return pl.pallas_call(
        _hyb_out_ffn_kernel,
        grid=(bsz, seq // tm),
        in_specs=[tile(d), tile(o_gla.shape[-1]), tile(o_nsa.shape[-1]), gate1,
                  _const_spec(w_out.shape), _const_spec((1, d)), shift2, scale2, gate2,
                  _const_spec(w_gu.shape), _const_spec(wd.shape)],
        out_specs=tile(d),
        out_shape=jax.ShapeDtypeStruct((bsz, seq, d), F32),
        scratch_shapes=[pltpu.VMEM((tm, FFN_DENSE), BF16)],
        compiler_params=_cparams(("parallel", "parallel"), MOE_VMEM_LIMIT),
        name="hyb_out_ffn",
    )(x, o_gla, o_nsa, mod1, w_out, norm_w[None, :], mod2, mod2, mod2, w_gu, wd)


SSM_DT_OFF = D_INNER + CONV_CH
SSM_COLS = SSM_DT_OFF + LANES
SSM_IN_CHUNK = 256


CONV_TAIL = 8


def _ssm_in_kernel(x_ref, nw_ref, shift_ref, scale_ref, w_ref, cw_ref, cb_ref, dtb_ref,
                   z_ref, xbc_ref, dt_ref, tail_ref):
    tm = x_ref.shape[1]

    @pl.when(pl.program_id(1) == 0)
    def _():
        tail_ref[...] = jnp.zeros(tail_ref.shape, F32)

    h = _modulated_norm(x_ref[0], nw_ref[...], shift_ref[0], scale_ref[0]).astype(BF16)
    for c in range(0, D_INNER, SSM_IN_CHUNK):
        z_ref[0, :, c:c + SSM_IN_CHUNK] = _silu(_dot(h, w_ref[:, c:c + SSM_IN_CHUNK]))
    for c in range(0, CONV_CH, SSM_IN_CHUNK):
        cols = slice(c, c + SSM_IN_CHUNK)
        x = _dot(h, w_ref[:, D_INNER + c:D_INNER + c + SSM_IN_CHUNK])
        tail = tail_ref[:, cols]
        acc = x * cw_ref[CONV_K - 1:CONV_K, cols] + cb_ref[:, cols]
        for j in range(1, CONV_K):
            xs = jnp.concatenate([tail[CONV_TAIL - j:], x[:tm - j]], axis=0)
            acc = acc + xs * cw_ref[CONV_K - 1 - j:CONV_K - j, cols]
        xbc_ref[0, :, cols] = _silu(acc)
        tail_ref[:, cols] = x[tm - CONV_TAIL:]
    dt_ref[0] = _softplus(_dot(h, w_ref[:, SSM_DT_OFF:SSM_COLS]) + dtb_ref[...])


def _ssm_in(x, norm_w, mod, w_perm, conv_w, conv_b, dt_bias_pad):
    bsz, seq, d = x.shape
    tm = 512
    shift_s, scale_s, _ = _mod_specs(2)
    tile = lambda width: pl.BlockSpec((1, tm, width), lambda b, i: (b, i, 0))
    return pl.pallas_call(
        _ssm_in_kernel,
        grid=(bsz, seq // tm),
        in_specs=[tile(d), _const_spec((1, d)), shift_s, scale_s, _const_spec(w_perm.shape),
                  _const_spec((CONV_K, CONV_CH)), _const_spec((1, CONV_CH)), _const_spec((1, LANES))],
        out_specs=[tile(D_INNER), tile(CONV_CH), tile(LANES)],
        out_shape=[jax.ShapeDtypeStruct((bsz, seq, D_INNER), F32),
                   jax.ShapeDtypeStruct((bsz, seq, CONV_CH), F32),
                   jax.ShapeDtypeStruct((bsz, seq, LANES), F32)],
        scratch_shapes=[pltpu.VMEM((CONV_TAIL, CONV_CH), F32)],
        compiler_params=_cparams(("parallel", "arbitrary")),
        name="ssm_in",
    )(x, norm_w[None, :], mod, mod, w_perm, conv_w, conv_b[None, :], dt_bias_pad)


SSD_ROWS = 512
SSD_GW = D_INNER // SSM_GROUPS
SSD_HPG = SSM_HEADS // SSM_GROUPS
SSD_B_OFF = D_INNER
SSD_C_OFF = D_INNER + SSM_GROUPS * SSM_STATE
SSD_SPLIT_PARTS = 3


def _ssd_kernel(xbc_ref, z_ref, dt_ref, alog_ref, dskip_ref, nw_ref, ex_ref, o_ref,
                state_ref, xdt_ref, cum_ref, y_ref):
    rb, q = SSD_ROWS, SSM_CHUNK
    nchunk = rb // q

    @pl.when(pl.program_id(1) == 0)
    def _():
        state_ref[...] = jnp.zeros(state_ref.shape, F32)

    dt = dt_ref[0]
    a = dt * (-jnp.exp(alog_ref[...]))
    row = lax.broadcasted_iota(jnp.int32, (rb, rb), 0)
    col = lax.broadcasted_iota(jnp.int32, (rb, rb), 1)
    tril_bd = jnp.where(((row // q) == (col // q)) & (col <= row), 1.0, 0.0).astype(BF16)
    cum = _dot_exact_lhs(tril_bd, a)

    head_lane = lax.broadcasted_iota(jnp.int32, (rb, LANES), 1) < SSM_HEADS

    def lane_parts(v):
        parts = _split_bf16(jnp.where(head_lane, v, 0.0), SSD_SPLIT_PARTS)
        packed = parts[0].astype(F32)
        for i in range(1, SSD_SPLIT_PARTS):
            packed = packed + pltpu.roll(parts[i].astype(F32), i * SSM_HEADS, 1)
        return packed.astype(BF16)

    dt_parts = lane_parts(dt)
    cum_parts = lane_parts(cum)

    lrow = lax.broadcasted_iota(jnp.int32, (q, SSD_GW), 0)
    lcol = lax.broadcasted_iota(jnp.int32, (q, SSD_GW), 1) % q
    causal_t = lcol <= lrow
    eye_t = lcol == lrow
    half = SSD_GW // 2
    brow = lax.broadcasted_iota(jnp.int32, (half, half), 0) // q
    bcol = lax.broadcasted_iota(jnp.int32, (half, half), 1) // SSM_HEAD_DIM
    same_head = brow == bcol

    for g in range(SSM_GROUPS):
        xg = xbc_ref[0, :, g * SSD_GW:(g + 1) * SSD_GW]
        y_ref[g] = dskip_ref[:, g * SSD_GW:(g + 1) * SSD_GW] * xg
        xdt_ref[g] = xg * _dot(dt_parts, ex_ref[g])
        cum_ref[g] = _dot(cum_parts, ex_ref[g])

    def chunk(n, carry):
        rows = pl.ds(pl.multiple_of(n * q, q), q)
        for g in range(SSM_GROUPS):
            cum_c = cum_ref[g, rows, :]
            cum_s = jnp.sum(jnp.where(eye_t, cum_c, 0.0), axis=0, keepdims=True)
            decay_l = jnp.where(causal_t, jnp.exp(cum_c - cum_s), 0.0)
            bc = xbc_ref[0, rows, SSD_B_OFF + g * SSM_STATE:SSD_B_OFF + (g + 1) * SSM_STATE].astype(BF16)
            cc = xbc_ref[0, rows, SSD_C_OFF + g * SSM_STATE:SSD_C_OFF + (g + 1) * SSM_STATE].astype(BF16)
            cb_t = _dot_nt(cc, jnp.concatenate([bc] * SSD_HPG, axis=0))
            mat = (cb_t * decay_l).astype(BF16)
            xdt_c = xdt_ref[g, rows, :]
            xdt_b = xdt_c.astype(BF16)
            y_diag = []
            for s in range(2):
                blk = xdt_b[:, s * half:(s + 1) * half]
                bd = jnp.where(same_head, jnp.concatenate([blk] * (half // q), axis=0), 0.0).astype(BF16)
                y_diag.append(_dot(mat[:, s * half:(s + 1) * half], bd))
            y = jnp.concatenate(y_diag, axis=1)
            cum_end = cum_c[q - 1:q, :]
            st = state_ref[g]
            y = y + _dot(cc, st.astype(BF16)) * jnp.exp(cum_c)
            x_end = (xdt_c * jnp.exp(cum_end - cum_c)).astype(BF16)
            state_ref[g] = st * jnp.exp(cum_end) + _dot_tn(bc, x_end)
            y_ref[g, rows, :] += y
        return carry

    lax.fori_loop(0, nchunk, chunk, 0)

    for g in range(SSM_GROUPS):
        cols = slice(g * SSD_GW, (g + 1) * SSD_GW)
        y = y_ref[g] * z_ref[0, :, cols]
        y = y * lax.rsqrt(jnp.mean(y * y, axis=-1, keepdims=True) + NORM_EPS) * nw_ref[:, cols]
        o_ref[0, :, cols] = y.astype(o_ref.dtype)


def _ssd(xbc, z, dt, a_log_pad, d_skip_x, norm_w, expand):
    bsz, seq, _ = xbc.shape
    rb = SSD_ROWS
    tile = lambda width: pl.BlockSpec((1, rb, width), lambda b, i: (b, i, 0))
    return pl.pallas_call(
        _ssd_kernel,
        grid=(bsz, seq // rb),
        in_specs=[tile(CONV_CH), tile(D_INNER), tile(LANES), _const_spec((1, LANES)),
                  _const_spec((1, D_INNER)), _const_spec((1, D_INNER)), _const_spec(expand.shape)],
        out_specs=tile(D_INNER),
        out_shape=jax.ShapeDtypeStruct((bsz, seq, D_INNER), BF16),
        scratch_shapes=[pltpu.VMEM((SSM_GROUPS, SSM_STATE, SSD_GW), F32),
                        pltpu.VMEM((SSM_GROUPS, rb, SSD_GW), F32), pltpu.VMEM((SSM_GROUPS, rb, SSD_GW), F32),
                        pltpu.VMEM((SSM_GROUPS, rb, SSD_GW), F32)],
        compiler_params=_cparams(("parallel", "arbitrary")),
        name="ssd",
    )(xbc, z, dt, a_log_pad, d_skip_x, norm_w[None, :], expand)


ROUTE_TM = 512


def _pack_bf16_pairs(a):
    w = a.shape[1] // 2
    bits = lax.bitcast_convert_type(a.astype(jnp.bfloat16).astype(F32), jnp.uint32)
    packed = bits[:, w:] | (bits[:, :w] >> 16)
    return lax.bitcast_convert_type(packed, jnp.int32)


def _unpack_bf16_pairs(p):
    bits = lax.bitcast_convert_type(p, jnp.uint32)
    lo = lax.bitcast_convert_type(bits << 16, F32)
    hi = lax.bitcast_convert_type(bits & jnp.uint32(0xFFFF0000), F32)
    return jnp.concatenate([lo, hi], axis=1).astype(BF16)


def _ssm_out_route_kernel(x_ref, y_ref, g1_ref, w_ref, nw_ref, shift_ref, scale_ref, r_ref,
                          x3_ref, h_ref, gw_ref, sel_ref, cnt_ref):
    x3 = x_ref[0] + g1_ref[0] * _dot(y_ref[0], w_ref[...])
    x3_ref[0] = x3
    h = _modulated_norm(x3, nw_ref[...], shift_ref[0], scale_ref[0])
    h_ref[0] = _pack_bf16_pairs(h)
    h_hi, h_lo = _split_bf16(h, 2)
    r_hi, r_lo = _split_bf16(r_ref[...], 2)
    both = _dot(h_hi, jnp.concatenate([r_hi, r_lo], axis=1))
    logits = both[:, :LANES] + (both[:, LANES:] + _dot(h_lo, r_hi))
    lane = lax.broadcasted_iota(jnp.int32, logits.shape, 1)
    logits = jnp.where(lane < N_EXPERTS, logits, -jnp.inf)
    m1 = jnp.max(logits, axis=-1, keepdims=True)
    i1 = jnp.min(jnp.where(logits == m1, lane, LANES), axis=-1, keepdims=True)
    rest = jnp.where(lane == i1, -jnp.inf, logits)
    m2 = jnp.max(rest, axis=-1, keepdims=True)
    i2 = jnp.min(jnp.where(rest == m2, lane, LANES), axis=-1, keepdims=True)
    e2 = jnp.exp(m2 - m1)
    w1 = 1.0 / (1.0 + e2)
    w2 = e2 / (1.0 + e2)
    chosen = (lane == i1) | (lane == i2)
    gw_ref[0] = jnp.where(lane == i1, w1, 0.0) + jnp.where(lane == i2, w2, 0.0)
    sel = jnp.where(chosen, 1.0, 0.0)
    sel_ref[0] = sel.astype(BF16)
    cnt_ref[0] = jnp.sum(sel, axis=0, keepdims=True)


def _ssm_out_route(x, y, mod1, w_out, norm_w, mod2, router_pad):
    bsz, seq, d = x.shape
    tm = ROUTE_TM
    tiles = seq // tm
    _, _, gate1 = _mod_specs(2)
    shift2, scale2, _ = _mod_specs(2)
    tile = lambda width: pl.BlockSpec((1, tm, width), lambda b, i: (b, i, 0))
    return pl.pallas_call(
        _ssm_out_route_kernel,
        grid=(bsz, tiles),
        in_specs=[tile(d), tile(D_INNER), gate1, _const_spec(w_out.shape), _const_spec((1, d)), shift2, scale2,
                  _const_spec(router_pad.shape)],
        out_specs=[tile(d), tile(d // 2), tile(LANES), tile(LANES),
                   pl.BlockSpec((1, 1, LANES), lambda b, i: (b * tiles + i, 0, 0))],
        out_shape=[jax.ShapeDtypeStruct((bsz, seq, d), F32), jax.ShapeDtypeStruct((bsz, seq, d // 2), jnp.int32),
                   jax.ShapeDtypeStruct((bsz, seq, LANES), F32), jax.ShapeDtypeStruct((bsz, seq, LANES), BF16),
                   jax.ShapeDtypeStruct((bsz * tiles, 1, LANES), F32)],
        compiler_params=_cparams(("parallel", "parallel")),
        name="ssm_out_route",
    )(x, y, mod1, w_out, norm_w[None, :], mod2, mod2, router_pad)


def _moe_plan_kernel(gw_ref, sel_ref, base_ref, pos_ref, wts_ref):
    tm = gw_ref.shape[0]
    gw = gw_ref[...]
    sel = sel_ref[...]
    lane = lax.broadcasted_iota(jnp.int32, (tm, LANES), 1)
    chosen = sel.astype(F32) > 0.5
    row = lax.broadcasted_iota(jnp.int32, (tm, tm), 0)
    col = lax.broadcasted_iota(jnp.int32, (tm, tm), 1)
    before = jnp.where(col < row, 1.0, 0.0).astype(BF16)
    rank = _dot(before, sel)
    dest = rank + base_ref[0]
    first = jnp.min(jnp.where(chosen, lane, LANES), axis=-1, keepdims=True)
    last = jnp.max(jnp.where(chosen, lane, -1), axis=-1, keepdims=True)
    ones = jnp.ones((8, LANES), BF16)
    for slot, pick in enumerate((lane == first, lane == last)):
        parts = _split_bf16(jnp.where(pick, dest, 0.0), 3)
        pos = _dot_nt(ones, parts[0]) + (_dot_nt(ones, parts[1]) + _dot_nt(ones, parts[2]))
        pos_ref[slot] = pos.astype(jnp.int32)
    w_lo = jnp.sum(jnp.where(lane == first, gw, 0.0), axis=-1, keepdims=True)
    w_hi = jnp.sum(jnp.where(lane == last, gw, 0.0), axis=-1, keepdims=True)
    wts_ref[...] = jnp.where(lane == 0, w_lo, jnp.where(lane == 1, w_hi, 0.0))


def _moe_plan(gw, sel, base):
    n = gw.shape[0]
    tm = ROUTE_TM
    return pl.pallas_call(
        _moe_plan_kernel,
        grid=(n // tm,),
        in_specs=[pl.BlockSpec((tm, LANES), lambda i: (i, 0)), pl.BlockSpec((tm, LANES), lambda i: (i, 0)),
                  pl.BlockSpec((1, 1, LANES), lambda i: (i, 0, 0))],
        out_specs=[pl.BlockSpec((2, 8, tm), lambda i: (0, 0, i)), pl.BlockSpec((tm, LANES), lambda i: (i, 0))],
        out_shape=[jax.ShapeDtypeStruct((2, 8, n), jnp.int32), jax.ShapeDtypeStruct((n, LANES), F32)],
        compiler_params=_cparams(("parallel",)),
        name="moe_plan",
    )(gw, sel, base)


SC_WINDOW = 128


SC_CORES = 2
SC_SUBCORES = 16


def _sc_workers():
    return SC_CORES, SC_CORES * SC_SUBCORES


def _sc_dispatch(rows, pos):
    n, width = rows.shape
    ncores, workers = _sc_workers()
    per_worker = n // workers
    steps = per_worker // SC_WINDOW
    mesh = plsc.VectorSubcoreMesh(core_axis_name="c", subcore_axis_name="s")

    @functools.partial(
        pl.kernel, mesh=mesh, out_type=jax.ShapeDtypeStruct((2 * n, width), rows.dtype),
        scratch_types=[pltpu.VMEM((2, steps, SC_WINDOW), jnp.int32), pltpu.VMEM((SC_WINDOW, width), rows.dtype)],
        name="moe_dispatch")
    def run(rows_hbm, pos_hbm, out_hbm, idx_v, rows_v):
        wid = lax.axis_index("s") * ncores + lax.axis_index("c")
        pltpu.sync_copy(pos_hbm.at[wid], idx_v)
        for j in range(steps):
            pltpu.sync_copy(rows_hbm.at[pl.ds(wid * per_worker + j * SC_WINDOW, SC_WINDOW)], rows_v)
            pltpu.sync_copy(rows_v, out_hbm.at[idx_v.at[0, j]])
            pltpu.sync_copy(rows_v, out_hbm.at[idx_v.at[1, j]])

    pos_w = pos.reshape(2, workers, steps, SC_WINDOW).transpose(1, 0, 2, 3)
    return run(rows, pos_w)


SC_GATHER_WINDOW = 128


def _sc_gather(table, pos):
    _, width = table.shape
    n = pos.shape[1]
    ncores, workers = _sc_workers()
    per_worker = n // workers
    win = SC_GATHER_WINDOW
    steps = per_worker // win
    mesh = plsc.VectorSubcoreMesh(core_axis_name="c", subcore_axis_name="s")

    @functools.partial(
        pl.kernel, mesh=mesh, out_type=jax.ShapeDtypeStruct((2, n, width), table.dtype),
        scratch_types=[pltpu.VMEM((2, steps, win), jnp.int32), pltpu.VMEM((win, width), table.dtype)],
        name="moe_gather")
    def run(table_hbm, pos_hbm, out_hbm, idx_v, rows_v):
        wid = lax.axis_index("s") * ncores + lax.axis_index("c")
        pltpu.sync_copy(pos_hbm.at[wid], idx_v)
        for s in range(2):
            for j in range(steps):
                pltpu.sync_copy(table_hbm.at[idx_v.at[s, j]], rows_v)
                pltpu.sync_copy(rows_v, out_hbm.at[s, pl.ds(wid * per_worker + j * win, win)])

    pos_w = pos.reshape(2, workers, steps, win).transpose(1, 0, 2, 3)
    return run(table, pos_w)


MOE_TM = 512
MOE_TF = 1024
MOE_VMEM_LIMIT = 62 * 1024 * 1024


def _moe_pairs(goff, n_rows):
    tiles = n_rows // MOE_TM
    steps = tiles + N_EXPERTS - 1
    first_row = jnp.arange(tiles, dtype=jnp.int32) * MOE_TM
    ends = goff[1:]
    e_lo = jnp.sum(first_row[:, None] >= ends[None, :], axis=1).astype(jnp.int32)
    e_hi = jnp.sum((first_row + (MOE_TM - 1))[:, None] >= ends[None, :], axis=1).astype(jnp.int32)
    count = e_hi - e_lo + 1
    start = jnp.cumsum(count) - count
    p = jnp.arange(steps, dtype=jnp.int32)
    tile = jnp.sum(start[None, :] <= p[:, None], axis=1).astype(jnp.int32) - 1
    expert = e_lo[tile] + (p - start[tile])
    valid = p < jnp.sum(count)
    return (jnp.where(valid, tile, tiles - 1).astype(jnp.int32),
            jnp.where(valid, expert, N_EXPERTS - 1).astype(jnp.int32), valid.astype(jnp.int32))


def _moe_group_kernel(tile_ref, exp_ref, valid_ref, goff_ref, x_ref, wg_ref, wu_ref, wd_ref, y_ref, act_ref):
    p = pl.program_id(0)
    tm = x_ref.shape[0]
    tile = tile_ref[p]
    expert = exp_ref[p]

    @pl.when((p == 0) | (tile != tile_ref[jnp.maximum(p - 1, 0)]))
    def _():
        y_ref[...] = jnp.zeros(y_ref.shape, y_ref.dtype)

    @pl.when(valid_ref[p] == 1)
    def _():
        x = _unpack_bf16_pairs(x_ref[...])
        for f, width in _column_chunks(FFN_EXPERT, MOE_TF):
            act_ref[:, f:f + width] = (_silu(_dot(x, wg_ref[0, :, f:f + width]))
                                       * _dot(x, wu_ref[0, :, f:f + width])).astype(BF16)
        out = _pack_bf16_pairs(_dot(act_ref[...], wd_ref[0]))
        rows = tile * tm + lax.broadcasted_iota(jnp.int32, (tm, 1), 0)
        mine = (rows >= goff_ref[expert]) & (rows < goff_ref[expert + 1])
        y_ref[...] = jnp.where(mine, out, y_ref[...])


def _moe_group(xs, goff, w_gu, wd):
    n_rows, half = xs.shape
    d = 2 * half
    tm = MOE_TM
    tile, expert, valid = _moe_pairs(goff, n_rows)

    def resident(shape, col_block=0):
        return pl.BlockSpec(shape, lambda p, t, e, v, g: (e[p], 0, col_block))

    grid_spec = pltpu.PrefetchScalarGridSpec(
        num_scalar_prefetch=4,
        grid=(tile.shape[0],),
        in_specs=[pl.BlockSpec((tm, half), lambda p, t, e, v, g: (t[p], 0)),
                  resident((1, d, FFN_EXPERT), 0), resident((1, d, FFN_EXPERT), 1), resident((1, FFN_EXPERT, d))],
        out_specs=pl.BlockSpec((tm, half), lambda p, t, e, v, g: (t[p], 0)),
        scratch_shapes=[pltpu.VMEM((tm, FFN_EXPERT), BF16)],
    )
    return pl.pallas_call(
        _moe_group_kernel,
        grid_spec=grid_spec,
        out_shape=jax.ShapeDtypeStruct((n_rows, half), jnp.int32),
        compiler_params=_cparams(("arbitrary",), MOE_VMEM_LIMIT),
        name="moe_group",
    )(tile, expert, valid, goff, xs, w_gu, w_gu, wd)


def _moe_combine_kernel(x_ref, yg_ref, wts_ref, g_ref, fn_ref, o_ref):
    w = wts_ref[0]
    mix = (w[:, 0:1] * _unpack_bf16_pairs(yg_ref[0, 0]).astype(F32)
           + w[:, 1:2] * _unpack_bf16_pairs(yg_ref[1, 0]).astype(F32))
    x4 = x_ref[0] + g_ref[0] * mix
    y = x4 * lax.rsqrt(jnp.mean(x4 * x4, axis=-1, keepdims=True) + NORM_EPS)
    o_ref[0] = y * fn_ref[...]


def _moe_combine(x, yg, wts, mod, final_norm):
    bsz, seq, d = x.shape
    tm = 512
    _, _, gate = _mod_specs(2)
    tile = lambda width: pl.BlockSpec((1, tm, width), lambda b, i: (b, i, 0))
    return pl.pallas_call(
        _moe_combine_kernel,
        grid=(bsz, seq // tm),
        in_specs=[tile(d), pl.BlockSpec((2, 1, tm, d // 2), lambda b, i: (0, b, i, 0)), tile(LANES), gate,
                  _const_spec((1, d))],
        out_specs=tile(d),
        out_shape=jax.ShapeDtypeStruct((bsz, seq, d), F32),
        compiler_params=_cparams(("parallel", "parallel")),
        name="moe_combine",
    )(x, yg, wts, mod, final_norm[None, :])


def _pad_cols(a, width):
    return jnp.pad(a, ((0, 0), (0, width - a.shape[1])))


def _split(a, sizes):
    return jnp.split(a, [int(s) for s in np.cumsum(sizes)[:-1]], axis=-1)


def _prep_hyb_w_in(w):
    q_a, k_a, v_a, lr_a, r_a, q_b, kv_b, g_b = _split(w, HYB_SPLITS)
    return jnp.concatenate([q_a, k_a, v_a, r_a, _pad_cols(lr_a, LANES), _pad_cols(g_b, LANES), q_b, kv_b],
                           axis=1).astype(BF16)


def _prep_cmp(pe, w1, w2):
    eye = jnp.eye(NSA_KV_GROUPS, dtype=F32)
    half = CMP_BLOCK // 2
    w1r = w1.reshape(2, 2, half, NSA_HEAD_DIM, CMP_HIDDEN)
    w1x = jnp.einsum("jstdc,gh->jstgdhc", w1r, eye).reshape(2, 2, half * LANES, NSA_KV_GROUPS * CMP_HIDDEN)
    w2x = jnp.einsum("jcd,gh->jgchd", w2, eye).reshape(2, NSA_KV_GROUPS * CMP_HIDDEN, LANES)
    per = pe.reshape(2, 2, half, 1, NSA_HEAD_DIM)
    pe_rows = jnp.broadcast_to(per, (2, 2, half, NSA_KV_GROUPS, NSA_HEAD_DIM)).reshape(2, 2, 1, half * LANES)
    return pe_rows, w1x.astype(BF16), w2x.astype(BF16)


def _overlap_matrix(seq):
    n_rows = seq // CMP_STRIDE
    n = np.arange(n_rows)[:, None] * CMP_STRIDE
    s = np.arange(LANES)[None, :] * SEL_BLOCK
    ov = (n < s + SEL_BLOCK) & (n + CMP_BLOCK - 1 >= s) & (np.arange(LANES)[None, :] < seq // SEL_BLOCK)
    return jnp.asarray(ov, BF16)


def _head_expand():
    ex = np.zeros((SSM_GROUPS, LANES, SSD_GW), np.float32)
    for g in range(SSM_GROUPS):
        for j in range(SSD_HPG):
            for part in range(SSD_SPLIT_PARTS):
                ex[g, part * SSM_HEADS + g * SSD_HPG + j, j * SSM_HEAD_DIM:(j + 1) * SSM_HEAD_DIM] = 1.0
    return jnp.asarray(ex, BF16)


def kernel(x, c, hyb_norm, hyb_mod_w, hyb_mod_b, hyb_w_in, gla_gk_up, gla_gk_bias, gla_out_norm, nsa_cmp_pe, nsa_cmp_w1, nsa_cmp_w2, hyb_w_out, dense_norm, dense_mod_w, dense_mod_b, dense_w_gu, dense_w_down, ssm_norm, ssm_mod_w, ssm_mod_b, ssm_w_in, ssm_conv_w, ssm_conv_b, ssm_dt_bias, ssm_a_log, ssm_d, ssm_gate_norm, ssm_w_out, moe_norm, moe_mod_w, moe_mod_b, moe_router, moe_w_gu, moe_w_down, final_norm):
    bsz, seq, d = x.shape
    mods = _adaln(c, (hyb_mod_w[0], dense_mod_w[0], ssm_mod_w[0], moe_mod_w[0]),
                  (hyb_mod_b[0], dense_mod_b[0], ssm_mod_b[0], moe_mod_b[0]))
    mods = mods.reshape(4, bsz, 1, 3 * d)

    gkup_pad = jnp.pad(gla_gk_up[0], ((0, LANES - GLA_LOWRANK), (0, 0))).astype(BF16)
    (gq, gk, la, gv, gr, nq, kcmp, vcmp, kslc, vslc, kwin, vwin, ng) = _hyb_in(
        x, hyb_norm[0], mods[0], _prep_hyb_w_in(hyb_w_in[0]), gkup_pad, gla_gk_bias[0])
    o_gla = _gla(gq, gk, la, gv, gr, gla_out_norm[0])
    pe_rows, w1x, w2x = _prep_cmp(nsa_cmp_pe[0], nsa_cmp_w1[0], nsa_cmp_w2[0])
    kc, vc = _nsa_compress(kcmp, vcmp, pe_rows, w1x, w2x)
    o_nsa = _nsa_attn(nq, ng, kc, vc, kslc, vslc, kwin, vwin, _overlap_matrix(seq))
    x = _hyb_out_ffn(x, o_gla, o_nsa, mods[0], hyb_w_out[0].astype(BF16), dense_norm[0], mods[1],
                     dense_w_gu[0].astype(BF16), dense_w_down[0].astype(BF16))

    wz, wxbc, wdt = _split(ssm_w_in[0], SSM_SPLITS)
    w_ssm = jnp.concatenate([wz, wxbc, _pad_cols(wdt, LANES)], axis=1).astype(BF16)
    pad_heads = lambda a: jnp.pad(a, (0, LANES - SSM_HEADS))[None, :]
    z, xbc, dt = _ssm_in(x, ssm_norm[0], mods[2], w_ssm, ssm_conv_w[0], ssm_conv_b[0], pad_heads(ssm_dt_bias[0]))
    y = _ssd(xbc, z, dt, pad_heads(ssm_a_log[0]), jnp.repeat(ssm_d[0], SSM_HEAD_DIM)[None, :],
             ssm_gate_norm[0], _head_expand())
    x3, h4, gw, sel, cnt = _ssm_out_route(x, y, mods[2], ssm_w_out[0].astype(BF16), moe_norm[0], mods[3],
                                          _pad_cols(moe_router[0], LANES))
    n = bsz * seq
    cnt = cnt[:, 0, :]
    totals = jnp.sum(cnt, axis=0)
    goff_f = jnp.cumsum(totals) - totals
    base = (jnp.cumsum(cnt, axis=0) - cnt + goff_f[None, :])[:, None, :]
    goff = jnp.concatenate([goff_f[:N_EXPERTS], jnp.full((1,), 2.0 * n, F32)]).astype(jnp.int32)
    pos, wts = _moe_plan(gw.reshape(n, LANES), sel.reshape(n, LANES), base)
    pos = pos[:, 0, :]
    xs = _sc_dispatch(h4.reshape(n, d // 2), pos)
    ys = _moe_group(xs, goff, moe_w_gu[0].astype(BF16), moe_w_down[0].astype(BF16))
    yg = _sc_gather(ys, pos)
    return _moe_combine(x3, yg.reshape(2, bsz, seq, d // 2), wts.reshape(bsz, seq, LANES), mods[3], final_norm)
```
